```python
import math
import jax, jax.numpy as jnp
from jax import lax
import numpy as np

D_MODEL = 1024
BATCH = 8
SEQ = 8192
DEPTH = 2

CHUNK = 64
CONV_WIDTH = 512
CONV_GROUPS = 8
CONV_TAPS = 3
N_HEADS = 8
HEAD_DIM = 64
ATTN_WIDTH = N_HEADS * HEAD_DIM
D_FF = 4 * D_MODEL
Q_BLOCK = 128
N_MOD = 6
EPS = 1e-6
IN_COLS = 3 * CONV_WIDTH + 3 * ATTN_WIDTH + 2 * D_MODEL

kernel_name = "hybrid_shortconv_stickbreaking_block"


def rms_norm(x, g):
    xf = x.astype(jnp.float32)
    y = xf * lax.rsqrt(jnp.mean(xf * xf, axis=-1, keepdims=True) + EPS)
    return (y * g.astype(jnp.float32)).astype(x.dtype)


def modulate(h, shift, scale):
    return h * (1.0 + scale[:, None, :]) + shift[:, None, :]


def short_conv_branch(b_gate, c_gate, u, conv_w):
    s = u.shape[1]
    v = c_gate * u
    vp = jnp.pad(v, ((0, 0), (CONV_TAPS - 1, 0), (0, 0)))
    y = sum(conv_w[k] * vp[:, k:k + s, :] for k in range(CONV_TAPS))
    return b_gate * y


def stick_breaking_attention(q, k, v):
    b, s, h, dh = q.shape
    qh = jnp.transpose(q, (0, 2, 1, 3))
    kh = jnp.transpose(k, (0, 2, 1, 3))
    vh = jnp.transpose(v, (0, 2, 1, 3))
    inv_sqrt = 1.0 / math.sqrt(dh)
    key_pos = jnp.arange(s)
    n_blocks = s // Q_BLOCK

    def block(i):
        start = i * Q_BLOCK
        q_blk = lax.dynamic_slice_in_dim(qh, start, Q_BLOCK, axis=2)
        z = jnp.einsum('bhqd,bhkd->bhqk', q_blk, kh).astype(jnp.float32) * inv_sqrt
        q_pos = start + jnp.arange(Q_BLOCK)
        mask = key_pos[None, :] < q_pos[:, None]
        log_1m_beta = jnp.where(mask, jax.nn.log_sigmoid(-z), 0.0)
        suffix = lax.cumsum(log_1m_beta, axis=3, reverse=True) - log_1m_beta
        a = jnp.where(mask, jnp.exp(jax.nn.log_sigmoid(z) + suffix), 0.0)
        return jnp.einsum('bhqk,bhkd->bhqd', a, vh.astype(jnp.float32)).astype(q.dtype)

    o = lax.map(block, jnp.arange(n_blocks))
    o = jnp.transpose(o, (1, 0, 3, 2, 4))
    return o.reshape(b, s, h * dh)


def _fwd_setup_inputs(seed: int = 0) -> dict:
    key = jax.random.key(seed)
    ks = jax.random.split(key, 16)
    f32 = jnp.float32
    L, D = DEPTH, D_MODEL

    def nrm(k, shape, fan_in):
        return jax.random.normal(k, shape, f32) * (fan_in ** -0.5)

    def gain(k):
        return 1.0 + 0.05 * jax.random.normal(k, (L, D), f32)

    return {
        "x": jax.random.normal(ks[0], (BATCH, SEQ, D), f32),
        "c": jax.random.normal(ks[1], (BATCH, D), f32),
        "w_ada": nrm(ks[2], (L, D, N_MOD * D), D) * 0.5,
        "b_ada": 0.01 * jax.random.normal(ks[3], (L, N_MOD * D), f32),
        "g_pre_mix": gain(ks[4]),
        "g_post_mix": gain(ks[5]),
        "g_pre_mlp": gain(ks[6]),
        "g_post_mlp": gain(ks[7]),
        "w_in": nrm(ks[8], (L, D, IN_COLS), D),
        "conv_w": nrm(ks[9], (L, CONV_TAPS, CONV_WIDTH), CONV_TAPS),
        "w_proj_conv": nrm(ks[10], (L, CONV_WIDTH, D), CONV_WIDTH),
        "w_proj_attn": nrm(ks[11], (L, ATTN_WIDTH, D), ATTN_WIDTH),
        "w_out": nrm(ks[12], (L, D, D), D),
        "w_mlp_in": nrm(ks[13], (L, D, D_FF), D),
        "w_mlp_out": nrm(ks[14], (L, D_FF, D), D_FF),
    }


def _fwd_reference(x, c, w_ada, b_ada, g_pre_mix, g_post_mix, g_pre_mlp, g_post_mlp,
              w_in, conv_w, w_proj_conv, w_proj_attn, w_out, w_mlp_in, w_mlp_out):
    b, s, d = x.shape
    split_pts = np.cumsum([CONV_WIDTH, CONV_WIDTH, CONV_WIDTH,
                           ATTN_WIDTH, ATTN_WIDTH, ATTN_WIDTH, D_MODEL])
    for l in range(DEPTH):
        mod = c @ w_ada[l] + b_ada[l]
        sh1, sc1, gt1, sh2, sc2, gt2 = jnp.split(mod, N_MOD, axis=-1)

        h = modulate(rms_norm(x, g_pre_mix[l]), sh1, sc1)
        proj = h @ w_in[l]
        bg, cg, u, q, k, v, ga, gb = jnp.split(proj, split_pts, axis=-1)
        y_conv = short_conv_branch(bg, cg, u, conv_w[l]) @ w_proj_conv[l]
        o = stick_breaking_attention(q.reshape(b, s, N_HEADS, HEAD_DIM),
                                     k.reshape(b, s, N_HEADS, HEAD_DIM),
                                     v.reshape(b, s, N_HEADS, HEAD_DIM))
        y_attn = o @ w_proj_attn[l]
        merged = jax.nn.sigmoid(ga) * y_conv + jax.nn.sigmoid(gb) * y_attn
        mix_out = merged @ w_out[l]
        x = x + gt1[:, None, :] * rms_norm(mix_out, g_post_mix[l])

        h2 = modulate(rms_norm(x, g_pre_mlp[l]), sh2, sc2)
        ff = jnp.square(jax.nn.relu(h2 @ w_mlp_in[l])) @ w_mlp_out[l]
        x = x + gt2[:, None, :] * rms_norm(ff, g_post_mlp[l])
    return x


import jax as _jax
import jax.numpy as _jnp

TWIN_FORMAT = 'train_step'
FWD_PARAMS = ['x', 'c', 'w_ada', 'b_ada', 'g_pre_mix', 'g_post_mix', 'g_pre_mlp', 'g_post_mlp', 'w_in', 'conv_w', 'w_proj_conv', 'w_proj_attn', 'w_out', 'w_mlp_in', 'w_mlp_out']
TWIN_WEIGHTS = ['w_ada', 'b_ada', 'g_pre_mix', 'g_post_mix', 'g_pre_mlp', 'g_post_mlp', 'w_in', 'conv_w', 'w_proj_conv', 'w_proj_attn', 'w_out', 'w_mlp_in', 'w_mlp_out']
TWIN_DIFF_INPUT = 'x'
TWIN_INPUTS = ['x', 'c', 'w_ada', 'b_ada', 'g_pre_mix', 'g_post_mix', 'g_pre_mlp', 'g_post_mlp', 'w_in', 'conv_w', 'w_proj_conv', 'w_proj_attn', 'w_out', 'w_mlp_in', 'w_mlp_out', 'loss_target', 'm_w_ada', 'm_b_ada', 'm_g_pre_mix', 'm_g_post_mix', 'm_g_pre_mlp', 'm_g_post_mlp', 'm_w_in', 'm_conv_w', 'm_w_proj_conv', 'm_w_proj_attn', 'm_w_out', 'm_w_mlp_in', 'm_w_mlp_out', 'v_w_ada', 'v_b_ada', 'v_g_pre_mix', 'v_g_post_mix', 'v_g_pre_mlp', 'v_g_post_mlp', 'v_w_in', 'v_conv_w', 'v_w_proj_conv', 'v_w_proj_attn', 'v_w_out', 'v_w_mlp_in', 'v_w_mlp_out']
TWIN_OUTPUTS = ['loss', 'grad_x', 'grad_w_ada', 'grad_b_ada', 'grad_g_pre_mix', 'grad_g_post_mix', 'grad_g_pre_mlp', 'grad_g_post_mlp', 'grad_w_in', 'grad_conv_w', 'grad_w_proj_conv', 'grad_w_proj_attn', 'grad_w_out', 'grad_w_mlp_in', 'grad_w_mlp_out', 'delta_w_ada', 'delta_b_ada', 'delta_g_pre_mix', 'delta_g_post_mix', 'delta_g_pre_mlp', 'delta_g_post_mlp', 'delta_w_in', 'delta_conv_w', 'delta_w_proj_conv', 'delta_w_proj_attn', 'delta_w_out', 'delta_w_mlp_in', 'delta_w_mlp_out', 'new_m_w_ada', 'new_m_b_ada', 'new_m_g_pre_mix', 'new_m_g_post_mix', 'new_m_g_pre_mlp', 'new_m_g_post_mlp', 'new_m_w_in', 'new_m_conv_w', 'new_m_w_proj_conv', 'new_m_w_proj_attn', 'new_m_w_out', 'new_m_w_mlp_in', 'new_m_w_mlp_out', 'new_v_w_ada', 'new_v_b_ada', 'new_v_g_pre_mix', 'new_v_g_post_mix', 'new_v_g_pre_mlp', 'new_v_g_post_mlp', 'new_v_w_in', 'new_v_conv_w', 'new_v_w_proj_conv', 'new_v_w_proj_attn', 'new_v_w_out', 'new_v_w_mlp_in', 'new_v_w_mlp_out']
TWIN_LEAF_KINDS = {'loss': 'loss', 'grad_x': 'grad_x', 'grad_w_ada': 'grad_w', 'grad_b_ada': 'grad_w', 'grad_g_pre_mix': 'grad_w', 'grad_g_post_mix': 'grad_w', 'grad_g_pre_mlp': 'grad_w', 'grad_g_post_mlp': 'grad_w', 'grad_w_in': 'grad_w', 'grad_conv_w': 'grad_w', 'grad_w_proj_conv': 'grad_w', 'grad_w_proj_attn': 'grad_w', 'grad_w_out': 'grad_w', 'grad_w_mlp_in': 'grad_w', 'grad_w_mlp_out': 'grad_w', 'delta_w_ada': 'delta_w', 'delta_b_ada': 'delta_w', 'delta_g_pre_mix': 'delta_w', 'delta_g_post_mix': 'delta_w', 'delta_g_pre_mlp': 'delta_w', 'delta_g_post_mlp': 'delta_w', 'delta_w_in': 'delta_w', 'delta_conv_w': 'delta_w', 'delta_w_proj_conv': 'delta_w', 'delta_w_proj_attn': 'delta_w', 'delta_w_out': 'delta_w', 'delta_w_mlp_in': 'delta_w', 'delta_w_mlp_out': 'delta_w', 'new_m_w_ada': 'new_m', 'new_m_b_ada': 'new_m', 'new_m_g_pre_mix': 'new_m', 'new_m_g_post_mix': 'new_m', 'new_m_g_pre_mlp': 'new_m', 'new_m_g_post_mlp': 'new_m', 'new_m_w_in': 'new_m', 'new_m_conv_w': 'new_m', 'new_m_w_proj_conv': 'new_m', 'new_m_w_proj_attn': 'new_m', 'new_m_w_out': 'new_m', 'new_m_w_mlp_in': 'new_m', 'new_m_w_mlp_out': 'new_m', 'new_v_w_ada': 'new_v', 'new_v_b_ada': 'new_v', 'new_v_g_pre_mix': 'new_v', 'new_v_g_post_mix': 'new_v', 'new_v_g_pre_mlp': 'new_v', 'new_v_g_post_mlp': 'new_v', 'new_v_w_in': 'new_v', 'new_v_conv_w': 'new_v', 'new_v_w_proj_conv': 'new_v', 'new_v_w_proj_attn': 'new_v', 'new_v_w_out': 'new_v', 'new_v_w_mlp_in': 'new_v', 'new_v_w_mlp_out': 'new_v'}


def _forward(args):
    return _fwd_reference(*[args[k] for k in FWD_PARAMS])


def _output_shape():
    def fwd():
        inp = _fwd_setup_inputs(0)
        return _fwd_reference(*[inp[k] for k in FWD_PARAMS])
    out = _jax.eval_shape(fwd)
    return out.shape, out.dtype

N_MICROBATCH = 1
ADAM_LR = 0.001
ADAM_B1 = 0.9
ADAM_B2 = 0.999
ADAM_EPS = 1e-08
ADAM_WD = 0.01
ADAM_STEP = 10
PER_EXAMPLE_BATCH_AXIS = {'x': 0, 'c': 0, 'loss_target': 0}
SHARED_INPUTS = []
_WEIGHT_DTYPES = {'w_ada': _jnp.float32, 'b_ada': _jnp.float32, 'g_pre_mix': _jnp.float32, 'g_post_mix': _jnp.float32, 'g_pre_mlp': _jnp.float32, 'g_post_mlp': _jnp.float32, 'w_in': _jnp.float32, 'conv_w': _jnp.float32, 'w_proj_conv': _jnp.float32, 'w_proj_attn': _jnp.float32, 'w_out': _jnp.float32, 'w_mlp_in': _jnp.float32, 'w_mlp_out': _jnp.float32}
MOMENT_SCALE = {'w_ada': 7.085163e+00, 'b_ada': 6.896024e+00, 'g_pre_mix': 5.447835e-01, 'g_post_mix': 1.828351e+01, 'g_pre_mlp': 5.383182e-01, 'g_post_mlp': 1.866388e+01, 'w_in': 4.565809e-01, 'conv_w': 5.148526e-01, 'w_proj_conv': 4.017789e-01, 'w_proj_attn': 1.107747e+00, 'w_out': 1.025132e+00, 'w_mlp_in': 7.608210e-01, 'w_mlp_out': 2.675975e+00}


def _to_microbatches(a, axis):
    t = _jnp.moveaxis(a, axis, 0)
    t = t.reshape((N_MICROBATCH, t.shape[0] // N_MICROBATCH) + t.shape[1:])
    return _jnp.moveaxis(t, 1, axis + 1)


def setup_inputs(seed: int = 0) -> dict:
    inp = _fwd_setup_inputs(seed)
    key = _jax.random.fold_in(_jax.random.key(seed), 7919)
    shape, _ = _output_shape()
    out = dict(inp)
    out["loss_target"] = _jax.random.normal(_jax.random.fold_in(key, 0), shape, _jnp.float32)
    for i, name in enumerate(TWIN_WEIGHTS):
        w = inp[name].astype(_jnp.float32)
        if MOMENT_SCALE is None:
            s = _jnp.sqrt(_jnp.mean(_jnp.square(w)) + 1e-30)
        else:
            s = MOMENT_SCALE[name]
        km, kv = _jax.random.split(_jax.random.fold_in(key, i + 1))
        out[name] = w
        out["m_" + name] = s * _jax.random.normal(km, w.shape, _jnp.float32)
        out["v_" + name] = (s * s) * _jax.random.uniform(kv, w.shape, _jnp.float32, 0.5, 1.5)
    if N_MICROBATCH > 1:
        for name, axis in PER_EXAMPLE_BATCH_AXIS.items():
            out[name] = _to_microbatches(out[name], axis)
    return {'x': out['x'], 'c': out['c'], 'w_ada': out['w_ada'], 'b_ada': out['b_ada'], 'g_pre_mix': out['g_pre_mix'], 'g_post_mix': out['g_post_mix'], 'g_pre_mlp': out['g_pre_mlp'], 'g_post_mlp': out['g_post_mlp'], 'w_in': out['w_in'], 'conv_w': out['conv_w'], 'w_proj_conv': out['w_proj_conv'], 'w_proj_attn': out['w_proj_attn'], 'w_out': out['w_out'], 'w_mlp_in': out['w_mlp_in'], 'w_mlp_out': out['w_mlp_out'], 'loss_target': out['loss_target'], 'm_w_ada': out['m_w_ada'], 'm_b_ada': out['m_b_ada'], 'm_g_pre_mix': out['m_g_pre_mix'], 'm_g_post_mix': out['m_g_post_mix'], 'm_g_pre_mlp': out['m_g_pre_mlp'], 'm_g_post_mlp': out['m_g_post_mlp'], 'm_w_in': out['m_w_in'], 'm_conv_w': out['m_conv_w'], 'm_w_proj_conv': out['m_w_proj_conv'], 'm_w_proj_attn': out['m_w_proj_attn'], 'm_w_out': out['m_w_out'], 'm_w_mlp_in': out['m_w_mlp_in'], 'm_w_mlp_out': out['m_w_mlp_out'], 'v_w_ada': out['v_w_ada'], 'v_b_ada': out['v_b_ada'], 'v_g_pre_mix': out['v_g_pre_mix'], 'v_g_post_mix': out['v_g_post_mix'], 'v_g_pre_mlp': out['v_g_pre_mlp'], 'v_g_post_mlp': out['v_g_post_mlp'], 'v_w_in': out['v_w_in'], 'v_conv_w': out['v_conv_w'], 'v_w_proj_conv': out['v_w_proj_conv'], 'v_w_proj_attn': out['v_w_proj_attn'], 'v_w_out': out['v_w_out'], 'v_w_mlp_in': out['v_w_mlp_in'], 'v_w_mlp_out': out['v_w_mlp_out']}


def _loss(weights, diff, rest, loss_target):
    with _jax.named_scope("forward"):
        args = {**rest, TWIN_DIFF_INPUT: diff, **{k: w.astype(_WEIGHT_DTYPES[k]) for k, w in weights.items()}}
        y = _forward(args)
    with _jax.named_scope("loss_head"):
        err = _jnp.square(y.astype(_jnp.float32) - loss_target)
        return 0.5 * _jnp.sum(_jnp.mean(err, axis=-1)) if err.ndim else 0.5 * err


def _adamw(w, g, m, v):
    m = ADAM_B1 * m + (1.0 - ADAM_B1) * g
    v = ADAM_B2 * v + (1.0 - ADAM_B2) * _jnp.square(g)
    m_hat = m / (1.0 - ADAM_B1 ** ADAM_STEP)
    v_hat = v / (1.0 - ADAM_B2 ** ADAM_STEP)
    delta = -ADAM_LR * (m_hat / (_jnp.sqrt(v_hat) + ADAM_EPS) + ADAM_WD * w)
    return delta, m, v


def reference(x, c, w_ada, b_ada, g_pre_mix, g_post_mix, g_pre_mlp, g_post_mlp, w_in, conv_w, w_proj_conv, w_proj_attn, w_out, w_mlp_in, w_mlp_out, loss_target, m_w_ada, m_b_ada, m_g_pre_mix, m_g_post_mix, m_g_pre_mlp, m_g_post_mlp, m_w_in, m_conv_w, m_w_proj_conv, m_w_proj_attn, m_w_out, m_w_mlp_in, m_w_mlp_out, v_w_ada, v_b_ada, v_g_pre_mix, v_g_post_mix, v_g_pre_mlp, v_g_post_mlp, v_w_in, v_conv_w, v_w_proj_conv, v_w_proj_attn, v_w_out, v_w_mlp_in, v_w_mlp_out):
    given = dict(x=x, c=c, w_ada=w_ada, b_ada=b_ada, g_pre_mix=g_pre_mix, g_post_mix=g_post_mix, g_pre_mlp=g_pre_mlp, g_post_mlp=g_post_mlp, w_in=w_in, conv_w=conv_w, w_proj_conv=w_proj_conv, w_proj_attn=w_proj_attn, w_out=w_out, w_mlp_in=w_mlp_in, w_mlp_out=w_mlp_out, loss_target=loss_target, m_w_ada=m_w_ada, m_b_ada=m_b_ada, m_g_pre_mix=m_g_pre_mix, m_g_post_mix=m_g_post_mix, m_g_pre_mlp=m_g_pre_mlp, m_g_post_mlp=m_g_post_mlp, m_w_in=m_w_in, m_conv_w=m_conv_w, m_w_proj_conv=m_w_proj_conv, m_w_proj_attn=m_w_proj_attn, m_w_out=m_w_out, m_w_mlp_in=m_w_mlp_in, m_w_mlp_out=m_w_mlp_out, v_w_ada=v_w_ada, v_b_ada=v_b_ada, v_g_pre_mix=v_g_pre_mix, v_g_post_mix=v_g_post_mix, v_g_pre_mlp=v_g_pre_mlp, v_g_post_mlp=v_g_post_mlp, v_w_in=v_w_in, v_conv_w=v_conv_w, v_w_proj_conv=v_w_proj_conv, v_w_proj_attn=v_w_proj_attn, v_w_out=v_w_out, v_w_mlp_in=v_w_mlp_in, v_w_mlp_out=v_w_mlp_out)
    weights = {n: given[n] for n in TWIN_WEIGHTS}
    shared = {n: given[n] for n in SHARED_INPUTS}
    per_example = {n: given[n] for n in ['x', 'c']}
    grad_fn = _jax.value_and_grad(_loss, argnums=(0, 1))

    def one_microbatch(ex, loss_target):
        ex = dict(ex)
        diff = ex.pop(TWIN_DIFF_INPUT)
        return grad_fn(weights, diff, {**shared, **ex}, loss_target)

    if N_MICROBATCH == 1:
        loss, (grad_w, grad_x) = one_microbatch(per_example, given["loss_target"])
    else:
        def body(carry, xs):
            loss_sum, grad_sum = carry
            l_k, (gw_k, gx_k) = one_microbatch(xs[0], xs[1])
            with _jax.named_scope("update"):
                return (loss_sum + l_k, _jax.tree.map(_jnp.add, grad_sum, gw_k)), gx_k

        init = (_jnp.zeros((), _jnp.float32), _jax.tree.map(_jnp.zeros_like, weights))
        (loss, grad_w), grad_x = _jax.lax.scan(body, init, (per_example, given["loss_target"]))
    with _jax.named_scope("update"):
        delta_w, new_m, new_v = {}, {}, {}
        for n in TWIN_WEIGHTS:
            delta_w[n], new_m[n], new_v[n] = _adamw(weights[n], grad_w[n], given["m_" + n], given["v_" + n])
    return (loss, grad_x, *[grad_w[n] for n in TWIN_WEIGHTS], *[delta_w[n] for n in TWIN_WEIGHTS],
            *[new_m[n] for n in TWIN_WEIGHTS], *[new_v[n] for n in TWIN_WEIGHTS])
```

```python
import jax
import jax.numpy as jnp
from jax import lax
from jax.experimental import pallas as pl
from jax.experimental.pallas import tpu as pltpu

F32 = jnp.float32
BF16 = jnp.bfloat16
MESH = pl.DeviceIdType.MESH

N_DEV = 8
D_MODEL = 1024
CONV_WIDTH = 512
N_HEADS = 8
HEAD_DIM = 64
ATTN_WIDTH = N_HEADS * HEAD_DIM
D_FF = 4 * D_MODEL
N_MOD = 6
DEPTH = 2
EPS = 1e-6
IN_COLS = 3 * CONV_WIDTH + 3 * ATTN_WIDTH + 2 * D_MODEL
LANES = 128

ADAM_LR = 0.001
ADAM_B1 = 0.9
ADAM_B2 = 0.999
ADAM_EPS = 1e-08
ADAM_WD = 0.01
ADAM_STEP = 10

TM = 512
TQ = 512
TK = 128
VMEM_LIMIT = 56 * 1024 * 1024

NN = (((1,), (0,)), ((), ()))
NT = (((1,), (1,)), ((), ()))
TN = (((0,), (0,)), ((), ()))
_DIMS = {"nn": NN, "nt": NT, "tn": TN}


def _call(body, **kw):
    return pl.pallas_call(body, **kw)


def _params(*sem):
    return pltpu.CompilerParams(dimension_semantics=sem, vmem_limit_bytes=VMEM_LIMIT)


def _dot(a, b, dims=NN):
    return lax.dot_general(a, b, dims, preferred_element_type=F32)


def _mesh_pos():
    return lax.axis_index("x"), lax.axis_index("y"), lax.axis_index("c")


def _all_gather(name, arrs):
    n = len(arrs)

    def body(*refs):
        ins, outs = refs[:n], refs[n:2 * n]
        send_sems, recv_sems, local_sems = refs[2 * n:]
        x, y, c = _mesh_pos()
        me, sibling = (x, y, c), (x, y, 1 - c)
        chips = [(1 - x, y), (x, 1 - y), (1 - x, 1 - y)]

        def blk(t, p):
            return outs[t].at[4 * p[0] + 2 * p[1] + p[2]]

        def copy(t, k, block, to, src=None):
            return pltpu.make_async_remote_copy(
                src_ref=blk(t, block) if src is None else src, dst_ref=blk(t, block),
                send_sem=send_sems.at[7 * t + k], recv_sem=recv_sems.at[7 * t + k],
                device_id=to, device_id_type=MESH)

        mine, first, passed = [], [], []
        for t in range(n):
            cp = pltpu.make_async_copy(ins[t], blk(t, me), local_sems.at[t])
            cp.start()
            mine.append(cp)
            cps = [copy(t, 0, me, sibling, src=ins[t])]
            cps += [copy(t, 1 + j, me, (*chip, c), src=ins[t]) for j, chip in enumerate(chips)]
            for cp in cps:
                cp.start()
            first += cps
        for t in range(n):
            for j, chip in enumerate(chips):
                copy(t, 1 + j, (*chip, c), me).wait_recv()
                cp = copy(t, 4 + j, (*chip, c), sibling)
                cp.start()
                passed.append(cp)
        for t in range(n):
            copy(t, 0, sibling, me).wait_recv()
            for j, chip in enumerate(chips):
                copy(t, 4 + j, (*chip, 1 - c), me).wait_recv()
        for cp in first + passed:
            cp.wait_send()
        for cp in mine:
            cp.wait()

    any_spec = pl.BlockSpec(memory_space=pl.ANY)
    return _call(
        body, name=name,
        out_shape=[jax.ShapeDtypeStruct((N_DEV,) + a.shape, a.dtype) for a in arrs],
        in_specs=[any_spec] * n, out_specs=[any_spec] * n,
        scratch_shapes=[pltpu.SemaphoreType.DMA((7 * n,)), pltpu.SemaphoreType.DMA((7 * n,)),
                        pltpu.SemaphoreType.DMA((n,))],
    )(*arrs)


def _exchange(name, arrs):
    n = len(arrs)

    def body(*refs):
        ins, outs = refs[:n], refs[n:2 * n]
        send_sems, recv_sems, local_sems = refs[2 * n:]
        x, y, c = _mesh_pos()
        my_idx = 4 * x + 2 * y + c
        peers = []
        for k in range(1, N_DEV):
            p = (1 - x if k & 4 else x, 1 - y if k & 2 else y, 1 - c if k & 1 else c)
            peers.append((k - 1, p, 4 * p[0] + 2 * p[1] + p[2]))

        def copy(t, k, p, p_idx):
            return (pltpu.make_async_remote_copy(
                        src_ref=ins[t].at[p_idx], dst_ref=outs[t].at[my_idx],
                        send_sem=send_sems.at[7 * t + k], recv_sem=recv_sems.at[7 * t + k],
                        device_id=p, device_id_type=MESH),
                    pltpu.make_async_remote_copy(
                        src_ref=ins[t].at[p_idx], dst_ref=outs[t].at[p_idx],
                        send_sem=send_sems.at[7 * t + k], recv_sem=recv_sems.at[7 * t + k],
                        device_id=p, device_id_type=MESH))

        mine, sends, recvs = [], [], []
        for t in range(n):
            cp = pltpu.make_async_copy(ins[t].at[my_idx], outs[t].at[my_idx], local_sems.at[t])
            cp.start()
            mine.append(cp)
            for k, p, p_idx in peers:
                send, recv = copy(t, k, p, p_idx)
                send.start()
                sends.append(send)
                recvs.append(recv)
        for cp in recvs:
            cp.wait_recv()
        for cp in sends:
            cp.wait_send()
        for cp in mine:
            cp.wait()

    any_spec = pl.BlockSpec(memory_space=pl.ANY)
    return _call(
        body, name=name,
        out_shape=[jax.ShapeDtypeStruct(a.shape, a.dtype) for a in arrs],
        in_specs=[any_spec] * n, out_specs=[any_spec] * n,
        scratch_shapes=[pltpu.SemaphoreType.DMA((7 * n,)), pltpu.SemaphoreType.DMA((7 * n,)),
                        pltpu.SemaphoreType.DMA((n,))],
    )(*arrs)


def _mm(name, a, b, mode, tm, tn, tk, out_dtypes, epi=None, extra=(), blocked_out=False, exact=False):
    if mode == "nn":
        (m, k), n = a.shape, b.shape[1]
    elif mode == "nt":
        (m, k), n = a.shape, b.shape[0]
    else:
        (k, m), n = a.shape, b.shape[1]
    nk = k // tk
    grid = (m // tm, n // tn, nk)
    n_extra, n_out = len(extra), len(out_dtypes)

    def body(*refs):
        a_ref, b_ref = refs[0], refs[1]
        extra_refs = refs[2:2 + n_extra]
        out_refs = refs[2 + n_extra:2 + n_extra + n_out]
        if exact:
            p = lax.dot_general(a_ref[...], b_ref[...], _DIMS[mode], preferred_element_type=F32,
                                precision=lax.Precision.HIGHEST)
        else:
            p = _dot(a_ref[...].astype(BF16), b_ref[...].astype(BF16), _DIMS[mode])

        def finish(acc):
            outs = (acc,) if epi is None else epi(acc, *[r[...] for r in extra_refs])
            for r, o in zip(out_refs, outs):
                r[...] = o.astype(r.dtype)

        if nk == 1:
            finish(p)
        else:
            acc_ref = refs[-1]
            kk = pl.program_id(2)

            @pl.when(kk == 0)
            def _():
                acc_ref[...] = p

            @pl.when(kk > 0)
            def _():
                acc_ref[...] += p

            @pl.when(kk == nk - 1)
            def _():
                finish(acc_ref[...])

    if mode == "tn":
        a_spec = pl.BlockSpec((tk, tm), lambda i, j, kk: (kk, i))
    else:
        a_spec = pl.BlockSpec((tm, tk), lambda i, j, kk: (i, kk))
    if mode == "nt":
        b_spec = pl.BlockSpec((tn, tk), lambda i, j, kk: (j, kk))
    else:
        b_spec = pl.BlockSpec((tk, tn), lambda i, j, kk: (kk, j))
    tile = pl.BlockSpec((tm, tn), lambda i, j, kk: (i, j))
    if blocked_out:
        o_shape, o_spec = (n // tn, m, tn), pl.BlockSpec((None, tm, tn), lambda i, j, kk: (j, i, 0))
    else:
        o_shape, o_spec = (m, n), tile
    out = _call(
        body, name=name, grid=grid,
        in_specs=[a_spec, b_spec] + [tile] * n_extra,
        out_specs=[o_spec] * n_out,
        out_shape=[jax.ShapeDtypeStruct(o_shape, dt) for dt in out_dtypes],
        scratch_shapes=[pltpu.VMEM((tm, tn), F32)] if nk > 1 else [],
        compiler_params=_params("parallel", "parallel", "arbitrary"),
    )(a, b, *extra)
    return out[0] if n_out == 1 else out


def _tile(width, col=0, rows=TM):
    return pl.BlockSpec((rows, width), lambda i: (i, col))


def _vec(width):
    return pl.BlockSpec((1, width), lambda i: (0, 0))


def _rstd(xf):
    return lax.rsqrt(jnp.mean(xf * xf, axis=-1, keepdims=True) + EPS)


def _colsum(v):
    return jnp.sum(v, axis=0, keepdims=True)


def _accumulate(refs, vals):
    first = pl.program_id(0) == 0

    @pl.when(first)
    def _():
        for r, v in zip(refs, vals):
            r[...] = v

    @pl.when(jnp.logical_not(first))
    def _():
        for r, v in zip(refs, vals):
            r[...] += v


def _prenorm_fwd(x, g, sc, sh):
    s, d = x.shape

    def body(x_ref, g_ref, sc_ref, sh_ref, h_ref):
        xf = x_ref[...]
        y = (xf * _rstd(xf)) * g_ref[...]
        h_ref[...] = (y * (1.0 + sc_ref[...]) + sh_ref[...]).astype(h_ref.dtype)

    return _call(body, name="prenorm_fwd", grid=(s // TM,),
                 in_specs=[_tile(d), _vec(d), _vec(d), _vec(d)], out_specs=_tile(d),
                 out_shape=jax.ShapeDtypeStruct((s, d), BF16), compiler_params=_params("parallel"))(x, g, sc, sh)


def _prenorm_bwd(dh, x, g, sc, dres):
    s, d = x.shape

    def body(dh_ref, x_ref, g_ref, sc_ref, dres_ref, dx_ref, dsh_ref, dsc_ref, dg_ref):
        xf, dhf = x_ref[...], dh_ref[...]
        rstd = _rstd(xf)
        xhat = xf * rstd
        one_sc = 1.0 + sc_ref[...]
        dxhat = dhf * (g_ref[...] * one_sc)
        dx_ref[...] = dres_ref[...] + rstd * (dxhat - xhat * jnp.mean(dxhat * xhat, axis=-1, keepdims=True))
        dhx = dhf * xhat
        _accumulate((dsh_ref, dsc_ref, dg_ref), (_colsum(dhf), _colsum(dhx) * g_ref[...], _colsum(dhx) * one_sc))

    vec_out = jax.ShapeDtypeStruct((1, d), F32)
    return _call(body, name="prenorm_bwd", grid=(s // TM,),
                 in_specs=[_tile(d), _tile(d), _vec(d), _vec(d), _tile(d)],
                 out_specs=[_tile(d), _vec(d), _vec(d), _vec(d)],
                 out_shape=[jax.ShapeDtypeStruct((s, d), F32), vec_out, vec_out, vec_out],
                 compiler_params=_params("arbitrary"))(dh, x, g, sc, dres)


def _postnorm_fwd(xres, m, g, gt):
    s, d = m.shape

    def body(x_ref, m_ref, g_ref, gt_ref, o_ref):
        mf = m_ref[...]
        o_ref[...] = x_ref[...] + gt_ref[...] * ((mf * _rstd(mf)) * g_ref[...])

    return _call(body, name="postnorm_fwd", grid=(s // TM,),
                 in_specs=[_tile(d), _tile(d), _vec(d), _vec(d)], out_specs=_tile(d),
                 out_shape=jax.ShapeDtypeStruct((s, d), F32), compiler_params=_params("parallel"))(xres, m, g, gt)


def _postnorm_bwd(dxn, m, g, gt):
    s, d = m.shape

    def body(dx_ref, m_ref, g_ref, gt_ref, dm_ref, dgt_ref, dg_ref):
        mf, dxf = m_ref[...], dx_ref[...]
        rstd = _rstd(mf)
        mhat = mf * rstd
        dmhat = dxf * (gt_ref[...] * g_ref[...])
        dm_ref[...] = (rstd * (dmhat - mhat * jnp.mean(dmhat * mhat, axis=-1, keepdims=True))).astype(dm_ref.dtype)
        dxm = _colsum(dxf * mhat)
        _accumulate((dgt_ref, dg_ref), (dxm * g_ref[...], dxm * gt_ref[...]))

    vec_out = jax.ShapeDtypeStruct((1, d), F32)
    return _call(body, name="postnorm_bwd", grid=(s // TM,),
                 in_specs=[_tile(d), _tile(d), _vec(d), _vec(d)], out_specs=[_tile(d), _vec(d), _vec(d)],
                 out_shape=[jax.ShapeDtypeStruct((s, d), BF16), vec_out, vec_out],
                 compiler_params=_params("arbitrary"))(dxn, m, g, gt)


def _loss(y, target):
    s, d = y.shape

    def body(y_ref, t_ref, dy_ref, sq_ref):
        err = y_ref[...] - t_ref[...]
        dy_ref[...] = err * (1.0 / d)
        tot = jnp.sum(_colsum(err * err), axis=1, keepdims=True)
        _accumulate((sq_ref,), (jnp.broadcast_to(tot, (1, LANES)),))

    return _call(body, name="loss", grid=(s // TM,), in_specs=[_tile(d), _tile(d)],
                 out_specs=[_tile(d), _vec(LANES)],
                 out_shape=[jax.ShapeDtypeStruct((s, d), F32), jax.ShapeDtypeStruct((1, LANES), F32)],
                 compiler_params=_params("arbitrary"))(y, target)


def _sigmoid(v):
    return 1.0 / (1.0 + jnp.exp(-v))


def _gate_fwd(proj, y_conv, y_attn):
    s, d = y_conv.shape
    ga_col, gb_col = (IN_COLS - 2 * d) // d, (IN_COLS - d) // d

    def body(ga_ref, gb_ref, yc_ref, ya_ref, o_ref):
        o_ref[...] = (_sigmoid(ga_ref[...]) * yc_ref[...] + _sigmoid(gb_ref[...]) * ya_ref[...]).astype(o_ref.dtype)

    return _call(body, name="gate_fwd", grid=(s // TM,),
                 in_specs=[_tile(d, ga_col), _tile(d, gb_col), _tile(d), _tile(d)], out_specs=_tile(d),
                 out_shape=jax.ShapeDtypeStruct((s, d), BF16),
                 compiler_params=_params("parallel"))(proj, proj, y_conv, y_attn)


def _gate_bwd(dmerged, proj, y_conv, y_attn):
    s, d = y_conv.shape
    ga_col, gb_col = (IN_COLS - 2 * d) // d, (IN_COLS - d) // d

    def body(dm_ref, ga_ref, gb_ref, yc_ref, ya_ref, dyc_ref, dya_ref, dga_ref, dgb_ref):
        dm = dm_ref[...]
        sa, sb = _sigmoid(ga_ref[...]), _sigmoid(gb_ref[...])
        dyc_ref[...] = (dm * sa).astype(BF16)
        dya_ref[...] = (dm * sb).astype(BF16)
        dga_ref[...] = (dm * yc_ref[...] * (sa * (1.0 - sa))).astype(BF16)
        dgb_ref[...] = (dm * ya_ref[...] * (sb * (1.0 - sb))).astype(BF16)

    out = jax.ShapeDtypeStruct((s, d), BF16)
    return _call(body, name="gate_bwd", grid=(s // TM,),
                 in_specs=[_tile(d), _tile(d, ga_col), _tile(d, gb_col), _tile(d), _tile(d)],
                 out_specs=[_tile(d)] * 4, out_shape=[out] * 4,
                 compiler_params=_params("parallel"))(dmerged, proj, proj, y_conv, y_attn)


def _shift_down(prev8, cur, by):
    ext = jnp.concatenate([prev8, cur], axis=0)
    return pltpu.roll(ext, by, 0)[8:]


def _shift_up(cur, next8, by):
    ext = jnp.concatenate([cur, next8], axis=0)
    return pltpu.roll(ext, ext.shape[0] - by, 0)[:cur.shape[0]]


def _conv_fwd(proj, conv_w):
    s, w = proj.shape[0], CONV_WIDTH
    per8 = TM // 8

    def prev(col):
        return pl.BlockSpec((8, w), lambda i: (jnp.maximum(i * per8 - 1, 0), col))

    def body(bg_ref, cg_ref, u_ref, cgp_ref, up_ref, w_ref, o_ref):
        vv = cg_ref[...] * u_ref[...]
        pv = cgp_ref[...] * up_ref[...] * jnp.where(pl.program_id(0) > 0, 1.0, 0.0)
        y = w_ref[0:1, :] * _shift_down(pv, vv, 2) + w_ref[1:2, :] * _shift_down(pv, vv, 1) + w_ref[2:3, :] * vv
        o_ref[...] = (bg_ref[...] * y).astype(o_ref.dtype)

    return _call(body, name="conv_fwd", grid=(s // TM,),
                 in_specs=[_tile(w, 0), _tile(w, 1), _tile(w, 2), prev(1), prev(2),
                           pl.BlockSpec((3, w), lambda i: (0, 0))],
                 out_specs=_tile(w), out_shape=jax.ShapeDtypeStruct((s, w), BF16),
                 compiler_params=_params("parallel"))(proj, proj, proj, proj, proj, conv_w)


def _conv_bwd(dyc, proj, conv_w):
    s, w = proj.shape[0], CONV_WIDTH
    per8 = TM // 8
    n_tiles = s // TM

    def prev(col):
        return pl.BlockSpec((8, w), lambda i: (jnp.maximum(i * per8 - 1, 0), col))

    def nxt(col):
        return pl.BlockSpec((8, w), lambda i: (jnp.minimum((i + 1) * per8, s // 8 - 1), col))

    def body(dyc_ref, bg_ref, cg_ref, u_ref, cgp_ref, up_ref, dycn_ref, bgn_ref, w_ref,
             dbg_ref, dcg_ref, du_ref, dw0_ref, dw1_ref, dw2_ref):
        i = pl.program_id(0)
        cg, u = cg_ref[...], u_ref[...]
        vv = cg * u
        pv = cgp_ref[...] * up_ref[...] * jnp.where(i > 0, 1.0, 0.0)
        v1, v2 = _shift_down(pv, vv, 1), _shift_down(pv, vv, 2)
        w0, w1, w2 = w_ref[0:1, :], w_ref[1:2, :], w_ref[2:3, :]
        dyc_t = dyc_ref[...]
        dbg_ref[...] = (dyc_t * (w0 * v2 + w1 * v1 + w2 * vv)).astype(BF16)
        dy = dyc_t * bg_ref[...]
        dyn = dycn_ref[...] * bgn_ref[...] * jnp.where(i < n_tiles - 1, 1.0, 0.0)
        dvv = w2 * dy + w1 * _shift_up(dy, dyn, 1) + w0 * _shift_up(dy, dyn, 2)
        dcg_ref[...] = (dvv * u).astype(BF16)
        du_ref[...] = (dvv * cg).astype(BF16)
        _accumulate((dw0_ref, dw1_ref, dw2_ref), (_colsum(dy * v2), _colsum(dy * v1), _colsum(dy * vv)))

    act = jax.ShapeDtypeStruct((s, w), BF16)
    tap = jax.ShapeDtypeStruct((1, w), F32)
    return _call(body, name="conv_bwd", grid=(n_tiles,),
                 in_specs=[_tile(w), _tile(w, 0), _tile(w, 1), _tile(w, 2), prev(1), prev(2), nxt(0), nxt(0),
                           pl.BlockSpec((3, w), lambda i: (0, 0))],
                 out_specs=[_tile(w)] * 3 + [_vec(w)] * 3, out_shape=[act] * 3 + [tap] * 3,
                 compiler_params=_params("arbitrary"))(dyc, proj, proj, proj, proj, proj, dyc, proj, conv_w)


Q_COL = 3 * CONV_WIDTH // LANES
K_COL = Q_COL + ATTN_WIDTH // LANES
V_COL = K_COL + ATTN_WIDTH // LANES
SCALE = HEAD_DIM ** -0.5


def _head_lanes(hh):
    lane = lax.broadcasted_iota(jnp.int32, (1, LANES), 1)
    return jnp.where((lane >= hh * HEAD_DIM) & (lane < (hh + 1) * HEAD_DIM), 1.0, 0.0)


def _tri(width, keep):
    j = lax.broadcasted_iota(jnp.int32, (TK, width), 0)
    s = lax.broadcasted_iota(jnp.int32, (TK, width), 1)
    return jnp.where((s >= TK) | keep(j, s), 1.0, 0.0).astype(BF16)


def _split_dot(v, tri):
    hi = v.astype(BF16)
    lo = (v - hi.astype(F32)).astype(BF16)
    return _dot(hi, tri) + _dot(lo, tri)


def _logits(qm, kblk, ks, tpos):
    z = _dot(qm, kblk, NT)
    mask = (ks + lax.broadcasted_iota(jnp.int32, (1, TK), 1)) < tpos
    e = jnp.exp(-jnp.abs(z))
    l0 = jnp.where(mask, -(jnp.maximum(z, 0.0) + jnp.log(1.0 + e)), 0.0)
    return z, mask, e, l0


def _attn_fwd(proj):
    s = proj.shape[0]
    nq = s // TQ
    assert s // TK <= LANES

    def body(q_ref, k_ref, v_ref, o_ref, rs_ref, r_scr, rall_scr, acc_scr):
        qi = pl.program_id(1)
        lane = lax.broadcasted_iota(jnp.int32, (TQ, LANES), 1)
        tri = _tri(2 * TK, lambda j, ss: j > ss)
        tpos = qi * TQ + lax.broadcasted_iota(jnp.int32, (TQ, 1), 0)
        nkb = (qi + 1) * (TQ // TK)
        acc_scr[...] = jnp.zeros_like(acc_scr)
        for hh in range(2):
            lm = _head_lanes(hh)
            qm = (q_ref[...] * (SCALE * lm)).astype(BF16)
            r_scr[...] = jnp.zeros_like(r_scr)
            rall_scr[...] = jnp.zeros_like(rall_scr)

            def step(it, carry):
                kb = nkb - 1 - it
                ks = pl.multiple_of(kb * TK, TK)
                kblk = k_ref[pl.ds(ks, TK), :].astype(BF16)
                vblk = (v_ref[pl.ds(ks, TK), :] * lm).astype(BF16)
                z, mask, _, l0 = _logits(qm, kblk, ks, tpos)
                cs = _split_dot(l0, tri)
                near = r_scr[...]
                a = jnp.where(mask, jnp.exp(l0 + z + cs[:, :TK] + near), 0.0)
                acc_scr[...] += _dot(a.astype(BF16), vblk)
                rall_scr[...] = jnp.where(lane == kb, near, rall_scr[...])
                r_scr[...] = near + cs[:, TK:]
                return carry

            lax.fori_loop(0, nkb, step, 0)
            rs_ref[hh] = rall_scr[...]
        o_ref[...] = acc_scr[...]

    def col(c0):
        return pl.BlockSpec((TQ, LANES), lambda p, qi: (qi, c0 + p)), pl.BlockSpec((s, LANES), lambda p, qi: (0, c0 + p))

    return _call(
        body, name="attn_fwd", grid=(N_HEADS // 2, nq),
        in_specs=[col(Q_COL)[0], col(K_COL)[1], col(V_COL)[1]],
        out_specs=[pl.BlockSpec((TQ, LANES), lambda p, qi: (qi, p)),
                   pl.BlockSpec((2, TQ, LANES), lambda p, qi: (p, qi, 0))],
        out_shape=[jax.ShapeDtypeStruct((s, ATTN_WIDTH), F32), jax.ShapeDtypeStruct((N_HEADS, s, LANES), F32)],
        scratch_shapes=[pltpu.VMEM((TQ, LANES), F32)] * 3,
        compiler_params=_params("parallel", "arbitrary"),
    )(proj, proj, proj)


def _attn_bwd(proj, do, rsave):
    s = proj.shape[0]
    nq = s // TQ

    def body(q_ref, k_ref, v_ref, do_ref, rs_ref, dq_ref, dk_ref, dv_ref, pg_scr, dq_scr, dk_scr, dv_scr):
        qi = pl.program_id(1)

        @pl.when(qi == 0)
        def _():
            dk_scr[...] = jnp.zeros_like(dk_scr)
            dv_scr[...] = jnp.zeros_like(dv_scr)

        lane = lax.broadcasted_iota(jnp.int32, (TQ, LANES), 1)
        tri_after = _tri(TK, lambda j, ss: j > ss)
        tri_before = _tri(2 * TK, lambda j, ss: j < ss)
        tpos = qi * TQ + lax.broadcasted_iota(jnp.int32, (TQ, 1), 0)
        nkb = (qi + 1) * (TQ // TK)
        dq_scr[...] = jnp.zeros_like(dq_scr)
        for hh in range(2):
            lm = _head_lanes(hh)
            qm = (q_ref[...] * (SCALE * lm)).astype(BF16)
            dom = (do_ref[...] * lm).astype(BF16)
            pg_scr[...] = jnp.zeros_like(pg_scr)

            def step(kb, carry):
                ks = pl.multiple_of(kb * TK, TK)
                kblk = k_ref[pl.ds(ks, TK), :].astype(BF16)
                vblk = v_ref[pl.ds(ks, TK), :].astype(BF16)
                z, mask, e, l0 = _logits(qm, kblk, ks, tpos)
                near = jnp.sum(jnp.where(lane == kb, rs_ref[hh], 0.0), axis=1, keepdims=True)
                a = jnp.where(mask, jnp.exp(l0 + z + _split_dot(l0, tri_after) + near), 0.0)
                g = a * _dot(dom, vblk, NT)
                cg = _split_dot(g, tri_before)
                before = cg[:, :TK] + pg_scr[...]
                rinv = 1.0 / (1.0 + e)
                er = e * rinv
                pos = z >= 0.0
                beta, one_minus_beta = jnp.where(pos, rinv, er), jnp.where(pos, er, rinv)
                dz = jnp.where(mask, g * one_minus_beta - beta * before, 0.0).astype(BF16)
                dq_scr[...] += _dot(dz, kblk) * lm
                dk_scr[pl.ds(ks, TK), :] += _dot(dz, qm, TN)
                dv_scr[pl.ds(ks, TK), :] += _dot(a.astype(BF16), dom, TN)
                pg_scr[...] += cg[:, TK:]
                return carry

            lax.fori_loop(0, nkb, step, 0)
        dq_ref[...] = (dq_scr[...] * SCALE).astype(dq_ref.dtype)

        @pl.when(qi == nq - 1)
        def _():
            dk_ref[...] = dk_scr[...].astype(dk_ref.dtype)
            dv_ref[...] = dv_scr[...].astype(dv_ref.dtype)

    def rows(c0):
        return pl.BlockSpec((TQ, LANES), lambda p, qi: (qi, c0 + p))

    def whole(c0):
        return pl.BlockSpec((s, LANES), lambda p, qi: (0, c0 + p))

    out = jax.ShapeDtypeStruct((s, ATTN_WIDTH), BF16)
    return _call(
        body, name="attn_bwd", grid=(N_HEADS // 2, nq),
        in_specs=[rows(Q_COL), whole(K_COL), whole(V_COL), rows(0),
                  pl.BlockSpec((2, TQ, LANES), lambda p, qi: (p, qi, 0))],
        out_specs=[rows(0), whole(0), whole(0)], out_shape=[out] * 3,
        scratch_shapes=[pltpu.VMEM((TQ, LANES), F32), pltpu.VMEM((TQ, LANES), F32),
                        pltpu.VMEM((s, LANES), F32), pltpu.VMEM((s, LANES), F32)],
        compiler_params=_params("arbitrary", "arbitrary"),
    )(proj, proj, proj, do, rsave)


def _sum_adamw(name, parts, w, m, v):
    n, r, c = parts.shape
    tr = r if r <= 256 else 256

    def body(p_ref, w_ref, m_ref, v_ref, g_ref, d_ref, nm_ref, nv_ref):
        g = p_ref[0].astype(F32)
        for j in range(1, n):
            g = g + p_ref[j].astype(F32)
        nm = ADAM_B1 * m_ref[...] + (1.0 - ADAM_B1) * g
        nv = ADAM_B2 * v_ref[...] + (1.0 - ADAM_B2) * (g * g)
        m_hat = nm / (1.0 - ADAM_B1 ** ADAM_STEP)
        v_hat = nv / (1.0 - ADAM_B2 ** ADAM_STEP)
        g_ref[...] = g
        d_ref[...] = -ADAM_LR * (m_hat / (jnp.sqrt(v_hat) + ADAM_EPS) + ADAM_WD * w_ref[...])
        nm_ref[...] = nm
        nv_ref[...] = nv

    mat = pl.BlockSpec((tr, c), lambda i: (i, 0))
    out = jax.ShapeDtypeStruct((r, c), F32)
    return _call(body, name=name, grid=(r // tr,),
                 in_specs=[pl.BlockSpec((n, tr, c), lambda i: (0, i, 0)), mat, mat, mat],
                 out_specs=[mat] * 4, out_shape=[out] * 4, compiler_params=_params("parallel"))(parts, w, m, v)


def _natural(gathered):
    _, k, n = gathered.shape
    return gathered.transpose(1, 0, 2).reshape(k, N_DEV * n)


def _relu2_epi(acc):
    r = jnp.maximum(acc, 0.0)
    return acc, r * r


def _relu2_bwd_epi(acc, a_act):
    return (acc * (2.0 * jnp.maximum(a_act, 0.0)),)


def kernel(x, c, w_ada, b_ada, g_pre_mix, g_post_mix, g_pre_mlp, g_post_mlp, w_in, conv_w, w_proj_conv, w_proj_attn, w_out, w_mlp_in, w_mlp_out, loss_target, m_w_ada, m_b_ada, m_g_pre_mix, m_g_post_mix, m_g_pre_mlp, m_g_post_mlp, m_w_in, m_conv_w, m_w_proj_conv, m_w_proj_attn, m_w_out, m_w_mlp_in, m_w_mlp_out, v_w_ada, v_b_ada, v_g_pre_mix, v_g_post_mix, v_g_pre_mlp, v_g_post_mlp, v_w_in, v_conv_w, v_w_proj_conv, v_w_proj_attn, v_w_out, v_w_mlp_in, v_w_mlp_out):
    xi, yi, ci = _mesh_pos()
    me = 4 * xi + 2 * yi + ci
    d = D_MODEL
    x0 = x[0]
    seq = x0.shape[0]
    ada_cols = w_ada.shape[2]
    conv_cols = conv_w.shape[2]

    small = jnp.concatenate([c.reshape(-1), conv_w.reshape(-1)])
    small = jnp.pad(small, (0, 2 * d - small.shape[0])).reshape(8, 2 * d // 8)
    small_all = _all_gather("gather_c", [small])[0].reshape(N_DEV, 2 * d)
    c_all = small_all[:, :d]
    conv_all = small_all[:, d:d + DEPTH * 3 * conv_cols].reshape(N_DEV, DEPTH, 3, conv_cols)
    conv_all = conv_all.transpose(1, 2, 0, 3).reshape(DEPTH, 3, N_DEV * conv_cols)
    mod_cols = jnp.stack([_mm("mod_mm", c_all, w_ada[l], "nn", N_DEV, ada_cols, d, [F32], exact=True)
                          for l in range(DEPTH)], axis=1)
    mod_all = _all_gather("gather_mod", [mod_cols.reshape(N_DEV, DEPTH * ada_cols)])[0]
    mod_mine = lax.dynamic_index_in_dim(mod_all, me, axis=1, keepdims=False).reshape(N_DEV, DEPTH, ada_cols)
    mod = mod_mine.transpose(1, 0, 2).reshape(DEPTH, N_MOD * d) + b_ada

    def gathered_weights(l):
        shards = [w_in[l], w_proj_conv[l], w_proj_attn[l], w_out[l], w_mlp_in[l], w_mlp_out[l]]
        g_in, g_pc, g_pa, g_out, g_mi, g_mo = _all_gather("gather_w", [w.astype(BF16) for w in shards])
        return (_natural(g_in), _natural(g_pc), _natural(g_pa), g_out.reshape(d, d), _natural(g_mi),
                g_mo.reshape(D_FF, d))

    weights = [gathered_weights(l) for l in range(DEPTH)]

    saved = []
    xl = x0
    for l in range(DEPTH):
        wg_in, wg_pc, wg_pa, wg_out, wg_mi, wg_mo = weights[l]
        sh1, sc1, gt1, sh2, sc2, gt2 = [mod[l:l + 1, i * d:(i + 1) * d] for i in range(N_MOD)]
        h = _prenorm_fwd(xl, g_pre_mix[l:l + 1], sc1, sh1)
        proj = _mm("proj", h, wg_in, "nn", TM, 1024, d, [F32])
        yc = _conv_fwd(proj, conv_all[l])
        y_conv = _mm("proj_conv", yc, wg_pc, "nn", TM, d, CONV_WIDTH, [F32])
        o, rsave = _attn_fwd(proj)
        y_attn = _mm("proj_attn", o, wg_pa, "nn", TM, d, ATTN_WIDTH, [F32])
        merged = _gate_fwd(proj, y_conv, y_attn)
        mix_out = _mm("mix_out", merged, wg_out, "nn", TM, d, d, [F32])
        x1 = _postnorm_fwd(xl, mix_out, g_post_mix[l:l + 1], gt1)
        h2 = _prenorm_fwd(x1, g_pre_mlp[l:l + 1], sc2, sh2)
        a_act, r = _mm("mlp_in", h2, wg_mi, "nn", TM, 1024, d, [F32, BF16], epi=_relu2_epi)
        ff = _mm("mlp_out", r, wg_mo, "nn", TM, d, 1024, [F32])
        x2 = _postnorm_fwd(x1, ff, g_post_mlp[l:l + 1], gt2)
        saved.append((xl, h, proj, yc, o, rsave, y_conv, y_attn, merged, mix_out, x1, h2, a_act, r, ff))
        xl = x2

    dxo, sq = _loss(xl, loss_target[0])
    loss = lax.psum(sq[0, 0] * (0.5 / d), ("x", "y", "c"))

    big = {}
    dmod, small_grads = [None] * DEPTH, [None] * DEPTH
    for l in reversed(range(DEPTH)):
        wg_in, wg_pc, wg_pa, wg_out, wg_mi, wg_mo = weights[l]
        xin, h, proj, yc, o, rsave, y_conv, y_attn, merged, mix_out, x1, h2, a_act, r, ff = saved[l]
        sh1, sc1, gt1, sh2, sc2, gt2 = [mod[l:l + 1, i * d:(i + 1) * d] for i in range(N_MOD)]

        dff, dgt2, dg_post_mlp = _postnorm_bwd(dxo, ff, g_post_mlp[l:l + 1], gt2)
        da = _mm("d_relu2", dff, wg_mo, "nt", TM, 1024, d, [BF16], epi=_relu2_bwd_epi, extra=(a_act,))
        gw_mo = _mm("gw_mlp_out", r, dff, "tn", 1024, d, 1024, [BF16])
        dh2 = _mm("d_h2", da, wg_mi, "nt", TM, d, 1024, [F32])
        gw_mi = _mm("gw_mlp_in", h2, da, "tn", d, D_FF // N_DEV, 1024, [BF16], blocked_out=True)
        dx1, dsh2, dsc2, dg_pre_mlp = _prenorm_bwd(dh2, x1, g_pre_mlp[l:l + 1], sc2, dxo)

        dmix, dgt1, dg_post_mix = _postnorm_bwd(dx1, mix_out, g_post_mix[l:l + 1], gt1)
        dmerged = _mm("d_merged", dmix, wg_out, "nt", TM, d, d, [F32])
        gw_out = _mm("gw_out", merged, dmix, "tn", d, d, 1024, [BF16])
        dy_conv, dy_attn, dga, dgb = _gate_bwd(dmerged, proj, y_conv, y_attn)
        do = _mm("d_o", dy_attn, wg_pa, "nt", TM, ATTN_WIDTH, d, [F32])
        gw_pa = _mm("gw_proj_attn", o, dy_attn, "tn", ATTN_WIDTH, d, 1024, [BF16])
        dyc = _mm("d_yc", dy_conv, wg_pc, "nt", TM, CONV_WIDTH, d, [F32])
        gw_pc = _mm("gw_proj_conv", yc, dy_conv, "tn", CONV_WIDTH, d, 1024, [BF16])
        dq, dk, dv = _attn_bwd(proj, do, rsave)
        dbg, dcg, du, dw0, dw1, dw2 = _conv_bwd(dyc, proj, conv_all[l])
        dproj = jnp.concatenate([dbg, dcg, du, dq, dk, dv, dga, dgb], axis=1)
        dh = _mm("d_h", dproj, wg_in, "nt", TM, d, 1024, [F32])
        gw_in = _mm("gw_in", h, dproj, "tn", d, IN_COLS // N_DEV, 1024, [BF16], blocked_out=True)
        dxo, dsh1, dsc1, dg_pre_mix = _prenorm_bwd(dh, xin, g_pre_mix[l:l + 1], sc1, dx1)

        dmod[l] = jnp.concatenate([dsh1, dsc1, dgt1, dsh2, dsc2, dgt2], axis=1)
        small_grads[l] = (dg_pre_mix, dg_post_mix, dg_pre_mlp, dg_post_mlp, jnp.concatenate([dw0, dw1, dw2], axis=0))

        def col_blocks(gw):
            k, n = gw.shape
            return gw.reshape(k, N_DEV, n // N_DEV).transpose(1, 0, 2)

        sent = [gw_in, col_blocks(gw_pc), col_blocks(gw_pa), gw_out.reshape(N_DEV, d // N_DEV, d), gw_mi,
                gw_mo.reshape(N_DEV, D_FF // N_DEV, d)]
        parts = _exchange("exchange_gw", sent)
        names = ["w_in", "w_proj_conv", "w_proj_attn", "w_out", "w_mlp_in", "w_mlp_out"]
        olds = [(w_in, m_w_in, v_w_in), (w_proj_conv, m_w_proj_conv, v_w_proj_conv),
                (w_proj_attn, m_w_proj_attn, v_w_proj_attn), (w_out, m_w_out, v_w_out),
                (w_mlp_in, m_w_mlp_in, v_w_mlp_in), (w_mlp_out, m_w_mlp_out, v_w_mlp_out)]
        for nm, p, (w_, m_, v_) in zip(names, parts, olds):
            big[(nm, l)] = _sum_adamw("adamw_" + nm, p, w_[l], m_[l], v_[l])

    vec = jnp.concatenate(
        [dmod[l].reshape(-1) for l in range(DEPTH)]
        + [small_grads[l][i].reshape(-1) for i in range(4) for l in range(DEPTH)]
        + [small_grads[l][4].reshape(-1) for l in range(DEPTH)])
    n_vec = vec.shape[0]
    vec_all = _all_gather("gather_small", [vec.reshape(8, n_vec // 8)])[0].reshape(N_DEV, n_vec)
    n_mod = DEPTH * N_MOD * d
    dmod_all = vec_all[:, :n_mod].reshape(N_DEV, DEPTH, N_MOD * d)
    res = {}
    res["b_ada"] = _sum_adamw("adamw_b_ada", dmod_all, b_ada, m_b_ada, v_b_ada)
    off = n_mod
    for nm, (w_, m_, v_) in zip(
            ["g_pre_mix", "g_post_mix", "g_pre_mlp", "g_post_mlp"],
            [(g_pre_mix, m_g_pre_mix, v_g_pre_mix), (g_post_mix, m_g_post_mix, v_g_post_mix),
             (g_pre_mlp, m_g_pre_mlp, v_g_pre_mlp), (g_post_mlp, m_g_post_mlp, v_g_post_mlp)]):
        res[nm] = _sum_adamw("adamw_gain", vec_all[:, off:off + DEPTH * d].reshape(N_DEV, DEPTH, d), w_, m_, v_)
        off += DEPTH * d
    dconv_all = vec_all[:, off:].reshape(N_DEV, DEPTH * 3, CONV_WIDTH)
    dconv_mine = lax.dynamic_slice_in_dim(dconv_all, me * conv_cols, conv_cols, axis=2)
    res["conv_w"] = [t.reshape(DEPTH, 3, conv_cols) for t in _sum_adamw(
        "adamw_conv_w", dconv_mine, conv_w.reshape(DEPTH * 3, conv_cols), m_conv_w.reshape(DEPTH * 3, conv_cols),
        v_conv_w.reshape(DEPTH * 3, conv_cols))]

    c_t = jnp.pad(c_all.T, ((0, 0), (0, LANES - N_DEV)))
    dmod_mine = lax.dynamic_slice_in_dim(dmod_all, me * ada_cols, ada_cols, axis=2)
    ada = []
    for l in range(DEPTH):
        dm_l = jnp.pad(dmod_mine[:, l, :], ((0, LANES - N_DEV), (0, 0)))
        gw_ada = _mm("gw_ada", c_t, dm_l, "nn", 256, ada_cols, LANES, [F32], exact=True)
        ada.append(_sum_adamw("adamw_w_ada", gw_ada[None], w_ada[l], m_w_ada[l], v_w_ada[l]))
    res["w_ada"] = [jnp.stack([ada[l][i] for l in range(DEPTH)]) for i in range(4)]
    for nm in ["w_in", "w_proj_conv", "w_proj_attn", "w_out", "w_mlp_in", "w_mlp_out"]:
        res[nm] = [jnp.stack([big[(nm, l)][i] for l in range(DEPTH)]) for i in range(4)]

    order = ["w_ada", "b_ada", "g_pre_mix", "g_post_mix", "g_pre_mlp", "g_post_mlp", "w_in", "conv_w",
             "w_proj_conv", "w_proj_attn", "w_out", "w_mlp_in", "w_mlp_out"]
    outs = [loss, dxo[None]]
    for i in range(4):
        outs += [res[nm][i] for nm in order]
    return tuple(outs)
```

```python
import jax
import jax.numpy as jnp
from jax import lax
from jax.experimental import pallas as pl
from jax.experimental.pallas import tpu as pltpu

F32 = jnp.float32
BF16 = jnp.bfloat16
MESH = pl.DeviceIdType.MESH

N_DEV = 8
D_MODEL = 1024
CONV_WIDTH = 512
N_HEADS = 8
HEAD_DIM = 64
ATTN_WIDTH = N_HEADS * HEAD_DIM
D_FF = 4 * D_MODEL
N_MOD = 6
DEPTH = 2
EPS = 1e-6
IN_COLS = 3 * CONV_WIDTH + 3 * ATTN_WIDTH + 2 * D_MODEL
LANES = 128

ADAM_LR = 0.001
ADAM_B1 = 0.9
ADAM_B2 = 0.999
ADAM_EPS = 1e-08
ADAM_WD = 0.01
ADAM_STEP = 10

TM = 512
TQ = 512
TK = 128
CH = 64
VMEM_LIMIT = 56 * 1024 * 1024

NN = (((1,), (0,)), ((), ()))
NT = (((1,), (1,)), ((), ()))
TN = (((0,), (0,)), ((), ()))
_DIMS = {"nn": NN, "nt": NT, "tn": TN}


def _call(body, **kw):
    return pl.pallas_call(body, **kw)


def _params(*sem):
    return pltpu.CompilerParams(dimension_semantics=sem, vmem_limit_bytes=VMEM_LIMIT)


def _dot(a, b, dims=NN):
    return lax.dot_general(a, b, dims, preferred_element_type=F32)


def _mesh_pos():
    return lax.axis_index("x"), lax.axis_index("y"), lax.axis_index("c")


def _all_gather(name, arrs):
    n = len(arrs)

    def body(*refs):
        ins, outs = refs[:n], refs[n:2 * n]
        send_sems, recv_sems, local_sems = refs[2 * n:]
        x, y, c = _mesh_pos()
        me, sibling = (x, y, c), (x, y, 1 - c)
        chips = [(1 - x, y), (x, 1 - y), (1 - x, 1 - y)]

        def blk(t, p):
            return outs[t].at[4 * p[0] + 2 * p[1] + p[2]]

        def copy(t, k, block, to, src=None):
            return pltpu.make_async_remote_copy(
                src_ref=blk(t, block) if src is None else src, dst_ref=blk(t, block),
                send_sem=send_sems.at[7 * t + k], recv_sem=recv_sems.at[7 * t + k],
                device_id=to, device_id_type=MESH)

        mine, first, passed = [], [], []
        for t in range(n):
            cp = pltpu.make_async_copy(ins[t], blk(t, me), local_sems.at[t])
            cp.start()
            mine.append(cp)
            cps = [copy(t, 0, me, sibling, src=ins[t])]
            cps += [copy(t, 1 + j, me, (*chip, c), src=ins[t]) for j, chip in enumerate(chips)]
            for cp in cps:
                cp.start()
            first += cps
        for t in range(n):
            for j, chip in enumerate(chips):
                copy(t, 1 + j, (*chip, c), me).wait_recv()
                cp = copy(t, 4 + j, (*chip, c), sibling)
                cp.start()
                passed.append(cp)
        for t in range(n):
            copy(t, 0, sibling, me).wait_recv()
            for j, chip in enumerate(chips):
                copy(t, 4 + j, (*chip, 1 - c), me).wait_recv()
        for cp in first + passed:
            cp.wait_send()
        for cp in mine:
            cp.wait()

    any_spec = pl.BlockSpec(memory_space=pl.ANY)
    return _call(
        body, name=name,
        out_shape=[jax.ShapeDtypeStruct((N_DEV,) + a.shape, a.dtype) for a in arrs],
        in_specs=[any_spec] * n, out_specs=[any_spec] * n,
        scratch_shapes=[pltpu.SemaphoreType.DMA((7 * n,)), pltpu.SemaphoreType.DMA((7 * n,)),
                        pltpu.SemaphoreType.DMA((n,))],
    )(*arrs)


def _exchange(name, arrs):
    n = len(arrs)

    def body(*refs):
        ins, outs = refs[:n], refs[n:2 * n]
        send_sems, recv_sems, local_sems = refs[2 * n:]
        x, y, c = _mesh_pos()
        my_idx = 4 * x + 2 * y + c
        peers = []
        for k in range(1, N_DEV):
            p = (1 - x if k & 4 else x, 1 - y if k & 2 else y, 1 - c if k & 1 else c)
            peers.append((k - 1, p, 4 * p[0] + 2 * p[1] + p[2]))

        def copy(t, k, p, p_idx):
            return (pltpu.make_async_remote_copy(
                        src_ref=ins[t].at[p_idx], dst_ref=outs[t].at[my_idx],
                        send_sem=send_sems.at[7 * t + k], recv_sem=recv_sems.at[7 * t + k],
                        device_id=p, device_id_type=MESH),
                    pltpu.make_async_remote_copy(
                        src_ref=ins[t].at[p_idx], dst_ref=outs[t].at[p_idx],
                        send_sem=send_sems.at[7 * t + k], recv_sem=recv_sems.at[7 * t + k],
                        device_id=p, device_id_type=MESH))

        mine, sends, recvs = [], [], []
        for t in range(n):
            cp = pltpu.make_async_copy(ins[t].at[my_idx], outs[t].at[my_idx], local_sems.at[t])
            cp.start()
            mine.append(cp)
            for k, p, p_idx in peers:
                send, recv = copy(t, k, p, p_idx)
                send.start()
                sends.append(send)
                recvs.append(recv)
        for cp in recvs:
            cp.wait_recv()
        for cp in sends:
            cp.wait_send()
        for cp in mine:
            cp.wait()

    any_spec = pl.BlockSpec(memory_space=pl.ANY)
    return _call(
        body, name=name,
        out_shape=[jax.ShapeDtypeStruct(a.shape, a.dtype) for a in arrs],
        in_specs=[any_spec] * n, out_specs=[any_spec] * n,
        scratch_shapes=[pltpu.SemaphoreType.DMA((7 * n,)), pltpu.SemaphoreType.DMA((7 * n,)),
                        pltpu.SemaphoreType.DMA((n,))],
    )(*arrs)


def _mm(name, a, b, mode, tm, tn, tk, out_dtypes, epi=None, extra=(), blocked_out=False, exact=False):
    if mode == "nn":
        (m, k), n = a.shape, b.shape[1]
    elif mode == "nt":
        (m, k), n = a.shape, b.shape[0]
    else:
        (k, m), n = a.shape, b.shape[1]
    nk = k // tk
    grid = (m // tm, n // tn, nk)
    n_extra, n_out = len(extra), len(out_dtypes)

    def body(*refs):
        a_ref, b_ref = refs[0], refs[1]
        extra_refs = refs[2:2 + n_extra]
        out_refs = refs[2 + n_extra:2 + n_extra + n_out]
        if exact:
            p = lax.dot_general(a_ref[...], b_ref[...], _DIMS[mode], preferred_element_type=F32,
                                precision=lax.Precision.HIGHEST)
        else:
            p = _dot(a_ref[...].astype(BF16), b_ref[...].astype(BF16), _DIMS[mode])

        def finish(acc):
            outs = (acc,) if epi is None else epi(acc, *[r[...] for r in extra_refs])
            for r, o in zip(out_refs, outs):
                r[...] = o.astype(r.dtype)

        if nk == 1:
            finish(p)
        else:
            acc_ref = refs[-1]
            kk = pl.program_id(2)

            @pl.when(kk == 0)
            def _():
                acc_ref[...] = p

            @pl.when(kk > 0)
            def _():
                acc_ref[...] += p

            @pl.when(kk == nk - 1)
            def _():
                finish(acc_ref[...])

    if mode == "tn":
        a_spec = pl.BlockSpec((tk, tm), lambda i, j, kk: (kk, i))
    else:
        a_spec = pl.BlockSpec((tm, tk), lambda i, j, kk: (i, kk))
    if mode == "nt":
        b_spec = pl.BlockSpec((tn, tk), lambda i, j, kk: (j, kk))
    else:
        b_spec = pl.BlockSpec((tk, tn), lambda i, j, kk: (kk, j))
    tile = pl.BlockSpec((tm, tn), lambda i, j, kk: (i, j))
    if blocked_out:
        o_shape, o_spec = (n // tn, m, tn), pl.BlockSpec((None, tm, tn), lambda i, j, kk: (j, i, 0))
    else:
        o_shape, o_spec = (m, n), tile
    out = _call(
        body, name=name, grid=grid,
        in_specs=[a_spec, b_spec] + [tile] * n_extra,
        out_specs=[o_spec] * n_out,
        out_shape=[jax.ShapeDtypeStruct(o_shape, dt) for dt in out_dtypes],
        scratch_shapes=[pltpu.VMEM((tm, tn), F32)] if nk > 1 else [],
        compiler_params=_params("parallel", "parallel", "arbitrary"),
    )(a, b, *extra)
    return out[0] if n_out == 1 else out


def _tile(width, col=0, rows=TM):
    return pl.BlockSpec((rows, width), lambda i: (i, col))


def _vec(width):
    return pl.BlockSpec((1, width), lambda i: (0, 0))


def _rstd(xf):
    return lax.rsqrt(jnp.mean(xf * xf, axis=-1, keepdims=True) + EPS)


def _colsum(v):
    return jnp.sum(v, axis=0, keepdims=True)


def _accumulate(refs, vals):
    first = pl.program_id(0) == 0

    @pl.when(first)
    def _():
        for r, v in zip(refs, vals):
            r[...] = v

    @pl.when(jnp.logical_not(first))
    def _():
        for r, v in zip(refs, vals):
            r[...] += v


def _prenorm_fwd(x, g, sc, sh):
    s, d = x.shape

    def body(x_ref, g_ref, sc_ref, sh_ref, h_ref):
        xf = x_ref[...]
        y = (xf * _rstd(xf)) * g_ref[...]
        h_ref[...] = (y * (1.0 + sc_ref[...]) + sh_ref[...]).astype(h_ref.dtype)

    return _call(body, name="prenorm_fwd", grid=(s // TM,),
                 in_specs=[_tile(d), _vec(d), _vec(d), _vec(d)], out_specs=_tile(d),
                 out_shape=jax.ShapeDtypeStruct((s, d), BF16), compiler_params=_params("parallel"))(x, g, sc, sh)


def _prenorm_bwd(dh, x, g, sc, dres):
    s, d = x.shape

    def body(dh_ref, x_ref, g_ref, sc_ref, dres_ref, dx_ref, dsh_ref, dsc_ref, dg_ref):
        xf, dhf = x_ref[...], dh_ref[...]
        rstd = _rstd(xf)
        xhat = xf * rstd
        one_sc = 1.0 + sc_ref[...]
        dxhat = dhf * (g_ref[...] * one_sc)
        dx_ref[...] = dres_ref[...] + rstd * (dxhat - xhat * jnp.mean(dxhat * xhat, axis=-1, keepdims=True))
        dhx = dhf * xhat
        _accumulate((dsh_ref, dsc_ref, dg_ref), (_colsum(dhf), _colsum(dhx) * g_ref[...], _colsum(dhx) * one_sc))

    vec_out = jax.ShapeDtypeStruct((1, d), F32)
    return _call(body, name="prenorm_bwd", grid=(s // TM,),
                 in_specs=[_tile(d), _tile(d), _vec(d), _vec(d), _tile(d)],
                 out_specs=[_tile(d), _vec(d), _vec(d), _vec(d)],
                 out_shape=[jax.ShapeDtypeStruct((s, d), F32), vec_out, vec_out, vec_out],
                 compiler_params=_params("arbitrary"))(dh, x, g, sc, dres)


def _postnorm_fwd(xres, m, g, gt):
    s, d = m.shape

    def body(x_ref, m_ref, g_ref, gt_ref, o_ref):
        mf = m_ref[...]
        o_ref[...] = x_ref[...] + gt_ref[...] * ((mf * _rstd(mf)) * g_ref[...])

    return _call(body, name="postnorm_fwd", grid=(s // TM,),
                 in_specs=[_tile(d), _tile(d), _vec(d), _vec(d)], out_specs=_tile(d),
                 out_shape=jax.ShapeDtypeStruct((s, d), F32), compiler_params=_params("parallel"))(xres, m, g, gt)


def _postnorm_bwd(dxn, m, g, gt):
    s, d = m.shape

    def body(dx_ref, m_ref, g_ref, gt_ref, dm_ref, dgt_ref, dg_ref):
        mf, dxf = m_ref[...], dx_ref[...]
        rstd = _rstd(mf)
        mhat = mf * rstd
        dmhat = dxf * (gt_ref[...] * g_ref[...])
        dm_ref[...] = (rstd * (dmhat - mhat * jnp.mean(dmhat * mhat, axis=-1, keepdims=True))).astype(dm_ref.dtype)
        dxm = _colsum(dxf * mhat)
        _accumulate((dgt_ref, dg_ref), (dxm * g_ref[...], dxm * gt_ref[...]))

    vec_out = jax.ShapeDtypeStruct((1, d), F32)
    return _call(body, name="postnorm_bwd", grid=(s // TM,),
                 in_specs=[_tile(d), _tile(d), _vec(d), _vec(d)], out_specs=[_tile(d), _vec(d), _vec(d)],
                 out_shape=[jax.ShapeDtypeStruct((s, d), BF16), vec_out, vec_out],
                 compiler_params=_params("arbitrary"))(dxn, m, g, gt)


def _loss(y, target):
    s, d = y.shape

    def body(y_ref, t_ref, dy_ref, sq_ref):
        err = y_ref[...] - t_ref[...]
        dy_ref[...] = err * (1.0 / d)
        tot = jnp.sum(_colsum(err * err), axis=1, keepdims=True)
        _accumulate((sq_ref,), (jnp.broadcast_to(tot, (1, LANES)),))

    return _call(body, name="loss", grid=(s // TM,), in_specs=[_tile(d), _tile(d)],
                 out_specs=[_tile(d), _vec(LANES)],
                 out_shape=[jax.ShapeDtypeStruct((s, d), F32), jax.ShapeDtypeStruct((1, LANES), F32)],
                 compiler_params=_params("arbitrary"))(y, target)


def _sigmoid(v):
    return 1.0 / (1.0 + jnp.exp(-v))


def _gate_fwd(proj, y_conv, y_attn):
    s, d = y_conv.shape
    ga_col, gb_col = (IN_COLS - 2 * d) // d, (IN_COLS - d) // d

    def body(ga_ref, gb_ref, yc_ref, ya_ref, o_ref):
        o_ref[...] = (_sigmoid(ga_ref[...]) * yc_ref[...] + _sigmoid(gb_ref[...]) * ya_ref[...]).astype(o_ref.dtype)

    return _call(body, name="gate_fwd", grid=(s // TM,),
                 in_specs=[_tile(d, ga_col), _tile(d, gb_col), _tile(d), _tile(d)], out_specs=_tile(d),
                 out_shape=jax.ShapeDtypeStruct((s, d), BF16),
                 compiler_params=_params("parallel"))(proj, proj, y_conv, y_attn)


def _gate_bwd(dmerged, proj, y_conv, y_attn):
    s, d = y_conv.shape
    ga_col, gb_col = (IN_COLS - 2 * d) // d, (IN_COLS - d) // d

    def body(dm_ref, ga_ref, gb_ref, yc_ref, ya_ref, dyc_ref, dya_ref, dga_ref, dgb_ref):
        dm = dm_ref[...]
        sa, sb = _sigmoid(ga_ref[...]), _sigmoid(gb_ref[...])
        dyc_ref[...] = (dm * sa).astype(BF16)
        dya_ref[...] = (dm * sb).astype(BF16)
        dga_ref[...] = (dm * yc_ref[...] * (sa * (1.0 - sa))).astype(BF16)
        dgb_ref[...] = (dm * ya_ref[...] * (sb * (1.0 - sb))).astype(BF16)

    out = jax.ShapeDtypeStruct((s, d), BF16)
    return _call(body, name="gate_bwd", grid=(s // TM,),
                 in_specs=[_tile(d), _tile(d, ga_col), _tile(d, gb_col), _tile(d), _tile(d)],
                 out_specs=[_tile(d)] * 4, out_shape=[out] * 4,
                 compiler_params=_params("parallel"))(dmerged, proj, proj, y_conv, y_attn)


def _shift_down(prev8, cur, by):
    ext = jnp.concatenate([prev8, cur], axis=0)
    return pltpu.roll(ext, by, 0)[8:]


def _shift_up(cur, next8, by):
    ext = jnp.concatenate([cur, next8], axis=0)
    return pltpu.roll(ext, ext.shape[0] - by, 0)[:cur.shape[0]]


def _conv_fwd(proj, conv_w):
    s, w = proj.shape[0], CONV_WIDTH
    per8 = TM // 8

    def prev(col):
        return pl.BlockSpec((8, w), lambda i: (jnp.maximum(i * per8 - 1, 0), col))

    def body(bg_ref, cg_ref, u_ref, cgp_ref, up_ref, w_ref, o_ref):
        vv = cg_ref[...] * u_ref[...]
        pv = cgp_ref[...] * up_ref[...] * jnp.where(pl.program_id(0) > 0, 1.0, 0.0)
        y = w_ref[0:1, :] * _shift_down(pv, vv, 2) + w_ref[1:2, :] * _shift_down(pv, vv, 1) + w_ref[2:3, :] * vv
        o_ref[...] = (bg_ref[...] * y).astype(o_ref.dtype)

    return _call(body, name="conv_fwd", grid=(s // TM,),
                 in_specs=[_tile(w, 0), _tile(w, 1), _tile(w, 2), prev(1), prev(2),
                           pl.BlockSpec((3, w), lambda i: (0, 0))],
                 out_specs=_tile(w), out_shape=jax.ShapeDtypeStruct((s, w), BF16),
                 compiler_params=_params("parallel"))(proj, proj, proj, proj, proj, conv_w)


def _conv_bwd(dyc, proj, conv_w):
    s, w = proj.shape[0], CONV_WIDTH
    per8 = TM // 8
    n_tiles = s // TM

    def prev(col):
        return pl.BlockSpec((8, w), lambda i: (jnp.maximum(i * per8 - 1, 0), col))

    def nxt(col):
        return pl.BlockSpec((8, w), lambda i: (jnp.minimum((i + 1) * per8, s // 8 - 1), col))

    def body(dyc_ref, bg_ref, cg_ref, u_ref, cgp_ref, up_ref, dycn_ref, bgn_ref, w_ref,
             dbg_ref, dcg_ref, du_ref, dw0_ref, dw1_ref, dw2_ref):
        i = pl.program_id(0)
        cg, u = cg_ref[...], u_ref[...]
        vv = cg * u
        pv = cgp_ref[...] * up_ref[...] * jnp.where(i > 0, 1.0, 0.0)
        v1, v2 = _shift_down(pv, vv, 1), _shift_down(pv, vv, 2)
        w0, w1, w2 = w_ref[0:1, :], w_ref[1:2, :], w_ref[2:3, :]
        dyc_t = dyc_ref[...]
        dbg_ref[...] = (dyc_t * (w0 * v2 + w1 * v1 + w2 * vv)).astype(BF16)
        dy = dyc_t * bg_ref[...]
        dyn = dycn_ref[...] * bgn_ref[...] * jnp.where(i < n_tiles - 1, 1.0, 0.0)
        dvv = w2 * dy + w1 * _shift_up(dy, dyn, 1) + w0 * _shift_up(dy, dyn, 2)
        dcg_ref[...] = (dvv * u).astype(BF16)
        du_ref[...] = (dvv * cg).astype(BF16)
        _accumulate((dw0_ref, dw1_ref, dw2_ref), (_colsum(dy * v2), _colsum(dy * v1), _colsum(dy * vv)))

    act = jax.ShapeDtypeStruct((s, w), BF16)
    tap = jax.ShapeDtypeStruct((1, w), F32)
    return _call(body, name="conv_bwd", grid=(n_tiles,),
                 in_specs=[_tile(w), _tile(w, 0), _tile(w, 1), _tile(w, 2), prev(1), prev(2), nxt(0), nxt(0),
                           pl.BlockSpec((3, w), lambda i: (0, 0))],
                 out_specs=[_tile(w)] * 3 + [_vec(w)] * 3, out_shape=[act] * 3 + [tap] * 3,
                 compiler_params=_params("arbitrary"))(dyc, proj, proj, proj, proj, proj, dyc, proj, conv_w)


Q_COL = 3 * CONV_WIDTH // LANES
K_COL = Q_COL + ATTN_WIDTH // LANES
V_COL = K_COL + ATTN_WIDTH // LANES
SCALE = HEAD_DIM ** -0.5


def _head_lanes(hh):
    lane = lax.broadcasted_iota(jnp.int32, (1, LANES), 1)
    return jnp.where((lane >= hh * HEAD_DIM) & (lane < (hh + 1) * HEAD_DIM), 1.0, 0.0)


def _tri(width, keep):
    j = lax.broadcasted_iota(jnp.int32, (TK, width), 0)
    s = lax.broadcasted_iota(jnp.int32, (TK, width), 1)
    return jnp.where((s >= TK) | keep(j, s), 1.0, 0.0).astype(BF16)


def _split_dot(v, tri):
    hi = v.astype(BF16)
    lo = (v - hi.astype(F32)).astype(BF16)
    return _dot(hi, tri) + _dot(lo, tri)


def _logits(qm, kblk, ks, tpos):
    z = _dot(qm, kblk, NT)
    mask = (ks + lax.broadcasted_iota(jnp.int32, (1, TK), 1)) < tpos
    e = jnp.exp(-jnp.abs(z))
    l0 = jnp.where(mask, -(jnp.maximum(z, 0.0) + jnp.log(1.0 + e)), 0.0)
    return z, mask, e, l0


def _neg_softplus(z):
    return jnp.minimum(-z, 0.0) - jnp.log(1.0 + jnp.exp(-jnp.abs(z)))


def _attn_fwd(proj):
    s = proj.shape[0]
    nq = s // TQ
    diag = TQ // TK
    n_ch = 2 * TQ // CH
    assert s // TK <= LANES

    def body(q_ref, k_ref, v_ref, o_ref, rs_ref, q2_scr, z_scr, l0_scr, cs_scr, a_scr, r_scr, rall_scr, acc_scr):
        qi = pl.program_id(1)
        lm = [_head_lanes(0), _head_lanes(1)]
        tri = _tri(2 * TK, lambda j, ss: j > ss)
        lane = lax.broadcasted_iota(jnp.int32, (CH, LANES), 1)
        row = lax.broadcasted_iota(jnp.int32, (CH, 1), 0)
        col = lax.broadcasted_iota(jnp.int32, (1, TK), 1)
        for hh in range(2):
            q2_scr[hh * TQ:(hh + 1) * TQ, :] = (q_ref[...] * (SCALE * lm[hh])).astype(BF16)
        r_scr[...] = jnp.zeros_like(r_scr)
        rall_scr[...] = jnp.zeros_like(rall_scr)
        acc_scr[...] = jnp.zeros_like(acc_scr)

        def step(kb, masked):
            ks = pl.multiple_of(kb * TK, TK)
            kblk = k_ref[pl.ds(ks, TK), :].astype(BF16)
            vf = v_ref[pl.ds(ks, TK), :]
            v2 = jnp.concatenate([(vf * lm[0]).astype(BF16), (vf * lm[1]).astype(BF16)], axis=0)
            z_scr[...] = _dot(q2_scr[...], kblk, NT)

            def causal(c):
                return (ks + col) < (qi * TQ + (c * CH) % TQ + row)

            for c in range(n_ch):
                rows = slice(c * CH, (c + 1) * CH)
                l0 = _neg_softplus(z_scr[rows, :])
                if masked:
                    l0 = jnp.where(causal(c), l0, 0.0)
                l0_scr[rows, :] = l0.astype(BF16)
            cs_scr[...] = _dot(l0_scr[...], tri)
            for c in range(n_ch):
                rows = slice(c * CH, (c + 1) * CH)
                hh, r0 = (c * CH) // TQ, (c * CH) % TQ
                near = r_scr[rows, :]
                a = jnp.exp(l0_scr[rows, :].astype(F32) + z_scr[rows, :] + cs_scr[rows, :TK] + near)
                if masked:
                    a = jnp.where(causal(c), a, 0.0)
                a_scr[r0:r0 + CH, hh * TK:(hh + 1) * TK] = a.astype(BF16)
                rall_scr[rows, :] = jnp.where(lane == kb, near, rall_scr[rows, :])
                r_scr[rows, :] = near + cs_scr[rows, TK:]
            acc_scr[...] += _dot(a_scr[...], v2)

        def diag_step(it, carry):
            step((qi + 1) * diag - 1 - it, True)
            return carry

        def inner_step(it, carry):
            step(qi * diag - 1 - it, False)
            return carry

        lax.fori_loop(0, diag, diag_step, 0)
        lax.fori_loop(0, qi * diag, inner_step, 0)
        for hh in range(2):
            rs_ref[hh] = rall_scr[hh * TQ:(hh + 1) * TQ, :]
        o_ref[...] = acc_scr[...]

    return _call(
        body, name="attn_fwd", grid=(N_HEADS // 2, nq),
        in_specs=[pl.BlockSpec((TQ, LANES), lambda p, qi: (qi, Q_COL + p)),
                  pl.BlockSpec((s, LANES), lambda p, qi: (0, K_COL + p)),
                  pl.BlockSpec((s, LANES), lambda p, qi: (0, V_COL + p))],
        out_specs=[pl.BlockSpec((TQ, LANES), lambda p, qi: (qi, p)),
                   pl.BlockSpec((2, TQ, LANES), lambda p, qi: (p, qi, 0))],
        out_shape=[jax.ShapeDtypeStruct((s, ATTN_WIDTH), F32), jax.ShapeDtypeStruct((N_HEADS, s, LANES), F32)],
        scratch_shapes=[pltpu.VMEM((2 * TQ, LANES), BF16), pltpu.VMEM((2 * TQ, TK), F32),
                        pltpu.VMEM((2 * TQ, TK), BF16), pltpu.VMEM((2 * TQ, 2 * TK), F32),
                        pltpu.VMEM((TQ, 2 * TK), BF16), pltpu.VMEM((2 * TQ, LANES), F32),
                        pltpu.VMEM((2 * TQ, LANES), F32), pltpu.VMEM((TQ, LANES), F32)],
        compiler_params=_params("parallel", "arbitrary"),
    )(proj, proj, proj)


def _attn_bwd(proj, do, rsave):
    s = proj.shape[0]
    nq = s // TQ

    diag = TQ // TK
    n_ch = 2 * TQ // CH

    def body(q_ref, k_ref, v_ref, do_ref, rs_ref, dq_ref, dk_ref, dv_ref,
             q2_scr, do2_scr, z_scr, da_scr, l0_scr, beta_scr, cs_scr, a_scr, g_scr, cg_scr, dz_scr,
             pg_scr, dq_scr, dk_scr, dv_scr):
        qi = pl.program_id(1)

        @pl.when(qi == 0)
        def _():
            dk_scr[...] = jnp.zeros_like(dk_scr)
            dv_scr[...] = jnp.zeros_like(dv_scr)

        lm = [_head_lanes(0), _head_lanes(1)]
        tri_after = _tri(TK, lambda j, ss: j > ss)
        tri_before = _tri(2 * TK, lambda j, ss: j < ss)
        lane = lax.broadcasted_iota(jnp.int32, (CH, LANES), 1)
        row = lax.broadcasted_iota(jnp.int32, (CH, 1), 0)
        col = lax.broadcasted_iota(jnp.int32, (1, TK), 1)
        for hh in range(2):
            q2_scr[hh * TQ:(hh + 1) * TQ, :] = (q_ref[...] * (SCALE * lm[hh])).astype(BF16)
            do2_scr[hh * TQ:(hh + 1) * TQ, :] = (do_ref[...] * lm[hh]).astype(BF16)
        pg_scr[...] = jnp.zeros_like(pg_scr)
        dq_scr[...] = jnp.zeros_like(dq_scr)

        def step(kb, masked):
            ks = pl.multiple_of(kb * TK, TK)
            kblk = k_ref[pl.ds(ks, TK), :].astype(BF16)
            vblk = v_ref[pl.ds(ks, TK), :].astype(BF16)
            z_scr[...] = _dot(q2_scr[...], kblk, NT)
            da_scr[...] = _dot(do2_scr[...], vblk, NT)

            def causal(c):
                return (ks + col) < (qi * TQ + (c * CH) % TQ + row)

            for c in range(n_ch):
                rows = slice(c * CH, (c + 1) * CH)
                z = z_scr[rows, :]
                e = jnp.exp(-jnp.abs(z))
                w = 1.0 + e
                l0 = jnp.minimum(-z, 0.0) - jnp.log(w)
                if masked:
                    l0 = jnp.where(causal(c), l0, 0.0)
                l0_scr[rows, :] = l0.astype(BF16)
                rinv = 1.0 / w
                beta_scr[rows, :] = jnp.where(z >= 0.0, rinv, e * rinv)
            cs_scr[...] = _dot(l0_scr[...], tri_after)
            for c in range(n_ch):
                rows = slice(c * CH, (c + 1) * CH)
                hh, r0 = (c * CH) // TQ, (c * CH) % TQ
                near = jnp.sum(jnp.where(lane == kb, rs_ref[hh, r0:r0 + CH, :], 0.0), axis=1, keepdims=True)
                a = jnp.exp(l0_scr[rows, :].astype(F32) + z_scr[rows, :] + cs_scr[rows, :] + near)
                if masked:
                    a = jnp.where(causal(c), a, 0.0)
                a_scr[rows, :] = a.astype(BF16)
                g_scr[rows, :] = (a * da_scr[rows, :]).astype(BF16)
            cg_scr[...] = _dot(g_scr[...], tri_before)
            for c in range(n_ch):
                rows = slice(c * CH, (c + 1) * CH)
                before = cg_scr[rows, :TK] + pg_scr[rows, :]
                beta = beta_scr[rows, :]
                dz = g_scr[rows, :].astype(F32) * (1.0 - beta) - beta * before
                if masked:
                    dz = jnp.where(causal(c), dz, 0.0)
                dz_scr[rows, :] = dz.astype(BF16)
                pg_scr[rows, :] += cg_scr[rows, TK:]
            dq_scr[...] += _dot(dz_scr[...], kblk)
            dk_scr[pl.ds(ks, TK), :] += _dot(dz_scr[...], q2_scr[...], TN)
            dv_scr[pl.ds(ks, TK), :] += _dot(a_scr[...], do2_scr[...], TN)

        def inner_step(kb, carry):
            step(kb, False)
            return carry

        def diag_step(it, carry):
            step(qi * diag + it, True)
            return carry

        lax.fori_loop(0, qi * diag, inner_step, 0)
        lax.fori_loop(0, diag, diag_step, 0)
        dq = dq_scr[:TQ, :] * lm[0] + dq_scr[TQ:, :] * lm[1]
        dq_ref[...] = (dq * SCALE).astype(dq_ref.dtype)

        @pl.when(qi == nq - 1)
        def _():
            dk_ref[...] = dk_scr[...].astype(dk_ref.dtype)
            dv_ref[...] = dv_scr[...].astype(dv_ref.dtype)

    stacked_f32 = pltpu.VMEM((2 * TQ, TK), F32)
    stacked_bf16 = pltpu.VMEM((2 * TQ, TK), BF16)

    def rows(c0):
        return pl.BlockSpec((TQ, LANES), lambda p, qi: (qi, c0 + p))

    def whole(c0):
        return pl.BlockSpec((s, LANES), lambda p, qi: (0, c0 + p))

    out = jax.ShapeDtypeStruct((s, ATTN_WIDTH), BF16)
    return _call(
        body, name="attn_bwd", grid=(N_HEADS // 2, nq),
        in_specs=[rows(Q_COL), whole(K_COL), whole(V_COL), rows(0),
                  pl.BlockSpec((2, TQ, LANES), lambda p, qi: (p, qi, 0))],
        out_specs=[rows(0), whole(0), whole(0)], out_shape=[out] * 3,
        scratch_shapes=[stacked_bf16, stacked_bf16, stacked_f32, stacked_f32, stacked_bf16, stacked_f32, stacked_f32,
                        stacked_bf16, stacked_bf16, pltpu.VMEM((2 * TQ, 2 * TK), F32), stacked_bf16,
                        stacked_f32, stacked_f32, pltpu.VMEM((s, LANES), F32), pltpu.VMEM((s, LANES), F32)],
        compiler_params=_params("arbitrary", "arbitrary"),
    )(proj, proj, proj, do, rsave)


def _sum_adamw(name, parts, w, m, v):
    n, r, c = parts.shape
    tr = r if r <= 256 else 256

    def body(p_ref, w_ref, m_ref, v_ref, g_ref, d_ref, nm_ref, nv_ref):
        g = p_ref[0].astype(F32)
        for j in range(1, n):
            g = g + p_ref[j].astype(F32)
        nm = ADAM_B1 * m_ref[...] + (1.0 - ADAM_B1) * g
        nv = ADAM_B2 * v_ref[...] + (1.0 - ADAM_B2) * (g * g)
        m_hat = nm / (1.0 - ADAM_B1 ** ADAM_STEP)
        v_hat = nv / (1.0 - ADAM_B2 ** ADAM_STEP)
        g_ref[...] = g
        d_ref[...] = -ADAM_LR * (m_hat / (jnp.sqrt(v_hat) + ADAM_EPS) + ADAM_WD * w_ref[...])
        nm_ref[...] = nm
        nv_ref[...] = nv

    mat = pl.BlockSpec((tr, c), lambda i: (i, 0))
    out = jax.ShapeDtypeStruct((r, c), F32)
    return _call(body, name=name, grid=(r // tr,),
                 in_specs=[pl.BlockSpec((n, tr, c), lambda i: (0, i, 0)), mat, mat, mat],
                 out_specs=[mat] * 4, out_shape=[out] * 4, compiler_params=_params("parallel"))(parts, w, m, v)


def _natural(gathered):
    _, k, n = gathered.shape
    return gathered.transpose(1, 0, 2).reshape(k, N_DEV * n)


def _relu2_epi(acc):
    r = jnp.maximum(acc, 0.0)
    return acc, r * r


def _relu2_bwd_epi(acc, a_act):
    return (acc * (2.0 * jnp.maximum(a_act, 0.0)),)


def kernel(x, c, w_ada, b_ada, g_pre_mix, g_post_mix, g_pre_mlp, g_post_mlp, w_in, conv_w, w_proj_conv, w_proj_attn, w_out, w_mlp_in, w_mlp_out, loss_target, m_w_ada, m_b_ada, m_g_pre_mix, m_g_post_mix, m_g_pre_mlp, m_g_post_mlp, m_w_in, m_conv_w, m_w_proj_conv, m_w_proj_attn, m_w_out, m_w_mlp_in, m_w_mlp_out, v_w_ada, v_b_ada, v_g_pre_mix, v_g_post_mix, v_g_pre_mlp, v_g_post_mlp, v_w_in, v_conv_w, v_w_proj_conv, v_w_proj_attn, v_w_out, v_w_mlp_in, v_w_mlp_out):
    xi, yi, ci = _mesh_pos()
    me = 4 * xi + 2 * yi + ci
    d = D_MODEL
    x0 = x[0]
    seq = x0.shape[0]
    ada_cols = w_ada.shape[2]
    conv_cols = conv_w.shape[2]

    small = jnp.concatenate([c.reshape(-1), conv_w.reshape(-1)])
    small = jnp.pad(small, (0, 2 * d - small.shape[0])).reshape(8, 2 * d // 8)
    small_all = _all_gather("gather_c", [small])[0].reshape(N_DEV, 2 * d)
    c_all = small_all[:, :d]
    conv_all = small_all[:, d:d + DEPTH * 3 * conv_cols].reshape(N_DEV, DEPTH, 3, conv_cols)
    conv_all = conv_all.transpose(1, 2, 0, 3).reshape(DEPTH, 3, N_DEV * conv_cols)
    mod_cols = jnp.stack([_mm("mod_mm", c_all, w_ada[l], "nn", N_DEV, ada_cols, d, [F32], exact=True)
                          for l in range(DEPTH)], axis=1)
    mod_all = _all_gather("gather_mod", [mod_cols.reshape(N_DEV, DEPTH * ada_cols)])[0]
    mod_mine = lax.dynamic_index_in_dim(mod_all, me, axis=1, keepdims=False).reshape(N_DEV, DEPTH, ada_cols)
    mod = mod_mine.transpose(1, 0, 2).reshape(DEPTH, N_MOD * d) + b_ada

    def gathered_weights(l):
        shards = [w_in[l], w_proj_conv[l], w_proj_attn[l], w_out[l], w_mlp_in[l], w_mlp_out[l]]
        g_in, g_pc, g_pa, g_out, g_mi, g_mo = _all_gather("gather_w", [w.astype(BF16) for w in shards])
        return (_natural(g_in), _natural(g_pc), _natural(g_pa), g_out.reshape(d, d), _natural(g_mi),
                g_mo.reshape(D_FF, d))

    weights = [gathered_weights(l) for l in range(DEPTH)]

    saved = []
    xl = x0
    for l in range(DEPTH):
        wg_in, wg_pc, wg_pa, wg_out, wg_mi, wg_mo = weights[l]
        sh1, sc1, gt1, sh2, sc2, gt2 = [mod[l:l + 1, i * d:(i + 1) * d] for i in range(N_MOD)]
        h = _prenorm_fwd(xl, g_pre_mix[l:l + 1], sc1, sh1)
        proj = _mm("proj", h, wg_in, "nn", TM, 1024, d, [F32])
        yc = _conv_fwd(proj, conv_all[l])
        y_conv = _mm("proj_conv", yc, wg_pc, "nn", TM, d, CONV_WIDTH, [F32])
        o, rsave = _attn_fwd(proj)
        y_attn = _mm("proj_attn", o, wg_pa, "nn", TM, d, ATTN_WIDTH, [F32])
        merged = _gate_fwd(proj, y_conv, y_attn)
        mix_out = _mm("mix_out", merged, wg_out, "nn", TM, d, d, [F32])
        x1 = _postnorm_fwd(xl, mix_out, g_post_mix[l:l + 1], gt1)
        h2 = _prenorm_fwd(x1, g_pre_mlp[l:l + 1], sc2, sh2)
        a_act, r = _mm("mlp_in", h2, wg_mi, "nn", TM, 1024, d, [F32, BF16], epi=_relu2_epi)
        ff = _mm("mlp_out", r, wg_mo, "nn", TM, d, 1024, [F32])
        x2 = _postnorm_fwd(x1, ff, g_post_mlp[l:l + 1], gt2)
        saved.append((xl, h, proj, yc, o, rsave, y_conv, y_attn, merged, mix_out, x1, h2, a_act, r, ff))
        xl = x2

    dxo, sq = _loss(xl, loss_target[0])
    loss = lax.psum(sq[0, 0] * (0.5 / d), ("x", "y", "c"))

    big = {}
    dmod, small_grads = [None] * DEPTH, [None] * DEPTH
    for l in reversed(range(DEPTH)):
        wg_in, wg_pc, wg_pa, wg_out, wg_mi, wg_mo = weights[l]
        xin, h, proj, yc, o, rsave, y_conv, y_attn, merged, mix_out, x1, h2, a_act, r, ff = saved[l]
        sh1, sc1, gt1, sh2, sc2, gt2 = [mod[l:l + 1, i * d:(i + 1) * d] for i in range(N_MOD)]

        dff, dgt2, dg_post_mlp = _postnorm_bwd(dxo, ff, g_post_mlp[l:l + 1], gt2)
        da = _mm("d_relu2", dff, wg_mo, "nt", TM, 1024, d, [BF16], epi=_relu2_bwd_epi, extra=(a_act,))
        gw_mo = _mm("gw_mlp_out", r, dff, "tn", 1024, d, 1024, [BF16])
        dh2 = _mm("d_h2", da, wg_mi, "nt", TM, d, 1024, [F32])
        gw_mi = _mm("gw_mlp_in", h2, da, "tn", d, D_FF // N_DEV, 1024, [BF16], blocked_out=True)
        dx1, dsh2, dsc2, dg_pre_mlp = _prenorm_bwd(dh2, x1, g_pre_mlp[l:l + 1], sc2, dxo)

        dmix, dgt1, dg_post_mix = _postnorm_bwd(dx1, mix_out, g_post_mix[l:l + 1], gt1)
        dmerged = _mm("d_merged", dmix, wg_out, "nt", TM, d, d, [F32])
        gw_out = _mm("gw_out", merged, dmix, "tn", d, d, 1024, [BF16])
        dy_conv, dy_attn, dga, dgb = _gate_bwd(dmerged, proj, y_conv, y_attn)
        do = _mm("d_o", dy_attn, wg_pa, "nt", TM, ATTN_WIDTH, d, [F32])
        gw_pa = _mm("gw_proj_attn", o, dy_attn, "tn", ATTN_WIDTH, d, 1024, [BF16])
        dyc = _mm("d_yc", dy_conv, wg_pc, "nt", TM, CONV_WIDTH, d, [F32])
        gw_pc = _mm("gw_proj_conv", yc, dy_conv, "tn", CONV_WIDTH, d, 1024, [BF16])
        dq, dk, dv = _attn_bwd(proj, do, rsave)
        dbg, dcg, du, dw0, dw1, dw2 = _conv_bwd(dyc, proj, conv_all[l])
        dproj = jnp.concatenate([dbg, dcg, du, dq, dk, dv, dga, dgb], axis=1)
        dh = _mm("d_h", dproj, wg_in, "nt", TM, d, 1024, [F32])
        gw_in = _mm("gw_in", h, dproj, "tn", d, IN_COLS // N_DEV, 1024, [BF16], blocked_out=True)
        dxo, dsh1, dsc1, dg_pre_mix = _prenorm_bwd(dh, xin, g_pre_mix[l:l + 1], sc1, dx1)

        dmod[l] = jnp.concatenate([dsh1, dsc1, dgt1, dsh2, dsc2, dgt2], axis=1)
        small_grads[l] = (dg_pre_mix, dg_post_mix, dg_pre_mlp, dg_post_mlp, jnp.concatenate([dw0, dw1, dw2], axis=0))

        def col_blocks(gw):
            k, n = gw.shape
            return gw.reshape(k, N_DEV, n // N_DEV).transpose(1, 0, 2)

        sent = [gw_in, col_blocks(gw_pc), col_blocks(gw_pa), gw_out.reshape(N_DEV, d // N_DEV, d), gw_mi,
                gw_mo.reshape(N_DEV, D_FF // N_DEV, d)]
        parts = _exchange("exchange_gw", sent)
        names = ["w_in", "w_proj_conv", "w_proj_attn", "w_out", "w_mlp_in", "w_mlp_out"]
        olds = [(w_in, m_w_in, v_w_in), (w_proj_conv, m_w_proj_conv, v_w_proj_conv),
                (w_proj_attn, m_w_proj_attn, v_w_proj_attn), (w_out, m_w_out, v_w_out),
                (w_mlp_in, m_w_mlp_in, v_w_mlp_in), (w_mlp_out, m_w_mlp_out, v_w_mlp_out)]
        for nm, p, (w_, m_, v_) in zip(names, parts, olds):
            big[(nm, l)] = _sum_adamw("adamw_" + nm, p, w_[l], m_[l], v_[l])

    vec = jnp.concatenate(
        [dmod[l].reshape(-1) for l in range(DEPTH)]
        + [small_grads[l][i].reshape(-1) for i in range(4) for l in range(DEPTH)]
        + [small_grads[l][4].reshape(-1) for l in range(DEPTH)])
    n_vec = vec.shape[0]
    vec_all = _all_gather("gather_small", [vec.reshape(8, n_vec // 8)])[0].reshape(N_DEV, n_vec)
    n_mod = DEPTH * N_MOD * d
    dmod_all = vec_all[:, :n_mod].reshape(N_DEV, DEPTH, N_MOD * d)
    res = {}
    res["b_ada"] = _sum_adamw("adamw_b_ada", dmod_all, b_ada, m_b_ada, v_b_ada)
    off = n_mod
    for nm, (w_, m_, v_) in zip(
            ["g_pre_mix", "g_post_mix", "g_pre_mlp", "g_post_mlp"],
            [(g_pre_mix, m_g_pre_mix, v_g_pre_mix), (g_post_mix, m_g_post_mix, v_g_post_mix),
             (g_pre_mlp, m_g_pre_mlp, v_g_pre_mlp), (g_post_mlp, m_g_post_mlp, v_g_post_mlp)]):
        res[nm] = _sum_adamw("adamw_gain", vec_all[:, off:off + DEPTH * d].reshape(N_DEV, DEPTH, d), w_, m_, v_)
        off += DEPTH * d
    dconv_all = vec_all[:, off:].reshape(N_DEV, DEPTH * 3, CONV_WIDTH)
    dconv_mine = lax.dynamic_slice_in_dim(dconv_all, me * conv_cols, conv_cols, axis=2)
    res["conv_w"] = [t.reshape(DEPTH, 3, conv_cols) for t in _sum_adamw(
        "adamw_conv_w", dconv_mine, conv_w.reshape(DEPTH * 3, conv_cols), m_conv_w.reshape(DEPTH * 3, conv_cols),
        v_conv_w.reshape(DEPTH * 3, conv_cols))]

    c_t = jnp.pad(c_all.T, ((0, 0), (0, LANES - N_DEV)))
    dmod_mine = lax.dynamic_slice_in_dim(dmod_all, me * ada_cols, ada_cols, axis=2)
    ada = []
    for l in range(DEPTH):
        dm_l = jnp.pad(dmod_mine[:, l, :], ((0, LANES - N_DEV), (0, 0)))
        gw_ada = _mm("gw_ada", c_t, dm_l, "nn", 256, ada_cols, LANES, [F32], exact=True)
        ada.append(_sum_adamw("adamw_w_ada", gw_ada[None], w_ada[l], m_w_ada[l], v_w_ada[l]))
    res["w_ada"] = [jnp.stack([ada[l][i] for l in range(DEPTH)]) for i in range(4)]
    for nm in ["w_in", "w_proj_conv", "w_proj_attn", "w_out", "w_mlp_in", "w_mlp_out"]:
        res[nm] = [jnp.stack([big[(nm, l)][i] for l in range(DEPTH)]) for i in range(4)]

    order = ["w_ada", "b_ada", "g_pre_mix", "g_post_mix", "g_pre_mlp", "g_post_mlp", "w_in", "conv_w",
             "w_proj_conv", "w_proj_attn", "w_out", "w_mlp_in", "w_mlp_out"]
    outs = [loss, dxo[None]]
    for i in range(4):
        outs += [res[nm][i] for nm in order]
    return tuple(outs)
```

```python
import jax
import jax.numpy as jnp
from jax import lax
from jax.experimental import pallas as pl
from jax.experimental.pallas import tpu as pltpu

F32 = jnp.float32
BF16 = jnp.bfloat16
MESH = pl.DeviceIdType.MESH

N_DEV = 8
D_MODEL = 1024
CONV_WIDTH = 512
N_HEADS = 8
HEAD_DIM = 64
ATTN_WIDTH = N_HEADS * HEAD_DIM
D_FF = 4 * D_MODEL
N_MOD = 6
DEPTH = 2
EPS = 1e-6
IN_COLS = 3 * CONV_WIDTH + 3 * ATTN_WIDTH + 2 * D_MODEL
LANES = 128

ADAM_LR = 0.001
ADAM_B1 = 0.9
ADAM_B2 = 0.999
ADAM_EPS = 1e-08
ADAM_WD = 0.01
ADAM_STEP = 10

TM = 512
TQ = 512
TK = 128
CH = 64
VMEM_LIMIT = 56 * 1024 * 1024

NN = (((1,), (0,)), ((), ()))
NT = (((1,), (1,)), ((), ()))
TN = (((0,), (0,)), ((), ()))
_DIMS = {"nn": NN, "nt": NT, "tn": TN}


def _call(body, **kw):
    return pl.pallas_call(body, **kw)


def _params(*sem):
    return pltpu.CompilerParams(dimension_semantics=sem, vmem_limit_bytes=VMEM_LIMIT)


def _dot(a, b, dims=NN):
    return lax.dot_general(a, b, dims, preferred_element_type=F32)


def _mesh_pos():
    return lax.axis_index("x"), lax.axis_index("y"), lax.axis_index("c")


def _all_gather(name, arrs):
    n = len(arrs)

    def body(*refs):
        ins, outs = refs[:n], refs[n:2 * n]
        send_sems, recv_sems, local_sems = refs[2 * n:]
        x, y, c = _mesh_pos()
        me, sibling = (x, y, c), (x, y, 1 - c)
        chips = [(1 - x, y), (x, 1 - y), (1 - x, 1 - y)]

        def blk(t, p):
            return outs[t].at[4 * p[0] + 2 * p[1] + p[2]]

        def copy(t, k, block, to, src=None):
            return pltpu.make_async_remote_copy(
                src_ref=blk(t, block) if src is None else src, dst_ref=blk(t, block),
                send_sem=send_sems.at[7 * t + k], recv_sem=recv_sems.at[7 * t + k],
                device_id=to, device_id_type=MESH)

        mine, first, passed = [], [], []
        for t in range(n):
            cp = pltpu.make_async_copy(ins[t], blk(t, me), local_sems.at[t])
            cp.start()
            mine.append(cp)
            cps = [copy(t, 0, me, sibling, src=ins[t])]
            cps += [copy(t, 1 + j, me, (*chip, c), src=ins[t]) for j, chip in enumerate(chips)]
            for cp in cps:
                cp.start()
            first += cps
        for t in range(n):
            for j, chip in enumerate(chips):
                copy(t, 1 + j, (*chip, c), me).wait_recv()
                cp = copy(t, 4 + j, (*chip, c), sibling)
                cp.start()
                passed.append(cp)
        for t in range(n):
            copy(t, 0, sibling, me).wait_recv()
            for j, chip in enumerate(chips):
                copy(t, 4 + j, (*chip, 1 - c), me).wait_recv()
        for cp in first + passed:
            cp.wait_send()
        for cp in mine:
            cp.wait()

    any_spec = pl.BlockSpec(memory_space=pl.ANY)
    return _call(
        body, name=name,
        out_shape=[jax.ShapeDtypeStruct((N_DEV,) + a.shape, a.dtype) for a in arrs],
        in_specs=[any_spec] * n, out_specs=[any_spec] * n,
        scratch_shapes=[pltpu.SemaphoreType.DMA((7 * n,)), pltpu.SemaphoreType.DMA((7 * n,)),
                        pltpu.SemaphoreType.DMA((n,))],
    )(*arrs)


def _exchange(name, arrs):
    n = len(arrs)

    def body(*refs):
        ins, outs = refs[:n], refs[n:2 * n]
        send_sems, recv_sems, local_sems = refs[2 * n:]
        x, y, c = _mesh_pos()
        my_idx = 4 * x + 2 * y + c
        peers = []
        for k in range(1, N_DEV):
            p = (1 - x if k & 4 else x, 1 - y if k & 2 else y, 1 - c if k & 1 else c)
            peers.append((k - 1, p, 4 * p[0] + 2 * p[1] + p[2]))

        def copy(t, k, p, p_idx):
            return (pltpu.make_async_remote_copy(
                        src_ref=ins[t].at[p_idx], dst_ref=outs[t].at[my_idx],
                        send_sem=send_sems.at[7 * t + k], recv_sem=recv_sems.at[7 * t + k],
                        device_id=p, device_id_type=MESH),
                    pltpu.make_async_remote_copy(
                        src_ref=ins[t].at[p_idx], dst_ref=outs[t].at[p_idx],
                        send_sem=send_sems.at[7 * t + k], recv_sem=recv_sems.at[7 * t + k],
                        device_id=p, device_id_type=MESH))

        mine, sends, recvs = [], [], []
        for t in range(n):
            cp = pltpu.make_async_copy(ins[t].at[my_idx], outs[t].at[my_idx], local_sems.at[t])
            cp.start()
            mine.append(cp)
            for k, p, p_idx in peers:
                send, recv = copy(t, k, p, p_idx)
                send.start()
                sends.append(send)
                recvs.append(recv)
        for cp in recvs:
            cp.wait_recv()
        for cp in sends:
            cp.wait_send()
        for cp in mine:
            cp.wait()

    any_spec = pl.BlockSpec(memory_space=pl.ANY)
    return _call(
        body, name=name,
        out_shape=[jax.ShapeDtypeStruct(a.shape, a.dtype) for a in arrs],
        in_specs=[any_spec] * n, out_specs=[any_spec] * n,
        scratch_shapes=[pltpu.SemaphoreType.DMA((7 * n,)), pltpu.SemaphoreType.DMA((7 * n,)),
                        pltpu.SemaphoreType.DMA((n,))],
    )(*arrs)


def _mm(name, a, b, mode, tm, tn, tk, out_dtypes, epi=None, extra=(), blocked_out=False, exact=False):
    if mode == "nn":
        (m, k), n = a.shape, b.shape[1]
    elif mode == "nt":
        (m, k), n = a.shape, b.shape[0]
    else:
        (k, m), n = a.shape, b.shape[1]
    nk = k // tk
    grid = (m // tm, n // tn, nk)
    n_extra, n_out = len(extra), len(out_dtypes)

    def body(*refs):
        a_ref, b_ref = refs[0], refs[1]
        extra_refs = refs[2:2 + n_extra]
        out_refs = refs[2 + n_extra:2 + n_extra + n_out]
        if exact:
            p = lax.dot_general(a_ref[...], b_ref[...], _DIMS[mode], preferred_element_type=F32,
                                precision=lax.Precision.HIGHEST)
        else:
            p = _dot(a_ref[...].astype(BF16), b_ref[...].astype(BF16), _DIMS[mode])

        def finish(acc):
            outs = (acc,) if epi is None else epi(acc, *[r[...] for r in extra_refs])
            for r, o in zip(out_refs, outs):
                r[...] = o.astype(r.dtype)

        if nk == 1:
            finish(p)
        else:
            acc_ref = refs[-1]
            kk = pl.program_id(2)

            @pl.when(kk == 0)
            def _():
                acc_ref[...] = p

            @pl.when(kk > 0)
            def _():
                acc_ref[...] += p

            @pl.when(kk == nk - 1)
            def _():
                finish(acc_ref[...])

    if mode == "tn":
        a_spec = pl.BlockSpec((tk, tm), lambda i, j, kk: (kk, i))
    else:
        a_spec = pl.BlockSpec((tm, tk), lambda i, j, kk: (i, kk))
    if mode == "nt":
        b_spec = pl.BlockSpec((tn, tk), lambda i, j, kk: (j, kk))
    else:
        b_spec = pl.BlockSpec((tk, tn), lambda i, j, kk: (kk, j))
    tile = pl.BlockSpec((tm, tn), lambda i, j, kk: (i, j))
    if blocked_out:
        o_shape, o_spec = (n // tn, m, tn), pl.BlockSpec((None, tm, tn), lambda i, j, kk: (j, i, 0))
    else:
        o_shape, o_spec = (m, n), tile
    out = _call(
        body, name=name, grid=grid,
        in_specs=[a_spec, b_spec] + [tile] * n_extra,
        out_specs=[o_spec] * n_out,
        out_shape=[jax.ShapeDtypeStruct(o_shape, dt) for dt in out_dtypes],
        scratch_shapes=[pltpu.VMEM((tm, tn), F32)] if nk > 1 else [],
        compiler_params=_params("parallel", "parallel", "arbitrary"),
    )(a, b, *extra)
    return out[0] if n_out == 1 else out


def _tile(width, col=0, rows=TM):
    return pl.BlockSpec((rows, width), lambda i: (i, col))


def _vec(width):
    return pl.BlockSpec((1, width), lambda i: (0, 0))


def _rstd(xf):
    return lax.rsqrt(jnp.mean(xf * xf, axis=-1, keepdims=True) + EPS)


def _colsum(v):
    return jnp.sum(v, axis=0, keepdims=True)


def _accumulate(refs, vals):
    first = pl.program_id(0) == 0

    @pl.when(first)
    def _():
        for r, v in zip(refs, vals):
            r[...] = v

    @pl.when(jnp.logical_not(first))
    def _():
        for r, v in zip(refs, vals):
            r[...] += v


def _prenorm_fwd(x, g, sc, sh):
    s, d = x.shape

    def body(x_ref, g_ref, sc_ref, sh_ref, h_ref):
        xf = x_ref[...]
        y = (xf * _rstd(xf)) * g_ref[...]
        h_ref[...] = (y * (1.0 + sc_ref[...]) + sh_ref[...]).astype(h_ref.dtype)

    return _call(body, name="prenorm_fwd", grid=(s // TM,),
                 in_specs=[_tile(d), _vec(d), _vec(d), _vec(d)], out_specs=_tile(d),
                 out_shape=jax.ShapeDtypeStruct((s, d), BF16), compiler_params=_params("parallel"))(x, g, sc, sh)


def _prenorm_bwd(dh, x, g, sc, dres):
    s, d = x.shape

    def body(dh_ref, x_ref, g_ref, sc_ref, dres_ref, dx_ref, dsh_ref, dsc_ref, dg_ref):
        xf, dhf = x_ref[...], dh_ref[...]
        rstd = _rstd(xf)
        xhat = xf * rstd
        one_sc = 1.0 + sc_ref[...]
        dxhat = dhf * (g_ref[...] * one_sc)
        dx_ref[...] = dres_ref[...] + rstd * (dxhat - xhat * jnp.mean(dxhat * xhat, axis=-1, keepdims=True))
        dhx = dhf * xhat
        _accumulate((dsh_ref, dsc_ref, dg_ref), (_colsum(dhf), _colsum(dhx) * g_ref[...], _colsum(dhx) * one_sc))

    vec_out = jax.ShapeDtypeStruct((1, d), F32)
    return _call(body, name="prenorm_bwd", grid=(s // TM,),
                 in_specs=[_tile(d), _tile(d), _vec(d), _vec(d), _tile(d)],
                 out_specs=[_tile(d), _vec(d), _vec(d), _vec(d)],
                 out_shape=[jax.ShapeDtypeStruct((s, d), F32), vec_out, vec_out, vec_out],
                 compiler_params=_params("arbitrary"))(dh, x, g, sc, dres)


def _postnorm_fwd(xres, m, g, gt):
    s, d = m.shape

    def body(x_ref, m_ref, g_ref, gt_ref, o_ref):
        mf = m_ref[...]
        o_ref[...] = x_ref[...] + gt_ref[...] * ((mf * _rstd(mf)) * g_ref[...])

    return _call(body, name="postnorm_fwd", grid=(s // TM,),
                 in_specs=[_tile(d), _tile(d), _vec(d), _vec(d)], out_specs=_tile(d),
                 out_shape=jax.ShapeDtypeStruct((s, d), F32), compiler_params=_params("parallel"))(xres, m, g, gt)


def _postnorm_bwd(dxn, m, g, gt):
    s, d = m.shape

    def body(dx_ref, m_ref, g_ref, gt_ref, dm_ref, dgt_ref, dg_ref):
        mf, dxf = m_ref[...], dx_ref[...]
        rstd = _rstd(mf)
        mhat = mf * rstd
        dmhat = dxf * (gt_ref[...] * g_ref[...])
        dm_ref[...] = (rstd * (dmhat - mhat * jnp.mean(dmhat * mhat, axis=-1, keepdims=True))).astype(dm_ref.dtype)
        dxm = _colsum(dxf * mhat)
        _accumulate((dgt_ref, dg_ref), (dxm * g_ref[...], dxm * gt_ref[...]))

    vec_out = jax.ShapeDtypeStruct((1, d), F32)
    return _call(body, name="postnorm_bwd", grid=(s // TM,),
                 in_specs=[_tile(d), _tile(d), _vec(d), _vec(d)], out_specs=[_tile(d), _vec(d), _vec(d)],
                 out_shape=[jax.ShapeDtypeStruct((s, d), BF16), vec_out, vec_out],
                 compiler_params=_params("arbitrary"))(dxn, m, g, gt)


def _loss(y, target):
    s, d = y.shape

    def body(y_ref, t_ref, dy_ref, sq_ref):
        err = y_ref[...] - t_ref[...]
        dy_ref[...] = err * (1.0 / d)
        tot = jnp.sum(_colsum(err * err), axis=1, keepdims=True)
        _accumulate((sq_ref,), (jnp.broadcast_to(tot, (1, LANES)),))

    return _call(body, name="loss", grid=(s // TM,), in_specs=[_tile(d), _tile(d)],
                 out_specs=[_tile(d), _vec(LANES)],
                 out_shape=[jax.ShapeDtypeStruct((s, d), F32), jax.ShapeDtypeStruct((1, LANES), F32)],
                 compiler_params=_params("arbitrary"))(y, target)


def _sigmoid(v):
    return 1.0 / (1.0 + jnp.exp(-v))


def _gate_fwd(proj, y_conv, y_attn):
    s, d = y_conv.shape
    ga_col, gb_col = (IN_COLS - 2 * d) // d, (IN_COLS - d) // d

    def body(ga_ref, gb_ref, yc_ref, ya_ref, o_ref):
        o_ref[...] = (_sigmoid(ga_ref[...]) * yc_ref[...] + _sigmoid(gb_ref[...]) * ya_ref[...]).astype(o_ref.dtype)

    return _call(body, name="gate_fwd", grid=(s // TM,),
                 in_specs=[_tile(d, ga_col), _tile(d, gb_col), _tile(d), _tile(d)], out_specs=_tile(d),
                 out_shape=jax.ShapeDtypeStruct((s, d), BF16),
                 compiler_params=_params("parallel"))(proj, proj, y_conv, y_attn)


def _gate_bwd(dmerged, proj, y_conv, y_attn):
    s, d = y_conv.shape
    ga_col, gb_col = (IN_COLS - 2 * d) // d, (IN_COLS - d) // d

    def body(dm_ref, ga_ref, gb_ref, yc_ref, ya_ref, dyc_ref, dya_ref, dga_ref, dgb_ref):
        dm = dm_ref[...]
        sa, sb = _sigmoid(ga_ref[...]), _sigmoid(gb_ref[...])
        dyc_ref[...] = (dm * sa).astype(BF16)
        dya_ref[...] = (dm * sb).astype(BF16)
        dga_ref[...] = (dm * yc_ref[...] * (sa * (1.0 - sa))).astype(BF16)
        dgb_ref[...] = (dm * ya_ref[...] * (sb * (1.0 - sb))).astype(BF16)

    out = jax.ShapeDtypeStruct((s, d), BF16)
    return _call(body, name="gate_bwd", grid=(s // TM,),
                 in_specs=[_tile(d), _tile(d, ga_col), _tile(d, gb_col), _tile(d), _tile(d)],
                 out_specs=[_tile(d)] * 4, out_shape=[out] * 4,
                 compiler_params=_params("parallel"))(dmerged, proj, proj, y_conv, y_attn)


def _shift_down(prev8, cur, by):
    ext = jnp.concatenate([prev8, cur], axis=0)
    return pltpu.roll(ext, by, 0)[8:]


def _shift_up(cur, next8, by):
    ext = jnp.concatenate([cur, next8], axis=0)
    return pltpu.roll(ext, ext.shape[0] - by, 0)[:cur.shape[0]]


def _conv_fwd(proj, conv_w):
    s, w = proj.shape[0], CONV_WIDTH
    per8 = TM // 8

    def prev(col):
        return pl.BlockSpec((8, w), lambda i: (jnp.maximum(i * per8 - 1, 0), col))

    def body(bg_ref, cg_ref, u_ref, cgp_ref, up_ref, w_ref, o_ref):
        vv = cg_ref[...] * u_ref[...]
        pv = cgp_ref[...] * up_ref[...] * jnp.where(pl.program_id(0) > 0, 1.0, 0.0)
        y = w_ref[0:1, :] * _shift_down(pv, vv, 2) + w_ref[1:2, :] * _shift_down(pv, vv, 1) + w_ref[2:3, :] * vv
        o_ref[...] = (bg_ref[...] * y).astype(o_ref.dtype)

    return _call(body, name="conv_fwd", grid=(s // TM,),
                 in_specs=[_tile(w, 0), _tile(w, 1), _tile(w, 2), prev(1), prev(2),
                           pl.BlockSpec((3, w), lambda i: (0, 0))],
                 out_specs=_tile(w), out_shape=jax.ShapeDtypeStruct((s, w), BF16),
                 compiler_params=_params("parallel"))(proj, proj, proj, proj, proj, conv_w)


def _conv_bwd(dyc, proj, conv_w):
    s, w = proj.shape[0], CONV_WIDTH
    per8 = TM // 8
    n_tiles = s // TM

    def prev(col):
        return pl.BlockSpec((8, w), lambda i: (jnp.maximum(i * per8 - 1, 0), col))

    def nxt(col):
        return pl.BlockSpec((8, w), lambda i: (jnp.minimum((i + 1) * per8, s // 8 - 1), col))

    def body(dyc_ref, bg_ref, cg_ref, u_ref, cgp_ref, up_ref, dycn_ref, bgn_ref, w_ref,
             dbg_ref, dcg_ref, du_ref, dw0_ref, dw1_ref, dw2_ref):
        i = pl.program_id(0)
        cg, u = cg_ref[...], u_ref[...]
        vv = cg * u
        pv = cgp_ref[...] * up_ref[...] * jnp.where(i > 0, 1.0, 0.0)
        v1, v2 = _shift_down(pv, vv, 1), _shift_down(pv, vv, 2)
        w0, w1, w2 = w_ref[0:1, :], w_ref[1:2, :], w_ref[2:3, :]
        dyc_t = dyc_ref[...]
        dbg_ref[...] = (dyc_t * (w0 * v2 + w1 * v1 + w2 * vv)).astype(BF16)
        dy = dyc_t * bg_ref[...]
        dyn = dycn_ref[...] * bgn_ref[...] * jnp.where(i < n_tiles - 1, 1.0, 0.0)
        dvv = w2 * dy + w1 * _shift_up(dy, dyn, 1) + w0 * _shift_up(dy, dyn, 2)
        dcg_ref[...] = (dvv * u).astype(BF16)
        du_ref[...] = (dvv * cg).astype(BF16)
        _accumulate((dw0_ref, dw1_ref, dw2_ref), (_colsum(dy * v2), _colsum(dy * v1), _colsum(dy * vv)))

    act = jax.ShapeDtypeStruct((s, w), BF16)
    tap = jax.ShapeDtypeStruct((1, w), F32)
    return _call(body, name="conv_bwd", grid=(n_tiles,),
                 in_specs=[_tile(w), _tile(w, 0), _tile(w, 1), _tile(w, 2), prev(1), prev(2), nxt(0), nxt(0),
                           pl.BlockSpec((3, w), lambda i: (0, 0))],
                 out_specs=[_tile(w)] * 3 + [_vec(w)] * 3, out_shape=[act] * 3 + [tap] * 3,
                 compiler_params=_params("arbitrary"))(dyc, proj, proj, proj, proj, proj, dyc, proj, conv_w)


Q_COL = 3 * CONV_WIDTH // LANES
K_COL = Q_COL + ATTN_WIDTH // LANES
V_COL = K_COL + ATTN_WIDTH // LANES
SCALE = HEAD_DIM ** -0.5


def _head_lanes(hh):
    lane = lax.broadcasted_iota(jnp.int32, (1, LANES), 1)
    return jnp.where((lane >= hh * HEAD_DIM) & (lane < (hh + 1) * HEAD_DIM), 1.0, 0.0)


def _tri(width, keep):
    j = lax.broadcasted_iota(jnp.int32, (TK, width), 0)
    s = lax.broadcasted_iota(jnp.int32, (TK, width), 1)
    return jnp.where((s >= TK) | keep(j, s), 1.0, 0.0).astype(BF16)


def _split_dot(v, tri):
    hi = v.astype(BF16)
    lo = (v - hi.astype(F32)).astype(BF16)
    return _dot(hi, tri) + _dot(lo, tri)


def _logits(qm, kblk, ks, tpos):
    z = _dot(qm, kblk, NT)
    mask = (ks + lax.broadcasted_iota(jnp.int32, (1, TK), 1)) < tpos
    e = jnp.exp(-jnp.abs(z))
    l0 = jnp.where(mask, -(jnp.maximum(z, 0.0) + jnp.log(1.0 + e)), 0.0)
    return z, mask, e, l0


def _neg_softplus(z):
    return jnp.minimum(-z, 0.0) - jnp.log(1.0 + jnp.exp(-jnp.abs(z)))


W2 = 2 * TK
PIPE = 4


def _pair_rows(ref, kb, lm):
    blk = ref[pl.ds(pl.multiple_of(kb * TK, TK), TK), :]
    return jnp.concatenate([(blk * lm[0]).astype(BF16), (blk * lm[1]).astype(BF16)], axis=0)


def _pair_tri(keep):
    j = lax.broadcasted_iota(jnp.int32, (W2, 2 * W2), 0)
    s = lax.broadcasted_iota(jnp.int32, (W2, 2 * W2), 1)
    same_head = (j >= TK) == ((s & (W2 - 1)) >= TK)
    return jnp.where(same_head & ((s >= W2) | keep(j & (TK - 1), s & (TK - 1))), 1.0, 0.0).astype(BF16)


def _attn_fwd(proj):
    s = proj.shape[0]
    nq = s // TQ
    diag = TQ // TK
    n_ch = TQ // CH
    assert s // TK <= TK and diag % PIPE == 0

    def body(q_ref, k_ref, v_ref, o_ref, rs_ref, qb_scr, tri_scr, z_scr, l0_scr, cs_scr, a_scr, r_scr, rall_scr, acc_scr):
        qi = pl.program_id(1)
        lm = [_head_lanes(0), _head_lanes(1)]
        lane = lax.broadcasted_iota(jnp.int32, (CH, W2), 1) & (TK - 1)
        row = lax.broadcasted_iota(jnp.int32, (CH, 1), 0)
        col = lax.broadcasted_iota(jnp.int32, (1, W2), 1) & (TK - 1)
        qb_scr[...] = (q_ref[...] * SCALE).astype(BF16)
        tri_scr[...] = _pair_tri(lambda j, ss: j > ss)
        r_scr[...] = jnp.zeros_like(r_scr)
        rall_scr[...] = jnp.zeros_like(rall_scr)
        acc_scr[...] = jnp.zeros_like(acc_scr)

        def causal(kb, c):
            return (kb * TK + col) < (qi * TQ + c * CH + row)

        def logits(kb, zb):
            z_scr[zb] = _dot(qb_scr[...], _pair_rows(k_ref, kb, lm), NT)

        def log_one_minus_beta(kb, zb, lb, masked):
            for c in range(n_ch):
                rows = slice(c * CH, (c + 1) * CH)
                l0 = _neg_softplus(z_scr[zb, rows, :])
                if masked:
                    l0 = jnp.where(causal(kb, c), l0, 0.0)
                l0_scr[lb, rows, :] = l0.astype(BF16)

        def sums(lb):
            cs_scr[...] = _dot(l0_scr[lb], tri_scr[...])

        def weights(kb, zb, lb, ab, masked):
            for c in range(n_ch):
                rows = slice(c * CH, (c + 1) * CH)
                near = r_scr[rows, :]
                a = jnp.exp(l0_scr[lb, rows, :].astype(F32) + z_scr[zb, rows, :] + cs_scr[rows, :W2] + near)
                if masked:
                    a = jnp.where(causal(kb, c), a, 0.0)
                a_scr[ab, rows, :] = a.astype(BF16)
                rall_scr[rows, :] = jnp.where(lane == kb, near, rall_scr[rows, :])
                r_scr[rows, :] = near + cs_scr[rows, W2:]

        def weighted_values(kb, ab):
            acc_scr[...] += _dot(a_scr[ab], _pair_rows(v_ref, kb, lm))

        def diag_step(it, carry):
            kb = (qi + 1) * diag - 1 - it
            logits(kb, 0)
            log_one_minus_beta(kb, 0, 0, True)
            sums(0)
            weights(kb, 0, 0, 0, True)
            weighted_values(kb, 0)
            return carry

        lax.fori_loop(0, diag, diag_step, 0)

        n = qi * diag

        def block(j):
            return jnp.maximum(n - 1 - j, 0)

        a_scr[...] = jnp.zeros_like(a_scr)
        logits(block(0), 0)
        logits(block(1), 1)
        log_one_minus_beta(block(0), 0, 0, False)

        def trip(m, carry):
            for u in range(PIPE):
                j = PIPE * m + u
                weighted_values(block(j - 1), (u - 1) % 2)
                sums(u % 2)
                logits(block(j + 2), (u + 2) % PIPE)
                log_one_minus_beta(block(j + 1), (u + 1) % PIPE, (u + 1) % 2, False)
                weights(block(j), u % PIPE, u % 2, u % 2, False)
            return carry

        lax.fori_loop(0, n // PIPE, trip, 0)
        weighted_values(block(n - 1), (PIPE - 1) % 2)
        rs_ref[...] = rall_scr[...]
        o_ref[...] = acc_scr[...]

    return _call(
        body, name="attn_fwd", grid=(N_HEADS // 2, nq),
        in_specs=[pl.BlockSpec((TQ, LANES), lambda p, qi: (qi, Q_COL + p)),
                  pl.BlockSpec((s, LANES), lambda p, qi: (0, K_COL + p)),
                  pl.BlockSpec((s, LANES), lambda p, qi: (0, V_COL + p))],
        out_specs=[pl.BlockSpec((TQ, LANES), lambda p, qi: (qi, p)),
                   pl.BlockSpec((TQ, W2), lambda p, qi: (qi, p))],
        out_shape=[jax.ShapeDtypeStruct((s, ATTN_WIDTH), F32), jax.ShapeDtypeStruct((s, N_HEADS // 2 * W2), F32)],
        scratch_shapes=[pltpu.VMEM((TQ, LANES), BF16), pltpu.VMEM((W2, 2 * W2), BF16),
                        pltpu.VMEM((PIPE, TQ, W2), F32), pltpu.VMEM((2, TQ, W2), BF16),
                        pltpu.VMEM((TQ, 2 * W2), F32), pltpu.VMEM((2, TQ, W2), BF16),
                        pltpu.VMEM((TQ, W2), F32), pltpu.VMEM((TQ, W2), F32), pltpu.VMEM((TQ, LANES), F32)],
        compiler_params=_params("parallel", "arbitrary"),
    )(proj, proj, proj)


def _attn_fwd_rows(proj):
    s = proj.shape[0]
    nq = s // TQ
    diag = TQ // TK
    n_ch = 2 * TQ // CH
    assert s // TK <= LANES

    def body(q_ref, k_ref, v_ref, o_ref, rs_ref, q2_scr, z_scr, l0_scr, cs_scr, a_scr, r_scr, rall_scr, acc_scr):
        qi = pl.program_id(1)
        lm = [_head_lanes(0), _head_lanes(1)]
        tri = _tri(2 * TK, lambda j, ss: j > ss)
        lane = lax.broadcasted_iota(jnp.int32, (CH, LANES), 1)
        row = lax.broadcasted_iota(jnp.int32, (CH, 1), 0)
        col = lax.broadcasted_iota(jnp.int32, (1, TK), 1)
        for hh in range(2):
            q2_scr[hh * TQ:(hh + 1) * TQ, :] = (q_ref[...] * (SCALE * lm[hh])).astype(BF16)
        r_scr[...] = jnp.zeros_like(r_scr)
        rall_scr[...] = jnp.zeros_like(rall_scr)
        acc_scr[...] = jnp.zeros_like(acc_scr)

        def step(kb, masked):
            ks = pl.multiple_of(kb * TK, TK)
            kblk = k_ref[pl.ds(ks, TK), :].astype(BF16)
            vf = v_ref[pl.ds(ks, TK), :]
            v2 = jnp.concatenate([(vf * lm[0]).astype(BF16), (vf * lm[1]).astype(BF16)], axis=0)
            z_scr[...] = _dot(q2_scr[...], kblk, NT)

            def causal(c):
                return (ks + col) < (qi * TQ + (c * CH) % TQ + row)

            for c in range(n_ch):
                rows = slice(c * CH, (c + 1) * CH)
                l0 = _neg_softplus(z_scr[rows, :])
                if masked:
                    l0 = jnp.where(causal(c), l0, 0.0)
                l0_scr[rows, :] = l0.astype(BF16)
            cs_scr[...] = _dot(l0_scr[...], tri)
            for c in range(n_ch):
                rows = slice(c * CH, (c + 1) * CH)
                hh, r0 = (c * CH) // TQ, (c * CH) % TQ
                near = r_scr[rows, :]
                a = jnp.exp(l0_scr[rows, :].astype(F32) + z_scr[rows, :] + cs_scr[rows, :TK] + near)
                if masked:
                    a = jnp.where(causal(c), a, 0.0)
                a_scr[r0:r0 + CH, hh * TK:(hh + 1) * TK] = a.astype(BF16)
                rall_scr[rows, :] = jnp.where(lane == kb, near, rall_scr[rows, :])
                r_scr[rows, :] = near + cs_scr[rows, TK:]
            acc_scr[...] += _dot(a_scr[...], v2)

        def diag_step(it, carry):
            step((qi + 1) * diag - 1 - it, True)
            return carry

        def inner_step(it, carry):
            step(qi * diag - 1 - it, False)
            return carry

        lax.fori_loop(0, diag, diag_step, 0)
        lax.fori_loop(0, qi * diag, inner_step, 0)
        for hh in range(2):
            rs_ref[hh] = rall_scr[hh * TQ:(hh + 1) * TQ, :]
        o_ref[...] = acc_scr[...]

    return _call(
        body, name="attn_fwd", grid=(N_HEADS // 2, nq),
        in_specs=[pl.BlockSpec((TQ, LANES), lambda p, qi: (qi, Q_COL + p)),
                  pl.BlockSpec((s, LANES), lambda p, qi: (0, K_COL + p)),
                  pl.BlockSpec((s, LANES), lambda p, qi: (0, V_COL + p))],
        out_specs=[pl.BlockSpec((TQ, LANES), lambda p, qi: (qi, p)),
                   pl.BlockSpec((2, TQ, LANES), lambda p, qi: (p, qi, 0))],
        out_shape=[jax.ShapeDtypeStruct((s, ATTN_WIDTH), F32), jax.ShapeDtypeStruct((N_HEADS, s, LANES), F32)],
        scratch_shapes=[pltpu.VMEM((2 * TQ, LANES), BF16), pltpu.VMEM((2 * TQ, TK), F32),
                        pltpu.VMEM((2 * TQ, TK), BF16), pltpu.VMEM((2 * TQ, 2 * TK), F32),
                        pltpu.VMEM((TQ, 2 * TK), BF16), pltpu.VMEM((2 * TQ, LANES), F32),
                        pltpu.VMEM((2 * TQ, LANES), F32), pltpu.VMEM((TQ, LANES), F32)],
        compiler_params=_params("parallel", "arbitrary"),
    )(proj, proj, proj)


def _attn_bwd(proj, do, rsave):
    s = proj.shape[0]
    nq = s // TQ
    diag = TQ // TK
    n_ch = TQ // CH
    assert diag % PIPE == 0

    def body(q_ref, k_ref, v_ref, do_ref, rs_ref, dq_ref, dk_ref, dv_ref,
             qb_scr, dob_scr, after_scr, before_scr, z_scr, da_scr, l0_scr, beta_scr, cs_scr, a_scr, g_scr, cg_scr,
             dz_scr, pg_scr, dq_scr, dk_scr, dv_scr):
        qi = pl.program_id(1)

        @pl.when(qi == 0)
        def _():
            dk_scr[...] = jnp.zeros_like(dk_scr)
            dv_scr[...] = jnp.zeros_like(dv_scr)

        lm = [_head_lanes(0), _head_lanes(1)]
        lane = lax.broadcasted_iota(jnp.int32, (CH, TK), 1)
        row = lax.broadcasted_iota(jnp.int32, (CH, 1), 0)
        col = lax.broadcasted_iota(jnp.int32, (1, W2), 1) & (TK - 1)
        qb_scr[...] = (q_ref[...] * SCALE).astype(BF16)
        dob_scr[...] = do_ref[...].astype(BF16)
        after_scr[...] = _pair_tri(lambda j, ss: j > ss)[:, :W2]
        before_scr[...] = _pair_tri(lambda j, ss: j < ss)
        pg_scr[...] = jnp.zeros_like(pg_scr)
        dq_scr[...] = jnp.zeros_like(dq_scr)
        dz_scr[...] = jnp.zeros_like(dz_scr)

        def causal(kb, c):
            return (kb * TK + col) < (qi * TQ + c * CH + row)

        def logits(kb, zb):
            z_scr[zb] = _dot(qb_scr[...], _pair_rows(k_ref, kb, lm), NT)

        def do_dot_v(kb, db):
            da_scr[db] = _dot(dob_scr[...], _pair_rows(v_ref, kb, lm), NT)

        def gates(kb, zb, lb, bb, masked):
            for c in range(n_ch):
                rows = slice(c * CH, (c + 1) * CH)
                z = z_scr[zb, rows, :]
                e = jnp.exp(-jnp.abs(z))
                w = 1.0 + e
                l0 = jnp.minimum(-z, 0.0) - jnp.log(w)
                if masked:
                    l0 = jnp.where(causal(kb, c), l0, 0.0)
                l0_scr[lb, rows, :] = l0.astype(BF16)
                rinv = 1.0 / w
                beta_scr[bb, rows, :] = jnp.where(z >= 0.0, rinv, e * rinv)

        def suffix_sums(lb):
            cs_scr[...] = _dot(l0_scr[lb], after_scr[...])

        def weights(kb, zb, lb, db, ab, masked):
            for c in range(n_ch):
                rows = slice(c * CH, (c + 1) * CH)
                keep = (kb * TK + lane) < (qi * TQ + c * CH + row) if masked else None
                for hh in range(2):
                    cols = slice(hh * TK, (hh + 1) * TK)
                    near = jnp.sum(jnp.where(lane == kb, rs_ref[rows, cols], 0.0), axis=1, keepdims=True)
                    a = jnp.exp(l0_scr[lb, rows, cols].astype(F32) + z_scr[zb, rows, cols] + cs_scr[rows, cols] + near)
                    if masked:
                        a = jnp.where(keep, a, 0.0)
                    a_scr[ab, rows, cols] = a.astype(BF16)
                    g_scr[ab, rows, cols] = (a * da_scr[db, rows, cols]).astype(BF16)

        def prefix_sums(ab):
            cg_scr[...] = _dot(g_scr[ab], before_scr[...])

        def dlogits(kb, ab, bb, zb2, masked):
            for c in range(n_ch):
                rows = slice(c * CH, (c + 1) * CH)
                before = cg_scr[rows, :W2] + pg_scr[rows, :]
                beta = beta_scr[bb, rows, :]
                dz = g_scr[ab, rows, :].astype(F32) * (1.0 - beta) - beta * before
                if masked:
                    dz = jnp.where(causal(kb, c), dz, 0.0)
                dz_scr[zb2, rows, :] = dz.astype(BF16)
                pg_scr[rows, :] += cg_scr[rows, W2:]

        def fold(t):
            return t[:TK, :] * lm[0] + t[TK:, :] * lm[1]

        def dq_dk(kb, zb2):
            dq_scr[...] += _dot(dz_scr[zb2], _pair_rows(k_ref, kb, lm))
            dk_scr[pl.ds(pl.multiple_of(kb * TK, TK), TK), :] += fold(_dot(dz_scr[zb2], qb_scr[...], TN))

        def dv(kb, ab):
            dv_scr[pl.ds(pl.multiple_of(kb * TK, TK), TK), :] += fold(_dot(a_scr[ab], dob_scr[...], TN))

        n = qi * diag

        def block(j):
            return jnp.clip(j, 0, jnp.maximum(n - 1, 0))

        logits(block(0), 0)
        logits(block(1), 1)
        logits(block(2), 2)
        do_dot_v(block(0), 0)
        do_dot_v(block(1), 1)
        gates(block(0), 0, 0, 0, False)
        gates(block(1), 1, 1, 1, False)
        suffix_sums(0)
        weights(block(0), 0, 0, 0, 0, False)

        def trip(m, carry):
            for u in range(PIPE):
                t = PIPE * m + u
                dq_dk(block(t - 1), (u - 1) % 2)
                dv(block(t), u % 2)
                prefix_sums(u % 2)
                suffix_sums((u + 1) % 2)
                logits(block(t + 3), (u + 3) % PIPE)
                do_dot_v(block(t + 2), u % 2)
                gates(block(t + 2), (u + 2) % PIPE, u % 2, (u + 2) % PIPE, False)
                weights(block(t + 1), (u + 1) % PIPE, (u + 1) % 2, (u + 1) % 2, (u + 1) % 2, False)
                dlogits(block(t), u % 2, u % PIPE, u % 2, False)
            return carry

        lax.fori_loop(0, n // PIPE, trip, 0)
        dq_dk(block(n - 1), (PIPE - 1) % 2)

        def diag_step(it, carry):
            kb = n + it
            logits(kb, 0)
            do_dot_v(kb, 0)
            gates(kb, 0, 0, 0, True)
            suffix_sums(0)
            weights(kb, 0, 0, 0, 0, True)
            dv(kb, 0)
            prefix_sums(0)
            dlogits(kb, 0, 0, 0, True)
            dq_dk(kb, 0)
            return carry

        lax.fori_loop(0, diag, diag_step, 0)
        dq_ref[...] = (dq_scr[...] * SCALE).astype(dq_ref.dtype)

        @pl.when(qi == nq - 1)
        def _():
            dk_ref[...] = dk_scr[...].astype(dk_ref.dtype)
            dv_ref[...] = dv_scr[...].astype(dv_ref.dtype)

    def rows(c0):
        return pl.BlockSpec((TQ, LANES), lambda p, qi: (qi, c0 + p))

    def whole(c0):
        return pl.BlockSpec((s, LANES), lambda p, qi: (0, c0 + p))

    def f32(*shape):
        return pltpu.VMEM(shape, F32)

    def bf16(*shape):
        return pltpu.VMEM(shape, BF16)

    out = jax.ShapeDtypeStruct((s, ATTN_WIDTH), BF16)
    return _call(
        body, name="attn_bwd", grid=(N_HEADS // 2, nq),
        in_specs=[rows(Q_COL), whole(K_COL), whole(V_COL), rows(0), pl.BlockSpec((TQ, W2), lambda p, qi: (qi, p))],
        out_specs=[rows(0), whole(0), whole(0)], out_shape=[out] * 3,
        scratch_shapes=[bf16(TQ, LANES), bf16(TQ, LANES), bf16(W2, W2), bf16(W2, 2 * W2),
                        f32(PIPE, TQ, W2), f32(2, TQ, W2), bf16(2, TQ, W2), f32(PIPE, TQ, W2), f32(TQ, W2),
                        bf16(2, TQ, W2), bf16(2, TQ, W2), f32(TQ, 2 * W2), bf16(2, TQ, W2),
                        f32(TQ, W2), f32(TQ, LANES), f32(s, LANES), f32(s, LANES)],
        compiler_params=_params("arbitrary", "arbitrary"),
    )(proj, proj, proj, do, rsave)


def _attn_bwd_rows(proj, do, rsave):
    s = proj.shape[0]
    nq = s // TQ

    diag = TQ // TK
    n_ch = 2 * TQ // CH

    def body(q_ref, k_ref, v_ref, do_ref, rs_ref, dq_ref, dk_ref, dv_ref,
             q2_scr, do2_scr, z_scr, da_scr, l0_scr, beta_scr, cs_scr, a_scr, g_scr, cg_scr, dz_scr,
             pg_scr, dq_scr, dk_scr, dv_scr):
        qi = pl.program_id(1)

        @pl.when(qi == 0)
        def _():
            dk_scr[...] = jnp.zeros_like(dk_scr)
            dv_scr[...] = jnp.zeros_like(dv_scr)

        lm = [_head_lanes(0), _head_lanes(1)]
        tri_after = _tri(TK, lambda j, ss: j > ss)
        tri_before = _tri(2 * TK, lambda j, ss: j < ss)
        lane = lax.broadcasted_iota(jnp.int32, (CH, LANES), 1)
        row = lax.broadcasted_iota(jnp.int32, (CH, 1), 0)
        col = lax.broadcasted_iota(jnp.int32, (1, TK), 1)
        for hh in range(2):
            q2_scr[hh * TQ:(hh + 1) * TQ, :] = (q_ref[...] * (SCALE * lm[hh])).astype(BF16)
            do2_scr[hh * TQ:(hh + 1) * TQ, :] = (do_ref[...] * lm[hh]).astype(BF16)
        pg_scr[...] = jnp.zeros_like(pg_scr)
        dq_scr[...] = jnp.zeros_like(dq_scr)

        def step(kb, masked):
            ks = pl.multiple_of(kb * TK, TK)
            kblk = k_ref[pl.ds(ks, TK), :].astype(BF16)
            vblk = v_ref[pl.ds(ks, TK), :].astype(BF16)
            z_scr[...] = _dot(q2_scr[...], kblk, NT)
            da_scr[...] = _dot(do2_scr[...], vblk, NT)

            def causal(c):
                return (ks + col) < (qi * TQ + (c * CH) % TQ + row)

            for c in range(n_ch):
                rows = slice(c * CH, (c + 1) * CH)
                z = z_scr[rows, :]
                e = jnp.exp(-jnp.abs(z))
                w = 1.0 + e
                l0 = jnp.minimum(-z, 0.0) - jnp.log(w)
                if masked:
                    l0 = jnp.where(causal(c), l0, 0.0)
                l0_scr[rows, :] = l0.astype(BF16)
                rinv = 1.0 / w
                beta_scr[rows, :] = jnp.where(z >= 0.0, rinv, e * rinv)
            cs_scr[...] = _dot(l0_scr[...], tri_after)
            for c in range(n_ch):
                rows = slice(c * CH, (c + 1) * CH)
                hh, r0 = (c * CH) // TQ, (c * CH) % TQ
                near = jnp.sum(jnp.where(lane == kb, rs_ref[hh, r0:r0 + CH, :], 0.0), axis=1, keepdims=True)
                a = jnp.exp(l0_scr[rows, :].astype(F32) + z_scr[rows, :] + cs_scr[rows, :] + near)
                if masked:
                    a = jnp.where(causal(c), a, 0.0)
                a_scr[rows, :] = a.astype(BF16)
                g_scr[rows, :] = (a * da_scr[rows, :]).astype(BF16)
            cg_scr[...] = _dot(g_scr[...], tri_before)
            for c in range(n_ch):
                rows = slice(c * CH, (c + 1) * CH)
                before = cg_scr[rows, :TK] + pg_scr[rows, :]
                beta = beta_scr[rows, :]
                dz = g_scr[rows, :].astype(F32) * (1.0 - beta) - beta * before
                if masked:
                    dz = jnp.where(causal(c), dz, 0.0)
                dz_scr[rows, :] = dz.astype(BF16)
                pg_scr[rows, :] += cg_scr[rows, TK:]
            dq_scr[...] += _dot(dz_scr[...], kblk)
            dk_scr[pl.ds(ks, TK), :] += _dot(dz_scr[...], q2_scr[...], TN)
            dv_scr[pl.ds(ks, TK), :] += _dot(a_scr[...], do2_scr[...], TN)

        def inner_step(kb, carry):
            step(kb, False)
            return carry

        def diag_step(it, carry):
            step(qi * diag + it, True)
            return carry

        lax.fori_loop(0, qi * diag, inner_step, 0)
        lax.fori_loop(0, diag, diag_step, 0)
        dq = dq_scr[:TQ, :] * lm[0] + dq_scr[TQ:, :] * lm[1]
        dq_ref[...] = (dq * SCALE).astype(dq_ref.dtype)

        @pl.when(qi == nq - 1)
        def _():
            dk_ref[...] = dk_scr[...].astype(dk_ref.dtype)
            dv_ref[...] = dv_scr[...].astype(dv_ref.dtype)

    stacked_f32 = pltpu.VMEM((2 * TQ, TK), F32)
    stacked_bf16 = pltpu.VMEM((2 * TQ, TK), BF16)

    def rows(c0):
        return pl.BlockSpec((TQ, LANES), lambda p, qi: (qi, c0 + p))

    def whole(c0):
        return pl.BlockSpec((s, LANES), lambda p, qi: (0, c0 + p))

    out = jax.ShapeDtypeStruct((s, ATTN_WIDTH), BF16)
    return _call(
        body, name="attn_bwd", grid=(N_HEADS // 2, nq),
        in_specs=[rows(Q_COL), whole(K_COL), whole(V_COL), rows(0),
                  pl.BlockSpec((2, TQ, LANES), lambda p, qi: (p, qi, 0))],
        out_specs=[rows(0), whole(0), whole(0)], out_shape=[out] * 3,
        scratch_shapes=[stacked_bf16, stacked_bf16, stacked_f32, stacked_f32, stacked_bf16, stacked_f32, stacked_f32,
                        stacked_bf16, stacked_bf16, pltpu.VMEM((2 * TQ, 2 * TK), F32), stacked_bf16,
                        stacked_f32, stacked_f32, pltpu.VMEM((s, LANES), F32), pltpu.VMEM((s, LANES), F32)],
        compiler_params=_params("arbitrary", "arbitrary"),
    )(proj, proj, proj, do, rsave)


def _sum_adamw(name, parts, w, m, v):
    n, r, c = parts.shape
    tr = r if r <= 256 else 256

    def body(p_ref, w_ref, m_ref, v_ref, g_ref, d_ref, nm_ref, nv_ref):
        g = p_ref[0].astype(F32)
        for j in range(1, n):
            g = g + p_ref[j].astype(F32)
        nm = ADAM_B1 * m_ref[...] + (1.0 - ADAM_B1) * g
        nv = ADAM_B2 * v_ref[...] + (1.0 - ADAM_B2) * (g * g)
        m_hat = nm / (1.0 - ADAM_B1 ** ADAM_STEP)
        v_hat = nv / (1.0 - ADAM_B2 ** ADAM_STEP)
        g_ref[...] = g
        d_ref[...] = -ADAM_LR * (m_hat / (jnp.sqrt(v_hat) + ADAM_EPS) + ADAM_WD * w_ref[...])
        nm_ref[...] = nm
        nv_ref[...] = nv

    mat = pl.BlockSpec((tr, c), lambda i: (i, 0))
    out = jax.ShapeDtypeStruct((r, c), F32)
    return _call(body, name=name, grid=(r // tr,),
                 in_specs=[pl.BlockSpec((n, tr, c), lambda i: (0, i, 0)), mat, mat, mat],
                 out_specs=[mat] * 4, out_shape=[out] * 4, compiler_params=_params("parallel"))(parts, w, m, v)


def _natural(gathered):
    _, k, n = gathered.shape
    return gathered.transpose(1, 0, 2).reshape(k, N_DEV * n)


def _relu2_epi(acc):
    r = jnp.maximum(acc, 0.0)
    return acc, r * r


def _relu2_bwd_epi(acc, a_act):
    return (acc * (2.0 * jnp.maximum(a_act, 0.0)),)


def kernel(x, c, w_ada, b_ada, g_pre_mix, g_post_mix, g_pre_mlp, g_post_mlp, w_in, conv_w, w_proj_conv, w_proj_attn, w_out, w_mlp_in, w_mlp_out, loss_target, m_w_ada, m_b_ada, m_g_pre_mix, m_g_post_mix, m_g_pre_mlp, m_g_post_mlp, m_w_in, m_conv_w, m_w_proj_conv, m_w_proj_attn, m_w_out, m_w_mlp_in, m_w_mlp_out, v_w_ada, v_b_ada, v_g_pre_mix, v_g_post_mix, v_g_pre_mlp, v_g_post_mlp, v_w_in, v_conv_w, v_w_proj_conv, v_w_proj_attn, v_w_out, v_w_mlp_in, v_w_mlp_out):
    xi, yi, ci = _mesh_pos()
    me = 4 * xi + 2 * yi + ci
    d = D_MODEL
    x0 = x[0]
    seq = x0.shape[0]
    ada_cols = w_ada.shape[2]
    conv_cols = conv_w.shape[2]

    small = jnp.concatenate([c.reshape(-1), conv_w.reshape(-1)])
    small = jnp.pad(small, (0, 2 * d - small.shape[0])).reshape(8, 2 * d // 8)
    small_all = _all_gather("gather_c", [small])[0].reshape(N_DEV, 2 * d)
    c_all = small_all[:, :d]
    conv_all = small_all[:, d:d + DEPTH * 3 * conv_cols].reshape(N_DEV, DEPTH, 3, conv_cols)
    conv_all = conv_all.transpose(1, 2, 0, 3).reshape(DEPTH, 3, N_DEV * conv_cols)
    mod_cols = jnp.stack([_mm("mod_mm", c_all, w_ada[l], "nn", N_DEV, ada_cols, d, [F32], exact=True)
                          for l in range(DEPTH)], axis=1)
    mod_all = _all_gather("gather_mod", [mod_cols.reshape(N_DEV, DEPTH * ada_cols)])[0]
    mod_mine = lax.dynamic_index_in_dim(mod_all, me, axis=1, keepdims=False).reshape(N_DEV, DEPTH, ada_cols)
    mod = mod_mine.transpose(1, 0, 2).reshape(DEPTH, N_MOD * d) + b_ada

    def gathered_weights(l):
        shards = [w_in[l], w_proj_conv[l], w_proj_attn[l], w_out[l], w_mlp_in[l], w_mlp_out[l]]
        g_in, g_pc, g_pa, g_out, g_mi, g_mo = _all_gather("gather_w", [w.astype(BF16) for w in shards])
        return (_natural(g_in), _natural(g_pc), _natural(g_pa), g_out.reshape(d, d), _natural(g_mi),
                g_mo.reshape(D_FF, d))

    weights = [gathered_weights(l) for l in range(DEPTH)]

    saved = []
    xl = x0
    for l in range(DEPTH):
        wg_in, wg_pc, wg_pa, wg_out, wg_mi, wg_mo = weights[l]
        sh1, sc1, gt1, sh2, sc2, gt2 = [mod[l:l + 1, i * d:(i + 1) * d] for i in range(N_MOD)]
        h = _prenorm_fwd(xl, g_pre_mix[l:l + 1], sc1, sh1)
        proj = _mm("proj", h, wg_in, "nn", TM, 1024, d, [F32])
        yc = _conv_fwd(proj, conv_all[l])
        y_conv = _mm("proj_conv", yc, wg_pc, "nn", TM, d, CONV_WIDTH, [F32])
        o, rsave = _attn_fwd(proj)
        y_attn = _mm("proj_attn", o, wg_pa, "nn", TM, d, ATTN_WIDTH, [F32])
        merged = _gate_fwd(proj, y_conv, y_attn)
        mix_out = _mm("mix_out", merged, wg_out, "nn", TM, d, d, [F32])
        x1 = _postnorm_fwd(xl, mix_out, g_post_mix[l:l + 1], gt1)
        h2 = _prenorm_fwd(x1, g_pre_mlp[l:l + 1], sc2, sh2)
        a_act, r = _mm("mlp_in", h2, wg_mi, "nn", TM, 1024, d, [F32, BF16], epi=_relu2_epi)
        ff = _mm("mlp_out", r, wg_mo, "nn", TM, d, 1024, [F32])
        x2 = _postnorm_fwd(x1, ff, g_post_mlp[l:l + 1], gt2)
        saved.append((xl, h, proj, yc, o, rsave, y_conv, y_attn, merged, mix_out, x1, h2, a_act, r, ff))
        xl = x2

    dxo, sq = _loss(xl, loss_target[0])
    loss = lax.psum(sq[0, 0] * (0.5 / d), ("x", "y", "c"))

    big = {}
    dmod, small_grads = [None] * DEPTH, [None] * DEPTH
    for l in reversed(range(DEPTH)):
        wg_in, wg_pc, wg_pa, wg_out, wg_mi, wg_mo = weights[l]
        xin, h, proj, yc, o, rsave, y_conv, y_attn, merged, mix_out, x1, h2, a_act, r, ff = saved[l]
        sh1, sc1, gt1, sh2, sc2, gt2 = [mod[l:l + 1, i * d:(i + 1) * d] for i in range(N_MOD)]

        dff, dgt2, dg_post_mlp = _postnorm_bwd(dxo, ff, g_post_mlp[l:l + 1], gt2)
        da = _mm("d_relu2", dff, wg_mo, "nt", TM, 1024, d, [BF16], epi=_relu2_bwd_epi, extra=(a_act,))
        gw_mo = _mm("gw_mlp_out", r, dff, "tn", 1024, d, 1024, [BF16])
        dh2 = _mm("d_h2", da, wg_mi, "nt", TM, d, 1024, [F32])
        gw_mi = _mm("gw_mlp_in", h2, da, "tn", d, D_FF // N_DEV, 1024, [BF16], blocked_out=True)
        dx1, dsh2, dsc2, dg_pre_mlp = _prenorm_bwd(dh2, x1, g_pre_mlp[l:l + 1], sc2, dxo)

        dmix, dgt1, dg_post_mix = _postnorm_bwd(dx1, mix_out, g_post_mix[l:l + 1], gt1)
        dmerged = _mm("d_merged", dmix, wg_out, "nt", TM, d, d, [F32])
        gw_out = _mm("gw_out", merged, dmix, "tn", d, d, 1024, [BF16])
        dy_conv, dy_attn, dga, dgb = _gate_bwd(dmerged, proj, y_conv, y_attn)
        do = _mm("d_o", dy_attn, wg_pa, "nt", TM, ATTN_WIDTH, d, [F32])
        gw_pa = _mm("gw_proj_attn", o, dy_attn, "tn", ATTN_WIDTH, d, 1024, [BF16])
        dyc = _mm("d_yc", dy_conv, wg_pc, "nt", TM, CONV_WIDTH, d, [F32])
        gw_pc = _mm("gw_proj_conv", yc, dy_conv, "tn", CONV_WIDTH, d, 1024, [BF16])
        dq, dk, dv = _attn_bwd(proj, do, rsave)
        dbg, dcg, du, dw0, dw1, dw2 = _conv_bwd(dyc, proj, conv_all[l])
        dproj = jnp.concatenate([dbg, dcg, du, dq, dk, dv, dga, dgb], axis=1)
        dh = _mm("d_h", dproj, wg_in, "nt", TM, d, 1024, [F32])
        gw_in = _mm("gw_in", h, dproj, "tn", d, IN_COLS // N_DEV, 1024, [BF16], blocked_out=True)
        dxo, dsh1, dsc1, dg_pre_mix = _prenorm_bwd(dh, xin, g_pre_mix[l:l + 1], sc1, dx1)

        dmod[l] = jnp.concatenate([dsh1, dsc1, dgt1, dsh2, dsc2, dgt2], axis=1)
        small_grads[l] = (dg_pre_mix, dg_post_mix, dg_pre_mlp, dg_post_mlp, jnp.concatenate([dw0, dw1, dw2], axis=0))

        def col_blocks(gw):
            k, n = gw.shape
            return gw.reshape(k, N_DEV, n // N_DEV).transpose(1, 0, 2)

        sent = [gw_in, col_blocks(gw_pc), col_blocks(gw_pa), gw_out.reshape(N_DEV, d // N_DEV, d), gw_mi,
                gw_mo.reshape(N_DEV, D_FF // N_DEV, d)]
        parts = _exchange("exchange_gw", sent)
        names = ["w_in", "w_proj_conv", "w_proj_attn", "w_out", "w_mlp_in", "w_mlp_out"]
        olds = [(w_in, m_w_in, v_w_in), (w_proj_conv, m_w_proj_conv, v_w_proj_conv),
                (w_proj_attn, m_w_proj_attn, v_w_proj_attn), (w_out, m_w_out, v_w_out),
                (w_mlp_in, m_w_mlp_in, v_w_mlp_in), (w_mlp_out, m_w_mlp_out, v_w_mlp_out)]
        for nm, p, (w_, m_, v_) in zip(names, parts, olds):
            big[(nm, l)] = _sum_adamw("adamw_" + nm, p, w_[l], m_[l], v_[l])

    vec = jnp.concatenate(
        [dmod[l].reshape(-1) for l in range(DEPTH)]
        + [small_grads[l][i].reshape(-1) for i in range(4) for l in range(DEPTH)]
        + [small_grads[l][4].reshape(-1) for l in range(DEPTH)])
    n_vec = vec.shape[0]
    vec_all = _all_gather("gather_small", [vec.reshape(8, n_vec // 8)])[0].reshape(N_DEV, n_vec)
    n_mod = DEPTH * N_MOD * d
    dmod_all = vec_all[:, :n_mod].reshape(N_DEV, DEPTH, N_MOD * d)
    res = {}
    res["b_ada"] = _sum_adamw("adamw_b_ada", dmod_all, b_ada, m_b_ada, v_b_ada)
    off = n_mod
    for nm, (w_, m_, v_) in zip(
            ["g_pre_mix", "g_post_mix", "g_pre_mlp", "g_post_mlp"],
            [(g_pre_mix, m_g_pre_mix, v_g_pre_mix), (g_post_mix, m_g_post_mix, v_g_post_mix),
             (g_pre_mlp, m_g_pre_mlp, v_g_pre_mlp), (g_post_mlp, m_g_post_mlp, v_g_post_mlp)]):
        res[nm] = _sum_adamw("adamw_gain", vec_all[:, off:off + DEPTH * d].reshape(N_DEV, DEPTH, d), w_, m_, v_)
        off += DEPTH * d
    dconv_all = vec_all[:, off:].reshape(N_DEV, DEPTH * 3, CONV_WIDTH)
    dconv_mine = lax.dynamic_slice_in_dim(dconv_all, me * conv_cols, conv_cols, axis=2)
    res["conv_w"] = [t.reshape(DEPTH, 3, conv_cols) for t in _sum_adamw(
        "adamw_conv_w", dconv_mine, conv_w.reshape(DEPTH * 3, conv_cols), m_conv_w.reshape(DEPTH * 3, conv_cols),
        v_conv_w.reshape(DEPTH * 3, conv_cols))]

    c_t = jnp.pad(c_all.T, ((0, 0), (0, LANES - N_DEV)))
    dmod_mine = lax.dynamic_slice_in_dim(dmod_all, me * ada_cols, ada_cols, axis=2)
    ada = []
    for l in range(DEPTH):
        dm_l = jnp.pad(dmod_mine[:, l, :], ((0, LANES - N_DEV), (0, 0)))
        gw_ada = _mm("gw_ada", c_t, dm_l, "nn", 256, ada_cols, LANES, [F32], exact=True)
        ada.append(_sum_adamw("adamw_w_ada", gw_ada[None], w_ada[l], m_w_ada[l], v_w_ada[l]))
    res["w_ada"] = [jnp.stack([ada[l][i] for l in range(DEPTH)]) for i in range(4)]
    for nm in ["w_in", "w_proj_conv", "w_proj_attn", "w_out", "w_mlp_in", "w_mlp_out"]:
        res[nm] = [jnp.stack([big[(nm, l)][i] for l in range(DEPTH)]) for i in range(4)]

    order = ["w_ada", "b_ada", "g_pre_mix", "g_post_mix", "g_pre_mlp", "g_post_mlp", "w_in", "conv_w",
             "w_proj_conv", "w_proj_attn", "w_out", "w_mlp_in", "w_mlp_out"]
    outs = [loss, dxo[None]]
    for i in range(4):
        outs += [res[nm][i] for nm in order]
    return tuple(outs)
```

```python
import jax
import jax.numpy as jnp
from jax import lax
from jax.experimental import pallas as pl
from jax.experimental.pallas import tpu as pltpu

F32 = jnp.float32
BF16 = jnp.bfloat16
MESH = pl.DeviceIdType.MESH

N_DEV = 8
D_MODEL = 1024
CONV_WIDTH = 512
N_HEADS = 8
HEAD_DIM = 64
ATTN_WIDTH = N_HEADS * HEAD_DIM
D_FF = 4 * D_MODEL
N_MOD = 6
DEPTH = 2
EPS = 1e-6
IN_COLS = 3 * CONV_WIDTH + 3 * ATTN_WIDTH + 2 * D_MODEL
LANES = 128

ADAM_LR = 0.001
ADAM_B1 = 0.9
ADAM_B2 = 0.999
ADAM_EPS = 1e-08
ADAM_WD = 0.01
ADAM_STEP = 10

TM = 512
TMM = 1024
TQ = 512
TK = 128
CH = 64
VMEM_LIMIT = 56 * 1024 * 1024

NN = (((1,), (0,)), ((), ()))
NT = (((1,), (1,)), ((), ()))
TN = (((0,), (0,)), ((), ()))
_DIMS = {"nn": NN, "nt": NT, "tn": TN}


def _call(body, **kw):
    return pl.pallas_call(body, **kw)


def _params(*sem):
    return pltpu.CompilerParams(dimension_semantics=sem, vmem_limit_bytes=VMEM_LIMIT)


def _dot(a, b, dims=NN):
    return lax.dot_general(a, b, dims, preferred_element_type=F32)


def _mesh_pos():
    return lax.axis_index("x"), lax.axis_index("y"), lax.axis_index("c")


def _all_gather(name, arrs):
    n = len(arrs)

    def body(*refs):
        ins, outs = refs[:n], refs[n:2 * n]
        send_sems, recv_sems, local_sems = refs[2 * n:]
        x, y, c = _mesh_pos()
        me, sibling = (x, y, c), (x, y, 1 - c)
        chips = [(1 - x, y), (x, 1 - y), (1 - x, 1 - y)]

        def blk(t, p):
            return outs[t].at[4 * p[0] + 2 * p[1] + p[2]]

        def copy(t, k, block, to, src=None):
            return pltpu.make_async_remote_copy(
                src_ref=blk(t, block) if src is None else src, dst_ref=blk(t, block),
                send_sem=send_sems.at[7 * t + k], recv_sem=recv_sems.at[7 * t + k],
                device_id=to, device_id_type=MESH)

        mine, first, passed = [], [], []
        for t in range(n):
            cp = pltpu.make_async_copy(ins[t], blk(t, me), local_sems.at[t])
            cp.start()
            mine.append(cp)
            cps = [copy(t, 0, me, sibling, src=ins[t])]
            cps += [copy(t, 1 + j, me, (*chip, c), src=ins[t]) for j, chip in enumerate(chips)]
            for cp in cps:
                cp.start()
            first += cps
        for t in range(n):
            for j, chip in enumerate(chips):
                copy(t, 1 + j, (*chip, c), me).wait_recv()
                cp = copy(t, 4 + j, (*chip, c), sibling)
                cp.start()
                passed.append(cp)
        for t in range(n):
            copy(t, 0, sibling, me).wait_recv()
            for j, chip in enumerate(chips):
                copy(t, 4 + j, (*chip, 1 - c), me).wait_recv()
        for cp in first + passed:
            cp.wait_send()
        for cp in mine:
            cp.wait()

    any_spec = pl.BlockSpec(memory_space=pl.ANY)
    return _call(
        body, name=name,
        out_shape=[jax.ShapeDtypeStruct((N_DEV,) + a.shape, a.dtype) for a in arrs],
        in_specs=[any_spec] * n, out_specs=[any_spec] * n,
        scratch_shapes=[pltpu.SemaphoreType.DMA((7 * n,)), pltpu.SemaphoreType.DMA((7 * n,)),
                        pltpu.SemaphoreType.DMA((n,))],
    )(*arrs)


def _exchange(name, arrs):
    n = len(arrs)

    def body(*refs):
        ins, outs = refs[:n], refs[n:2 * n]
        send_sems, recv_sems, local_sems = refs[2 * n:]
        x, y, c = _mesh_pos()
        my_idx = 4 * x + 2 * y + c
        peers = []
        for k in range(1, N_DEV):
            p = (1 - x if k & 4 else x, 1 - y if k & 2 else y, 1 - c if k & 1 else c)
            peers.append((k - 1, p, 4 * p[0] + 2 * p[1] + p[2]))

        def copy(t, k, p, p_idx):
            return (pltpu.make_async_remote_copy(
                        src_ref=ins[t].at[p_idx], dst_ref=outs[t].at[my_idx],
                        send_sem=send_sems.at[7 * t + k], recv_sem=recv_sems.at[7 * t + k],
                        device_id=p, device_id_type=MESH),
                    pltpu.make_async_remote_copy(
                        src_ref=ins[t].at[p_idx], dst_ref=outs[t].at[p_idx],
                        send_sem=send_sems.at[7 * t + k], recv_sem=recv_sems.at[7 * t + k],
                        device_id=p, device_id_type=MESH))

        mine, sends, recvs = [], [], []
        for t in range(n):
            cp = pltpu.make_async_copy(ins[t].at[my_idx], outs[t].at[my_idx], local_sems.at[t])
            cp.start()
            mine.append(cp)
            for k, p, p_idx in peers:
                send, recv = copy(t, k, p, p_idx)
                send.start()
                sends.append(send)
                recvs.append(recv)
        for cp in recvs:
            cp.wait_recv()
        for cp in sends:
            cp.wait_send()
        for cp in mine:
            cp.wait()

    any_spec = pl.BlockSpec(memory_space=pl.ANY)
    return _call(
        body, name=name,
        out_shape=[jax.ShapeDtypeStruct(a.shape, a.dtype) for a in arrs],
        in_specs=[any_spec] * n, out_specs=[any_spec] * n,
        scratch_shapes=[pltpu.SemaphoreType.DMA((7 * n,)), pltpu.SemaphoreType.DMA((7 * n,)),
                        pltpu.SemaphoreType.DMA((n,))],
    )(*arrs)


def _mm(name, a, b, mode, tm, tn, tk, out_dtypes, epi=None, extra=(), blocked_out=False, exact=False):
    if mode == "nn":
        (m, k), n = a.shape, b.shape[1]
    elif mode == "nt":
        (m, k), n = a.shape, b.shape[0]
    else:
        (k, m), n = a.shape, b.shape[1]
    nk = k // tk
    grid = (m // tm, n // tn, nk)
    n_extra, n_out = len(extra), len(out_dtypes)

    def body(*refs):
        a_ref, b_ref = refs[0], refs[1]
        extra_refs = refs[2:2 + n_extra]
        out_refs = refs[2 + n_extra:2 + n_extra + n_out]
        if exact:
            p = lax.dot_general(a_ref[...], b_ref[...], _DIMS[mode], preferred_element_type=F32,
                                precision=lax.Precision.HIGHEST)
        else:
            p = _dot(a_ref[...].astype(BF16), b_ref[...].astype(BF16), _DIMS[mode])

        def finish(acc):
            outs = (acc,) if epi is None else epi(acc, *[r[...] for r in extra_refs])
            for r, o in zip(out_refs, outs):
                r[...] = o.astype(r.dtype)

        if nk == 1:
            finish(p)
        else:
            acc_ref = refs[-1]
            kk = pl.program_id(2)

            @pl.when(kk == 0)
            def _():
                acc_ref[...] = p

            @pl.when(kk > 0)
            def _():
                acc_ref[...] += p

            @pl.when(kk == nk - 1)
            def _():
                finish(acc_ref[...])

    if mode == "tn":
        a_spec = pl.BlockSpec((tk, tm), lambda i, j, kk: (kk, i))
    else:
        a_spec = pl.BlockSpec((tm, tk), lambda i, j, kk: (i, kk))
    if mode == "nt":
        b_spec = pl.BlockSpec((tn, tk), lambda i, j, kk: (j, kk))
    else:
        b_spec = pl.BlockSpec((tk, tn), lambda i, j, kk: (kk, j))
    tile = pl.BlockSpec((tm, tn), lambda i, j, kk: (i, j))
    if blocked_out:
        o_shape, o_spec = (n // tn, m, tn), pl.BlockSpec((None, tm, tn), lambda i, j, kk: (j, i, 0))
    else:
        o_shape, o_spec = (m, n), tile
    out = _call(
        body, name=name, grid=grid,
        in_specs=[a_spec, b_spec] + [tile] * n_extra,
        out_specs=[o_spec] * n_out,
        out_shape=[jax.ShapeDtypeStruct(o_shape, dt) for dt in out_dtypes],
        scratch_shapes=[pltpu.VMEM((tm, tn), F32)] if nk > 1 else [],
        compiler_params=_params("parallel", "parallel", "arbitrary"),
    )(a, b, *extra)
    return out[0] if n_out == 1 else out


def _tile(width, col=0, rows=TM):
    return pl.BlockSpec((rows, width), lambda i: (i, col))


def _vec(width):
    return pl.BlockSpec((1, width), lambda i: (0, 0))


def _rstd(xf):
    return lax.rsqrt(jnp.mean(xf * xf, axis=-1, keepdims=True) + EPS)


def _colsum(v):
    return jnp.sum(v, axis=0, keepdims=True)


def _accumulate(refs, vals):
    first = pl.program_id(0) == 0

    @pl.when(first)
    def _():
        for r, v in zip(refs, vals):
            r[...] = v

    @pl.when(jnp.logical_not(first))
    def _():
        for r, v in zip(refs, vals):
            r[...] += v


def _prenorm_fwd(x, g, sc, sh):
    s, d = x.shape

    def body(x_ref, g_ref, sc_ref, sh_ref, h_ref):
        xf = x_ref[...]
        y = (xf * _rstd(xf)) * g_ref[...]
        h_ref[...] = (y * (1.0 + sc_ref[...]) + sh_ref[...]).astype(h_ref.dtype)

    return _call(body, name="prenorm_fwd", grid=(s // TM,),
                 in_specs=[_tile(d), _vec(d), _vec(d), _vec(d)], out_specs=_tile(d),
                 out_shape=jax.ShapeDtypeStruct((s, d), BF16), compiler_params=_params("parallel"))(x, g, sc, sh)


def _prenorm_bwd(dh, x, g, sc, dres):
    s, d = x.shape

    def body(dh_ref, x_ref, g_ref, sc_ref, dres_ref, dx_ref, dsh_ref, dsc_ref, dg_ref):
        xf, dhf = x_ref[...], dh_ref[...]
        rstd = _rstd(xf)
        xhat = xf * rstd
        one_sc = 1.0 + sc_ref[...]
        dxhat = dhf * (g_ref[...] * one_sc)
        dx_ref[...] = dres_ref[...] + rstd * (dxhat - xhat * jnp.mean(dxhat * xhat, axis=-1, keepdims=True))
        dhx = dhf * xhat
        _accumulate((dsh_ref, dsc_ref, dg_ref), (_colsum(dhf), _colsum(dhx) * g_ref[...], _colsum(dhx) * one_sc))

    vec_out = jax.ShapeDtypeStruct((1, d), F32)
    return _call(body, name="prenorm_bwd", grid=(s // TM,),
                 in_specs=[_tile(d), _tile(d), _vec(d), _vec(d), _tile(d)],
                 out_specs=[_tile(d), _vec(d), _vec(d), _vec(d)],
                 out_shape=[jax.ShapeDtypeStruct((s, d), F32), vec_out, vec_out, vec_out],
                 compiler_params=_params("arbitrary"))(dh, x, g, sc, dres)


def _postnorm_fwd(xres, m, g, gt):
    s, d = m.shape

    def body(x_ref, m_ref, g_ref, gt_ref, o_ref):
        mf = m_ref[...]
        o_ref[...] = x_ref[...] + gt_ref[...] * ((mf * _rstd(mf)) * g_ref[...])

    return _call(body, name="postnorm_fwd", grid=(s // TM,),
                 in_specs=[_tile(d), _tile(d), _vec(d), _vec(d)], out_specs=_tile(d),
                 out_shape=jax.ShapeDtypeStruct((s, d), F32), compiler_params=_params("parallel"))(xres, m, g, gt)


def _postnorm_bwd(dxn, m, g, gt):
    s, d = m.shape

    def body(dx_ref, m_ref, g_ref, gt_ref, dm_ref, dgt_ref, dg_ref):
        mf, dxf = m_ref[...], dx_ref[...]
        rstd = _rstd(mf)
        mhat = mf * rstd
        dmhat = dxf * (gt_ref[...] * g_ref[...])
        dm_ref[...] = (rstd * (dmhat - mhat * jnp.mean(dmhat * mhat, axis=-1, keepdims=True))).astype(dm_ref.dtype)
        dxm = _colsum(dxf * mhat)
        _accumulate((dgt_ref, dg_ref), (dxm * g_ref[...], dxm * gt_ref[...]))

    vec_out = jax.ShapeDtypeStruct((1, d), F32)
    return _call(body, name="postnorm_bwd", grid=(s // TM,),
                 in_specs=[_tile(d), _tile(d), _vec(d), _vec(d)], out_specs=[_tile(d), _vec(d), _vec(d)],
                 out_shape=[jax.ShapeDtypeStruct((s, d), BF16), vec_out, vec_out],
                 compiler_params=_params("arbitrary"))(dxn, m, g, gt)


def _loss(y, target):
    s, d = y.shape

    def body(y_ref, t_ref, dy_ref, sq_ref):
        err = y_ref[...] - t_ref[...]
        dy_ref[...] = err * (1.0 / d)
        tot = jnp.sum(_colsum(err * err), axis=1, keepdims=True)
        _accumulate((sq_ref,), (jnp.broadcast_to(tot, (1, LANES)),))

    return _call(body, name="loss", grid=(s // TM,), in_specs=[_tile(d), _tile(d)],
                 out_specs=[_tile(d), _vec(LANES)],
                 out_shape=[jax.ShapeDtypeStruct((s, d), F32), jax.ShapeDtypeStruct((1, LANES), F32)],
                 compiler_params=_params("arbitrary"))(y, target)


def _sigmoid(v):
    return 1.0 / (1.0 + jnp.exp(-v))


def _gate_fwd(proj, y_conv, y_attn):
    s, d = y_conv.shape
    ga_col, gb_col = (IN_COLS - 2 * d) // d, (IN_COLS - d) // d

    def body(ga_ref, gb_ref, yc_ref, ya_ref, o_ref):
        o_ref[...] = (_sigmoid(ga_ref[...]) * yc_ref[...] + _sigmoid(gb_ref[...]) * ya_ref[...]).astype(o_ref.dtype)

    return _call(body, name="gate_fwd", grid=(s // TM,),
                 in_specs=[_tile(d, ga_col), _tile(d, gb_col), _tile(d), _tile(d)], out_specs=_tile(d),
                 out_shape=jax.ShapeDtypeStruct((s, d), BF16),
                 compiler_params=_params("parallel"))(proj, proj, y_conv, y_attn)


def _gate_bwd(dmerged, proj, y_conv, y_attn):
    s, d = y_conv.shape
    ga_col, gb_col = (IN_COLS - 2 * d) // d, (IN_COLS - d) // d

    def body(dm_ref, ga_ref, gb_ref, yc_ref, ya_ref, dyc_ref, dya_ref, dga_ref, dgb_ref):
        dm = dm_ref[...]
        sa, sb = _sigmoid(ga_ref[...]), _sigmoid(gb_ref[...])
        dyc_ref[...] = (dm * sa).astype(BF16)
        dya_ref[...] = (dm * sb).astype(BF16)
        dga_ref[...] = (dm * yc_ref[...] * (sa * (1.0 - sa))).astype(BF16)
        dgb_ref[...] = (dm * ya_ref[...] * (sb * (1.0 - sb))).astype(BF16)

    out = jax.ShapeDtypeStruct((s, d), BF16)
    return _call(body, name="gate_bwd", grid=(s // TM,),
                 in_specs=[_tile(d), _tile(d, ga_col), _tile(d, gb_col), _tile(d), _tile(d)],
                 out_specs=[_tile(d)] * 4, out_shape=[out] * 4,
                 compiler_params=_params("parallel"))(dmerged, proj, proj, y_conv, y_attn)


def _shift_down(prev8, cur, by):
    ext = jnp.concatenate([prev8, cur], axis=0)
    return pltpu.roll(ext, by, 0)[8:]


def _shift_up(cur, next8, by):
    ext = jnp.concatenate([cur, next8], axis=0)
    return pltpu.roll(ext, ext.shape[0] - by, 0)[:cur.shape[0]]


def _conv_fwd(proj, conv_w):
    s, w = proj.shape[0], CONV_WIDTH
    per8 = TM // 8

    def prev(col):
        return pl.BlockSpec((8, w), lambda i: (jnp.maximum(i * per8 - 1, 0), col))

    def body(bg_ref, cg_ref, u_ref, cgp_ref, up_ref, w_ref, o_ref):
        vv = cg_ref[...] * u_ref[...]
        pv = cgp_ref[...] * up_ref[...] * jnp.where(pl.program_id(0) > 0, 1.0, 0.0)
        y = w_ref[0:1, :] * _shift_down(pv, vv, 2) + w_ref[1:2, :] * _shift_down(pv, vv, 1) + w_ref[2:3, :] * vv
        o_ref[...] = (bg_ref[...] * y).astype(o_ref.dtype)

    return _call(body, name="conv_fwd", grid=(s // TM,),
                 in_specs=[_tile(w, 0), _tile(w, 1), _tile(w, 2), prev(1), prev(2),
                           pl.BlockSpec((3, w), lambda i: (0, 0))],
                 out_specs=_tile(w), out_shape=jax.ShapeDtypeStruct((s, w), BF16),
                 compiler_params=_params("parallel"))(proj, proj, proj, proj, proj, conv_w)


def _conv_bwd(dyc, proj, conv_w):
    s, w = proj.shape[0], CONV_WIDTH
    per8 = TM // 8
    n_tiles = s // TM

    def prev(col):
        return pl.BlockSpec((8, w), lambda i: (jnp.maximum(i * per8 - 1, 0), col))

    def nxt(col):
        return pl.BlockSpec((8, w), lambda i: (jnp.minimum((i + 1) * per8, s // 8 - 1), col))

    def body(dyc_ref, bg_ref, cg_ref, u_ref, cgp_ref, up_ref, dycn_ref, bgn_ref, w_ref,
             dbg_ref, dcg_ref, du_ref, dw0_ref, dw1_ref, dw2_ref):
        i = pl.program_id(0)
        cg, u = cg_ref[...], u_ref[...]
        vv = cg * u
        pv = cgp_ref[...] * up_ref[...] * jnp.where(i > 0, 1.0, 0.0)
        v1, v2 = _shift_down(pv, vv, 1), _shift_down(pv, vv, 2)
        w0, w1, w2 = w_ref[0:1, :], w_ref[1:2, :], w_ref[2:3, :]
        dyc_t = dyc_ref[...]
        dbg_ref[...] = (dyc_t * (w0 * v2 + w1 * v1 + w2 * vv)).astype(BF16)
        dy = dyc_t * bg_ref[...]
        dyn = dycn_ref[...] * bgn_ref[...] * jnp.where(i < n_tiles - 1, 1.0, 0.0)
        dvv = w2 * dy + w1 * _shift_up(dy, dyn, 1) + w0 * _shift_up(dy, dyn, 2)
        dcg_ref[...] = (dvv * u).astype(BF16)
        du_ref[...] = (dvv * cg).astype(BF16)
        _accumulate((dw0_ref, dw1_ref, dw2_ref), (_colsum(dy * v2), _colsum(dy * v1), _colsum(dy * vv)))

    act = jax.ShapeDtypeStruct((s, w), BF16)
    tap = jax.ShapeDtypeStruct((1, w), F32)
    return _call(body, name="conv_bwd", grid=(n_tiles,),
                 in_specs=[_tile(w), _tile(w, 0), _tile(w, 1), _tile(w, 2), prev(1), prev(2), nxt(0), nxt(0),
                           pl.BlockSpec((3, w), lambda i: (0, 0))],
                 out_specs=[_tile(w)] * 3 + [_vec(w)] * 3, out_shape=[act] * 3 + [tap] * 3,
                 compiler_params=_params("arbitrary"))(dyc, proj, proj, proj, proj, proj, dyc, proj, conv_w)


Q_COL = 3 * CONV_WIDTH // LANES
K_COL = Q_COL + ATTN_WIDTH // LANES
V_COL = K_COL + ATTN_WIDTH // LANES
SCALE = HEAD_DIM ** -0.5


def _head_lanes(hh):
    lane = lax.broadcasted_iota(jnp.int32, (1, LANES), 1)
    return jnp.where((lane >= hh * HEAD_DIM) & (lane < (hh + 1) * HEAD_DIM), 1.0, 0.0)


def _tri(width, keep):
    j = lax.broadcasted_iota(jnp.int32, (TK, width), 0)
    s = lax.broadcasted_iota(jnp.int32, (TK, width), 1)
    return jnp.where((s >= TK) | keep(j, s), 1.0, 0.0).astype(BF16)


def _split_dot(v, tri):
    hi = v.astype(BF16)
    lo = (v - hi.astype(F32)).astype(BF16)
    return _dot(hi, tri) + _dot(lo, tri)


def _logits(qm, kblk, ks, tpos):
    z = _dot(qm, kblk, NT)
    mask = (ks + lax.broadcasted_iota(jnp.int32, (1, TK), 1)) < tpos
    e = jnp.exp(-jnp.abs(z))
    l0 = jnp.where(mask, -(jnp.maximum(z, 0.0) + jnp.log(1.0 + e)), 0.0)
    return z, mask, e, l0


def _neg_softplus(z):
    return jnp.minimum(-z, 0.0) - jnp.log(1.0 + jnp.exp(-jnp.abs(z)))


W2 = 2 * TK
PIPE = 4


def _softplus(z):
    neg_abs = lax.bitcast_convert_type(lax.bitcast_convert_type(z, jnp.uint32) | jnp.uint32(0x80000000), F32)
    return jnp.maximum(z, 0.0) + jnp.log(1.0 + jnp.exp(neg_abs))


def _pair_rows(ref, kb, lm):
    blk = ref[pl.ds(pl.multiple_of(kb * TK, TK), TK), :]
    return jnp.concatenate([(blk * lm[0]).astype(BF16), (blk * lm[1]).astype(BF16)], axis=0)


def _pair_tri(keep):
    j = lax.broadcasted_iota(jnp.int32, (W2, 2 * W2), 0)
    s = lax.broadcasted_iota(jnp.int32, (W2, 2 * W2), 1)
    same_head = (j >= TK) == ((s & (W2 - 1)) >= TK)
    return jnp.where(same_head & ((s >= W2) | keep(j & (TK - 1), s & (TK - 1))), 1.0, 0.0).astype(BF16)


def _attn_fwd(proj):
    s = proj.shape[0]
    nq = s // TQ
    diag = TQ // TK
    n_ch = TQ // CH
    assert s // TK <= TK and diag % PIPE == 0

    def body(q_ref, k_ref, v_ref, o_ref, rs_ref, qb_scr, tri_scr, z_scr, l0_scr, cs_scr, a_scr, r_scr, rall_scr, acc_scr):
        qi = pl.program_id(1)
        lm = [_head_lanes(0), _head_lanes(1)]
        lane = lax.broadcasted_iota(jnp.int32, (CH, W2), 1) & (TK - 1)
        row = lax.broadcasted_iota(jnp.int32, (CH, 1), 0)
        col = lax.broadcasted_iota(jnp.int32, (1, W2), 1) & (TK - 1)
        qb_scr[...] = (q_ref[...] * SCALE).astype(BF16)
        tri_scr[...] = _pair_tri(lambda j, ss: j > ss)
        r_scr[...] = jnp.zeros_like(r_scr)
        rall_scr[...] = jnp.zeros_like(rall_scr)
        acc_scr[...] = jnp.zeros_like(acc_scr)

        def causal(kb, c):
            return (kb * TK + col) < (qi * TQ + c * CH + row)

        def logits(kb, zb):
            z_scr[zb] = _dot(qb_scr[...], _pair_rows(k_ref, kb, lm), NT)

        def log_one_minus_beta(kb, zb, lb, masked):
            for c in range(n_ch):
                rows = slice(c * CH, (c + 1) * CH)
                sp = _softplus(z_scr[zb, rows, :])
                if masked:
                    sp = jnp.where(causal(kb, c), sp, 0.0)
                l0_scr[lb, rows, :] = sp.astype(BF16)

        def sums(lb):
            cs_scr[...] = _dot(l0_scr[lb], tri_scr[...])

        def weights(kb, zb, lb, ab, masked):
            for c in range(n_ch):
                rows = slice(c * CH, (c + 1) * CH)
                near = r_scr[rows, :]
                a = jnp.exp(z_scr[zb, rows, :] - l0_scr[lb, rows, :].astype(F32) - cs_scr[rows, :W2] - near)
                if masked:
                    a = jnp.where(causal(kb, c), a, 0.0)
                a_scr[ab, rows, :] = a.astype(BF16)
                rall_scr[rows, :] = jnp.where(lane == kb, near, rall_scr[rows, :])
                r_scr[rows, :] = near + cs_scr[rows, W2:]

        def weighted_values(kb, ab):
            acc_scr[...] += _dot(a_scr[ab], _pair_rows(v_ref, kb, lm))

        def diag_step(it, carry):
            kb = (qi + 1) * diag - 1 - it
            logits(kb, 0)
            log_one_minus_beta(kb, 0, 0, True)
            sums(0)
            weights(kb, 0, 0, 0, True)
            weighted_values(kb, 0)
            return carry

        lax.fori_loop(0, diag, diag_step, 0)

        n = qi * diag

        def block(j):
            return jnp.maximum(n - 1 - j, 0)

        a_scr[...] = jnp.zeros_like(a_scr)
        logits(block(0), 0)
        logits(block(1), 1)
        log_one_minus_beta(block(0), 0, 0, False)

        def trip(m, carry):
            for u in range(PIPE):
                j = PIPE * m + u
                weighted_values(block(j - 1), (u - 1) % 2)
                sums(u % 2)
                logits(block(j + 2), (u + 2) % PIPE)
                log_one_minus_beta(block(j + 1), (u + 1) % PIPE, (u + 1) % 2, False)
                weights(block(j), u % PIPE, u % 2, u % 2, False)
            return carry

        lax.fori_loop(0, n // PIPE, trip, 0)
        weighted_values(block(n - 1), (PIPE - 1) % 2)
        rs_ref[...] = rall_scr[...]
        o_ref[...] = acc_scr[...]

    return _call(
        body, name="attn_fwd", grid=(N_HEADS // 2, nq),
        in_specs=[pl.BlockSpec((TQ, LANES), lambda p, qi: (qi, Q_COL + p)),
                  pl.BlockSpec((s, LANES), lambda p, qi: (0, K_COL + p)),
                  pl.BlockSpec((s, LANES), lambda p, qi: (0, V_COL + p))],
        out_specs=[pl.BlockSpec((TQ, LANES), lambda p, qi: (qi, p)),
                   pl.BlockSpec((TQ, W2), lambda p, qi: (qi, p))],
        out_shape=[jax.ShapeDtypeStruct((s, ATTN_WIDTH), F32), jax.ShapeDtypeStruct((s, N_HEADS // 2 * W2), F32)],
        scratch_shapes=[pltpu.VMEM((TQ, LANES), BF16), pltpu.VMEM((W2, 2 * W2), BF16),
                        pltpu.VMEM((PIPE, TQ, W2), F32), pltpu.VMEM((2, TQ, W2), BF16),
                        pltpu.VMEM((TQ, 2 * W2), F32), pltpu.VMEM((2, TQ, W2), BF16),
                        pltpu.VMEM((TQ, W2), F32), pltpu.VMEM((TQ, W2), F32), pltpu.VMEM((TQ, LANES), F32)],
        compiler_params=_params("parallel", "arbitrary"),
    )(proj, proj, proj)


def _attn_fwd_rows(proj):
    s = proj.shape[0]
    nq = s // TQ
    diag = TQ // TK
    n_ch = 2 * TQ // CH
    assert s // TK <= LANES

    def body(q_ref, k_ref, v_ref, o_ref, rs_ref, q2_scr, z_scr, l0_scr, cs_scr, a_scr, r_scr, rall_scr, acc_scr):
        qi = pl.program_id(1)
        lm = [_head_lanes(0), _head_lanes(1)]
        tri = _tri(2 * TK, lambda j, ss: j > ss)
        lane = lax.broadcasted_iota(jnp.int32, (CH, LANES), 1)
        row = lax.broadcasted_iota(jnp.int32, (CH, 1), 0)
        col = lax.broadcasted_iota(jnp.int32, (1, TK), 1)
        for hh in range(2):
            q2_scr[hh * TQ:(hh + 1) * TQ, :] = (q_ref[...] * (SCALE * lm[hh])).astype(BF16)
        r_scr[...] = jnp.zeros_like(r_scr)
        rall_scr[...] = jnp.zeros_like(rall_scr)
        acc_scr[...] = jnp.zeros_like(acc_scr)

        def step(kb, masked):
            ks = pl.multiple_of(kb * TK, TK)
            kblk = k_ref[pl.ds(ks, TK), :].astype(BF16)
            vf = v_ref[pl.ds(ks, TK), :]
            v2 = jnp.concatenate([(vf * lm[0]).astype(BF16), (vf * lm[1]).astype(BF16)], axis=0)
            z_scr[...] = _dot(q2_scr[...], kblk, NT)

            def causal(c):
                return (ks + col) < (qi * TQ + (c * CH) % TQ + row)

            for c in range(n_ch):
                rows = slice(c * CH, (c + 1) * CH)
                l0 = _neg_softplus(z_scr[rows, :])
                if masked:
                    l0 = jnp.where(causal(c), l0, 0.0)
                l0_scr[rows, :] = l0.astype(BF16)
            cs_scr[...] = _dot(l0_scr[...], tri)
            for c in range(n_ch):
                rows = slice(c * CH, (c + 1) * CH)
                hh, r0 = (c * CH) // TQ, (c * CH) % TQ
                near = r_scr[rows, :]
                a = jnp.exp(l0_scr[rows, :].astype(F32) + z_scr[rows, :] + cs_scr[rows, :TK] + near)
                if masked:
                    a = jnp.where(causal(c), a, 0.0)
                a_scr[r0:r0 + CH, hh * TK:(hh + 1) * TK] = a.astype(BF16)
                rall_scr[rows, :] = jnp.where(lane == kb, near, rall_scr[rows, :])
                r_scr[rows, :] = near + cs_scr[rows, TK:]
            acc_scr[...] += _dot(a_scr[...], v2)

        def diag_step(it, carry):
            step((qi + 1) * diag - 1 - it, True)
            return carry

        def inner_step(it, carry):
            step(qi * diag - 1 - it, False)
            return carry

        lax.fori_loop(0, diag, diag_step, 0)
        lax.fori_loop(0, qi * diag, inner_step, 0)
        for hh in range(2):
            rs_ref[hh] = rall_scr[hh * TQ:(hh + 1) * TQ, :]
        o_ref[...] = acc_scr[...]

    return _call(
        body, name="attn_fwd", grid=(N_HEADS // 2, nq),
        in_specs=[pl.BlockSpec((TQ, LANES), lambda p, qi: (qi, Q_COL + p)),
                  pl.BlockSpec((s, LANES), lambda p, qi: (0, K_COL + p)),
                  pl.BlockSpec((s, LANES), lambda p, qi: (0, V_COL + p))],
        out_specs=[pl.BlockSpec((TQ, LANES), lambda p, qi: (qi, p)),
                   pl.BlockSpec((2, TQ, LANES), lambda p, qi: (p, qi, 0))],
        out_shape=[jax.ShapeDtypeStruct((s, ATTN_WIDTH), F32), jax.ShapeDtypeStruct((N_HEADS, s, LANES), F32)],
        scratch_shapes=[pltpu.VMEM((2 * TQ, LANES), BF16), pltpu.VMEM((2 * TQ, TK), F32),
                        pltpu.VMEM((2 * TQ, TK), BF16), pltpu.VMEM((2 * TQ, 2 * TK), F32),
                        pltpu.VMEM((TQ, 2 * TK), BF16), pltpu.VMEM((2 * TQ, LANES), F32),
                        pltpu.VMEM((2 * TQ, LANES), F32), pltpu.VMEM((TQ, LANES), F32)],
        compiler_params=_params("parallel", "arbitrary"),
    )(proj, proj, proj)


def _attn_bwd(proj, do, rsave):
    s = proj.shape[0]
    nq = s // TQ
    diag = TQ // TK
    n_ch = TQ // CH
    assert diag % PIPE == 0

    def body(q_ref, k_ref, v_ref, do_ref, rs_ref, dq_ref, dk_ref, dv_ref,
             qb_scr, dob_scr, after_scr, before_scr, z_scr, da_scr, l0_scr, beta_scr, cs_scr, a_scr, g_scr, cg_scr,
             dz_scr, pg_scr, dq_scr, dk_scr, dv_scr):
        qi = pl.program_id(1)

        @pl.when(qi == 0)
        def _():
            dk_scr[...] = jnp.zeros_like(dk_scr)
            dv_scr[...] = jnp.zeros_like(dv_scr)

        lm = [_head_lanes(0), _head_lanes(1)]
        lane = lax.broadcasted_iota(jnp.int32, (CH, TK), 1)
        row = lax.broadcasted_iota(jnp.int32, (CH, 1), 0)
        col = lax.broadcasted_iota(jnp.int32, (1, W2), 1) & (TK - 1)
        qb_scr[...] = (q_ref[...] * SCALE).astype(BF16)
        dob_scr[...] = do_ref[...].astype(BF16)
        after_scr[...] = _pair_tri(lambda j, ss: j > ss)[:, :W2]
        before_scr[...] = _pair_tri(lambda j, ss: j < ss)
        pg_scr[...] = jnp.zeros_like(pg_scr)
        dq_scr[...] = jnp.zeros_like(dq_scr)
        dz_scr[...] = jnp.zeros_like(dz_scr)

        def causal(kb, c):
            return (kb * TK + col) < (qi * TQ + c * CH + row)

        def logits(kb, zb):
            z_scr[zb] = _dot(qb_scr[...], _pair_rows(k_ref, kb, lm), NT)

        def do_dot_v(kb, db):
            da_scr[db] = _dot(dob_scr[...], _pair_rows(v_ref, kb, lm), NT)

        def gates(kb, zb, lb, bb, masked):
            for c in range(n_ch):
                rows = slice(c * CH, (c + 1) * CH)
                z = z_scr[zb, rows, :]
                sp = _softplus(z)
                beta_scr[bb, rows, :] = jnp.exp(z - sp)
                if masked:
                    sp = jnp.where(causal(kb, c), sp, 0.0)
                l0_scr[lb, rows, :] = sp.astype(BF16)

        def suffix_sums(lb):
            cs_scr[...] = _dot(l0_scr[lb], after_scr[...])

        def weights(kb, zb, lb, db, ab, masked):
            for c in range(n_ch):
                rows = slice(c * CH, (c + 1) * CH)
                keep = (kb * TK + lane) < (qi * TQ + c * CH + row) if masked else None
                for hh in range(2):
                    cols = slice(hh * TK, (hh + 1) * TK)
                    near = jnp.sum(jnp.where(lane == kb, rs_ref[rows, cols], 0.0), axis=1, keepdims=True)
                    a = jnp.exp(z_scr[zb, rows, cols] - l0_scr[lb, rows, cols].astype(F32) - cs_scr[rows, cols] - near)
                    if masked:
                        a = jnp.where(keep, a, 0.0)
                    a_scr[ab, rows, cols] = a.astype(BF16)
                    g_scr[ab, rows, cols] = (a * da_scr[db, rows, cols]).astype(BF16)

        def prefix_sums(ab):
            cg_scr[...] = _dot(g_scr[ab], before_scr[...])

        def dlogits(kb, ab, bb, zb2, masked):
            for c in range(n_ch):
                rows = slice(c * CH, (c + 1) * CH)
                before = cg_scr[rows, :W2] + pg_scr[rows, :]
                beta = beta_scr[bb, rows, :]
                dz = g_scr[ab, rows, :].astype(F32) * (1.0 - beta) - beta * before
                if masked:
                    dz = jnp.where(causal(kb, c), dz, 0.0)
                dz_scr[zb2, rows, :] = dz.astype(BF16)
                pg_scr[rows, :] += cg_scr[rows, W2:]

        def fold(t):
            return t[:TK, :] * lm[0] + t[TK:, :] * lm[1]

        def dq_dk(kb, zb2):
            dq_scr[...] += _dot(dz_scr[zb2], _pair_rows(k_ref, kb, lm))
            dk_scr[pl.ds(pl.multiple_of(kb * TK, TK), TK), :] += fold(_dot(dz_scr[zb2], qb_scr[...], TN))

        def dv(kb, ab):
            dv_scr[pl.ds(pl.multiple_of(kb * TK, TK), TK), :] += fold(_dot(a_scr[ab], dob_scr[...], TN))

        n = qi * diag

        def block(j):
            return jnp.clip(j, 0, jnp.maximum(n - 1, 0))

        logits(block(0), 0)
        logits(block(1), 1)
        logits(block(2), 2)
        do_dot_v(block(0), 0)
        do_dot_v(block(1), 1)
        gates(block(0), 0, 0, 0, False)
        gates(block(1), 1, 1, 1, False)
        suffix_sums(0)
        weights(block(0), 0, 0, 0, 0, False)

        def trip(m, carry):
            for u in range(PIPE):
                t = PIPE * m + u
                dq_dk(block(t - 1), (u - 1) % 2)
                dv(block(t), u % 2)
                prefix_sums(u % 2)
                suffix_sums((u + 1) % 2)
                logits(block(t + 3), (u + 3) % PIPE)
                do_dot_v(block(t + 2), u % 2)
                gates(block(t + 2), (u + 2) % PIPE, u % 2, (u + 2) % PIPE, False)
                weights(block(t + 1), (u + 1) % PIPE, (u + 1) % 2, (u + 1) % 2, (u + 1) % 2, False)
                dlogits(block(t), u % 2, u % PIPE, u % 2, False)
            return carry

        lax.fori_loop(0, n // PIPE, trip, 0)
        dq_dk(block(n - 1), (PIPE - 1) % 2)

        def diag_step(it, carry):
            kb = n + it
            logits(kb, 0)
            do_dot_v(kb, 0)
            gates(kb, 0, 0, 0, True)
            suffix_sums(0)
            weights(kb, 0, 0, 0, 0, True)
            dv(kb, 0)
            prefix_sums(0)
            dlogits(kb, 0, 0, 0, True)
            dq_dk(kb, 0)
            return carry

        lax.fori_loop(0, diag, diag_step, 0)
        dq_ref[...] = (dq_scr[...] * SCALE).astype(dq_ref.dtype)

        @pl.when(qi == nq - 1)
        def _():
            dk_ref[...] = dk_scr[...].astype(dk_ref.dtype)
            dv_ref[...] = dv_scr[...].astype(dv_ref.dtype)

    def rows(c0):
        return pl.BlockSpec((TQ, LANES), lambda p, qi: (qi, c0 + p))

    def whole(c0):
        return pl.BlockSpec((s, LANES), lambda p, qi: (0, c0 + p))

    def f32(*shape):
        return pltpu.VMEM(shape, F32)

    def bf16(*shape):
        return pltpu.VMEM(shape, BF16)

    out = jax.ShapeDtypeStruct((s, ATTN_WIDTH), BF16)
    return _call(
        body, name="attn_bwd", grid=(N_HEADS // 2, nq),
        in_specs=[rows(Q_COL), whole(K_COL), whole(V_COL), rows(0), pl.BlockSpec((TQ, W2), lambda p, qi: (qi, p))],
        out_specs=[rows(0), whole(0), whole(0)], out_shape=[out] * 3,
        scratch_shapes=[bf16(TQ, LANES), bf16(TQ, LANES), bf16(W2, W2), bf16(W2, 2 * W2),
                        f32(PIPE, TQ, W2), f32(2, TQ, W2), bf16(2, TQ, W2), f32(PIPE, TQ, W2), f32(TQ, W2),
                        bf16(2, TQ, W2), bf16(2, TQ, W2), f32(TQ, 2 * W2), bf16(2, TQ, W2),
                        f32(TQ, W2), f32(TQ, LANES), f32(s, LANES), f32(s, LANES)],
        compiler_params=_params("arbitrary", "arbitrary"),
    )(proj, proj, proj, do, rsave)


def _attn_bwd_rows(proj, do, rsave):
    s = proj.shape[0]
    nq = s // TQ

    diag = TQ // TK
    n_ch = 2 * TQ // CH

    def body(q_ref, k_ref, v_ref, do_ref, rs_ref, dq_ref, dk_ref, dv_ref,
             q2_scr, do2_scr, z_scr, da_scr, l0_scr, beta_scr, cs_scr, a_scr, g_scr, cg_scr, dz_scr,
             pg_scr, dq_scr, dk_scr, dv_scr):
        qi = pl.program_id(1)

        @pl.when(qi == 0)
        def _():
            dk_scr[...] = jnp.zeros_like(dk_scr)
            dv_scr[...] = jnp.zeros_like(dv_scr)

        lm = [_head_lanes(0), _head_lanes(1)]
        tri_after = _tri(TK, lambda j, ss: j > ss)
        tri_before = _tri(2 * TK, lambda j, ss: j < ss)
        lane = lax.broadcasted_iota(jnp.int32, (CH, LANES), 1)
        row = lax.broadcasted_iota(jnp.int32, (CH, 1), 0)
        col = lax.broadcasted_iota(jnp.int32, (1, TK), 1)
        for hh in range(2):
            q2_scr[hh * TQ:(hh + 1) * TQ, :] = (q_ref[...] * (SCALE * lm[hh])).astype(BF16)
            do2_scr[hh * TQ:(hh + 1) * TQ, :] = (do_ref[...] * lm[hh]).astype(BF16)
        pg_scr[...] = jnp.zeros_like(pg_scr)
        dq_scr[...] = jnp.zeros_like(dq_scr)

        def step(kb, masked):
            ks = pl.multiple_of(kb * TK, TK)
            kblk = k_ref[pl.ds(ks, TK), :].astype(BF16)
            vblk = v_ref[pl.ds(ks, TK), :].astype(BF16)
            z_scr[...] = _dot(q2_scr[...], kblk, NT)
            da_scr[...] = _dot(do2_scr[...], vblk, NT)

            def causal(c):
                return (ks + col) < (qi * TQ + (c * CH) % TQ + row)

            for c in range(n_ch):
                rows = slice(c * CH, (c + 1) * CH)
                z = z_scr[rows, :]
                e = jnp.exp(-jnp.abs(z))
                w = 1.0 + e
                l0 = jnp.minimum(-z, 0.0) - jnp.log(w)
                if masked:
                    l0 = jnp.where(causal(c), l0, 0.0)
                l0_scr[rows, :] = l0.astype(BF16)
                rinv = 1.0 / w
                beta_scr[rows, :] = jnp.where(z >= 0.0, rinv, e * rinv)
            cs_scr[...] = _dot(l0_scr[...], tri_after)
            for c in range(n_ch):
                rows = slice(c * CH, (c + 1) * CH)
                hh, r0 = (c * CH) // TQ, (c * CH) % TQ
                near = jnp.sum(jnp.where(lane == kb, rs_ref[hh, r0:r0 + CH, :], 0.0), axis=1, keepdims=True)
                a = jnp.exp(l0_scr[rows, :].astype(F32) + z_scr[rows, :] + cs_scr[rows, :] + near)
                if masked:
                    a = jnp.where(causal(c), a, 0.0)
                a_scr[rows, :] = a.astype(BF16)
                g_scr[rows, :] = (a * da_scr[rows, :]).astype(BF16)
            cg_scr[...] = _dot(g_scr[...], tri_before)
            for c in range(n_ch):
                rows = slice(c * CH, (c + 1) * CH)
                before = cg_scr[rows, :TK] + pg_scr[rows, :]
                beta = beta_scr[rows, :]
                dz = g_scr[rows, :].astype(F32) * (1.0 - beta) - beta * before
                if masked:
                    dz = jnp.where(causal(c), dz, 0.0)
                dz_scr[rows, :] = dz.astype(BF16)
                pg_scr[rows, :] += cg_scr[rows, TK:]
            dq_scr[...] += _dot(dz_scr[...], kblk)
            dk_scr[pl.ds(ks, TK), :] += _dot(dz_scr[...], q2_scr[...], TN)
            dv_scr[pl.ds(ks, TK), :] += _dot(a_scr[...], do2_scr[...], TN)

        def inner_step(kb, carry):
            step(kb, False)
            return carry

        def diag_step(it, carry):
            step(qi * diag + it, True)
            return carry

        lax.fori_loop(0, qi * diag, inner_step, 0)
        lax.fori_loop(0, diag, diag_step, 0)
        dq = dq_scr[:TQ, :] * lm[0] + dq_scr[TQ:, :] * lm[1]
        dq_ref[...] = (dq * SCALE).astype(dq_ref.dtype)

        @pl.when(qi == nq - 1)
        def _():
            dk_ref[...] = dk_scr[...].astype(dk_ref.dtype)
            dv_ref[...] = dv_scr[...].astype(dv_ref.dtype)

    stacked_f32 = pltpu.VMEM((2 * TQ, TK), F32)
    stacked_bf16 = pltpu.VMEM((2 * TQ, TK), BF16)

    def rows(c0):
        return pl.BlockSpec((TQ, LANES), lambda p, qi: (qi, c0 + p))

    def whole(c0):
        return pl.BlockSpec((s, LANES), lambda p, qi: (0, c0 + p))

    out = jax.ShapeDtypeStruct((s, ATTN_WIDTH), BF16)
    return _call(
        body, name="attn_bwd", grid=(N_HEADS // 2, nq),
        in_specs=[rows(Q_COL), whole(K_COL), whole(V_COL), rows(0),
                  pl.BlockSpec((2, TQ, LANES), lambda p, qi: (p, qi, 0))],
        out_specs=[rows(0), whole(0), whole(0)], out_shape=[out] * 3,
        scratch_shapes=[stacked_bf16, stacked_bf16, stacked_f32, stacked_f32, stacked_bf16, stacked_f32, stacked_f32,
                        stacked_bf16, stacked_bf16, pltpu.VMEM((2 * TQ, 2 * TK), F32), stacked_bf16,
                        stacked_f32, stacked_f32, pltpu.VMEM((s, LANES), F32), pltpu.VMEM((s, LANES), F32)],
        compiler_params=_params("arbitrary", "arbitrary"),
    )(proj, proj, proj, do, rsave)


def _sum_adamw(name, parts, w, m, v):
    n, r, c = parts.shape
    tr = r if r <= 256 else 256

    def body(p_ref, w_ref, m_ref, v_ref, g_ref, d_ref, nm_ref, nv_ref):
        g = p_ref[0].astype(F32)
        for j in range(1, n):
            g = g + p_ref[j].astype(F32)
        nm = ADAM_B1 * m_ref[...] + (1.0 - ADAM_B1) * g
        nv = ADAM_B2 * v_ref[...] + (1.0 - ADAM_B2) * (g * g)
        m_hat = nm / (1.0 - ADAM_B1 ** ADAM_STEP)
        v_hat = nv / (1.0 - ADAM_B2 ** ADAM_STEP)
        g_ref[...] = g
        d_ref[...] = -ADAM_LR * (m_hat / (jnp.sqrt(v_hat) + ADAM_EPS) + ADAM_WD * w_ref[...])
        nm_ref[...] = nm
        nv_ref[...] = nv

    mat = pl.BlockSpec((tr, c), lambda i: (i, 0))
    out = jax.ShapeDtypeStruct((r, c), F32)
    return _call(body, name=name, grid=(r // tr,),
                 in_specs=[pl.BlockSpec((n, tr, c), lambda i: (0, i, 0)), mat, mat, mat],
                 out_specs=[mat] * 4, out_shape=[out] * 4, compiler_params=_params("parallel"))(parts, w, m, v)


def _natural(gathered):
    _, k, n = gathered.shape
    return gathered.transpose(1, 0, 2).reshape(k, N_DEV * n)


def _relu2_epi(acc):
    r = jnp.maximum(acc, 0.0)
    return acc, r * r


def _relu2_bwd_epi(acc, a_act):
    return (acc * (2.0 * jnp.maximum(a_act, 0.0)),)


def kernel(x, c, w_ada, b_ada, g_pre_mix, g_post_mix, g_pre_mlp, g_post_mlp, w_in, conv_w, w_proj_conv, w_proj_attn, w_out, w_mlp_in, w_mlp_out, loss_target, m_w_ada, m_b_ada, m_g_pre_mix, m_g_post_mix, m_g_pre_mlp, m_g_post_mlp, m_w_in, m_conv_w, m_w_proj_conv, m_w_proj_attn, m_w_out, m_w_mlp_in, m_w_mlp_out, v_w_ada, v_b_ada, v_g_pre_mix, v_g_post_mix, v_g_pre_mlp, v_g_post_mlp, v_w_in, v_conv_w, v_w_proj_conv, v_w_proj_attn, v_w_out, v_w_mlp_in, v_w_mlp_out):
    xi, yi, ci = _mesh_pos()
    me = 4 * xi + 2 * yi + ci
    d = D_MODEL
    x0 = x[0]
    seq = x0.shape[0]
    ada_cols = w_ada.shape[2]
    conv_cols = conv_w.shape[2]

    small = jnp.concatenate([c.reshape(-1), conv_w.reshape(-1)])
    small = jnp.pad(small, (0, 2 * d - small.shape[0])).reshape(8, 2 * d // 8)
    small_all = _all_gather("gather_c", [small])[0].reshape(N_DEV, 2 * d)
    c_all = small_all[:, :d]
    conv_all = small_all[:, d:d + DEPTH * 3 * conv_cols].reshape(N_DEV, DEPTH, 3, conv_cols)
    conv_all = conv_all.transpose(1, 2, 0, 3).reshape(DEPTH, 3, N_DEV * conv_cols)
    mod_cols = jnp.stack([_mm("mod_mm", c_all, w_ada[l], "nn", N_DEV, ada_cols, d, [F32], exact=True)
                          for l in range(DEPTH)], axis=1)
    mod_all = _all_gather("gather_mod", [mod_cols.reshape(N_DEV, DEPTH * ada_cols)])[0]
    mod_mine = lax.dynamic_index_in_dim(mod_all, me, axis=1, keepdims=False).reshape(N_DEV, DEPTH, ada_cols)
    mod = mod_mine.transpose(1, 0, 2).reshape(DEPTH, N_MOD * d) + b_ada

    def gathered_weights(l):
        shards = [w_in[l], w_proj_conv[l], w_proj_attn[l], w_out[l], w_mlp_in[l], w_mlp_out[l]]
        g_in, g_pc, g_pa, g_out, g_mi, g_mo = _all_gather("gather_w", [w.astype(BF16) for w in shards])
        return (_natural(g_in), _natural(g_pc), _natural(g_pa), g_out.reshape(d, d), _natural(g_mi),
                g_mo.reshape(D_FF, d))

    weights = [gathered_weights(l) for l in range(DEPTH)]

    saved = []
    xl = x0
    for l in range(DEPTH):
        wg_in, wg_pc, wg_pa, wg_out, wg_mi, wg_mo = weights[l]
        sh1, sc1, gt1, sh2, sc2, gt2 = [mod[l:l + 1, i * d:(i + 1) * d] for i in range(N_MOD)]
        h = _prenorm_fwd(xl, g_pre_mix[l:l + 1], sc1, sh1)
        proj = _mm("proj", h, wg_in, "nn", TMM,1024, d, [F32])
        yc = _conv_fwd(proj, conv_all[l])
        y_conv = _mm("proj_conv", yc, wg_pc, "nn", TMM,d, CONV_WIDTH, [F32])
        o, rsave = _attn_fwd(proj)
        y_attn = _mm("proj_attn", o, wg_pa, "nn", TMM,d, ATTN_WIDTH, [F32])
        merged = _gate_fwd(proj, y_conv, y_attn)
        mix_out = _mm("mix_out", merged, wg_out, "nn", TMM,d, d, [F32])
        x1 = _postnorm_fwd(xl, mix_out, g_post_mix[l:l + 1], gt1)
        h2 = _prenorm_fwd(x1, g_pre_mlp[l:l + 1], sc2, sh2)
        a_act, r = _mm("mlp_in", h2, wg_mi, "nn", TMM,1024, d, [F32, BF16], epi=_relu2_epi)
        ff = _mm("mlp_out", r, wg_mo, "nn", TMM,d, 1024, [F32])
        x2 = _postnorm_fwd(x1, ff, g_post_mlp[l:l + 1], gt2)
        saved.append((xl, h, proj, yc, o, rsave, y_conv, y_attn, merged, mix_out, x1, h2, a_act, r, ff))
        xl = x2

    dxo, sq = _loss(xl, loss_target[0])
    loss = lax.psum(sq[0, 0] * (0.5 / d), ("x", "y", "c"))

    big = {}
    dmod, small_grads = [None] * DEPTH, [None] * DEPTH
    for l in reversed(range(DEPTH)):
        wg_in, wg_pc, wg_pa, wg_out, wg_mi, wg_mo = weights[l]
        xin, h, proj, yc, o, rsave, y_conv, y_attn, merged, mix_out, x1, h2, a_act, r, ff = saved[l]
        sh1, sc1, gt1, sh2, sc2, gt2 = [mod[l:l + 1, i * d:(i + 1) * d] for i in range(N_MOD)]

        dff, dgt2, dg_post_mlp = _postnorm_bwd(dxo, ff, g_post_mlp[l:l + 1], gt2)
        da = _mm("d_relu2", dff, wg_mo, "nt", TMM,1024, d, [BF16], epi=_relu2_bwd_epi, extra=(a_act,))
        gw_mo = _mm("gw_mlp_out", r, dff, "tn", 1024, d, 1024, [BF16])
        dh2 = _mm("d_h2", da, wg_mi, "nt", TMM,d, 1024, [F32])
        gw_mi = _mm("gw_mlp_in", h2, da, "tn", d, D_FF // N_DEV, 1024, [BF16], blocked_out=True)
        dx1, dsh2, dsc2, dg_pre_mlp = _prenorm_bwd(dh2, x1, g_pre_mlp[l:l + 1], sc2, dxo)

        dmix, dgt1, dg_post_mix = _postnorm_bwd(dx1, mix_out, g_post_mix[l:l + 1], gt1)
        dmerged = _mm("d_merged", dmix, wg_out, "nt", TMM,d, d, [F32])
        gw_out = _mm("gw_out", merged, dmix, "tn", d, d, 1024, [BF16])
        dy_conv, dy_attn, dga, dgb = _gate_bwd(dmerged, proj, y_conv, y_attn)
        do = _mm("d_o", dy_attn, wg_pa, "nt", TMM,ATTN_WIDTH, d, [F32])
        gw_pa = _mm("gw_proj_attn", o, dy_attn, "tn", ATTN_WIDTH, d, 1024, [BF16])
        dyc = _mm("d_yc", dy_conv, wg_pc, "nt", TMM,CONV_WIDTH, d, [F32])
        gw_pc = _mm("gw_proj_conv", yc, dy_conv, "tn", CONV_WIDTH, d, 1024, [BF16])
        dq, dk, dv = _attn_bwd(proj, do, rsave)
        dbg, dcg, du, dw0, dw1, dw2 = _conv_bwd(dyc, proj, conv_all[l])
        dproj = jnp.concatenate([dbg, dcg, du, dq, dk, dv, dga, dgb], axis=1)
        dh = _mm("d_h", dproj, wg_in, "nt", TMM,d, 1024, [F32])
        gw_in = _mm("gw_in", h, dproj, "tn", d, IN_COLS // N_DEV, 1024, [BF16], blocked_out=True)
        dxo, dsh1, dsc1, dg_pre_mix = _prenorm_bwd(dh, xin, g_pre_mix[l:l + 1], sc1, dx1)

        dmod[l] = jnp.concatenate([dsh1, dsc1, dgt1, dsh2, dsc2, dgt2], axis=1)
        small_grads[l] = (dg_pre_mix, dg_post_mix, dg_pre_mlp, dg_post_mlp, jnp.concatenate([dw0, dw1, dw2], axis=0))

        def col_blocks(gw):
            k, n = gw.shape
            return gw.reshape(k, N_DEV, n // N_DEV).transpose(1, 0, 2)

        sent = [gw_in, col_blocks(gw_pc), col_blocks(gw_pa), gw_out.reshape(N_DEV, d // N_DEV, d), gw_mi,
                gw_mo.reshape(N_DEV, D_FF // N_DEV, d)]
        parts = _exchange("exchange_gw", sent)
        names = ["w_in", "w_proj_conv", "w_proj_attn", "w_out", "w_mlp_in", "w_mlp_out"]
        olds = [(w_in, m_w_in, v_w_in), (w_proj_conv, m_w_proj_conv, v_w_proj_conv),
                (w_proj_attn, m_w_proj_attn, v_w_proj_attn), (w_out, m_w_out, v_w_out),
                (w_mlp_in, m_w_mlp_in, v_w_mlp_in), (w_mlp_out, m_w_mlp_out, v_w_mlp_out)]
        for nm, p, (w_, m_, v_) in zip(names, parts, olds):
            big[(nm, l)] = _sum_adamw("adamw_" + nm, p, w_[l], m_[l], v_[l])

    vec = jnp.concatenate(
        [dmod[l].reshape(-1) for l in range(DEPTH)]
        + [small_grads[l][i].reshape(-1) for i in range(4) for l in range(DEPTH)]
        + [small_grads[l][4].reshape(-1) for l in range(DEPTH)])
    n_vec = vec.shape[0]
    vec_all = _all_gather("gather_small", [vec.reshape(8, n_vec // 8)])[0].reshape(N_DEV, n_vec)
    n_mod = DEPTH * N_MOD * d
    dmod_all = vec_all[:, :n_mod].reshape(N_DEV, DEPTH, N_MOD * d)
    res = {}
    res["b_ada"] = _sum_adamw("adamw_b_ada", dmod_all, b_ada, m_b_ada, v_b_ada)
    off = n_mod
    for nm, (w_, m_, v_) in zip(
            ["g_pre_mix", "g_post_mix", "g_pre_mlp", "g_post_mlp"],
            [(g_pre_mix, m_g_pre_mix, v_g_pre_mix), (g_post_mix, m_g_post_mix, v_g_post_mix),
             (g_pre_mlp, m_g_pre_mlp, v_g_pre_mlp), (g_post_mlp, m_g_post_mlp, v_g_post_mlp)]):
        res[nm] = _sum_adamw("adamw_gain", vec_all[:, off:off + DEPTH * d].reshape(N_DEV, DEPTH, d), w_, m_, v_)
        off += DEPTH * d
    dconv_all = vec_all[:, off:].reshape(N_DEV, DEPTH * 3, CONV_WIDTH)
    dconv_mine = lax.dynamic_slice_in_dim(dconv_all, me * conv_cols, conv_cols, axis=2)
    res["conv_w"] = [t.reshape(DEPTH, 3, conv_cols) for t in _sum_adamw(
        "adamw_conv_w", dconv_mine, conv_w.reshape(DEPTH * 3, conv_cols), m_conv_w.reshape(DEPTH * 3, conv_cols),
        v_conv_w.reshape(DEPTH * 3, conv_cols))]

    c_t = jnp.pad(c_all.T, ((0, 0), (0, LANES - N_DEV)))
    dmod_mine = lax.dynamic_slice_in_dim(dmod_all, me * ada_cols, ada_cols, axis=2)
    ada = []
    for l in range(DEPTH):
        dm_l = jnp.pad(dmod_mine[:, l, :], ((0, LANES - N_DEV), (0, 0)))
        gw_ada = _mm("gw_ada", c_t, dm_l, "nn", 256, ada_cols, LANES, [F32], exact=True)
        ada.append(_sum_adamw("adamw_w_ada", gw_ada[None], w_ada[l], m_w_ada[l], v_w_ada[l]))
    res["w_ada"] = [jnp.stack([ada[l][i] for l in range(DEPTH)]) for i in range(4)]
    for nm in ["w_in", "w_proj_conv", "w_proj_attn", "w_out", "w_mlp_in", "w_mlp_out"]:
        res[nm] = [jnp.stack([big[(nm, l)][i] for l in range(DEPTH)]) for i in range(4)]

    order = ["w_ada", "b_ada", "g_pre_mix", "g_post_mix", "g_pre_mlp", "g_post_mlp", "w_in", "conv_w",
             "w_proj_conv", "w_proj_attn", "w_out", "w_mlp_in", "w_mlp_out"]
    outs = [loss, dxo[None]]
    for i in range(4):
        outs += [res[nm][i] for nm in order]
    return tuple(outs)
```

```python
import jax
import jax.numpy as jnp
from jax import lax
from jax.experimental import pallas as pl
from jax.experimental.pallas import tpu as pltpu

F32 = jnp.float32
BF16 = jnp.bfloat16
MESH = pl.DeviceIdType.MESH

N_DEV = 8
D_MODEL = 1024
CONV_WIDTH = 512
N_HEADS = 8
HEAD_DIM = 64
ATTN_WIDTH = N_HEADS * HEAD_DIM
D_FF = 4 * D_MODEL
N_MOD = 6
DEPTH = 2
EPS = 1e-6
IN_COLS = 3 * CONV_WIDTH + 3 * ATTN_WIDTH + 2 * D_MODEL
LANES = 128

ADAM_LR = 0.001
ADAM_B1 = 0.9
ADAM_B2 = 0.999
ADAM_EPS = 1e-08
ADAM_WD = 0.01
ADAM_STEP = 10

TM = 512
TMM = 1024
TQ = 512
TK = 128
CH = 64
VMEM_LIMIT = 56 * 1024 * 1024

NN = (((1,), (0,)), ((), ()))
NT = (((1,), (1,)), ((), ()))
TN = (((0,), (0,)), ((), ()))
_DIMS = {"nn": NN, "nt": NT, "tn": TN}


def _call(body, **kw):
    return pl.pallas_call(body, **kw)


def _params(*sem):
    return pltpu.CompilerParams(dimension_semantics=sem, vmem_limit_bytes=VMEM_LIMIT)


def _dot(a, b, dims=NN):
    return lax.dot_general(a, b, dims, preferred_element_type=F32)


def _mesh_pos():
    return lax.axis_index("x"), lax.axis_index("y"), lax.axis_index("c")


def _all_gather(name, arrs):
    n = len(arrs)

    def body(*refs):
        ins, outs = refs[:n], refs[n:2 * n]
        send_sems, recv_sems, local_sems = refs[2 * n:]
        x, y, c = _mesh_pos()
        me, sibling = (x, y, c), (x, y, 1 - c)
        chips = [(1 - x, y), (x, 1 - y), (1 - x, 1 - y)]

        def blk(t, p):
            return outs[t].at[4 * p[0] + 2 * p[1] + p[2]]

        def copy(t, k, block, to, src=None):
            return pltpu.make_async_remote_copy(
                src_ref=blk(t, block) if src is None else src, dst_ref=blk(t, block),
                send_sem=send_sems.at[7 * t + k], recv_sem=recv_sems.at[7 * t + k],
                device_id=to, device_id_type=MESH)

        mine, first, passed = [], [], []
        for t in range(n):
            cp = pltpu.make_async_copy(ins[t], blk(t, me), local_sems.at[t])
            cp.start()
            mine.append(cp)
            cps = [copy(t, 0, me, sibling, src=ins[t])]
            cps += [copy(t, 1 + j, me, (*chip, c), src=ins[t]) for j, chip in enumerate(chips)]
            for cp in cps:
                cp.start()
            first += cps
        for t in range(n):
            for j, chip in enumerate(chips):
                copy(t, 1 + j, (*chip, c), me).wait_recv()
                cp = copy(t, 4 + j, (*chip, c), sibling)
                cp.start()
                passed.append(cp)
        for t in range(n):
            copy(t, 0, sibling, me).wait_recv()
            for j, chip in enumerate(chips):
                copy(t, 4 + j, (*chip, 1 - c), me).wait_recv()
        for cp in first + passed:
            cp.wait_send()
        for cp in mine:
            cp.wait()

    any_spec = pl.BlockSpec(memory_space=pl.ANY)
    return _call(
        body, name=name,
        out_shape=[jax.ShapeDtypeStruct((N_DEV,) + a.shape, a.dtype) for a in arrs],
        in_specs=[any_spec] * n, out_specs=[any_spec] * n,
        scratch_shapes=[pltpu.SemaphoreType.DMA((7 * n,)), pltpu.SemaphoreType.DMA((7 * n,)),
                        pltpu.SemaphoreType.DMA((n,))],
    )(*arrs)


def _exchange(name, arrs):
    n = len(arrs)

    def body(*refs):
        ins, outs = refs[:n], refs[n:2 * n]
        send_sems, recv_sems, local_sems = refs[2 * n:]
        x, y, c = _mesh_pos()
        my_idx = 4 * x + 2 * y + c
        peers = []
        for k in range(1, N_DEV):
            p = (1 - x if k & 4 else x, 1 - y if k & 2 else y, 1 - c if k & 1 else c)
            peers.append((k - 1, p, 4 * p[0] + 2 * p[1] + p[2]))

        def copy(t, k, p, p_idx):
            return (pltpu.make_async_remote_copy(
                        src_ref=ins[t].at[p_idx], dst_ref=outs[t].at[my_idx],
                        send_sem=send_sems.at[7 * t + k], recv_sem=recv_sems.at[7 * t + k],
                        device_id=p, device_id_type=MESH),
                    pltpu.make_async_remote_copy(
                        src_ref=ins[t].at[p_idx], dst_ref=outs[t].at[p_idx],
                        send_sem=send_sems.at[7 * t + k], recv_sem=recv_sems.at[7 * t + k],
                        device_id=p, device_id_type=MESH))

        mine, sends, recvs = [], [], []
        for t in range(n):
            cp = pltpu.make_async_copy(ins[t].at[my_idx], outs[t].at[my_idx], local_sems.at[t])
            cp.start()
            mine.append(cp)
            for k, p, p_idx in peers:
                send, recv = copy(t, k, p, p_idx)
                send.start()
                sends.append(send)
                recvs.append(recv)
        for cp in recvs:
            cp.wait_recv()
        for cp in sends:
            cp.wait_send()
        for cp in mine:
            cp.wait()

    any_spec = pl.BlockSpec(memory_space=pl.ANY)
    return _call(
        body, name=name,
        out_shape=[jax.ShapeDtypeStruct(a.shape, a.dtype) for a in arrs],
        in_specs=[any_spec] * n, out_specs=[any_spec] * n,
        scratch_shapes=[pltpu.SemaphoreType.DMA((7 * n,)), pltpu.SemaphoreType.DMA((7 * n,)),
                        pltpu.SemaphoreType.DMA((n,))],
    )(*arrs)


def _mm(name, a, b, mode, tm, tn, tk, out_dtypes, epi=None, extra=(), blocked_out=False, exact=False):
    if mode == "nn":
        (m, k), n = a.shape, b.shape[1]
    elif mode == "nt":
        (m, k), n = a.shape, b.shape[0]
    else:
        (k, m), n = a.shape, b.shape[1]
    nk = k // tk
    grid = (m // tm, n // tn, nk)
    n_extra, n_out = len(extra), len(out_dtypes)

    def body(*refs):
        a_ref, b_ref = refs[0], refs[1]
        extra_refs = refs[2:2 + n_extra]
        out_refs = refs[2 + n_extra:2 + n_extra + n_out]
        if exact:
            p = lax.dot_general(a_ref[...], b_ref[...], _DIMS[mode], preferred_element_type=F32,
                                precision=lax.Precision.HIGHEST)
        else:
            p = _dot(a_ref[...].astype(BF16), b_ref[...].astype(BF16), _DIMS[mode])

        def finish(acc):
            outs = (acc,) if epi is None else epi(acc, *[r[...] for r in extra_refs])
            for r, o in zip(out_refs, outs):
                r[...] = o.astype(r.dtype)

        if nk == 1:
            finish(p)
        else:
            acc_ref = refs[-1]
            kk = pl.program_id(2)

            @pl.when(kk == 0)
            def _():
                acc_ref[...] = p

            @pl.when(kk > 0)
            def _():
                acc_ref[...] += p

            @pl.when(kk == nk - 1)
            def _():
                finish(acc_ref[...])

    if mode == "tn":
        a_spec = pl.BlockSpec((tk, tm), lambda i, j, kk: (kk, i))
    else:
        a_spec = pl.BlockSpec((tm, tk), lambda i, j, kk: (i, kk))
    if mode == "nt":
        b_spec = pl.BlockSpec((tn, tk), lambda i, j, kk: (j, kk))
    else:
        b_spec = pl.BlockSpec((tk, tn), lambda i, j, kk: (kk, j))
    tile = pl.BlockSpec((tm, tn), lambda i, j, kk: (i, j))
    if blocked_out:
        o_shape, o_spec = (n // tn, m, tn), pl.BlockSpec((None, tm, tn), lambda i, j, kk: (j, i, 0))
    else:
        o_shape, o_spec = (m, n), tile
    out = _call(
        body, name=name, grid=grid,
        in_specs=[a_spec, b_spec] + [tile] * n_extra,
        out_specs=[o_spec] * n_out,
        out_shape=[jax.ShapeDtypeStruct(o_shape, dt) for dt in out_dtypes],
        scratch_shapes=[pltpu.VMEM((tm, tn), F32)] if nk > 1 else [],
        compiler_params=_params("parallel", "parallel", "arbitrary"),
    )(a, b, *extra)
    return out[0] if n_out == 1 else out


def _tile(width, col=0, rows=TM):
    return pl.BlockSpec((rows, width), lambda i: (i, col))


def _vec(width):
    return pl.BlockSpec((1, width), lambda i: (0, 0))


def _rstd(xf):
    return lax.rsqrt(jnp.mean(xf * xf, axis=-1, keepdims=True) + EPS)


def _colsum(v):
    return jnp.sum(v, axis=0, keepdims=True)


def _accumulate(refs, vals):
    first = pl.program_id(0) == 0

    @pl.when(first)
    def _():
        for r, v in zip(refs, vals):
            r[...] = v

    @pl.when(jnp.logical_not(first))
    def _():
        for r, v in zip(refs, vals):
            r[...] += v


def _prenorm_fwd(x, g, sc, sh):
    s, d = x.shape

    def body(x_ref, g_ref, sc_ref, sh_ref, h_ref):
        xf = x_ref[...]
        y = (xf * _rstd(xf)) * g_ref[...]
        h_ref[...] = (y * (1.0 + sc_ref[...]) + sh_ref[...]).astype(h_ref.dtype)

    return _call(body, name="prenorm_fwd", grid=(s // TM,),
                 in_specs=[_tile(d), _vec(d), _vec(d), _vec(d)], out_specs=_tile(d),
                 out_shape=jax.ShapeDtypeStruct((s, d), BF16), compiler_params=_params("parallel"))(x, g, sc, sh)


def _prenorm_bwd(dh, x, g, sc, dres):
    s, d = x.shape

    def body(dh_ref, x_ref, g_ref, sc_ref, dres_ref, dx_ref, dsh_ref, dsc_ref, dg_ref):
        xf, dhf = x_ref[...], dh_ref[...]
        rstd = _rstd(xf)
        xhat = xf * rstd
        one_sc = 1.0 + sc_ref[...]
        dxhat = dhf * (g_ref[...] * one_sc)
        dx_ref[...] = dres_ref[...] + rstd * (dxhat - xhat * jnp.mean(dxhat * xhat, axis=-1, keepdims=True))
        dhx = dhf * xhat
        _accumulate((dsh_ref, dsc_ref, dg_ref), (_colsum(dhf), _colsum(dhx) * g_ref[...], _colsum(dhx) * one_sc))

    vec_out = jax.ShapeDtypeStruct((1, d), F32)
    return _call(body, name="prenorm_bwd", grid=(s // TM,),
                 in_specs=[_tile(d), _tile(d), _vec(d), _vec(d), _tile(d)],
                 out_specs=[_tile(d), _vec(d), _vec(d), _vec(d)],
                 out_shape=[jax.ShapeDtypeStruct((s, d), F32), vec_out, vec_out, vec_out],
                 compiler_params=_params("arbitrary"))(dh, x, g, sc, dres)


def _postnorm_fwd(xres, m, g, gt):
    s, d = m.shape

    def body(x_ref, m_ref, g_ref, gt_ref, o_ref):
        mf = m_ref[...]
        o_ref[...] = x_ref[...] + gt_ref[...] * ((mf * _rstd(mf)) * g_ref[...])

    return _call(body, name="postnorm_fwd", grid=(s // TM,),
                 in_specs=[_tile(d), _tile(d), _vec(d), _vec(d)], out_specs=_tile(d),
                 out_shape=jax.ShapeDtypeStruct((s, d), F32), compiler_params=_params("parallel"))(xres, m, g, gt)


def _postnorm_bwd(dxn, m, g, gt):
    s, d = m.shape

    def body(dx_ref, m_ref, g_ref, gt_ref, dm_ref, dgt_ref, dg_ref):
        mf, dxf = m_ref[...], dx_ref[...]
        rstd = _rstd(mf)
        mhat = mf * rstd
        dmhat = dxf * (gt_ref[...] * g_ref[...])
        dm_ref[...] = (rstd * (dmhat - mhat * jnp.mean(dmhat * mhat, axis=-1, keepdims=True))).astype(dm_ref.dtype)
        dxm = _colsum(dxf * mhat)
        _accumulate((dgt_ref, dg_ref), (dxm * g_ref[...], dxm * gt_ref[...]))

    vec_out = jax.ShapeDtypeStruct((1, d), F32)
    return _call(body, name="postnorm_bwd", grid=(s // TM,),
                 in_specs=[_tile(d), _tile(d), _vec(d), _vec(d)], out_specs=[_tile(d), _vec(d), _vec(d)],
                 out_shape=[jax.ShapeDtypeStruct((s, d), BF16), vec_out, vec_out],
                 compiler_params=_params("arbitrary"))(dxn, m, g, gt)


def _loss(y, target):
    s, d = y.shape

    def body(y_ref, t_ref, dy_ref, sq_ref):
        err = y_ref[...] - t_ref[...]
        dy_ref[...] = err * (1.0 / d)
        tot = jnp.sum(_colsum(err * err), axis=1, keepdims=True)
        _accumulate((sq_ref,), (jnp.broadcast_to(tot, (1, LANES)),))

    return _call(body, name="loss", grid=(s // TM,), in_specs=[_tile(d), _tile(d)],
                 out_specs=[_tile(d), _vec(LANES)],
                 out_shape=[jax.ShapeDtypeStruct((s, d), F32), jax.ShapeDtypeStruct((1, LANES), F32)],
                 compiler_params=_params("arbitrary"))(y, target)


def _sigmoid(v):
    return 1.0 / (1.0 + jnp.exp(-v))


def _gate_fwd(proj, y_conv, y_attn):
    s, d = y_conv.shape
    ga_col, gb_col = (IN_COLS - 2 * d) // d, (IN_COLS - d) // d

    def body(ga_ref, gb_ref, yc_ref, ya_ref, o_ref):
        o_ref[...] = (_sigmoid(ga_ref[...]) * yc_ref[...] + _sigmoid(gb_ref[...]) * ya_ref[...]).astype(o_ref.dtype)

    return _call(body, name="gate_fwd", grid=(s // TM,),
                 in_specs=[_tile(d, ga_col), _tile(d, gb_col), _tile(d), _tile(d)], out_specs=_tile(d),
                 out_shape=jax.ShapeDtypeStruct((s, d), BF16),
                 compiler_params=_params("parallel"))(proj, proj, y_conv, y_attn)


def _gate_bwd(dmerged, proj, y_conv, y_attn):
    s, d = y_conv.shape
    ga_col, gb_col = (IN_COLS - 2 * d) // d, (IN_COLS - d) // d

    def body(dm_ref, ga_ref, gb_ref, yc_ref, ya_ref, dyc_ref, dya_ref, dga_ref, dgb_ref):
        dm = dm_ref[...]
        sa, sb = _sigmoid(ga_ref[...]), _sigmoid(gb_ref[...])
        dyc_ref[...] = (dm * sa).astype(BF16)
        dya_ref[...] = (dm * sb).astype(BF16)
        dga_ref[...] = (dm * yc_ref[...] * (sa * (1.0 - sa))).astype(BF16)
        dgb_ref[...] = (dm * ya_ref[...] * (sb * (1.0 - sb))).astype(BF16)

    out = jax.ShapeDtypeStruct((s, d), BF16)
    return _call(body, name="gate_bwd", grid=(s // TM,),
                 in_specs=[_tile(d), _tile(d, ga_col), _tile(d, gb_col), _tile(d), _tile(d)],
                 out_specs=[_tile(d)] * 4, out_shape=[out] * 4,
                 compiler_params=_params("parallel"))(dmerged, proj, proj, y_conv, y_attn)


def _shift_down(prev8, cur, by):
    ext = jnp.concatenate([prev8, cur], axis=0)
    return pltpu.roll(ext, by, 0)[8:]


def _shift_up(cur, next8, by):
    ext = jnp.concatenate([cur, next8], axis=0)
    return pltpu.roll(ext, ext.shape[0] - by, 0)[:cur.shape[0]]


def _conv_fwd(proj, conv_w):
    s, w = proj.shape[0], CONV_WIDTH
    per8 = TM // 8

    def prev(col):
        return pl.BlockSpec((8, w), lambda i: (jnp.maximum(i * per8 - 1, 0), col))

    def body(bg_ref, cg_ref, u_ref, cgp_ref, up_ref, w_ref, o_ref):
        vv = cg_ref[...] * u_ref[...]
        pv = cgp_ref[...] * up_ref[...] * jnp.where(pl.program_id(0) > 0, 1.0, 0.0)
        y = w_ref[0:1, :] * _shift_down(pv, vv, 2) + w_ref[1:2, :] * _shift_down(pv, vv, 1) + w_ref[2:3, :] * vv
        o_ref[...] = (bg_ref[...] * y).astype(o_ref.dtype)

    return _call(body, name="conv_fwd", grid=(s // TM,),
                 in_specs=[_tile(w, 0), _tile(w, 1), _tile(w, 2), prev(1), prev(2),
                           pl.BlockSpec((3, w), lambda i: (0, 0))],
                 out_specs=_tile(w), out_shape=jax.ShapeDtypeStruct((s, w), BF16),
                 compiler_params=_params("parallel"))(proj, proj, proj, proj, proj, conv_w)


def _conv_bwd(dyc, proj, conv_w):
    s, w = proj.shape[0], CONV_WIDTH
    per8 = TM // 8
    n_tiles = s // TM

    def prev(col):
        return pl.BlockSpec((8, w), lambda i: (jnp.maximum(i * per8 - 1, 0), col))

    def nxt(col):
        return pl.BlockSpec((8, w), lambda i: (jnp.minimum((i + 1) * per8, s // 8 - 1), col))

    def body(dyc_ref, bg_ref, cg_ref, u_ref, cgp_ref, up_ref, dycn_ref, bgn_ref, w_ref,
             dbg_ref, dcg_ref, du_ref, dw0_ref, dw1_ref, dw2_ref):
        i = pl.program_id(0)
        cg, u = cg_ref[...], u_ref[...]
        vv = cg * u
        pv = cgp_ref[...] * up_ref[...] * jnp.where(i > 0, 1.0, 0.0)
        v1, v2 = _shift_down(pv, vv, 1), _shift_down(pv, vv, 2)
        w0, w1, w2 = w_ref[0:1, :], w_ref[1:2, :], w_ref[2:3, :]
        dyc_t = dyc_ref[...]
        dbg_ref[...] = (dyc_t * (w0 * v2 + w1 * v1 + w2 * vv)).astype(BF16)
        dy = dyc_t * bg_ref[...]
        dyn = dycn_ref[...] * bgn_ref[...] * jnp.where(i < n_tiles - 1, 1.0, 0.0)
        dvv = w2 * dy + w1 * _shift_up(dy, dyn, 1) + w0 * _shift_up(dy, dyn, 2)
        dcg_ref[...] = (dvv * u).astype(BF16)
        du_ref[...] = (dvv * cg).astype(BF16)
        _accumulate((dw0_ref, dw1_ref, dw2_ref), (_colsum(dy * v2), _colsum(dy * v1), _colsum(dy * vv)))

    act = jax.ShapeDtypeStruct((s, w), BF16)
    tap = jax.ShapeDtypeStruct((1, w), F32)
    return _call(body, name="conv_bwd", grid=(n_tiles,),
                 in_specs=[_tile(w), _tile(w, 0), _tile(w, 1), _tile(w, 2), prev(1), prev(2), nxt(0), nxt(0),
                           pl.BlockSpec((3, w), lambda i: (0, 0))],
                 out_specs=[_tile(w)] * 3 + [_vec(w)] * 3, out_shape=[act] * 3 + [tap] * 3,
                 compiler_params=_params("arbitrary"))(dyc, proj, proj, proj, proj, proj, dyc, proj, conv_w)


Q_COL = 3 * CONV_WIDTH // LANES
K_COL = Q_COL + ATTN_WIDTH // LANES
V_COL = K_COL + ATTN_WIDTH // LANES
SCALE = HEAD_DIM ** -0.5


def _head_lanes(hh):
    lane = lax.broadcasted_iota(jnp.int32, (1, LANES), 1)
    return jnp.where((lane >= hh * HEAD_DIM) & (lane < (hh + 1) * HEAD_DIM), 1.0, 0.0)


def _tri(width, keep):
    j = lax.broadcasted_iota(jnp.int32, (TK, width), 0)
    s = lax.broadcasted_iota(jnp.int32, (TK, width), 1)
    return jnp.where((s >= TK) | keep(j, s), 1.0, 0.0).astype(BF16)


def _split_dot(v, tri):
    hi = v.astype(BF16)
    lo = (v - hi.astype(F32)).astype(BF16)
    return _dot(hi, tri) + _dot(lo, tri)


def _logits(qm, kblk, ks, tpos):
    z = _dot(qm, kblk, NT)
    mask = (ks + lax.broadcasted_iota(jnp.int32, (1, TK), 1)) < tpos
    e = jnp.exp(-jnp.abs(z))
    l0 = jnp.where(mask, -(jnp.maximum(z, 0.0) + jnp.log(1.0 + e)), 0.0)
    return z, mask, e, l0


def _neg_softplus(z):
    return jnp.minimum(-z, 0.0) - jnp.log(1.0 + jnp.exp(-jnp.abs(z)))


W2 = 2 * TK
PIPE = 4


def _softplus(z):
    neg_abs = lax.bitcast_convert_type(lax.bitcast_convert_type(z, jnp.uint32) | jnp.uint32(0x80000000), F32)
    return jnp.maximum(z, 0.0) + jnp.log(1.0 + jnp.exp(neg_abs))


def _pair_rows(ref, kb, lm):
    blk = ref[pl.ds(pl.multiple_of(kb * TK, TK), TK), :]
    return jnp.concatenate([(blk * lm[0]).astype(BF16), (blk * lm[1]).astype(BF16)], axis=0)


def _pair_tri(keep):
    j = lax.broadcasted_iota(jnp.int32, (W2, 2 * W2), 0)
    s = lax.broadcasted_iota(jnp.int32, (W2, 2 * W2), 1)
    same_head = (j >= TK) == ((s & (W2 - 1)) >= TK)
    return jnp.where(same_head & ((s >= W2) | keep(j & (TK - 1), s & (TK - 1))), 1.0, 0.0).astype(BF16)


def _attn_fwd(proj):
    s = proj.shape[0]
    nq = s // TQ
    diag = TQ // TK
    n_ch = TQ // CH
    assert s // TK <= TK and diag % PIPE == 0

    def body(q_ref, k_ref, v_ref, o_ref, rs_ref, qb_scr, tri_scr, z_scr, l0_scr, cs_scr, a_scr, r_scr, rall_scr, acc_scr):
        qi = pl.program_id(1)
        lm = [_head_lanes(0), _head_lanes(1)]
        lane = lax.broadcasted_iota(jnp.int32, (CH, W2), 1) & (TK - 1)
        row = lax.broadcasted_iota(jnp.int32, (CH, 1), 0)
        col = lax.broadcasted_iota(jnp.int32, (1, W2), 1) & (TK - 1)
        qb_scr[...] = (q_ref[...] * SCALE).astype(BF16)
        tri_scr[...] = _pair_tri(lambda j, ss: j > ss)
        r_scr[...] = jnp.zeros_like(r_scr)
        rall_scr[...] = jnp.zeros_like(rall_scr)
        acc_scr[...] = jnp.zeros_like(acc_scr)

        def causal(kb, c):
            return (kb * TK + col) < (qi * TQ + c * CH + row)

        def chunks(r0):
            return range(0 if r0 is None else r0 // CH, n_ch)

        def on_diagonal(r0, c):
            return r0 is not None and c * CH < r0 + TK

        def logits(kb, zb, r0=0):
            z_scr[zb, r0:, :] = _dot(qb_scr[r0:, :], _pair_rows(k_ref, kb, lm), NT)

        def log_one_minus_beta(kb, zb, lb, r0=None):
            for c in chunks(r0):
                rows = slice(c * CH, (c + 1) * CH)
                sp = _softplus(z_scr[zb, rows, :])
                if on_diagonal(r0, c):
                    sp = jnp.where(causal(kb, c), sp, 0.0)
                l0_scr[lb, rows, :] = sp.astype(BF16)

        def sums(lb, r0=0):
            cs_scr[r0:, :] = _dot(l0_scr[lb, r0:, :], tri_scr[...])

        def weights(kb, zb, lb, ab, r0=None):
            for c in chunks(r0):
                rows = slice(c * CH, (c + 1) * CH)
                near = r_scr[rows, :]
                a = jnp.exp(z_scr[zb, rows, :] - l0_scr[lb, rows, :].astype(F32) - cs_scr[rows, :W2] - near)
                if on_diagonal(r0, c):
                    a = jnp.where(causal(kb, c), a, 0.0)
                a_scr[ab, rows, :] = a.astype(BF16)
                rall_scr[rows, :] = jnp.where(lane == kb, near, rall_scr[rows, :])
                r_scr[rows, :] = near + cs_scr[rows, W2:]

        def weighted_values(kb, ab, r0=0):
            acc_scr[r0:, :] += _dot(a_scr[ab, r0:, :], _pair_rows(v_ref, kb, lm))

        n = qi * diag
        for dd in reversed(range(diag)):
            kb, r0 = n + dd, dd * TK
            logits(kb, dd % PIPE, r0)
            log_one_minus_beta(kb, dd % PIPE, dd % 2, r0)
            sums(dd % 2, r0)
            weights(kb, dd % PIPE, dd % 2, dd % 2, r0)
            weighted_values(kb, dd % 2, r0)

        def block(j):
            return jnp.maximum(n - 1 - j, 0)

        a_scr[...] = jnp.zeros_like(a_scr)
        logits(block(0), 0)
        logits(block(1), 1)
        log_one_minus_beta(block(0), 0, 0)

        def trip(m, carry):
            for u in range(PIPE):
                j = PIPE * m + u
                weighted_values(block(j - 1), (u - 1) % 2)
                sums(u % 2)
                logits(block(j + 2), (u + 2) % PIPE)
                log_one_minus_beta(block(j + 1), (u + 1) % PIPE, (u + 1) % 2)
                weights(block(j), u % PIPE, u % 2, u % 2)
            return carry

        lax.fori_loop(0, n // PIPE, trip, 0)
        weighted_values(block(n - 1), (PIPE - 1) % 2)
        rs_ref[...] = rall_scr[...]
        o_ref[...] = acc_scr[...]

    return _call(
        body, name="attn_fwd", grid=(N_HEADS // 2, nq),
        in_specs=[pl.BlockSpec((TQ, LANES), lambda p, qi: (qi, Q_COL + p)),
                  pl.BlockSpec((s, LANES), lambda p, qi: (0, K_COL + p)),
                  pl.BlockSpec((s, LANES), lambda p, qi: (0, V_COL + p))],
        out_specs=[pl.BlockSpec((TQ, LANES), lambda p, qi: (qi, p)),
                   pl.BlockSpec((TQ, W2), lambda p, qi: (qi, p))],
        out_shape=[jax.ShapeDtypeStruct((s, ATTN_WIDTH), F32), jax.ShapeDtypeStruct((s, N_HEADS // 2 * W2), F32)],
        scratch_shapes=[pltpu.VMEM((TQ, LANES), BF16), pltpu.VMEM((W2, 2 * W2), BF16),
                        pltpu.VMEM((PIPE, TQ, W2), F32), pltpu.VMEM((2, TQ, W2), BF16),
                        pltpu.VMEM((TQ, 2 * W2), F32), pltpu.VMEM((2, TQ, W2), BF16),
                        pltpu.VMEM((TQ, W2), F32), pltpu.VMEM((TQ, W2), F32), pltpu.VMEM((TQ, LANES), F32)],
        compiler_params=_params("parallel", "arbitrary"),
    )(proj, proj, proj)


def _attn_fwd_rows(proj):
    s = proj.shape[0]
    nq = s // TQ
    diag = TQ // TK
    n_ch = 2 * TQ // CH
    assert s // TK <= LANES

    def body(q_ref, k_ref, v_ref, o_ref, rs_ref, q2_scr, z_scr, l0_scr, cs_scr, a_scr, r_scr, rall_scr, acc_scr):
        qi = pl.program_id(1)
        lm = [_head_lanes(0), _head_lanes(1)]
        tri = _tri(2 * TK, lambda j, ss: j > ss)
        lane = lax.broadcasted_iota(jnp.int32, (CH, LANES), 1)
        row = lax.broadcasted_iota(jnp.int32, (CH, 1), 0)
        col = lax.broadcasted_iota(jnp.int32, (1, TK), 1)
        for hh in range(2):
            q2_scr[hh * TQ:(hh + 1) * TQ, :] = (q_ref[...] * (SCALE * lm[hh])).astype(BF16)
        r_scr[...] = jnp.zeros_like(r_scr)
        rall_scr[...] = jnp.zeros_like(rall_scr)
        acc_scr[...] = jnp.zeros_like(acc_scr)

        def step(kb, masked):
            ks = pl.multiple_of(kb * TK, TK)
            kblk = k_ref[pl.ds(ks, TK), :].astype(BF16)
            vf = v_ref[pl.ds(ks, TK), :]
            v2 = jnp.concatenate([(vf * lm[0]).astype(BF16), (vf * lm[1]).astype(BF16)], axis=0)
            z_scr[...] = _dot(q2_scr[...], kblk, NT)

            def causal(c):
                return (ks + col) < (qi * TQ + (c * CH) % TQ + row)

            for c in range(n_ch):
                rows = slice(c * CH, (c + 1) * CH)
                l0 = _neg_softplus(z_scr[rows, :])
                if masked:
                    l0 = jnp.where(causal(c), l0, 0.0)
                l0_scr[rows, :] = l0.astype(BF16)
            cs_scr[...] = _dot(l0_scr[...], tri)
            for c in range(n_ch):
                rows = slice(c * CH, (c + 1) * CH)
                hh, r0 = (c * CH) // TQ, (c * CH) % TQ
                near = r_scr[rows, :]
                a = jnp.exp(l0_scr[rows, :].astype(F32) + z_scr[rows, :] + cs_scr[rows, :TK] + near)
                if masked:
                    a = jnp.where(causal(c), a, 0.0)
                a_scr[r0:r0 + CH, hh * TK:(hh + 1) * TK] = a.astype(BF16)
                rall_scr[rows, :] = jnp.where(lane == kb, near, rall_scr[rows, :])
                r_scr[rows, :] = near + cs_scr[rows, TK:]
            acc_scr[...] += _dot(a_scr[...], v2)

        def diag_step(it, carry):
            step((qi + 1) * diag - 1 - it, True)
            return carry

        def inner_step(it, carry):
            step(qi * diag - 1 - it, False)
            return carry

        lax.fori_loop(0, diag, diag_step, 0)
        lax.fori_loop(0, qi * diag, inner_step, 0)
        for hh in range(2):
            rs_ref[hh] = rall_scr[hh * TQ:(hh + 1) * TQ, :]
        o_ref[...] = acc_scr[...]

    return _call(
        body, name="attn_fwd", grid=(N_HEADS // 2, nq),
        in_specs=[pl.BlockSpec((TQ, LANES), lambda p, qi: (qi, Q_COL + p)),
                  pl.BlockSpec((s, LANES), lambda p, qi: (0, K_COL + p)),
                  pl.BlockSpec((s, LANES), lambda p, qi: (0, V_COL + p))],
        out_specs=[pl.BlockSpec((TQ, LANES), lambda p, qi: (qi, p)),
                   pl.BlockSpec((2, TQ, LANES), lambda p, qi: (p, qi, 0))],
        out_shape=[jax.ShapeDtypeStruct((s, ATTN_WIDTH), F32), jax.ShapeDtypeStruct((N_HEADS, s, LANES), F32)],
        scratch_shapes=[pltpu.VMEM((2 * TQ, LANES), BF16), pltpu.VMEM((2 * TQ, TK), F32),
                        pltpu.VMEM((2 * TQ, TK), BF16), pltpu.VMEM((2 * TQ, 2 * TK), F32),
                        pltpu.VMEM((TQ, 2 * TK), BF16), pltpu.VMEM((2 * TQ, LANES), F32),
                        pltpu.VMEM((2 * TQ, LANES), F32), pltpu.VMEM((TQ, LANES), F32)],
        compiler_params=_params("parallel", "arbitrary"),
    )(proj, proj, proj)


def _attn_bwd(proj, do, rsave):
    s = proj.shape[0]
    nq = s // TQ
    diag = TQ // TK
    n_ch = TQ // CH
    assert diag % PIPE == 0

    def body(q_ref, k_ref, v_ref, do_ref, rs_ref, dq_ref, dk_ref, dv_ref,
             qb_scr, dob_scr, after_scr, before_scr, z_scr, da_scr, l0_scr, beta_scr, cs_scr, a_scr, g_scr, cg_scr,
             dz_scr, pg_scr, dq_scr, dk_scr, dv_scr):
        qi = pl.program_id(1)

        @pl.when(qi == 0)
        def _():
            dk_scr[...] = jnp.zeros_like(dk_scr)
            dv_scr[...] = jnp.zeros_like(dv_scr)

        lm = [_head_lanes(0), _head_lanes(1)]
        lane = lax.broadcasted_iota(jnp.int32, (CH, TK), 1)
        row = lax.broadcasted_iota(jnp.int32, (CH, 1), 0)
        col = lax.broadcasted_iota(jnp.int32, (1, W2), 1) & (TK - 1)
        qb_scr[...] = (q_ref[...] * SCALE).astype(BF16)
        dob_scr[...] = do_ref[...].astype(BF16)
        after_scr[...] = _pair_tri(lambda j, ss: j > ss)[:, :W2]
        before_scr[...] = _pair_tri(lambda j, ss: j < ss)
        pg_scr[...] = jnp.zeros_like(pg_scr)
        dq_scr[...] = jnp.zeros_like(dq_scr)
        dz_scr[...] = jnp.zeros_like(dz_scr)

        def causal(kb, c):
            return (kb * TK + col) < (qi * TQ + c * CH + row)

        def chunks(r0):
            return range(0 if r0 is None else r0 // CH, n_ch)

        def on_diagonal(r0, c):
            return r0 is not None and c * CH < r0 + TK

        def logits(kb, zb, r0=0):
            z_scr[zb, r0:, :] = _dot(qb_scr[r0:, :], _pair_rows(k_ref, kb, lm), NT)

        def do_dot_v(kb, db, r0=0):
            da_scr[db, r0:, :] = _dot(dob_scr[r0:, :], _pair_rows(v_ref, kb, lm), NT)

        def gates(kb, zb, lb, bb, r0=None):
            for c in chunks(r0):
                rows = slice(c * CH, (c + 1) * CH)
                z = z_scr[zb, rows, :]
                sp = _softplus(z)
                beta_scr[bb, rows, :] = jnp.exp(z - sp)
                if on_diagonal(r0, c):
                    sp = jnp.where(causal(kb, c), sp, 0.0)
                l0_scr[lb, rows, :] = sp.astype(BF16)

        def suffix_sums(lb, r0=0):
            cs_scr[r0:, :] = _dot(l0_scr[lb, r0:, :], after_scr[...])

        def weights(kb, zb, lb, db, ab, r0=None):
            for c in chunks(r0):
                rows = slice(c * CH, (c + 1) * CH)
                keep = (kb * TK + lane) < (qi * TQ + c * CH + row) if on_diagonal(r0, c) else None
                for hh in range(2):
                    cols = slice(hh * TK, (hh + 1) * TK)
                    near = jnp.sum(jnp.where(lane == kb, rs_ref[rows, cols], 0.0), axis=1, keepdims=True)
                    a = jnp.exp(z_scr[zb, rows, cols] - l0_scr[lb, rows, cols].astype(F32) - cs_scr[rows, cols] - near)
                    if keep is not None:
                        a = jnp.where(keep, a, 0.0)
                    a_scr[ab, rows, cols] = a.astype(BF16)
                    g_scr[ab, rows, cols] = (a * da_scr[db, rows, cols]).astype(BF16)

        def prefix_sums(ab, r0=0):
            cg_scr[r0:, :] = _dot(g_scr[ab, r0:, :], before_scr[...])

        def dlogits(kb, ab, bb, zb2, r0=None):
            for c in chunks(r0):
                rows = slice(c * CH, (c + 1) * CH)
                before = cg_scr[rows, :W2] + pg_scr[rows, :]
                beta = beta_scr[bb, rows, :]
                dz = g_scr[ab, rows, :].astype(F32) * (1.0 - beta) - beta * before
                if on_diagonal(r0, c):
                    dz = jnp.where(causal(kb, c), dz, 0.0)
                dz_scr[zb2, rows, :] = dz.astype(BF16)
                pg_scr[rows, :] += cg_scr[rows, W2:]

        def fold(t):
            return t[:TK, :] * lm[0] + t[TK:, :] * lm[1]

        def dq_dk(kb, zb2, r0=0):
            dq_scr[r0:, :] += _dot(dz_scr[zb2, r0:, :], _pair_rows(k_ref, kb, lm))
            dk_scr[pl.ds(pl.multiple_of(kb * TK, TK), TK), :] += fold(_dot(dz_scr[zb2, r0:, :], qb_scr[r0:, :], TN))

        def dv(kb, ab, r0=0):
            dv_scr[pl.ds(pl.multiple_of(kb * TK, TK), TK), :] += fold(_dot(a_scr[ab, r0:, :], dob_scr[r0:, :], TN))

        n = qi * diag

        def block(j):
            return jnp.clip(j, 0, jnp.maximum(n - 1, 0))

        logits(block(0), 0)
        logits(block(1), 1)
        logits(block(2), 2)
        do_dot_v(block(0), 0)
        do_dot_v(block(1), 1)
        gates(block(0), 0, 0, 0)
        gates(block(1), 1, 1, 1)
        suffix_sums(0)
        weights(block(0), 0, 0, 0, 0)

        def trip(m, carry):
            for u in range(PIPE):
                t = PIPE * m + u
                dq_dk(block(t - 1), (u - 1) % 2)
                dv(block(t), u % 2)
                prefix_sums(u % 2)
                suffix_sums((u + 1) % 2)
                logits(block(t + 3), (u + 3) % PIPE)
                do_dot_v(block(t + 2), u % 2)
                gates(block(t + 2), (u + 2) % PIPE, u % 2, (u + 2) % PIPE)
                weights(block(t + 1), (u + 1) % PIPE, (u + 1) % 2, (u + 1) % 2, (u + 1) % 2)
                dlogits(block(t), u % 2, u % PIPE, u % 2)
            return carry

        lax.fori_loop(0, n // PIPE, trip, 0)
        dq_dk(block(n - 1), (PIPE - 1) % 2)

        for dd in range(diag):
            kb, r0, two, four = n + dd, dd * TK, dd % 2, dd % PIPE
            logits(kb, four, r0)
            do_dot_v(kb, two, r0)
            gates(kb, four, two, four, r0)
            suffix_sums(two, r0)
            weights(kb, four, two, two, two, r0)
            dv(kb, two, r0)
            prefix_sums(two, r0)
            dlogits(kb, two, four, two, r0)
            dq_dk(kb, two, r0)
        dq_ref[...] = (dq_scr[...] * SCALE).astype(dq_ref.dtype)

        @pl.when(qi == nq - 1)
        def _():
            dk_ref[...] = dk_scr[...].astype(dk_ref.dtype)
            dv_ref[...] = dv_scr[...].astype(dv_ref.dtype)

    def rows(c0):
        return pl.BlockSpec((TQ, LANES), lambda p, qi: (qi, c0 + p))

    def whole(c0):
        return pl.BlockSpec((s, LANES), lambda p, qi: (0, c0 + p))

    def f32(*shape):
        return pltpu.VMEM(shape, F32)

    def bf16(*shape):
        return pltpu.VMEM(shape, BF16)

    out = jax.ShapeDtypeStruct((s, ATTN_WIDTH), BF16)
    return _call(
        body, name="attn_bwd", grid=(N_HEADS // 2, nq),
        in_specs=[rows(Q_COL), whole(K_COL), whole(V_COL), rows(0), pl.BlockSpec((TQ, W2), lambda p, qi: (qi, p))],
        out_specs=[rows(0), whole(0), whole(0)], out_shape=[out] * 3,
        scratch_shapes=[bf16(TQ, LANES), bf16(TQ, LANES), bf16(W2, W2), bf16(W2, 2 * W2),
                        f32(PIPE, TQ, W2), f32(2, TQ, W2), bf16(2, TQ, W2), f32(PIPE, TQ, W2), f32(TQ, W2),
                        bf16(2, TQ, W2), bf16(2, TQ, W2), f32(TQ, 2 * W2), bf16(2, TQ, W2),
                        f32(TQ, W2), f32(TQ, LANES), f32(s, LANES), f32(s, LANES)],
        compiler_params=_params("arbitrary", "arbitrary"),
    )(proj, proj, proj, do, rsave)


def _attn_bwd_rows(proj, do, rsave):
    s = proj.shape[0]
    nq = s // TQ

    diag = TQ // TK
    n_ch = 2 * TQ // CH

    def body(q_ref, k_ref, v_ref, do_ref, rs_ref, dq_ref, dk_ref, dv_ref,
             q2_scr, do2_scr, z_scr, da_scr, l0_scr, beta_scr, cs_scr, a_scr, g_scr, cg_scr, dz_scr,
             pg_scr, dq_scr, dk_scr, dv_scr):
        qi = pl.program_id(1)

        @pl.when(qi == 0)
        def _():
            dk_scr[...] = jnp.zeros_like(dk_scr)
            dv_scr[...] = jnp.zeros_like(dv_scr)

        lm = [_head_lanes(0), _head_lanes(1)]
        tri_after = _tri(TK, lambda j, ss: j > ss)
        tri_before = _tri(2 * TK, lambda j, ss: j < ss)
        lane = lax.broadcasted_iota(jnp.int32, (CH, LANES), 1)
        row = lax.broadcasted_iota(jnp.int32, (CH, 1), 0)
        col = lax.broadcasted_iota(jnp.int32, (1, TK), 1)
        for hh in range(2):
            q2_scr[hh * TQ:(hh + 1) * TQ, :] = (q_ref[...] * (SCALE * lm[hh])).astype(BF16)
            do2_scr[hh * TQ:(hh + 1) * TQ, :] = (do_ref[...] * lm[hh]).astype(BF16)
        pg_scr[...] = jnp.zeros_like(pg_scr)
        dq_scr[...] = jnp.zeros_like(dq_scr)

        def step(kb, masked):
            ks = pl.multiple_of(kb * TK, TK)
            kblk = k_ref[pl.ds(ks, TK), :].astype(BF16)
            vblk = v_ref[pl.ds(ks, TK), :].astype(BF16)
            z_scr[...] = _dot(q2_scr[...], kblk, NT)
            da_scr[...] = _dot(do2_scr[...], vblk, NT)

            def causal(c):
                return (ks + col) < (qi * TQ + (c * CH) % TQ + row)

            for c in range(n_ch):
                rows = slice(c * CH, (c + 1) * CH)
                z = z_scr[rows, :]
                e = jnp.exp(-jnp.abs(z))
                w = 1.0 + e
                l0 = jnp.minimum(-z, 0.0) - jnp.log(w)
                if masked:
                    l0 = jnp.where(causal(c), l0, 0.0)
                l0_scr[rows, :] = l0.astype(BF16)
                rinv = 1.0 / w
                beta_scr[rows, :] = jnp.where(z >= 0.0, rinv, e * rinv)
            cs_scr[...] = _dot(l0_scr[...], tri_after)
            for c in range(n_ch):
                rows = slice(c * CH, (c + 1) * CH)
                hh, r0 = (c * CH) // TQ, (c * CH) % TQ
                near = jnp.sum(jnp.where(lane == kb, rs_ref[hh, r0:r0 + CH, :], 0.0), axis=1, keepdims=True)
                a = jnp.exp(l0_scr[rows, :].astype(F32) + z_scr[rows, :] + cs_scr[rows, :] + near)
                if masked:
                    a = jnp.where(causal(c), a, 0.0)
                a_scr[rows, :] = a.astype(BF16)
                g_scr[rows, :] = (a * da_scr[rows, :]).astype(BF16)
            cg_scr[...] = _dot(g_scr[...], tri_before)
            for c in range(n_ch):
                rows = slice(c * CH, (c + 1) * CH)
                before = cg_scr[rows, :TK] + pg_scr[rows, :]
                beta = beta_scr[rows, :]
                dz = g_scr[rows, :].astype(F32) * (1.0 - beta) - beta * before
                if masked:
                    dz = jnp.where(causal(c), dz, 0.0)
                dz_scr[rows, :] = dz.astype(BF16)
                pg_scr[rows, :] += cg_scr[rows, TK:]
            dq_scr[...] += _dot(dz_scr[...], kblk)
            dk_scr[pl.ds(ks, TK), :] += _dot(dz_scr[...], q2_scr[...], TN)
            dv_scr[pl.ds(ks, TK), :] += _dot(a_scr[...], do2_scr[...], TN)

        def inner_step(kb, carry):
            step(kb, False)
            return carry

        def diag_step(it, carry):
            step(qi * diag + it, True)
            return carry

        lax.fori_loop(0, qi * diag, inner_step, 0)
        lax.fori_loop(0, diag, diag_step, 0)
        dq = dq_scr[:TQ, :] * lm[0] + dq_scr[TQ:, :] * lm[1]
        dq_ref[...] = (dq * SCALE).astype(dq_ref.dtype)

        @pl.when(qi == nq - 1)
        def _():
            dk_ref[...] = dk_scr[...].astype(dk_ref.dtype)
            dv_ref[...] = dv_scr[...].astype(dv_ref.dtype)

    stacked_f32 = pltpu.VMEM((2 * TQ, TK), F32)
    stacked_bf16 = pltpu.VMEM((2 * TQ, TK), BF16)

    def rows(c0):
        return pl.BlockSpec((TQ, LANES), lambda p, qi: (qi, c0 + p))

    def whole(c0):
        return pl.BlockSpec((s, LANES), lambda p, qi: (0, c0 + p))

    out = jax.ShapeDtypeStruct((s, ATTN_WIDTH), BF16)
    return _call(
        body, name="attn_bwd", grid=(N_HEADS // 2, nq),
        in_specs=[rows(Q_COL), whole(K_COL), whole(V_COL), rows(0),
                  pl.BlockSpec((2, TQ, LANES), lambda p, qi: (p, qi, 0))],
        out_specs=[rows(0), whole(0), whole(0)], out_shape=[out] * 3,
        scratch_shapes=[stacked_bf16, stacked_bf16, stacked_f32, stacked_f32, stacked_bf16, stacked_f32, stacked_f32,
                        stacked_bf16, stacked_bf16, pltpu.VMEM((2 * TQ, 2 * TK), F32), stacked_bf16,
                        stacked_f32, stacked_f32, pltpu.VMEM((s, LANES), F32), pltpu.VMEM((s, LANES), F32)],
        compiler_params=_params("arbitrary", "arbitrary"),
    )(proj, proj, proj, do, rsave)


def _sum_adamw(name, parts, w, m, v, layer=None, into=None):
    n, r, c = parts.shape
    tr = r if r <= 256 else 256

    def body(p_ref, w_ref, m_ref, v_ref, g_ref, d_ref, nm_ref, nv_ref):
        g = p_ref[0].astype(F32)
        for j in range(1, n):
            g = g + p_ref[j].astype(F32)
        nm = ADAM_B1 * m_ref[...] + (1.0 - ADAM_B1) * g
        nv = ADAM_B2 * v_ref[...] + (1.0 - ADAM_B2) * (g * g)
        m_hat = nm / (1.0 - ADAM_B1 ** ADAM_STEP)
        v_hat = nv / (1.0 - ADAM_B2 ** ADAM_STEP)
        g_ref[...] = g
        d_ref[...] = -ADAM_LR * (m_hat / (jnp.sqrt(v_hat) + ADAM_EPS) + ADAM_WD * w_ref[...])
        nm_ref[...] = nm
        nv_ref[...] = nv

    if layer is None:
        mat = pl.BlockSpec((tr, c), lambda i: (i, 0))
        out = jax.ShapeDtypeStruct((r, c), F32)
    else:
        mat = pl.BlockSpec((None, tr, c), lambda i: (layer, i, 0))
        out = jax.ShapeDtypeStruct((DEPTH, r, c), F32)
    earlier = () if into is None else tuple(into)
    return _call(body if into is None else lambda *refs: body(*refs[:4], *refs[8:]),
                 name=name, grid=(r // tr,),
                 in_specs=[pl.BlockSpec((n, tr, c), lambda i: (0, i, 0)), mat, mat, mat]
                 + [pl.BlockSpec(memory_space=pl.ANY)] * len(earlier),
                 out_specs=[mat] * 4, out_shape=[out] * 4,
                 input_output_aliases={4 + k: k for k in range(len(earlier))},
                 compiler_params=_params("parallel"))(parts, w, m, v, *earlier)


def _natural(gathered):
    _, k, n = gathered.shape
    return gathered.transpose(1, 0, 2).reshape(k, N_DEV * n)


def _relu2_epi(acc):
    r = jnp.maximum(acc, 0.0)
    return acc, r * r


def _relu2_bwd_epi(acc, a_act):
    return (acc * (2.0 * jnp.maximum(a_act, 0.0)),)


def kernel(x, c, w_ada, b_ada, g_pre_mix, g_post_mix, g_pre_mlp, g_post_mlp, w_in, conv_w, w_proj_conv, w_proj_attn, w_out, w_mlp_in, w_mlp_out, loss_target, m_w_ada, m_b_ada, m_g_pre_mix, m_g_post_mix, m_g_pre_mlp, m_g_post_mlp, m_w_in, m_conv_w, m_w_proj_conv, m_w_proj_attn, m_w_out, m_w_mlp_in, m_w_mlp_out, v_w_ada, v_b_ada, v_g_pre_mix, v_g_post_mix, v_g_pre_mlp, v_g_post_mlp, v_w_in, v_conv_w, v_w_proj_conv, v_w_proj_attn, v_w_out, v_w_mlp_in, v_w_mlp_out):
    xi, yi, ci = _mesh_pos()
    me = 4 * xi + 2 * yi + ci
    d = D_MODEL
    x0 = x[0]
    seq = x0.shape[0]
    ada_cols = w_ada.shape[2]
    conv_cols = conv_w.shape[2]

    small = jnp.concatenate([c.reshape(-1), conv_w.reshape(-1)])
    small = jnp.pad(small, (0, 2 * d - small.shape[0])).reshape(8, 2 * d // 8)
    small_all = _all_gather("gather_c", [small])[0].reshape(N_DEV, 2 * d)
    c_all = small_all[:, :d]
    conv_all = small_all[:, d:d + DEPTH * 3 * conv_cols].reshape(N_DEV, DEPTH, 3, conv_cols)
    conv_all = conv_all.transpose(1, 2, 0, 3).reshape(DEPTH, 3, N_DEV * conv_cols)
    mod_cols = jnp.stack([_mm("mod_mm", c_all, w_ada[l], "nn", N_DEV, ada_cols, d, [F32], exact=True)
                          for l in range(DEPTH)], axis=1)
    mod_all = _all_gather("gather_mod", [mod_cols.reshape(N_DEV, DEPTH * ada_cols)])[0]
    mod_mine = lax.dynamic_index_in_dim(mod_all, me, axis=1, keepdims=False).reshape(N_DEV, DEPTH, ada_cols)
    mod = mod_mine.transpose(1, 0, 2).reshape(DEPTH, N_MOD * d) + b_ada

    def gathered_weights(l):
        shards = [w_in[l], w_proj_conv[l], w_proj_attn[l], w_out[l], w_mlp_in[l], w_mlp_out[l]]
        g_in, g_pc, g_pa, g_out, g_mi, g_mo = _all_gather("gather_w", [w.astype(BF16) for w in shards])
        return (_natural(g_in), _natural(g_pc), _natural(g_pa), g_out.reshape(d, d), _natural(g_mi),
                g_mo.reshape(D_FF, d))

    weights = [gathered_weights(l) for l in range(DEPTH)]

    saved = []
    xl = x0
    for l in range(DEPTH):
        wg_in, wg_pc, wg_pa, wg_out, wg_mi, wg_mo = weights[l]
        sh1, sc1, gt1, sh2, sc2, gt2 = [mod[l:l + 1, i * d:(i + 1) * d] for i in range(N_MOD)]
        h = _prenorm_fwd(xl, g_pre_mix[l:l + 1], sc1, sh1)
        proj = _mm("proj", h, wg_in, "nn", TMM,1024, d, [F32])
        yc = _conv_fwd(proj, conv_all[l])
        y_conv = _mm("proj_conv", yc, wg_pc, "nn", TMM,d, CONV_WIDTH, [F32])
        o, rsave = _attn_fwd(proj)
        y_attn = _mm("proj_attn", o, wg_pa, "nn", TMM,d, ATTN_WIDTH, [F32])
        merged = _gate_fwd(proj, y_conv, y_attn)
        mix_out = _mm("mix_out", merged, wg_out, "nn", TMM,d, d, [F32])
        x1 = _postnorm_fwd(xl, mix_out, g_post_mix[l:l + 1], gt1)
        h2 = _prenorm_fwd(x1, g_pre_mlp[l:l + 1], sc2, sh2)
        a_act, r = _mm("mlp_in", h2, wg_mi, "nn", TMM,1024, d, [F32, BF16], epi=_relu2_epi)
        ff = _mm("mlp_out", r, wg_mo, "nn", TMM,d, 1024, [F32])
        x2 = _postnorm_fwd(x1, ff, g_post_mlp[l:l + 1], gt2)
        saved.append((xl, h, proj, yc, o, rsave, y_conv, y_attn, merged, mix_out, x1, h2, a_act, r, ff))
        xl = x2

    dxo, sq = _loss(xl, loss_target[0])
    loss = lax.psum(sq[0, 0] * (0.5 / d), ("x", "y", "c"))

    big = {}
    dmod, small_grads = [None] * DEPTH, [None] * DEPTH
    for l in reversed(range(DEPTH)):
        wg_in, wg_pc, wg_pa, wg_out, wg_mi, wg_mo = weights[l]
        xin, h, proj, yc, o, rsave, y_conv, y_attn, merged, mix_out, x1, h2, a_act, r, ff = saved[l]
        sh1, sc1, gt1, sh2, sc2, gt2 = [mod[l:l + 1, i * d:(i + 1) * d] for i in range(N_MOD)]

        dff, dgt2, dg_post_mlp = _postnorm_bwd(dxo, ff, g_post_mlp[l:l + 1], gt2)
        da = _mm("d_relu2", dff, wg_mo, "nt", TMM,1024, d, [BF16], epi=_relu2_bwd_epi, extra=(a_act,))
        gw_mo = _mm("gw_mlp_out", r, dff, "tn", 1024, d, 1024, [BF16])
        dh2 = _mm("d_h2", da, wg_mi, "nt", TMM,d, 1024, [F32])
        gw_mi = _mm("gw_mlp_in", h2, da, "tn", d, 1024, 1024, [BF16])
        dx1, dsh2, dsc2, dg_pre_mlp = _prenorm_bwd(dh2, x1, g_pre_mlp[l:l + 1], sc2, dxo)

        dmix, dgt1, dg_post_mix = _postnorm_bwd(dx1, mix_out, g_post_mix[l:l + 1], gt1)
        dmerged = _mm("d_merged", dmix, wg_out, "nt", TMM,d, d, [F32])
        gw_out = _mm("gw_out", merged, dmix, "tn", d, d, 1024, [BF16])
        dy_conv, dy_attn, dga, dgb = _gate_bwd(dmerged, proj, y_conv, y_attn)
        do = _mm("d_o", dy_attn, wg_pa, "nt", TMM,ATTN_WIDTH, d, [F32])
        gw_pa = _mm("gw_proj_attn", o, dy_attn, "tn", ATTN_WIDTH, d, 1024, [BF16])
        dyc = _mm("d_yc", dy_conv, wg_pc, "nt", TMM,CONV_WIDTH, d, [F32])
        gw_pc = _mm("gw_proj_conv", yc, dy_conv, "tn", CONV_WIDTH, d, 1024, [BF16])
        dq, dk, dv = _attn_bwd(proj, do, rsave)
        dbg, dcg, du, dw0, dw1, dw2 = _conv_bwd(dyc, proj, conv_all[l])
        dproj = jnp.concatenate([dbg, dcg, du, dq, dk, dv, dga, dgb], axis=1)
        dh = _mm("d_h", dproj, wg_in, "nt", TMM,d, 1024, [F32])
        gw_in = _mm("gw_in", h, dproj, "tn", d, 1024, 1024, [BF16])
        dxo, dsh1, dsc1, dg_pre_mix = _prenorm_bwd(dh, xin, g_pre_mix[l:l + 1], sc1, dx1)

        dmod[l] = jnp.concatenate([dsh1, dsc1, dgt1, dsh2, dsc2, dgt2], axis=1)
        small_grads[l] = (dg_pre_mix, dg_post_mix, dg_pre_mlp, dg_post_mlp, jnp.concatenate([dw0, dw1, dw2], axis=0))

        def col_blocks(gw):
            k, n = gw.shape
            return gw.reshape(k, N_DEV, n // N_DEV).transpose(1, 0, 2)

        sent = [col_blocks(gw_in), col_blocks(gw_pc), col_blocks(gw_pa), gw_out.reshape(N_DEV, d // N_DEV, d), col_blocks(gw_mi),
                gw_mo.reshape(N_DEV, D_FF // N_DEV, d)]
        parts = _exchange("exchange_gw", sent)
        names = ["w_in", "w_proj_conv", "w_proj_attn", "w_out", "w_mlp_in", "w_mlp_out"]
        olds = [(w_in, m_w_in, v_w_in), (w_proj_conv, m_w_proj_conv, v_w_proj_conv),
                (w_proj_attn, m_w_proj_attn, v_w_proj_attn), (w_out, m_w_out, v_w_out),
                (w_mlp_in, m_w_mlp_in, v_w_mlp_in), (w_mlp_out, m_w_mlp_out, v_w_mlp_out)]
        for nm, p, (w_, m_, v_) in zip(names, parts, olds):
            big[nm] = _sum_adamw("adamw_" + nm, p, w_, m_, v_, layer=l, into=big.get(nm))

    vec = jnp.concatenate(
        [dmod[l].reshape(-1) for l in range(DEPTH)]
        + [small_grads[l][i].reshape(-1) for i in range(4) for l in range(DEPTH)]
        + [small_grads[l][4].reshape(-1) for l in range(DEPTH)])
    n_vec = vec.shape[0]
    vec_all = _all_gather("gather_small", [vec.reshape(8, n_vec // 8)])[0].reshape(N_DEV, n_vec)
    n_mod = DEPTH * N_MOD * d
    dmod_all = vec_all[:, :n_mod].reshape(N_DEV, DEPTH, N_MOD * d)
    res = {}
    res["b_ada"] = _sum_adamw("adamw_b_ada", dmod_all, b_ada, m_b_ada, v_b_ada)
    off = n_mod
    for nm, (w_, m_, v_) in zip(
            ["g_pre_mix", "g_post_mix", "g_pre_mlp", "g_post_mlp"],
            [(g_pre_mix, m_g_pre_mix, v_g_pre_mix), (g_post_mix, m_g_post_mix, v_g_post_mix),
             (g_pre_mlp, m_g_pre_mlp, v_g_pre_mlp), (g_post_mlp, m_g_post_mlp, v_g_post_mlp)]):
        res[nm] = _sum_adamw("adamw_gain", vec_all[:, off:off + DEPTH * d].reshape(N_DEV, DEPTH, d), w_, m_, v_)
        off += DEPTH * d
    dconv_all = vec_all[:, off:].reshape(N_DEV, DEPTH * 3, CONV_WIDTH)
    dconv_mine = lax.dynamic_slice_in_dim(dconv_all, me * conv_cols, conv_cols, axis=2)
    res["conv_w"] = [t.reshape(DEPTH, 3, conv_cols) for t in _sum_adamw(
        "adamw_conv_w", dconv_mine, conv_w.reshape(DEPTH * 3, conv_cols), m_conv_w.reshape(DEPTH * 3, conv_cols),
        v_conv_w.reshape(DEPTH * 3, conv_cols))]

    c_t = jnp.pad(c_all.T, ((0, 0), (0, LANES - N_DEV)))
    dmod_mine = lax.dynamic_slice_in_dim(dmod_all, me * ada_cols, ada_cols, axis=2)
    for l in range(DEPTH):
        dm_l = jnp.pad(dmod_mine[:, l, :], ((0, LANES - N_DEV), (0, 0)))
        gw_ada = _mm("gw_ada", c_t, dm_l, "nn", 256, ada_cols, LANES, [F32], exact=True)
        res["w_ada"] = _sum_adamw("adamw_w_ada", gw_ada[None], w_ada, m_w_ada, v_w_ada, layer=l, into=res.get("w_ada"))
    res.update(big)

    order = ["w_ada", "b_ada", "g_pre_mix", "g_post_mix", "g_pre_mlp", "g_post_mlp", "w_in", "conv_w",
             "w_proj_conv", "w_proj_attn", "w_out", "w_mlp_in", "w_mlp_out"]
    outs = [loss, dxo[None]]
    for i in range(4):
        outs += [res[nm][i] for nm in order]
    return tuple(outs)
```

```python
import jax
import jax.numpy as jnp
from jax import lax
from jax.experimental import pallas as pl
from jax.experimental.pallas import tpu as pltpu

F32 = jnp.float32
BF16 = jnp.bfloat16
MESH = pl.DeviceIdType.MESH

N_DEV = 8
D_MODEL = 1024
CONV_WIDTH = 512
N_HEADS = 8
HEAD_DIM = 64
ATTN_WIDTH = N_HEADS * HEAD_DIM
D_FF = 4 * D_MODEL
N_MOD = 6
DEPTH = 2
EPS = 1e-6
IN_COLS = 3 * CONV_WIDTH + 3 * ATTN_WIDTH + 2 * D_MODEL
LANES = 128

ADAM_LR = 0.001
ADAM_B1 = 0.9
ADAM_B2 = 0.999
ADAM_EPS = 1e-08
ADAM_WD = 0.01
ADAM_STEP = 10

TM = 512
TMM = 1024
TQ = 512
TK = 128
CH = 64
VMEM_LIMIT = 56 * 1024 * 1024

NN = (((1,), (0,)), ((), ()))
NT = (((1,), (1,)), ((), ()))
TN = (((0,), (0,)), ((), ()))
_DIMS = {"nn": NN, "nt": NT, "tn": TN}


def _call(body, **kw):
    return pl.pallas_call(body, **kw)


def _params(*sem):
    return pltpu.CompilerParams(dimension_semantics=sem, vmem_limit_bytes=VMEM_LIMIT)


def _dot(a, b, dims=NN):
    return lax.dot_general(a, b, dims, preferred_element_type=F32)


def _mesh_pos():
    return lax.axis_index("x"), lax.axis_index("y"), lax.axis_index("c")


def _all_gather(name, arrs):
    n = len(arrs)

    def body(*refs):
        ins, outs = refs[:n], refs[n:2 * n]
        send_sems, recv_sems, local_sems = refs[2 * n:]
        x, y, c = _mesh_pos()
        me, sibling = (x, y, c), (x, y, 1 - c)
        chips = [(1 - x, y), (x, 1 - y), (1 - x, 1 - y)]

        def blk(t, p):
            return outs[t].at[4 * p[0] + 2 * p[1] + p[2]]

        def copy(t, k, block, to, src=None):
            return pltpu.make_async_remote_copy(
                src_ref=blk(t, block) if src is None else src, dst_ref=blk(t, block),
                send_sem=send_sems.at[7 * t + k], recv_sem=recv_sems.at[7 * t + k],
                device_id=to, device_id_type=MESH)

        mine, first, passed = [], [], []
        for t in range(n):
            cp = pltpu.make_async_copy(ins[t], blk(t, me), local_sems.at[t])
            cp.start()
            mine.append(cp)
            cps = [copy(t, 0, me, sibling, src=ins[t])]
            cps += [copy(t, 1 + j, me, (*chip, c), src=ins[t]) for j, chip in enumerate(chips)]
            for cp in cps:
                cp.start()
            first += cps
        for t in range(n):
            for j, chip in enumerate(chips):
                copy(t, 1 + j, (*chip, c), me).wait_recv()
                cp = copy(t, 4 + j, (*chip, c), sibling)
                cp.start()
                passed.append(cp)
        for t in range(n):
            copy(t, 0, sibling, me).wait_recv()
            for j, chip in enumerate(chips):
                copy(t, 4 + j, (*chip, 1 - c), me).wait_recv()
        for cp in first + passed:
            cp.wait_send()
        for cp in mine:
            cp.wait()

    any_spec = pl.BlockSpec(memory_space=pl.ANY)
    return _call(
        body, name=name,
        out_shape=[jax.ShapeDtypeStruct((N_DEV,) + a.shape, a.dtype) for a in arrs],
        in_specs=[any_spec] * n, out_specs=[any_spec] * n,
        scratch_shapes=[pltpu.SemaphoreType.DMA((7 * n,)), pltpu.SemaphoreType.DMA((7 * n,)),
                        pltpu.SemaphoreType.DMA((n,))],
    )(*arrs)


class _AllToAll:
    def __init__(self, whole):
        self.whole = list(whole)
        self.n = len(self.whole)

    def sem_shapes(self):
        return [pltpu.SemaphoreType.DMA((7 * self.n,)), pltpu.SemaphoreType.DMA((7 * self.n,)),
                pltpu.SemaphoreType.DMA((self.n,))]

    def out_shapes(self, arrs):
        return [jax.ShapeDtypeStruct(((N_DEV,) + a.shape) if w else a.shape, a.dtype) for a, w in zip(arrs, self.whole)]

    def _copies(self, ins, outs, sems):
        send_sems, recv_sems, local_sems = sems
        x, y, c = _mesh_pos()
        my_idx = 4 * x + 2 * y + c
        mine, sends, recvs = [], [], []
        for t in range(self.n):
            def src(idx):
                return ins[t] if self.whole[t] else ins[t].at[idx]
            mine.append(pltpu.make_async_copy(src(my_idx), outs[t].at[my_idx], local_sems.at[t]))
            for k in range(1, N_DEV):
                p = (1 - x if k & 4 else x, 1 - y if k & 2 else y, 1 - c if k & 1 else c)
                p_idx = 4 * p[0] + 2 * p[1] + p[2]
                for dst_idx, group in ((my_idx, sends), (p_idx, recvs)):
                    group.append(pltpu.make_async_remote_copy(
                        src_ref=src(p_idx), dst_ref=outs[t].at[dst_idx],
                        send_sem=send_sems.at[7 * t + k - 1], recv_sem=recv_sems.at[7 * t + k - 1],
                        device_id=p, device_id_type=MESH))
        return mine, sends, recvs

    def start(self, ins, outs, sems):
        mine, sends, _ = self._copies(ins, outs, sems)
        for cp in mine + sends:
            cp.start()

    def finish(self, ins, outs, sems):
        mine, sends, recvs = self._copies(ins, outs, sems)
        for cp in recvs:
            cp.wait_recv()
        for cp in sends:
            cp.wait_send()
        for cp in mine:
            cp.wait()


def _call_hosting(body, args, comm, comm_args, *, name, grid, in_specs, out_specs, out_shape, scratch_shapes):
    if comm is None:
        return _call(body, name=name, grid=grid, in_specs=in_specs, out_specs=out_specs, out_shape=out_shape,
                     scratch_shapes=scratch_shapes, compiler_params=_params(*["arbitrary"] * len(grid)))(*args), ()
    n, n_in, n_out, n_scr = comm.n, len(in_specs), len(out_specs), len(scratch_shapes)

    def hosted(*refs):
        ins, refs = refs[:n_in], refs[n_in:]
        c_ins, refs = refs[:n], refs[n:]
        outs, refs = refs[:n_out], refs[n_out:]
        c_outs, refs = refs[:n], refs[n:]
        scratch, sems = refs[:n_scr], refs[n_scr:]
        step = [pl.program_id(i) for i in range(len(grid))]

        def at(ends):
            hit = step[0] == ends[0]
            for sidx, e in zip(step[1:], ends[1:]):
                hit = jnp.logical_and(hit, sidx == e)
            return hit

        @pl.when(at([0] * len(grid)))
        def _():
            comm.start(c_ins, c_outs, sems)

        body(*ins, *outs, *scratch)

        @pl.when(at([g - 1 for g in grid]))
        def _():
            comm.finish(c_ins, c_outs, sems)

    any_spec = pl.BlockSpec(memory_space=pl.ANY)
    res = _call(hosted, name=name + "_hosting", grid=grid, in_specs=list(in_specs) + [any_spec] * n,
                out_specs=list(out_specs) + [any_spec] * n, out_shape=list(out_shape) + comm.out_shapes(comm_args),
                scratch_shapes=list(scratch_shapes) + comm.sem_shapes(),
                compiler_params=_params(*["arbitrary"] * len(grid)))(*args, *comm_args)
    return res[:n_out], res[n_out:]


def _exchange(name, arrs):
    n = len(arrs)
    comm = _AllToAll([False] * n)

    def body(*refs):
        ins, outs, sems = refs[:n], refs[n:2 * n], refs[2 * n:]
        comm.start(ins, outs, sems)
        comm.finish(ins, outs, sems)

    any_spec = pl.BlockSpec(memory_space=pl.ANY)
    return _call(body, name=name, out_shape=comm.out_shapes(arrs), in_specs=[any_spec] * n, out_specs=[any_spec] * n,
                 scratch_shapes=comm.sem_shapes())(*arrs)


def _mm(name, a, b, mode, tm, tn, tk, out_dtypes, epi=None, extra=(), blocked_out=False, exact=False):
    if mode == "nn":
        (m, k), n = a.shape, b.shape[1]
    elif mode == "nt":
        (m, k), n = a.shape, b.shape[0]
    else:
        (k, m), n = a.shape, b.shape[1]
    nk = k // tk
    grid = (m // tm, n // tn, nk)
    n_extra, n_out = len(extra), len(out_dtypes)

    def body(*refs):
        a_ref, b_ref = refs[0], refs[1]
        extra_refs = refs[2:2 + n_extra]
        out_refs = refs[2 + n_extra:2 + n_extra + n_out]
        if exact:
            p = lax.dot_general(a_ref[...], b_ref[...], _DIMS[mode], preferred_element_type=F32,
                                precision=lax.Precision.HIGHEST)
        else:
            p = _dot(a_ref[...].astype(BF16), b_ref[...].astype(BF16), _DIMS[mode])

        def finish(acc):
            outs = (acc,) if epi is None else epi(acc, *[r[...] for r in extra_refs])
            for r, o in zip(out_refs, outs):
                r[...] = o.astype(r.dtype)

        if nk == 1:
            finish(p)
        else:
            acc_ref = refs[-1]
            kk = pl.program_id(2)

            @pl.when(kk == 0)
            def _():
                acc_ref[...] = p

            @pl.when(kk > 0)
            def _():
                acc_ref[...] += p

            @pl.when(kk == nk - 1)
            def _():
                finish(acc_ref[...])

    if mode == "tn":
        a_spec = pl.BlockSpec((tk, tm), lambda i, j, kk: (kk, i))
    else:
        a_spec = pl.BlockSpec((tm, tk), lambda i, j, kk: (i, kk))
    if mode == "nt":
        b_spec = pl.BlockSpec((tn, tk), lambda i, j, kk: (j, kk))
    else:
        b_spec = pl.BlockSpec((tk, tn), lambda i, j, kk: (kk, j))
    tile = pl.BlockSpec((tm, tn), lambda i, j, kk: (i, j))
    if blocked_out:
        o_shape, o_spec = (n // tn, m, tn), pl.BlockSpec((None, tm, tn), lambda i, j, kk: (j, i, 0))
    else:
        o_shape, o_spec = (m, n), tile
    out = _call(
        body, name=name, grid=grid,
        in_specs=[a_spec, b_spec] + [tile] * n_extra,
        out_specs=[o_spec] * n_out,
        out_shape=[jax.ShapeDtypeStruct(o_shape, dt) for dt in out_dtypes],
        scratch_shapes=[pltpu.VMEM((tm, tn), F32)] if nk > 1 else [],
        compiler_params=_params("parallel", "parallel", "arbitrary"),
    )(a, b, *extra)
    return out[0] if n_out == 1 else out


def _tile(width, col=0, rows=TM):
    return pl.BlockSpec((rows, width), lambda i: (i, col))


def _vec(width):
    return pl.BlockSpec((1, width), lambda i: (0, 0))


def _rstd(xf):
    return lax.rsqrt(jnp.mean(xf * xf, axis=-1, keepdims=True) + EPS)


def _colsum(v):
    return jnp.sum(v, axis=0, keepdims=True)


def _accumulate(refs, vals):
    first = pl.program_id(0) == 0

    @pl.when(first)
    def _():
        for r, v in zip(refs, vals):
            r[...] = v

    @pl.when(jnp.logical_not(first))
    def _():
        for r, v in zip(refs, vals):
            r[...] += v


def _prenorm_fwd(x, g, sc, sh):
    s, d = x.shape

    def body(x_ref, g_ref, sc_ref, sh_ref, h_ref):
        xf = x_ref[...]
        y = (xf * _rstd(xf)) * g_ref[...]
        h_ref[...] = (y * (1.0 + sc_ref[...]) + sh_ref[...]).astype(h_ref.dtype)

    return _call(body, name="prenorm_fwd", grid=(s // TM,),
                 in_specs=[_tile(d), _vec(d), _vec(d), _vec(d)], out_specs=_tile(d),
                 out_shape=jax.ShapeDtypeStruct((s, d), BF16), compiler_params=_params("parallel"))(x, g, sc, sh)


def _prenorm_bwd(dh, x, g, sc, dres):
    s, d = x.shape

    def body(dh_ref, x_ref, g_ref, sc_ref, dres_ref, dx_ref, dsh_ref, dsc_ref, dg_ref):
        xf, dhf = x_ref[...], dh_ref[...]
        rstd = _rstd(xf)
        xhat = xf * rstd
        one_sc = 1.0 + sc_ref[...]
        dxhat = dhf * (g_ref[...] * one_sc)
        dx_ref[...] = dres_ref[...] + rstd * (dxhat - xhat * jnp.mean(dxhat * xhat, axis=-1, keepdims=True))
        dhx = dhf * xhat
        _accumulate((dsh_ref, dsc_ref, dg_ref), (_colsum(dhf), _colsum(dhx) * g_ref[...], _colsum(dhx) * one_sc))

    vec_out = jax.ShapeDtypeStruct((1, d), F32)
    return _call(body, name="prenorm_bwd", grid=(s // TM,),
                 in_specs=[_tile(d), _tile(d), _vec(d), _vec(d), _tile(d)],
                 out_specs=[_tile(d), _vec(d), _vec(d), _vec(d)],
                 out_shape=[jax.ShapeDtypeStruct((s, d), F32), vec_out, vec_out, vec_out],
                 compiler_params=_params("arbitrary"))(dh, x, g, sc, dres)


def _postnorm_fwd(xres, m, g, gt):
    s, d = m.shape

    def body(x_ref, m_ref, g_ref, gt_ref, o_ref):
        mf = m_ref[...]
        o_ref[...] = x_ref[...] + gt_ref[...] * ((mf * _rstd(mf)) * g_ref[...])

    return _call(body, name="postnorm_fwd", grid=(s // TM,),
                 in_specs=[_tile(d), _tile(d), _vec(d), _vec(d)], out_specs=_tile(d),
                 out_shape=jax.ShapeDtypeStruct((s, d), F32), compiler_params=_params("parallel"))(xres, m, g, gt)


def _postnorm_bwd(dxn, m, g, gt):
    s, d = m.shape

    def body(dx_ref, m_ref, g_ref, gt_ref, dm_ref, dgt_ref, dg_ref):
        mf, dxf = m_ref[...], dx_ref[...]
        rstd = _rstd(mf)
        mhat = mf * rstd
        dmhat = dxf * (gt_ref[...] * g_ref[...])
        dm_ref[...] = (rstd * (dmhat - mhat * jnp.mean(dmhat * mhat, axis=-1, keepdims=True))).astype(dm_ref.dtype)
        dxm = _colsum(dxf * mhat)
        _accumulate((dgt_ref, dg_ref), (dxm * g_ref[...], dxm * gt_ref[...]))

    vec_out = jax.ShapeDtypeStruct((1, d), F32)
    return _call(body, name="postnorm_bwd", grid=(s // TM,),
                 in_specs=[_tile(d), _tile(d), _vec(d), _vec(d)], out_specs=[_tile(d), _vec(d), _vec(d)],
                 out_shape=[jax.ShapeDtypeStruct((s, d), BF16), vec_out, vec_out],
                 compiler_params=_params("arbitrary"))(dxn, m, g, gt)


def _loss(y, target):
    s, d = y.shape

    def body(y_ref, t_ref, dy_ref, sq_ref):
        err = y_ref[...] - t_ref[...]
        dy_ref[...] = err * (1.0 / d)
        tot = jnp.sum(_colsum(err * err), axis=1, keepdims=True)
        _accumulate((sq_ref,), (jnp.broadcast_to(tot, (1, LANES)),))

    return _call(body, name="loss", grid=(s // TM,), in_specs=[_tile(d), _tile(d)],
                 out_specs=[_tile(d), _vec(LANES)],
                 out_shape=[jax.ShapeDtypeStruct((s, d), F32), jax.ShapeDtypeStruct((1, LANES), F32)],
                 compiler_params=_params("arbitrary"))(y, target)


def _sigmoid(v):
    return 1.0 / (1.0 + jnp.exp(-v))


def _gate_fwd(proj, y_conv, y_attn):
    s, d = y_conv.shape
    ga_col, gb_col = (IN_COLS - 2 * d) // d, (IN_COLS - d) // d

    def body(ga_ref, gb_ref, yc_ref, ya_ref, o_ref):
        o_ref[...] = (_sigmoid(ga_ref[...]) * yc_ref[...] + _sigmoid(gb_ref[...]) * ya_ref[...]).astype(o_ref.dtype)

    return _call(body, name="gate_fwd", grid=(s // TM,),
                 in_specs=[_tile(d, ga_col), _tile(d, gb_col), _tile(d), _tile(d)], out_specs=_tile(d),
                 out_shape=jax.ShapeDtypeStruct((s, d), BF16),
                 compiler_params=_params("parallel"))(proj, proj, y_conv, y_attn)


def _gate_bwd(dmerged, proj, y_conv, y_attn):
    s, d = y_conv.shape
    ga_col, gb_col = (IN_COLS - 2 * d) // d, (IN_COLS - d) // d

    def body(dm_ref, ga_ref, gb_ref, yc_ref, ya_ref, dyc_ref, dya_ref, dga_ref, dgb_ref):
        dm = dm_ref[...]
        sa, sb = _sigmoid(ga_ref[...]), _sigmoid(gb_ref[...])
        dyc_ref[...] = (dm * sa).astype(BF16)
        dya_ref[...] = (dm * sb).astype(BF16)
        dga_ref[...] = (dm * yc_ref[...] * (sa * (1.0 - sa))).astype(BF16)
        dgb_ref[...] = (dm * ya_ref[...] * (sb * (1.0 - sb))).astype(BF16)

    out = jax.ShapeDtypeStruct((s, d), BF16)
    return _call(body, name="gate_bwd", grid=(s // TM,),
                 in_specs=[_tile(d), _tile(d, ga_col), _tile(d, gb_col), _tile(d), _tile(d)],
                 out_specs=[_tile(d)] * 4, out_shape=[out] * 4,
                 compiler_params=_params("parallel"))(dmerged, proj, proj, y_conv, y_attn)


def _shift_down(prev8, cur, by):
    ext = jnp.concatenate([prev8, cur], axis=0)
    return pltpu.roll(ext, by, 0)[8:]


def _shift_up(cur, next8, by):
    ext = jnp.concatenate([cur, next8], axis=0)
    return pltpu.roll(ext, ext.shape[0] - by, 0)[:cur.shape[0]]


def _conv_fwd(proj, conv_w):
    s, w = proj.shape[0], CONV_WIDTH
    per8 = TM // 8

    def prev(col):
        return pl.BlockSpec((8, w), lambda i: (jnp.maximum(i * per8 - 1, 0), col))

    def body(bg_ref, cg_ref, u_ref, cgp_ref, up_ref, w_ref, o_ref):
        vv = cg_ref[...] * u_ref[...]
        pv = cgp_ref[...] * up_ref[...] * jnp.where(pl.program_id(0) > 0, 1.0, 0.0)
        y = w_ref[0:1, :] * _shift_down(pv, vv, 2) + w_ref[1:2, :] * _shift_down(pv, vv, 1) + w_ref[2:3, :] * vv
        o_ref[...] = (bg_ref[...] * y).astype(o_ref.dtype)

    return _call(body, name="conv_fwd", grid=(s // TM,),
                 in_specs=[_tile(w, 0), _tile(w, 1), _tile(w, 2), prev(1), prev(2),
                           pl.BlockSpec((3, w), lambda i: (0, 0))],
                 out_specs=_tile(w), out_shape=jax.ShapeDtypeStruct((s, w), BF16),
                 compiler_params=_params("parallel"))(proj, proj, proj, proj, proj, conv_w)


def _conv_bwd(dyc, proj, conv_w):
    s, w = proj.shape[0], CONV_WIDTH
    per8 = TM // 8
    n_tiles = s // TM

    def prev(col):
        return pl.BlockSpec((8, w), lambda i: (jnp.maximum(i * per8 - 1, 0), col))

    def nxt(col):
        return pl.BlockSpec((8, w), lambda i: (jnp.minimum((i + 1) * per8, s // 8 - 1), col))

    def body(dyc_ref, bg_ref, cg_ref, u_ref, cgp_ref, up_ref, dycn_ref, bgn_ref, w_ref,
             dbg_ref, dcg_ref, du_ref, dw0_ref, dw1_ref, dw2_ref):
        i = pl.program_id(0)
        cg, u = cg_ref[...], u_ref[...]
        vv = cg * u
        pv = cgp_ref[...] * up_ref[...] * jnp.where(i > 0, 1.0, 0.0)
        v1, v2 = _shift_down(pv, vv, 1), _shift_down(pv, vv, 2)
        w0, w1, w2 = w_ref[0:1, :], w_ref[1:2, :], w_ref[2:3, :]
        dyc_t = dyc_ref[...]
        dbg_ref[...] = (dyc_t * (w0 * v2 + w1 * v1 + w2 * vv)).astype(BF16)
        dy = dyc_t * bg_ref[...]
        dyn = dycn_ref[...] * bgn_ref[...] * jnp.where(i < n_tiles - 1, 1.0, 0.0)
        dvv = w2 * dy + w1 * _shift_up(dy, dyn, 1) + w0 * _shift_up(dy, dyn, 2)
        dcg_ref[...] = (dvv * u).astype(BF16)
        du_ref[...] = (dvv * cg).astype(BF16)
        _accumulate((dw0_ref, dw1_ref, dw2_ref), (_colsum(dy * v2), _colsum(dy * v1), _colsum(dy * vv)))

    act = jax.ShapeDtypeStruct((s, w), BF16)
    tap = jax.ShapeDtypeStruct((1, w), F32)
    return _call(body, name="conv_bwd", grid=(n_tiles,),
                 in_specs=[_tile(w), _tile(w, 0), _tile(w, 1), _tile(w, 2), prev(1), prev(2), nxt(0), nxt(0),
                           pl.BlockSpec((3, w), lambda i: (0, 0))],
                 out_specs=[_tile(w)] * 3 + [_vec(w)] * 3, out_shape=[act] * 3 + [tap] * 3,
                 compiler_params=_params("arbitrary"))(dyc, proj, proj, proj, proj, proj, dyc, proj, conv_w)


Q_COL = 3 * CONV_WIDTH // LANES
K_COL = Q_COL + ATTN_WIDTH // LANES
V_COL = K_COL + ATTN_WIDTH // LANES
SCALE = HEAD_DIM ** -0.5


def _head_lanes(hh):
    lane = lax.broadcasted_iota(jnp.int32, (1, LANES), 1)
    return jnp.where((lane >= hh * HEAD_DIM) & (lane < (hh + 1) * HEAD_DIM), 1.0, 0.0)


def _tri(width, keep):
    j = lax.broadcasted_iota(jnp.int32, (TK, width), 0)
    s = lax.broadcasted_iota(jnp.int32, (TK, width), 1)
    return jnp.where((s >= TK) | keep(j, s), 1.0, 0.0).astype(BF16)


def _split_dot(v, tri):
    hi = v.astype(BF16)
    lo = (v - hi.astype(F32)).astype(BF16)
    return _dot(hi, tri) + _dot(lo, tri)


def _logits(qm, kblk, ks, tpos):
    z = _dot(qm, kblk, NT)
    mask = (ks + lax.broadcasted_iota(jnp.int32, (1, TK), 1)) < tpos
    e = jnp.exp(-jnp.abs(z))
    l0 = jnp.where(mask, -(jnp.maximum(z, 0.0) + jnp.log(1.0 + e)), 0.0)
    return z, mask, e, l0


def _neg_softplus(z):
    return jnp.minimum(-z, 0.0) - jnp.log(1.0 + jnp.exp(-jnp.abs(z)))


W2 = 2 * TK
PIPE = 4


def _softplus(z):
    neg_abs = lax.bitcast_convert_type(lax.bitcast_convert_type(z, jnp.uint32) | jnp.uint32(0x80000000), F32)
    return jnp.maximum(z, 0.0) + jnp.log(1.0 + jnp.exp(neg_abs))


def _pair_rows(ref, kb, lm):
    blk = ref[pl.ds(pl.multiple_of(kb * TK, TK), TK), :]
    return jnp.concatenate([(blk * lm[0]).astype(BF16), (blk * lm[1]).astype(BF16)], axis=0)


def _pair_tri(keep):
    j = lax.broadcasted_iota(jnp.int32, (W2, 2 * W2), 0)
    s = lax.broadcasted_iota(jnp.int32, (W2, 2 * W2), 1)
    same_head = (j >= TK) == ((s & (W2 - 1)) >= TK)
    return jnp.where(same_head & ((s >= W2) | keep(j & (TK - 1), s & (TK - 1))), 1.0, 0.0).astype(BF16)


def _attn_fwd(proj, shards=()):
    s = proj.shape[0]
    nq = s // TQ
    diag = TQ // TK
    n_ch = TQ // CH
    assert s // TK <= TK and diag % PIPE == 0

    def body(q_ref, k_ref, v_ref, o_ref, rs_ref, qb_scr, tri_scr, z_scr, l0_scr, cs_scr, a_scr, r_scr, rall_scr, acc_scr):
        qi = pl.program_id(1)
        lm = [_head_lanes(0), _head_lanes(1)]
        lane = lax.broadcasted_iota(jnp.int32, (CH, W2), 1) & (TK - 1)
        row = lax.broadcasted_iota(jnp.int32, (CH, 1), 0)
        col = lax.broadcasted_iota(jnp.int32, (1, W2), 1) & (TK - 1)
        qb_scr[...] = (q_ref[...] * SCALE).astype(BF16)
        tri_scr[...] = _pair_tri(lambda j, ss: j > ss)
        r_scr[...] = jnp.zeros_like(r_scr)
        rall_scr[...] = jnp.zeros_like(rall_scr)
        acc_scr[...] = jnp.zeros_like(acc_scr)

        def causal(kb, c):
            return (kb * TK + col) < (qi * TQ + c * CH + row)

        def chunks(r0):
            return range(0 if r0 is None else r0 // CH, n_ch)

        def on_diagonal(r0, c):
            return r0 is not None and c * CH < r0 + TK

        def logits(kb, zb, r0=0):
            z_scr[zb, r0:, :] = _dot(qb_scr[r0:, :], _pair_rows(k_ref, kb, lm), NT)

        def log_one_minus_beta(kb, zb, lb, r0=None):
            for c in chunks(r0):
                rows = slice(c * CH, (c + 1) * CH)
                sp = _softplus(z_scr[zb, rows, :])
                if on_diagonal(r0, c):
                    sp = jnp.where(causal(kb, c), sp, 0.0)
                l0_scr[lb, rows, :] = sp.astype(BF16)

        def sums(lb, r0=0):
            cs_scr[r0:, :] = _dot(l0_scr[lb, r0:, :], tri_scr[...])

        def weights(kb, zb, lb, ab, r0=None):
            for c in chunks(r0):
                rows = slice(c * CH, (c + 1) * CH)
                near = r_scr[rows, :]
                a = jnp.exp(z_scr[zb, rows, :] - l0_scr[lb, rows, :].astype(F32) - cs_scr[rows, :W2] - near)
                if on_diagonal(r0, c):
                    a = jnp.where(causal(kb, c), a, 0.0)
                a_scr[ab, rows, :] = a.astype(BF16)
                rall_scr[rows, :] = jnp.where(lane == kb, near, rall_scr[rows, :])
                r_scr[rows, :] = near + cs_scr[rows, W2:]

        def weighted_values(kb, ab, r0=0):
            acc_scr[r0:, :] += _dot(a_scr[ab, r0:, :], _pair_rows(v_ref, kb, lm))

        n = qi * diag
        for dd in reversed(range(diag)):
            kb, r0 = n + dd, dd * TK
            logits(kb, dd % PIPE, r0)
            log_one_minus_beta(kb, dd % PIPE, dd % 2, r0)
            sums(dd % 2, r0)
            weights(kb, dd % PIPE, dd % 2, dd % 2, r0)
            weighted_values(kb, dd % 2, r0)

        def block(j):
            return jnp.maximum(n - 1 - j, 0)

        a_scr[...] = jnp.zeros_like(a_scr)
        logits(block(0), 0)
        logits(block(1), 1)
        log_one_minus_beta(block(0), 0, 0)

        def trip(m, carry):
            for u in range(PIPE):
                j = PIPE * m + u
                weighted_values(block(j - 1), (u - 1) % 2)
                sums(u % 2)
                logits(block(j + 2), (u + 2) % PIPE)
                log_one_minus_beta(block(j + 1), (u + 1) % PIPE, (u + 1) % 2)
                weights(block(j), u % PIPE, u % 2, u % 2)
            return carry

        lax.fori_loop(0, n // PIPE, trip, 0)
        weighted_values(block(n - 1), (PIPE - 1) % 2)
        rs_ref[...] = rall_scr[...]
        o_ref[...] = acc_scr[...]

    (o, rsave), gathered = _call_hosting(
        body, (proj, proj, proj), _AllToAll([True] * len(shards)) if shards else None, shards,
        name="attn_fwd", grid=(N_HEADS // 2, nq),
        in_specs=[pl.BlockSpec((TQ, LANES), lambda p, qi: (qi, Q_COL + p)),
                  pl.BlockSpec((s, LANES), lambda p, qi: (0, K_COL + p)),
                  pl.BlockSpec((s, LANES), lambda p, qi: (0, V_COL + p))],
        out_specs=[pl.BlockSpec((TQ, LANES), lambda p, qi: (qi, p)),
                   pl.BlockSpec((TQ, W2), lambda p, qi: (qi, p))],
        out_shape=[jax.ShapeDtypeStruct((s, ATTN_WIDTH), F32), jax.ShapeDtypeStruct((s, N_HEADS // 2 * W2), F32)],
        scratch_shapes=[pltpu.VMEM((TQ, LANES), BF16), pltpu.VMEM((W2, 2 * W2), BF16),
                        pltpu.VMEM((PIPE, TQ, W2), F32), pltpu.VMEM((2, TQ, W2), BF16),
                        pltpu.VMEM((TQ, 2 * W2), F32), pltpu.VMEM((2, TQ, W2), BF16),
                        pltpu.VMEM((TQ, W2), F32), pltpu.VMEM((TQ, W2), F32), pltpu.VMEM((TQ, LANES), F32)])
    return o, rsave, gathered


def _attn_fwd_rows(proj):
    s = proj.shape[0]
    nq = s // TQ
    diag = TQ // TK
    n_ch = 2 * TQ // CH
    assert s // TK <= LANES

    def body(q_ref, k_ref, v_ref, o_ref, rs_ref, q2_scr, z_scr, l0_scr, cs_scr, a_scr, r_scr, rall_scr, acc_scr):
        qi = pl.program_id(1)
        lm = [_head_lanes(0), _head_lanes(1)]
        tri = _tri(2 * TK, lambda j, ss: j > ss)
        lane = lax.broadcasted_iota(jnp.int32, (CH, LANES), 1)
        row = lax.broadcasted_iota(jnp.int32, (CH, 1), 0)
        col = lax.broadcasted_iota(jnp.int32, (1, TK), 1)
        for hh in range(2):
            q2_scr[hh * TQ:(hh + 1) * TQ, :] = (q_ref[...] * (SCALE * lm[hh])).astype(BF16)
        r_scr[...] = jnp.zeros_like(r_scr)
        rall_scr[...] = jnp.zeros_like(rall_scr)
        acc_scr[...] = jnp.zeros_like(acc_scr)

        def step(kb, masked):
            ks = pl.multiple_of(kb * TK, TK)
            kblk = k_ref[pl.ds(ks, TK), :].astype(BF16)
            vf = v_ref[pl.ds(ks, TK), :]
            v2 = jnp.concatenate([(vf * lm[0]).astype(BF16), (vf * lm[1]).astype(BF16)], axis=0)
            z_scr[...] = _dot(q2_scr[...], kblk, NT)

            def causal(c):
                return (ks + col) < (qi * TQ + (c * CH) % TQ + row)

            for c in range(n_ch):
                rows = slice(c * CH, (c + 1) * CH)
                l0 = _neg_softplus(z_scr[rows, :])
                if masked:
                    l0 = jnp.where(causal(c), l0, 0.0)
                l0_scr[rows, :] = l0.astype(BF16)
            cs_scr[...] = _dot(l0_scr[...], tri)
            for c in range(n_ch):
                rows = slice(c * CH, (c + 1) * CH)
                hh, r0 = (c * CH) // TQ, (c * CH) % TQ
                near = r_scr[rows, :]
                a = jnp.exp(l0_scr[rows, :].astype(F32) + z_scr[rows, :] + cs_scr[rows, :TK] + near)
                if masked:
                    a = jnp.where(causal(c), a, 0.0)
                a_scr[r0:r0 + CH, hh * TK:(hh + 1) * TK] = a.astype(BF16)
                rall_scr[rows, :] = jnp.where(lane == kb, near, rall_scr[rows, :])
                r_scr[rows, :] = near + cs_scr[rows, TK:]
            acc_scr[...] += _dot(a_scr[...], v2)

        def diag_step(it, carry):
            step((qi + 1) * diag - 1 - it, True)
            return carry

        def inner_step(it, carry):
            step(qi * diag - 1 - it, False)
            return carry

        lax.fori_loop(0, diag, diag_step, 0)
        lax.fori_loop(0, qi * diag, inner_step, 0)
        for hh in range(2):
            rs_ref[hh] = rall_scr[hh * TQ:(hh + 1) * TQ, :]
        o_ref[...] = acc_scr[...]

    return _call(
        body, name="attn_fwd", grid=(N_HEADS // 2, nq),
        in_specs=[pl.BlockSpec((TQ, LANES), lambda p, qi: (qi, Q_COL + p)),
                  pl.BlockSpec((s, LANES), lambda p, qi: (0, K_COL + p)),
                  pl.BlockSpec((s, LANES), lambda p, qi: (0, V_COL + p))],
        out_specs=[pl.BlockSpec((TQ, LANES), lambda p, qi: (qi, p)),
                   pl.BlockSpec((2, TQ, LANES), lambda p, qi: (p, qi, 0))],
        out_shape=[jax.ShapeDtypeStruct((s, ATTN_WIDTH), F32), jax.ShapeDtypeStruct((N_HEADS, s, LANES), F32)],
        scratch_shapes=[pltpu.VMEM((2 * TQ, LANES), BF16), pltpu.VMEM((2 * TQ, TK), F32),
                        pltpu.VMEM((2 * TQ, TK), BF16), pltpu.VMEM((2 * TQ, 2 * TK), F32),
                        pltpu.VMEM((TQ, 2 * TK), BF16), pltpu.VMEM((2 * TQ, LANES), F32),
                        pltpu.VMEM((2 * TQ, LANES), F32), pltpu.VMEM((TQ, LANES), F32)],
        compiler_params=_params("parallel", "arbitrary"),
    )(proj, proj, proj)


def _attn_bwd(proj, do, rsave, sent=()):
    s = proj.shape[0]
    nq = s // TQ
    diag = TQ // TK
    n_ch = TQ // CH
    assert diag % PIPE == 0

    def body(q_ref, k_ref, v_ref, do_ref, rs_ref, dq_ref, dk_ref, dv_ref,
             qb_scr, dob_scr, after_scr, before_scr, z_scr, da_scr, l0_scr, beta_scr, cs_scr, a_scr, g_scr, cg_scr,
             dz_scr, pg_scr, dq_scr, dk_scr, dv_scr):
        qi = pl.program_id(1)

        @pl.when(qi == 0)
        def _():
            dk_scr[...] = jnp.zeros_like(dk_scr)
            dv_scr[...] = jnp.zeros_like(dv_scr)

        lm = [_head_lanes(0), _head_lanes(1)]
        lane = lax.broadcasted_iota(jnp.int32, (CH, TK), 1)
        row = lax.broadcasted_iota(jnp.int32, (CH, 1), 0)
        col = lax.broadcasted_iota(jnp.int32, (1, W2), 1) & (TK - 1)
        qb_scr[...] = (q_ref[...] * SCALE).astype(BF16)
        dob_scr[...] = do_ref[...].astype(BF16)
        after_scr[...] = _pair_tri(lambda j, ss: j > ss)[:, :W2]
        before_scr[...] = _pair_tri(lambda j, ss: j < ss)
        pg_scr[...] = jnp.zeros_like(pg_scr)
        dq_scr[...] = jnp.zeros_like(dq_scr)
        dz_scr[...] = jnp.zeros_like(dz_scr)

        def causal(kb, c):
            return (kb * TK + col) < (qi * TQ + c * CH + row)

        def chunks(r0):
            return range(0 if r0 is None else r0 // CH, n_ch)

        def on_diagonal(r0, c):
            return r0 is not None and c * CH < r0 + TK

        def logits(kb, zb, r0=0):
            z_scr[zb, r0:, :] = _dot(qb_scr[r0:, :], _pair_rows(k_ref, kb, lm), NT)

        def do_dot_v(kb, db, r0=0):
            da_scr[db, r0:, :] = _dot(dob_scr[r0:, :], _pair_rows(v_ref, kb, lm), NT)

        def gates(kb, zb, lb, bb, r0=None):
            for c in chunks(r0):
                rows = slice(c * CH, (c + 1) * CH)
                z = z_scr[zb, rows, :]
                sp = _softplus(z)
                beta_scr[bb, rows, :] = jnp.exp(z - sp)
                if on_diagonal(r0, c):
                    sp = jnp.where(causal(kb, c), sp, 0.0)
                l0_scr[lb, rows, :] = sp.astype(BF16)

        def suffix_sums(lb, r0=0):
            cs_scr[r0:, :] = _dot(l0_scr[lb, r0:, :], after_scr[...])

        def weights(kb, zb, lb, db, ab, r0=None):
            for c in chunks(r0):
                rows = slice(c * CH, (c + 1) * CH)
                keep = (kb * TK + lane) < (qi * TQ + c * CH + row) if on_diagonal(r0, c) else None
                for hh in range(2):
                    cols = slice(hh * TK, (hh + 1) * TK)
                    near = jnp.sum(jnp.where(lane == kb, rs_ref[rows, cols], 0.0), axis=1, keepdims=True)
                    a = jnp.exp(z_scr[zb, rows, cols] - l0_scr[lb, rows, cols].astype(F32) - cs_scr[rows, cols] - near)
                    if keep is not None:
                        a = jnp.where(keep, a, 0.0)
                    a_scr[ab, rows, cols] = a.astype(BF16)
                    g_scr[ab, rows, cols] = (a * da_scr[db, rows, cols]).astype(BF16)

        def prefix_sums(ab, r0=0):
            cg_scr[r0:, :] = _dot(g_scr[ab, r0:, :], before_scr[...])

        def dlogits(kb, ab, bb, zb2, r0=None):
            for c in chunks(r0):
                rows = slice(c * CH, (c + 1) * CH)
                before = cg_scr[rows, :W2] + pg_scr[rows, :]
                beta = beta_scr[bb, rows, :]
                dz = g_scr[ab, rows, :].astype(F32) * (1.0 - beta) - beta * before
                if on_diagonal(r0, c):
                    dz = jnp.where(causal(kb, c), dz, 0.0)
                dz_scr[zb2, rows, :] = dz.astype(BF16)
                pg_scr[rows, :] += cg_scr[rows, W2:]

        def fold(t):
            return t[:TK, :] * lm[0] + t[TK:, :] * lm[1]

        def dq_dk(kb, zb2, r0=0):
            dq_scr[r0:, :] += _dot(dz_scr[zb2, r0:, :], _pair_rows(k_ref, kb, lm))
            dk_scr[pl.ds(pl.multiple_of(kb * TK, TK), TK), :] += fold(_dot(dz_scr[zb2, r0:, :], qb_scr[r0:, :], TN))

        def dv(kb, ab, r0=0):
            dv_scr[pl.ds(pl.multiple_of(kb * TK, TK), TK), :] += fold(_dot(a_scr[ab, r0:, :], dob_scr[r0:, :], TN))

        n = qi * diag

        def block(j):
            return jnp.clip(j, 0, jnp.maximum(n - 1, 0))

        logits(block(0), 0)
        logits(block(1), 1)
        logits(block(2), 2)
        do_dot_v(block(0), 0)
        do_dot_v(block(1), 1)
        gates(block(0), 0, 0, 0)
        gates(block(1), 1, 1, 1)
        suffix_sums(0)
        weights(block(0), 0, 0, 0, 0)

        def trip(m, carry):
            for u in range(PIPE):
                t = PIPE * m + u
                dq_dk(block(t - 1), (u - 1) % 2)
                dv(block(t), u % 2)
                prefix_sums(u % 2)
                suffix_sums((u + 1) % 2)
                logits(block(t + 3), (u + 3) % PIPE)
                do_dot_v(block(t + 2), u % 2)
                gates(block(t + 2), (u + 2) % PIPE, u % 2, (u + 2) % PIPE)
                weights(block(t + 1), (u + 1) % PIPE, (u + 1) % 2, (u + 1) % 2, (u + 1) % 2)
                dlogits(block(t), u % 2, u % PIPE, u % 2)
            return carry

        lax.fori_loop(0, n // PIPE, trip, 0)
        dq_dk(block(n - 1), (PIPE - 1) % 2)

        for dd in range(diag):
            kb, r0, two, four = n + dd, dd * TK, dd % 2, dd % PIPE
            logits(kb, four, r0)
            do_dot_v(kb, two, r0)
            gates(kb, four, two, four, r0)
            suffix_sums(two, r0)
            weights(kb, four, two, two, two, r0)
            dv(kb, two, r0)
            prefix_sums(two, r0)
            dlogits(kb, two, four, two, r0)
            dq_dk(kb, two, r0)
        dq_ref[...] = (dq_scr[...] * SCALE).astype(dq_ref.dtype)

        @pl.when(qi == nq - 1)
        def _():
            dk_ref[...] = dk_scr[...].astype(dk_ref.dtype)
            dv_ref[...] = dv_scr[...].astype(dv_ref.dtype)

    def rows(c0):
        return pl.BlockSpec((TQ, LANES), lambda p, qi: (qi, c0 + p))

    def whole(c0):
        return pl.BlockSpec((s, LANES), lambda p, qi: (0, c0 + p))

    def f32(*shape):
        return pltpu.VMEM(shape, F32)

    def bf16(*shape):
        return pltpu.VMEM(shape, BF16)

    out = jax.ShapeDtypeStruct((s, ATTN_WIDTH), BF16)
    (dq, dk, dv), parts = _call_hosting(
        body, (proj, proj, proj, do, rsave), _AllToAll([False] * len(sent)) if sent else None, sent,
        name="attn_bwd", grid=(N_HEADS // 2, nq),
        in_specs=[rows(Q_COL), whole(K_COL), whole(V_COL), rows(0), pl.BlockSpec((TQ, W2), lambda p, qi: (qi, p))],
        out_specs=[rows(0), whole(0), whole(0)], out_shape=[out] * 3,
        scratch_shapes=[bf16(TQ, LANES), bf16(TQ, LANES), bf16(W2, W2), bf16(W2, 2 * W2),
                        f32(PIPE, TQ, W2), f32(2, TQ, W2), bf16(2, TQ, W2), f32(PIPE, TQ, W2), f32(TQ, W2),
                        bf16(2, TQ, W2), bf16(2, TQ, W2), f32(TQ, 2 * W2), bf16(2, TQ, W2),
                        f32(TQ, W2), f32(TQ, LANES), f32(s, LANES), f32(s, LANES)])
    return dq, dk, dv, parts


def _attn_bwd_rows(proj, do, rsave):
    s = proj.shape[0]
    nq = s // TQ

    diag = TQ // TK
    n_ch = 2 * TQ // CH

    def body(q_ref, k_ref, v_ref, do_ref, rs_ref, dq_ref, dk_ref, dv_ref,
             q2_scr, do2_scr, z_scr, da_scr, l0_scr, beta_scr, cs_scr, a_scr, g_scr, cg_scr, dz_scr,
             pg_scr, dq_scr, dk_scr, dv_scr):
        qi = pl.program_id(1)

        @pl.when(qi == 0)
        def _():
            dk_scr[...] = jnp.zeros_like(dk_scr)
            dv_scr[...] = jnp.zeros_like(dv_scr)

        lm = [_head_lanes(0), _head_lanes(1)]
        tri_after = _tri(TK, lambda j, ss: j > ss)
        tri_before = _tri(2 * TK, lambda j, ss: j < ss)
        lane = lax.broadcasted_iota(jnp.int32, (CH, LANES), 1)
        row = lax.broadcasted_iota(jnp.int32, (CH, 1), 0)
        col = lax.broadcasted_iota(jnp.int32, (1, TK), 1)
        for hh in range(2):
            q2_scr[hh * TQ:(hh + 1) * TQ, :] = (q_ref[...] * (SCALE * lm[hh])).astype(BF16)
            do2_scr[hh * TQ:(hh + 1) * TQ, :] = (do_ref[...] * lm[hh]).astype(BF16)
        pg_scr[...] = jnp.zeros_like(pg_scr)
        dq_scr[...] = jnp.zeros_like(dq_scr)

        def step(kb, masked):
            ks = pl.multiple_of(kb * TK, TK)
            kblk = k_ref[pl.ds(ks, TK), :].astype(BF16)
            vblk = v_ref[pl.ds(ks, TK), :].astype(BF16)
            z_scr[...] = _dot(q2_scr[...], kblk, NT)
            da_scr[...] = _dot(do2_scr[...], vblk, NT)

            def causal(c):
                return (ks + col) < (qi * TQ + (c * CH) % TQ + row)

            for c in range(n_ch):
                rows = slice(c * CH, (c + 1) * CH)
                z = z_scr[rows, :]
                e = jnp.exp(-jnp.abs(z))
                w = 1.0 + e
                l0 = jnp.minimum(-z, 0.0) - jnp.log(w)
                if masked:
                    l0 = jnp.where(causal(c), l0, 0.0)
                l0_scr[rows, :] = l0.astype(BF16)
                rinv = 1.0 / w
                beta_scr[rows, :] = jnp.where(z >= 0.0, rinv, e * rinv)
            cs_scr[...] = _dot(l0_scr[...], tri_after)
            for c in range(n_ch):
                rows = slice(c * CH, (c + 1) * CH)
                hh, r0 = (c * CH) // TQ, (c * CH) % TQ
                near = jnp.sum(jnp.where(lane == kb, rs_ref[hh, r0:r0 + CH, :], 0.0), axis=1, keepdims=True)
                a = jnp.exp(l0_scr[rows, :].astype(F32) + z_scr[rows, :] + cs_scr[rows, :] + near)
                if masked:
                    a = jnp.where(causal(c), a, 0.0)
                a_scr[rows, :] = a.astype(BF16)
                g_scr[rows, :] = (a * da_scr[rows, :]).astype(BF16)
            cg_scr[...] = _dot(g_scr[...], tri_before)
            for c in range(n_ch):
                rows = slice(c * CH, (c + 1) * CH)
                before = cg_scr[rows, :TK] + pg_scr[rows, :]
                beta = beta_scr[rows, :]
                dz = g_scr[rows, :].astype(F32) * (1.0 - beta) - beta * before
                if masked:
                    dz = jnp.where(causal(c), dz, 0.0)
                dz_scr[rows, :] = dz.astype(BF16)
                pg_scr[rows, :] += cg_scr[rows, TK:]
            dq_scr[...] += _dot(dz_scr[...], kblk)
            dk_scr[pl.ds(ks, TK), :] += _dot(dz_scr[...], q2_scr[...], TN)
            dv_scr[pl.ds(ks, TK), :] += _dot(a_scr[...], do2_scr[...], TN)

        def inner_step(kb, carry):
            step(kb, False)
            return carry

        def diag_step(it, carry):
            step(qi * diag + it, True)
            return carry

        lax.fori_loop(0, qi * diag, inner_step, 0)
        lax.fori_loop(0, diag, diag_step, 0)
        dq = dq_scr[:TQ, :] * lm[0] + dq_scr[TQ:, :] * lm[1]
        dq_ref[...] = (dq * SCALE).astype(dq_ref.dtype)

        @pl.when(qi == nq - 1)
        def _():
            dk_ref[...] = dk_scr[...].astype(dk_ref.dtype)
            dv_ref[...] = dv_scr[...].astype(dv_ref.dtype)

    stacked_f32 = pltpu.VMEM((2 * TQ, TK), F32)
    stacked_bf16 = pltpu.VMEM((2 * TQ, TK), BF16)

    def rows(c0):
        return pl.BlockSpec((TQ, LANES), lambda p, qi: (qi, c0 + p))

    def whole(c0):
        return pl.BlockSpec((s, LANES), lambda p, qi: (0, c0 + p))

    out = jax.ShapeDtypeStruct((s, ATTN_WIDTH), BF16)
    return _call(
        body, name="attn_bwd", grid=(N_HEADS // 2, nq),
        in_specs=[rows(Q_COL), whole(K_COL), whole(V_COL), rows(0),
                  pl.BlockSpec((2, TQ, LANES), lambda p, qi: (p, qi, 0))],
        out_specs=[rows(0), whole(0), whole(0)], out_shape=[out] * 3,
        scratch_shapes=[stacked_bf16, stacked_bf16, stacked_f32, stacked_f32, stacked_bf16, stacked_f32, stacked_f32,
                        stacked_bf16, stacked_bf16, pltpu.VMEM((2 * TQ, 2 * TK), F32), stacked_bf16,
                        stacked_f32, stacked_f32, pltpu.VMEM((s, LANES), F32), pltpu.VMEM((s, LANES), F32)],
        compiler_params=_params("arbitrary", "arbitrary"),
    )(proj, proj, proj, do, rsave)


def _sum_adamw(name, parts, w, m, v, layer=None, into=None):
    n, r, c = parts.shape
    tr = r if r <= 256 else 256

    def body(p_ref, w_ref, m_ref, v_ref, g_ref, d_ref, nm_ref, nv_ref):
        g = p_ref[0].astype(F32)
        for j in range(1, n):
            g = g + p_ref[j].astype(F32)
        nm = ADAM_B1 * m_ref[...] + (1.0 - ADAM_B1) * g
        nv = ADAM_B2 * v_ref[...] + (1.0 - ADAM_B2) * (g * g)
        m_hat = nm / (1.0 - ADAM_B1 ** ADAM_STEP)
        v_hat = nv / (1.0 - ADAM_B2 ** ADAM_STEP)
        g_ref[...] = g
        d_ref[...] = -ADAM_LR * (m_hat / (jnp.sqrt(v_hat) + ADAM_EPS) + ADAM_WD * w_ref[...])
        nm_ref[...] = nm
        nv_ref[...] = nv

    if layer is None:
        mat = pl.BlockSpec((tr, c), lambda i: (i, 0))
        out = jax.ShapeDtypeStruct((r, c), F32)
    else:
        mat = pl.BlockSpec((None, tr, c), lambda i: (layer, i, 0))
        out = jax.ShapeDtypeStruct((DEPTH, r, c), F32)
    earlier = () if into is None else tuple(into)
    return _call(body if into is None else lambda *refs: body(*refs[:4], *refs[8:]),
                 name=name, grid=(r // tr,),
                 in_specs=[pl.BlockSpec((n, tr, c), lambda i: (0, i, 0)), mat, mat, mat]
                 + [pl.BlockSpec(memory_space=pl.ANY)] * len(earlier),
                 out_specs=[mat] * 4, out_shape=[out] * 4,
                 input_output_aliases={4 + k: k for k in range(len(earlier))},
                 compiler_params=_params("parallel"))(parts, w, m, v, *earlier)


def _natural(gathered):
    _, k, n = gathered.shape
    return gathered.transpose(1, 0, 2).reshape(k, N_DEV * n)


def _relu2_epi(acc):
    r = jnp.maximum(acc, 0.0)
    return acc, r * r


def _relu2_bwd_epi(acc, a_act):
    return (acc * (2.0 * jnp.maximum(a_act, 0.0)),)


def kernel(x, c, w_ada, b_ada, g_pre_mix, g_post_mix, g_pre_mlp, g_post_mlp, w_in, conv_w, w_proj_conv, w_proj_attn, w_out, w_mlp_in, w_mlp_out, loss_target, m_w_ada, m_b_ada, m_g_pre_mix, m_g_post_mix, m_g_pre_mlp, m_g_post_mlp, m_w_in, m_conv_w, m_w_proj_conv, m_w_proj_attn, m_w_out, m_w_mlp_in, m_w_mlp_out, v_w_ada, v_b_ada, v_g_pre_mix, v_g_post_mix, v_g_pre_mlp, v_g_post_mlp, v_w_in, v_conv_w, v_w_proj_conv, v_w_proj_attn, v_w_out, v_w_mlp_in, v_w_mlp_out):
    xi, yi, ci = _mesh_pos()
    me = 4 * xi + 2 * yi + ci
    d = D_MODEL
    x0 = x[0]
    seq = x0.shape[0]
    ada_cols = w_ada.shape[2]
    conv_cols = conv_w.shape[2]

    small = jnp.concatenate([c.reshape(-1), conv_w.reshape(-1)])
    small = jnp.pad(small, (0, 2 * d - small.shape[0])).reshape(8, 2 * d // 8)
    small_all = _all_gather("gather_c", [small])[0].reshape(N_DEV, 2 * d)
    c_all = small_all[:, :d]
    conv_all = small_all[:, d:d + DEPTH * 3 * conv_cols].reshape(N_DEV, DEPTH, 3, conv_cols)
    conv_all = conv_all.transpose(1, 2, 0, 3).reshape(DEPTH, 3, N_DEV * conv_cols)
    mod_cols = jnp.stack([_mm("mod_mm", c_all, w_ada[l], "nn", N_DEV, ada_cols, d, [F32], exact=True)
                          for l in range(DEPTH)], axis=1)
    mod_all = _all_gather("gather_mod", [mod_cols.reshape(N_DEV, DEPTH * ada_cols)])[0]
    mod_mine = lax.dynamic_index_in_dim(mod_all, me, axis=1, keepdims=False).reshape(N_DEV, DEPTH, ada_cols)
    mod = mod_mine.transpose(1, 0, 2).reshape(DEPTH, N_MOD * d) + b_ada

    def shards_of(l):
        return [w[l].astype(BF16) for w in (w_in, w_proj_conv, w_proj_attn, w_out, w_mlp_in, w_mlp_out)]

    def natural_weights(gathered):
        g_in, g_pc, g_pa, g_out, g_mi, g_mo = gathered
        return (_natural(g_in), _natural(g_pc), _natural(g_pa), g_out.reshape(d, d), _natural(g_mi),
                g_mo.reshape(D_FF, d))

    weights = [None] * DEPTH
    weights[0] = natural_weights(_all_gather("gather_w", shards_of(0)))
    saved = []
    xl = x0
    for l in range(DEPTH):
        wg_in, wg_pc, wg_pa, wg_out, wg_mi, wg_mo = weights[l]
        sh1, sc1, gt1, sh2, sc2, gt2 = [mod[l:l + 1, i * d:(i + 1) * d] for i in range(N_MOD)]
        h = _prenorm_fwd(xl, g_pre_mix[l:l + 1], sc1, sh1)
        proj = _mm("proj", h, wg_in, "nn", TMM,1024, d, [F32])
        yc = _conv_fwd(proj, conv_all[l])
        y_conv = _mm("proj_conv", yc, wg_pc, "nn", TMM,d, CONV_WIDTH, [F32])
        o, rsave, gathered = _attn_fwd(proj, shards_of(l + 1) if l + 1 < DEPTH else ())
        if l + 1 < DEPTH:
            weights[l + 1] = natural_weights(gathered)
        y_attn = _mm("proj_attn", o, wg_pa, "nn", TMM,d, ATTN_WIDTH, [F32])
        merged = _gate_fwd(proj, y_conv, y_attn)
        mix_out = _mm("mix_out", merged, wg_out, "nn", TMM,d, d, [F32])
        x1 = _postnorm_fwd(xl, mix_out, g_post_mix[l:l + 1], gt1)
        h2 = _prenorm_fwd(x1, g_pre_mlp[l:l + 1], sc2, sh2)
        a_act, r = _mm("mlp_in", h2, wg_mi, "nn", TMM,1024, d, [F32, BF16], epi=_relu2_epi)
        ff = _mm("mlp_out", r, wg_mo, "nn", TMM,d, 1024, [F32])
        x2 = _postnorm_fwd(x1, ff, g_post_mlp[l:l + 1], gt2)
        saved.append((xl, h, proj, yc, o, rsave, y_conv, y_attn, merged, mix_out, x1, h2, a_act, r, ff))
        xl = x2

    dxo, sq = _loss(xl, loss_target[0])
    loss = lax.psum(sq[0, 0] * (0.5 / d), ("x", "y", "c"))

    olds = {"w_in": (w_in, m_w_in, v_w_in), "w_proj_conv": (w_proj_conv, m_w_proj_conv, v_w_proj_conv),
            "w_proj_attn": (w_proj_attn, m_w_proj_attn, v_w_proj_attn), "w_out": (w_out, m_w_out, v_w_out),
            "w_mlp_in": (w_mlp_in, m_w_mlp_in, v_w_mlp_in), "w_mlp_out": (w_mlp_out, m_w_mlp_out, v_w_mlp_out)}
    big = {}
    pending = []

    def col_blocks(gw):
        k, n = gw.shape
        return gw.reshape(k, N_DEV, n // N_DEV).transpose(1, 0, 2)

    def update(entries, parts):
        for (nm, ll, _), part in zip(entries, parts):
            w_, m_, v_ = olds[nm]
            big[nm] = _sum_adamw("adamw_" + nm, part, w_, m_, v_, layer=ll, into=big.get(nm))

    dmod, small_grads = [None] * DEPTH, [None] * DEPTH
    for l in reversed(range(DEPTH)):
        wg_in, wg_pc, wg_pa, wg_out, wg_mi, wg_mo = weights[l]
        xin, h, proj, yc, o, rsave, y_conv, y_attn, merged, mix_out, x1, h2, a_act, r, ff = saved[l]
        sh1, sc1, gt1, sh2, sc2, gt2 = [mod[l:l + 1, i * d:(i + 1) * d] for i in range(N_MOD)]

        dff, dgt2, dg_post_mlp = _postnorm_bwd(dxo, ff, g_post_mlp[l:l + 1], gt2)
        da = _mm("d_relu2", dff, wg_mo, "nt", TMM,1024, d, [BF16], epi=_relu2_bwd_epi, extra=(a_act,))
        gw_mo = _mm("gw_mlp_out", r, dff, "tn", 1024, d, 1024, [BF16])
        dh2 = _mm("d_h2", da, wg_mi, "nt", TMM,d, 1024, [F32])
        gw_mi = _mm("gw_mlp_in", h2, da, "tn", d, 1024, 1024, [BF16])
        dx1, dsh2, dsc2, dg_pre_mlp = _prenorm_bwd(dh2, x1, g_pre_mlp[l:l + 1], sc2, dxo)

        dmix, dgt1, dg_post_mix = _postnorm_bwd(dx1, mix_out, g_post_mix[l:l + 1], gt1)
        dmerged = _mm("d_merged", dmix, wg_out, "nt", TMM,d, d, [F32])
        gw_out = _mm("gw_out", merged, dmix, "tn", d, d, 1024, [BF16])
        dy_conv, dy_attn, dga, dgb = _gate_bwd(dmerged, proj, y_conv, y_attn)
        do = _mm("d_o", dy_attn, wg_pa, "nt", TMM,ATTN_WIDTH, d, [F32])
        gw_pa = _mm("gw_proj_attn", o, dy_attn, "tn", ATTN_WIDTH, d, 1024, [BF16])
        dyc = _mm("d_yc", dy_conv, wg_pc, "nt", TMM,CONV_WIDTH, d, [F32])
        gw_pc = _mm("gw_proj_conv", yc, dy_conv, "tn", CONV_WIDTH, d, 1024, [BF16])
        pending += [("w_mlp_out", l, gw_mo.reshape(N_DEV, D_FF // N_DEV, d)), ("w_mlp_in", l, col_blocks(gw_mi)),
                    ("w_out", l, gw_out.reshape(N_DEV, d // N_DEV, d)), ("w_proj_attn", l, col_blocks(gw_pa)),
                    ("w_proj_conv", l, col_blocks(gw_pc))]
        if l == 0:
            dq, dk, dv, parts = _attn_bwd(proj, do, rsave, [blocks for _, _, blocks in pending])
            update(pending, parts)
            pending = []
        else:
            dq, dk, dv, _ = _attn_bwd(proj, do, rsave)
        dbg, dcg, du, dw0, dw1, dw2 = _conv_bwd(dyc, proj, conv_all[l])
        dproj = jnp.concatenate([dbg, dcg, du, dq, dk, dv, dga, dgb], axis=1)
        dh = _mm("d_h", dproj, wg_in, "nt", TMM,d, 1024, [F32])
        gw_in = _mm("gw_in", h, dproj, "tn", d, 1024, 1024, [BF16])
        dxo, dsh1, dsc1, dg_pre_mix = _prenorm_bwd(dh, xin, g_pre_mix[l:l + 1], sc1, dx1)

        dmod[l] = jnp.concatenate([dsh1, dsc1, dgt1, dsh2, dsc2, dgt2], axis=1)
        small_grads[l] = (dg_pre_mix, dg_post_mix, dg_pre_mlp, dg_post_mlp, jnp.concatenate([dw0, dw1, dw2], axis=0))

        pending.append(("w_in", l, col_blocks(gw_in)))
    update(pending, _exchange("exchange_gw", [blocks for _, _, blocks in pending]))

    vec = jnp.concatenate(
        [dmod[l].reshape(-1) for l in range(DEPTH)]
        + [small_grads[l][i].reshape(-1) for i in range(4) for l in range(DEPTH)]
        + [small_grads[l][4].reshape(-1) for l in range(DEPTH)])
    n_vec = vec.shape[0]
    vec_all = _all_gather("gather_small", [vec.reshape(8, n_vec // 8)])[0].reshape(N_DEV, n_vec)
    n_mod = DEPTH * N_MOD * d
    dmod_all = vec_all[:, :n_mod].reshape(N_DEV, DEPTH, N_MOD * d)
    res = {}
    res["b_ada"] = _sum_adamw("adamw_b_ada", dmod_all, b_ada, m_b_ada, v_b_ada)
    off = n_mod
    for nm, (w_, m_, v_) in zip(
            ["g_pre_mix", "g_post_mix", "g_pre_mlp", "g_post_mlp"],
            [(g_pre_mix, m_g_pre_mix, v_g_pre_mix), (g_post_mix, m_g_post_mix, v_g_post_mix),
             (g_pre_mlp, m_g_pre_mlp, v_g_pre_mlp), (g_post_mlp, m_g_post_mlp, v_g_post_mlp)]):
        res[nm] = _sum_adamw("adamw_gain", vec_all[:, off:off + DEPTH * d].reshape(N_DEV, DEPTH, d), w_, m_, v_)
        off += DEPTH * d
    dconv_all = vec_all[:, off:].reshape(N_DEV, DEPTH * 3, CONV_WIDTH)
    dconv_mine = lax.dynamic_slice_in_dim(dconv_all, me * conv_cols, conv_cols, axis=2)
    res["conv_w"] = [t.reshape(DEPTH, 3, conv_cols) for t in _sum_adamw(
        "adamw_conv_w", dconv_mine, conv_w.reshape(DEPTH * 3, conv_cols), m_conv_w.reshape(DEPTH * 3, conv_cols),
        v_conv_w.reshape(DEPTH * 3, conv_cols))]

    c_t = jnp.pad(c_all.T, ((0, 0), (0, LANES - N_DEV)))
    dmod_mine = lax.dynamic_slice_in_dim(dmod_all, me * ada_cols, ada_cols, axis=2)
    for l in range(DEPTH):
        dm_l = jnp.pad(dmod_mine[:, l, :], ((0, LANES - N_DEV), (0, 0)))
        gw_ada = _mm("gw_ada", c_t, dm_l, "nn", 256, ada_cols, LANES, [F32], exact=True)
        res["w_ada"] = _sum_adamw("adamw_w_ada", gw_ada[None], w_ada, m_w_ada, v_w_ada, layer=l, into=res.get("w_ada"))
    res.update(big)

    order = ["w_ada", "b_ada", "g_pre_mix", "g_post_mix", "g_pre_mlp", "g_post_mlp", "w_in", "conv_w",
             "w_proj_conv", "w_proj_attn", "w_out", "w_mlp_in", "w_mlp_out"]
    outs = [loss, dxo[None]]
    for i in range(4):
        outs += [res[nm][i] for nm in order]
    return tuple(outs)
```

```python
import jax
import jax.numpy as jnp
from jax import lax
from jax.experimental import pallas as pl
from jax.experimental.pallas import tpu as pltpu

F32 = jnp.float32
BF16 = jnp.bfloat16
MESH = pl.DeviceIdType.MESH

N_DEV = 8
D_MODEL = 1024
CONV_WIDTH = 512
N_HEADS = 8
HEAD_DIM = 64
ATTN_WIDTH = N_HEADS * HEAD_DIM
D_FF = 4 * D_MODEL
N_MOD = 6
DEPTH = 2
EPS = 1e-6
IN_COLS = 3 * CONV_WIDTH + 3 * ATTN_WIDTH + 2 * D_MODEL
LANES = 128

ADAM_LR = 0.001
ADAM_B1 = 0.9
ADAM_B2 = 0.999
ADAM_EPS = 1e-08
ADAM_WD = 0.01
ADAM_STEP = 10

TM = 512
TMM = 2048
TQ = 512
TK = 128
CH = 64
VMEM_LIMIT = 56 * 1024 * 1024

NN = (((1,), (0,)), ((), ()))
NT = (((1,), (1,)), ((), ()))
TN = (((0,), (0,)), ((), ()))
_DIMS = {"nn": NN, "nt": NT, "tn": TN}


def _call(body, **kw):
    return pl.pallas_call(body, **kw)


def _params(*sem):
    return pltpu.CompilerParams(dimension_semantics=sem, vmem_limit_bytes=VMEM_LIMIT)


def _dot(a, b, dims=NN):
    return lax.dot_general(a, b, dims, preferred_element_type=F32)


def _mesh_pos():
    return lax.axis_index("x"), lax.axis_index("y"), lax.axis_index("c")


def _all_gather(name, arrs):
    n = len(arrs)

    def body(*refs):
        ins, outs = refs[:n], refs[n:2 * n]
        send_sems, recv_sems, local_sems = refs[2 * n:]
        x, y, c = _mesh_pos()
        me, sibling = (x, y, c), (x, y, 1 - c)
        chips = [(1 - x, y), (x, 1 - y), (1 - x, 1 - y)]

        def blk(t, p):
            return outs[t].at[4 * p[0] + 2 * p[1] + p[2]]

        def copy(t, k, block, to, src=None):
            return pltpu.make_async_remote_copy(
                src_ref=blk(t, block) if src is None else src, dst_ref=blk(t, block),
                send_sem=send_sems.at[7 * t + k], recv_sem=recv_sems.at[7 * t + k],
                device_id=to, device_id_type=MESH)

        mine, first, passed = [], [], []
        for t in range(n):
            cp = pltpu.make_async_copy(ins[t], blk(t, me), local_sems.at[t])
            cp.start()
            mine.append(cp)
            cps = [copy(t, 0, me, sibling, src=ins[t])]
            cps += [copy(t, 1 + j, me, (*chip, c), src=ins[t]) for j, chip in enumerate(chips)]
            for cp in cps:
                cp.start()
            first += cps
        for t in range(n):
            for j, chip in enumerate(chips):
                copy(t, 1 + j, (*chip, c), me).wait_recv()
                cp = copy(t, 4 + j, (*chip, c), sibling)
                cp.start()
                passed.append(cp)
        for t in range(n):
            copy(t, 0, sibling, me).wait_recv()
            for j, chip in enumerate(chips):
                copy(t, 4 + j, (*chip, 1 - c), me).wait_recv()
        for cp in first + passed:
            cp.wait_send()
        for cp in mine:
            cp.wait()

    any_spec = pl.BlockSpec(memory_space=pl.ANY)
    return _call(
        body, name=name,
        out_shape=[jax.ShapeDtypeStruct((N_DEV,) + a.shape, a.dtype) for a in arrs],
        in_specs=[any_spec] * n, out_specs=[any_spec] * n,
        scratch_shapes=[pltpu.SemaphoreType.DMA((7 * n,)), pltpu.SemaphoreType.DMA((7 * n,)),
                        pltpu.SemaphoreType.DMA((n,))],
    )(*arrs)


class _AllToAll:
    def __init__(self, whole):
        self.whole = list(whole)
        self.n = len(self.whole)

    def sem_shapes(self):
        return [pltpu.SemaphoreType.DMA((7 * self.n,)), pltpu.SemaphoreType.DMA((7 * self.n,)),
                pltpu.SemaphoreType.DMA((self.n,))]

    def out_shapes(self, arrs):
        return [jax.ShapeDtypeStruct(((N_DEV,) + a.shape) if w else a.shape, a.dtype) for a, w in zip(arrs, self.whole)]

    def _copies(self, ins, outs, sems):
        send_sems, recv_sems, local_sems = sems
        x, y, c = _mesh_pos()
        my_idx = 4 * x + 2 * y + c
        mine, sends, recvs = [], [], []
        for t in range(self.n):
            def src(idx):
                return ins[t] if self.whole[t] else ins[t].at[idx]
            mine.append(pltpu.make_async_copy(src(my_idx), outs[t].at[my_idx], local_sems.at[t]))
            for k in range(1, N_DEV):
                p = (1 - x if k & 4 else x, 1 - y if k & 2 else y, 1 - c if k & 1 else c)
                p_idx = 4 * p[0] + 2 * p[1] + p[2]
                for dst_idx, group in ((my_idx, sends), (p_idx, recvs)):
                    group.append(pltpu.make_async_remote_copy(
                        src_ref=src(p_idx), dst_ref=outs[t].at[dst_idx],
                        send_sem=send_sems.at[7 * t + k - 1], recv_sem=recv_sems.at[7 * t + k - 1],
                        device_id=p, device_id_type=MESH))
        return mine, sends, recvs

    def start(self, ins, outs, sems):
        mine, sends, _ = self._copies(ins, outs, sems)
        for cp in mine + sends:
            cp.start()

    def finish(self, ins, outs, sems):
        mine, sends, recvs = self._copies(ins, outs, sems)
        for cp in recvs:
            cp.wait_recv()
        for cp in sends:
            cp.wait_send()
        for cp in mine:
            cp.wait()


def _call_hosting(body, args, comm, comm_args, *, name, grid, in_specs, out_specs, out_shape, scratch_shapes):
    if comm is None:
        return _call(body, name=name, grid=grid, in_specs=in_specs, out_specs=out_specs, out_shape=out_shape,
                     scratch_shapes=scratch_shapes, compiler_params=_params(*["arbitrary"] * len(grid)))(*args), ()
    n, n_in, n_out, n_scr = comm.n, len(in_specs), len(out_specs), len(scratch_shapes)

    def hosted(*refs):
        ins, refs = refs[:n_in], refs[n_in:]
        c_ins, refs = refs[:n], refs[n:]
        outs, refs = refs[:n_out], refs[n_out:]
        c_outs, refs = refs[:n], refs[n:]
        scratch, sems = refs[:n_scr], refs[n_scr:]
        step = [pl.program_id(i) for i in range(len(grid))]

        def at(ends):
            hit = step[0] == ends[0]
            for sidx, e in zip(step[1:], ends[1:]):
                hit = jnp.logical_and(hit, sidx == e)
            return hit

        @pl.when(at([0] * len(grid)))
        def _():
            comm.start(c_ins, c_outs, sems)

        body(*ins, *outs, *scratch)

        @pl.when(at([g - 1 for g in grid]))
        def _():
            comm.finish(c_ins, c_outs, sems)

    any_spec = pl.BlockSpec(memory_space=pl.ANY)
    res = _call(hosted, name=name + "_hosting", grid=grid, in_specs=list(in_specs) + [any_spec] * n,
                out_specs=list(out_specs) + [any_spec] * n, out_shape=list(out_shape) + comm.out_shapes(comm_args),
                scratch_shapes=list(scratch_shapes) + comm.sem_shapes(),
                compiler_params=_params(*["arbitrary"] * len(grid)))(*args, *comm_args)
    return res[:n_out], res[n_out:]


def _exchange(name, arrs):
    n = len(arrs)
    comm = _AllToAll([False] * n)

    def body(*refs):
        ins, outs, sems = refs[:n], refs[n:2 * n], refs[2 * n:]
        comm.start(ins, outs, sems)
        comm.finish(ins, outs, sems)

    any_spec = pl.BlockSpec(memory_space=pl.ANY)
    return _call(body, name=name, out_shape=comm.out_shapes(arrs), in_specs=[any_spec] * n, out_specs=[any_spec] * n,
                 scratch_shapes=comm.sem_shapes())(*arrs)


def _mm(name, a, b, mode, tm, tn, tk, out_dtypes, epi=None, extra=(), blocked_out=False, exact=False, sent=()):
    if mode == "nn":
        (m, k), n = a.shape, b.shape[1]
    elif mode == "nt":
        (m, k), n = a.shape, b.shape[0]
    else:
        (k, m), n = a.shape, b.shape[1]
    tm, tn, tk = min(tm, m), min(tn, n), min(tk, k)
    nk = k // tk
    grid = (m // tm, n // tn, nk)
    n_extra, n_out = len(extra), len(out_dtypes)

    def body(*refs):
        a_ref, b_ref = refs[0], refs[1]
        extra_refs = refs[2:2 + n_extra]
        out_refs = refs[2 + n_extra:2 + n_extra + n_out]
        if exact:
            p = lax.dot_general(a_ref[...], b_ref[...], _DIMS[mode], preferred_element_type=F32,
                                precision=lax.Precision.HIGHEST)
        else:
            p = _dot(a_ref[...].astype(BF16), b_ref[...].astype(BF16), _DIMS[mode])

        def finish(acc):
            outs = (acc,) if epi is None else epi(acc, *[r[...] for r in extra_refs])
            for r, o in zip(out_refs, outs):
                r[...] = o.astype(r.dtype)

        if nk == 1:
            finish(p)
        else:
            acc_ref = refs[-1]
            kk = pl.program_id(2)

            @pl.when(kk == 0)
            def _():
                acc_ref[...] = p

            @pl.when(kk > 0)
            def _():
                acc_ref[...] += p

            @pl.when(kk == nk - 1)
            def _():
                finish(acc_ref[...])

    if mode == "tn":
        a_spec = pl.BlockSpec((tk, tm), lambda i, j, kk: (kk, i))
    else:
        a_spec = pl.BlockSpec((tm, tk), lambda i, j, kk: (i, kk))
    if mode == "nt":
        b_spec = pl.BlockSpec((tn, tk), lambda i, j, kk: (j, kk))
    else:
        b_spec = pl.BlockSpec((tk, tn), lambda i, j, kk: (kk, j))
    tile = pl.BlockSpec((tm, tn), lambda i, j, kk: (i, j))
    if blocked_out:
        o_shape, o_spec = (n // tn, m, tn), pl.BlockSpec((None, tm, tn), lambda i, j, kk: (j, i, 0))
    else:
        o_shape, o_spec = (m, n), tile
    out, parts = _call_hosting(
        body, (a, b, *extra), _AllToAll([False] * len(sent)) if sent else None, sent,
        name=name, grid=grid,
        in_specs=[a_spec, b_spec] + [tile] * n_extra,
        out_specs=[o_spec] * n_out,
        out_shape=[jax.ShapeDtypeStruct(o_shape, dt) for dt in out_dtypes],
        scratch_shapes=[pltpu.VMEM((tm, tn), F32)] if nk > 1 else [])
    out = out[0] if n_out == 1 else out
    return (out, parts) if sent else out


def _tile(width, col=0, rows=TM):
    return pl.BlockSpec((rows, width), lambda i: (i, col))


def _vec(width):
    return pl.BlockSpec((1, width), lambda i: (0, 0))


def _rstd(xf):
    return lax.rsqrt(jnp.mean(xf * xf, axis=-1, keepdims=True) + EPS)


def _colsum(v):
    return jnp.sum(v, axis=0, keepdims=True)


def _accumulate(refs, vals):
    first = pl.program_id(0) == 0

    @pl.when(first)
    def _():
        for r, v in zip(refs, vals):
            r[...] = v

    @pl.when(jnp.logical_not(first))
    def _():
        for r, v in zip(refs, vals):
            r[...] += v


def _prenorm_fwd(x, g, sc, sh):
    s, d = x.shape

    def body(x_ref, g_ref, sc_ref, sh_ref, h_ref):
        xf = x_ref[...]
        y = (xf * _rstd(xf)) * g_ref[...]
        h_ref[...] = (y * (1.0 + sc_ref[...]) + sh_ref[...]).astype(h_ref.dtype)

    return _call(body, name="prenorm_fwd", grid=(s // TM,),
                 in_specs=[_tile(d), _vec(d), _vec(d), _vec(d)], out_specs=_tile(d),
                 out_shape=jax.ShapeDtypeStruct((s, d), BF16), compiler_params=_params("parallel"))(x, g, sc, sh)


def _prenorm_bwd(dh, x, g, sc, dres):
    s, d = x.shape

    def body(dh_ref, x_ref, g_ref, sc_ref, dres_ref, dx_ref, dsh_ref, dsc_ref, dg_ref):
        xf, dhf = x_ref[...], dh_ref[...]
        rstd = _rstd(xf)
        xhat = xf * rstd
        one_sc = 1.0 + sc_ref[...]
        dxhat = dhf * (g_ref[...] * one_sc)
        dx_ref[...] = dres_ref[...] + rstd * (dxhat - xhat * jnp.mean(dxhat * xhat, axis=-1, keepdims=True))
        dhx = dhf * xhat
        _accumulate((dsh_ref, dsc_ref, dg_ref), (_colsum(dhf), _colsum(dhx) * g_ref[...], _colsum(dhx) * one_sc))

    vec_out = jax.ShapeDtypeStruct((1, d), F32)
    return _call(body, name="prenorm_bwd", grid=(s // TM,),
                 in_specs=[_tile(d), _tile(d), _vec(d), _vec(d), _tile(d)],
                 out_specs=[_tile(d), _vec(d), _vec(d), _vec(d)],
                 out_shape=[jax.ShapeDtypeStruct((s, d), F32), vec_out, vec_out, vec_out],
                 compiler_params=_params("arbitrary"))(dh, x, g, sc, dres)


def _postnorm_fwd(xres, m, g, gt):
    s, d = m.shape

    def body(x_ref, m_ref, g_ref, gt_ref, o_ref):
        mf = m_ref[...]
        o_ref[...] = x_ref[...] + gt_ref[...] * ((mf * _rstd(mf)) * g_ref[...])

    return _call(body, name="postnorm_fwd", grid=(s // TM,),
                 in_specs=[_tile(d), _tile(d), _vec(d), _vec(d)], out_specs=_tile(d),
                 out_shape=jax.ShapeDtypeStruct((s, d), F32), compiler_params=_params("parallel"))(xres, m, g, gt)


def _postnorm_bwd(dxn, m, g, gt):
    s, d = m.shape

    def body(dx_ref, m_ref, g_ref, gt_ref, dm_ref, dgt_ref, dg_ref):
        mf, dxf = m_ref[...], dx_ref[...]
        rstd = _rstd(mf)
        mhat = mf * rstd
        dmhat = dxf * (gt_ref[...] * g_ref[...])
        dm_ref[...] = (rstd * (dmhat - mhat * jnp.mean(dmhat * mhat, axis=-1, keepdims=True))).astype(dm_ref.dtype)
        dxm = _colsum(dxf * mhat)
        _accumulate((dgt_ref, dg_ref), (dxm * g_ref[...], dxm * gt_ref[...]))

    vec_out = jax.ShapeDtypeStruct((1, d), F32)
    return _call(body, name="postnorm_bwd", grid=(s // TM,),
                 in_specs=[_tile(d), _tile(d), _vec(d), _vec(d)], out_specs=[_tile(d), _vec(d), _vec(d)],
                 out_shape=[jax.ShapeDtypeStruct((s, d), BF16), vec_out, vec_out],
                 compiler_params=_params("arbitrary"))(dxn, m, g, gt)


def _loss(y, target):
    s, d = y.shape

    def body(y_ref, t_ref, dy_ref, sq_ref):
        err = y_ref[...] - t_ref[...]
        dy_ref[...] = err * (1.0 / d)
        tot = jnp.sum(_colsum(err * err), axis=1, keepdims=True)
        _accumulate((sq_ref,), (jnp.broadcast_to(tot, (1, LANES)),))

    return _call(body, name="loss", grid=(s // TM,), in_specs=[_tile(d), _tile(d)],
                 out_specs=[_tile(d), _vec(LANES)],
                 out_shape=[jax.ShapeDtypeStruct((s, d), F32), jax.ShapeDtypeStruct((1, LANES), F32)],
                 compiler_params=_params("arbitrary"))(y, target)


def _sigmoid(v):
    return 1.0 / (1.0 + jnp.exp(-v))


def _gate_fwd(proj, y_conv, y_attn):
    s, d = y_conv.shape
    ga_col, gb_col = (IN_COLS - 2 * d) // d, (IN_COLS - d) // d

    def body(ga_ref, gb_ref, yc_ref, ya_ref, o_ref):
        o_ref[...] = (_sigmoid(ga_ref[...]) * yc_ref[...] + _sigmoid(gb_ref[...]) * ya_ref[...]).astype(o_ref.dtype)

    return _call(body, name="gate_fwd", grid=(s // TM,),
                 in_specs=[_tile(d, ga_col), _tile(d, gb_col), _tile(d), _tile(d)], out_specs=_tile(d),
                 out_shape=jax.ShapeDtypeStruct((s, d), BF16),
                 compiler_params=_params("parallel"))(proj, proj, y_conv, y_attn)


def _gate_bwd(dmerged, proj, y_conv, y_attn):
    s, d = y_conv.shape
    ga_col, gb_col = (IN_COLS - 2 * d) // d, (IN_COLS - d) // d

    def body(dm_ref, ga_ref, gb_ref, yc_ref, ya_ref, dyc_ref, dya_ref, dga_ref, dgb_ref):
        dm = dm_ref[...]
        sa, sb = _sigmoid(ga_ref[...]), _sigmoid(gb_ref[...])
        dyc_ref[...] = (dm * sa).astype(BF16)
        dya_ref[...] = (dm * sb).astype(BF16)
        dga_ref[...] = (dm * yc_ref[...] * (sa * (1.0 - sa))).astype(BF16)
        dgb_ref[...] = (dm * ya_ref[...] * (sb * (1.0 - sb))).astype(BF16)

    out = jax.ShapeDtypeStruct((s, d), BF16)
    return _call(body, name="gate_bwd", grid=(s // TM,),
                 in_specs=[_tile(d), _tile(d, ga_col), _tile(d, gb_col), _tile(d), _tile(d)],
                 out_specs=[_tile(d)] * 4, out_shape=[out] * 4,
                 compiler_params=_params("parallel"))(dmerged, proj, proj, y_conv, y_attn)


def _shift_down(prev8, cur, by):
    ext = jnp.concatenate([prev8, cur], axis=0)
    return pltpu.roll(ext, by, 0)[8:]


def _shift_up(cur, next8, by):
    ext = jnp.concatenate([cur, next8], axis=0)
    return pltpu.roll(ext, ext.shape[0] - by, 0)[:cur.shape[0]]


def _conv_fwd(proj, conv_w):
    s, w = proj.shape[0], CONV_WIDTH
    per8 = TM // 8

    def prev(col):
        return pl.BlockSpec((8, w), lambda i: (jnp.maximum(i * per8 - 1, 0), col))

    def body(bg_ref, cg_ref, u_ref, cgp_ref, up_ref, w_ref, o_ref):
        vv = cg_ref[...] * u_ref[...]
        pv = cgp_ref[...] * up_ref[...] * jnp.where(pl.program_id(0) > 0, 1.0, 0.0)
        y = w_ref[0:1, :] * _shift_down(pv, vv, 2) + w_ref[1:2, :] * _shift_down(pv, vv, 1) + w_ref[2:3, :] * vv
        o_ref[...] = (bg_ref[...] * y).astype(o_ref.dtype)

    return _call(body, name="conv_fwd", grid=(s // TM,),
                 in_specs=[_tile(w, 0), _tile(w, 1), _tile(w, 2), prev(1), prev(2),
                           pl.BlockSpec((3, w), lambda i: (0, 0))],
                 out_specs=_tile(w), out_shape=jax.ShapeDtypeStruct((s, w), BF16),
                 compiler_params=_params("parallel"))(proj, proj, proj, proj, proj, conv_w)


def _conv_bwd(dyc, proj, conv_w):
    s, w = proj.shape[0], CONV_WIDTH
    per8 = TM // 8
    n_tiles = s // TM

    def prev(col):
        return pl.BlockSpec((8, w), lambda i: (jnp.maximum(i * per8 - 1, 0), col))

    def nxt(col):
        return pl.BlockSpec((8, w), lambda i: (jnp.minimum((i + 1) * per8, s // 8 - 1), col))

    def body(dyc_ref, bg_ref, cg_ref, u_ref, cgp_ref, up_ref, dycn_ref, bgn_ref, w_ref,
             dbg_ref, dcg_ref, du_ref, dw0_ref, dw1_ref, dw2_ref):
        i = pl.program_id(0)
        cg, u = cg_ref[...], u_ref[...]
        vv = cg * u
        pv = cgp_ref[...] * up_ref[...] * jnp.where(i > 0, 1.0, 0.0)
        v1, v2 = _shift_down(pv, vv, 1), _shift_down(pv, vv, 2)
        w0, w1, w2 = w_ref[0:1, :], w_ref[1:2, :], w_ref[2:3, :]
        dyc_t = dyc_ref[...]
        dbg_ref[...] = (dyc_t * (w0 * v2 + w1 * v1 + w2 * vv)).astype(BF16)
        dy = dyc_t * bg_ref[...]
        dyn = dycn_ref[...] * bgn_ref[...] * jnp.where(i < n_tiles - 1, 1.0, 0.0)
        dvv = w2 * dy + w1 * _shift_up(dy, dyn, 1) + w0 * _shift_up(dy, dyn, 2)
        dcg_ref[...] = (dvv * u).astype(BF16)
        du_ref[...] = (dvv * cg).astype(BF16)
        _accumulate((dw0_ref, dw1_ref, dw2_ref), (_colsum(dy * v2), _colsum(dy * v1), _colsum(dy * vv)))

    act = jax.ShapeDtypeStruct((s, w), BF16)
    tap = jax.ShapeDtypeStruct((1, w), F32)
    return _call(body, name="conv_bwd", grid=(n_tiles,),
                 in_specs=[_tile(w), _tile(w, 0), _tile(w, 1), _tile(w, 2), prev(1), prev(2), nxt(0), nxt(0),
                           pl.BlockSpec((3, w), lambda i: (0, 0))],
                 out_specs=[_tile(w)] * 3 + [_vec(w)] * 3, out_shape=[act] * 3 + [tap] * 3,
                 compiler_params=_params("arbitrary"))(dyc, proj, proj, proj, proj, proj, dyc, proj, conv_w)


Q_COL = 3 * CONV_WIDTH // LANES
K_COL = Q_COL + ATTN_WIDTH // LANES
V_COL = K_COL + ATTN_WIDTH // LANES
SCALE = HEAD_DIM ** -0.5


def _head_lanes(hh):
    lane = lax.broadcasted_iota(jnp.int32, (1, LANES), 1)
    return jnp.where((lane >= hh * HEAD_DIM) & (lane < (hh + 1) * HEAD_DIM), 1.0, 0.0)


def _tri(width, keep):
    j = lax.broadcasted_iota(jnp.int32, (TK, width), 0)
    s = lax.broadcasted_iota(jnp.int32, (TK, width), 1)
    return jnp.where((s >= TK) | keep(j, s), 1.0, 0.0).astype(BF16)


def _split_dot(v, tri):
    hi = v.astype(BF16)
    lo = (v - hi.astype(F32)).astype(BF16)
    return _dot(hi, tri) + _dot(lo, tri)


def _logits(qm, kblk, ks, tpos):
    z = _dot(qm, kblk, NT)
    mask = (ks + lax.broadcasted_iota(jnp.int32, (1, TK), 1)) < tpos
    e = jnp.exp(-jnp.abs(z))
    l0 = jnp.where(mask, -(jnp.maximum(z, 0.0) + jnp.log(1.0 + e)), 0.0)
    return z, mask, e, l0


def _neg_softplus(z):
    return jnp.minimum(-z, 0.0) - jnp.log(1.0 + jnp.exp(-jnp.abs(z)))


W2 = 2 * TK
PIPE = 4


def _softplus(z):
    neg_abs = lax.bitcast_convert_type(lax.bitcast_convert_type(z, jnp.uint32) | jnp.uint32(0x80000000), F32)
    return jnp.maximum(z, 0.0) + jnp.log(1.0 + jnp.exp(neg_abs))


def _pair_rows(ref, kb, lm):
    blk = ref[pl.ds(pl.multiple_of(kb * TK, TK), TK), :]
    return jnp.concatenate([(blk * lm[0]).astype(BF16), (blk * lm[1]).astype(BF16)], axis=0)


def _pair_tri(keep):
    j = lax.broadcasted_iota(jnp.int32, (W2, 2 * W2), 0)
    s = lax.broadcasted_iota(jnp.int32, (W2, 2 * W2), 1)
    same_head = (j >= TK) == ((s & (W2 - 1)) >= TK)
    return jnp.where(same_head & ((s >= W2) | keep(j & (TK - 1), s & (TK - 1))), 1.0, 0.0).astype(BF16)


def _attn_fwd(proj, shards=()):
    s = proj.shape[0]
    nq = s // TQ
    diag = TQ // TK
    n_ch = TQ // CH
    assert s // TK <= TK and diag % PIPE == 0

    def body(q_ref, k_ref, v_ref, o_ref, rs_ref, qb_scr, tri_scr, z_scr, l0_scr, cs_scr, a_scr, r_scr, rall_scr, acc_scr):
        qi = pl.program_id(1)
        lm = [_head_lanes(0), _head_lanes(1)]
        lane = lax.broadcasted_iota(jnp.int32, (CH, W2), 1) & (TK - 1)
        row = lax.broadcasted_iota(jnp.int32, (CH, 1), 0)
        col = lax.broadcasted_iota(jnp.int32, (1, W2), 1) & (TK - 1)
        qb_scr[...] = (q_ref[...] * SCALE).astype(BF16)
        tri_scr[...] = _pair_tri(lambda j, ss: j > ss)
        r_scr[...] = jnp.zeros_like(r_scr)
        rall_scr[...] = jnp.zeros_like(rall_scr)
        acc_scr[...] = jnp.zeros_like(acc_scr)

        def causal(kb, c):
            return (kb * TK + col) < (qi * TQ + c * CH + row)

        def chunks(r0):
            return range(0 if r0 is None else r0 // CH, n_ch)

        def on_diagonal(r0, c):
            return r0 is not None and c * CH < r0 + TK

        def logits(kb, zb, r0=0):
            z_scr[zb, r0:, :] = _dot(qb_scr[r0:, :], _pair_rows(k_ref, kb, lm), NT)

        def log_one_minus_beta(kb, zb, lb, r0=None):
            for c in chunks(r0):
                rows = slice(c * CH, (c + 1) * CH)
                sp = _softplus(z_scr[zb, rows, :])
                if on_diagonal(r0, c):
                    sp = jnp.where(causal(kb, c), sp, 0.0)
                l0_scr[lb, rows, :] = sp.astype(BF16)

        def sums(lb, r0=0):
            cs_scr[r0:, :] = _dot(l0_scr[lb, r0:, :], tri_scr[...])

        def weights(kb, zb, lb, ab, r0=None):
            for c in chunks(r0):
                rows = slice(c * CH, (c + 1) * CH)
                near = r_scr[rows, :]
                a = jnp.exp(z_scr[zb, rows, :] - l0_scr[lb, rows, :].astype(F32) - cs_scr[rows, :W2] - near)
                if on_diagonal(r0, c):
                    a = jnp.where(causal(kb, c), a, 0.0)
                a_scr[ab, rows, :] = a.astype(BF16)
                rall_scr[rows, :] = jnp.where(lane == kb, near, rall_scr[rows, :])
                r_scr[rows, :] = near + cs_scr[rows, W2:]

        def weighted_values(kb, ab, r0=0):
            acc_scr[r0:, :] += _dot(a_scr[ab, r0:, :], _pair_rows(v_ref, kb, lm))

        n = qi * diag
        for dd in reversed(range(diag)):
            kb, r0 = n + dd, dd * TK
            logits(kb, dd % PIPE, r0)
            log_one_minus_beta(kb, dd % PIPE, dd % 2, r0)
            sums(dd % 2, r0)
            weights(kb, dd % PIPE, dd % 2, dd % 2, r0)
            weighted_values(kb, dd % 2, r0)

        def block(j):
            return jnp.maximum(n - 1 - j, 0)

        a_scr[...] = jnp.zeros_like(a_scr)
        logits(block(0), 0)
        logits(block(1), 1)
        log_one_minus_beta(block(0), 0, 0)

        def trip(m, carry):
            for u in range(PIPE):
                j = PIPE * m + u
                weighted_values(block(j - 1), (u - 1) % 2)
                sums(u % 2)
                logits(block(j + 2), (u + 2) % PIPE)
                log_one_minus_beta(block(j + 1), (u + 1) % PIPE, (u + 1) % 2)
                weights(block(j), u % PIPE, u % 2, u % 2)
            return carry

        lax.fori_loop(0, n // PIPE, trip, 0)
        weighted_values(block(n - 1), (PIPE - 1) % 2)
        rs_ref[...] = rall_scr[...]
        o_ref[...] = acc_scr[...]

    (o, rsave), gathered = _call_hosting(
        body, (proj, proj, proj), _AllToAll([True] * len(shards)) if shards else None, shards,
        name="attn_fwd", grid=(N_HEADS // 2, nq),
        in_specs=[pl.BlockSpec((TQ, LANES), lambda p, qi: (qi, Q_COL + p)),
                  pl.BlockSpec((s, LANES), lambda p, qi: (0, K_COL + p)),
                  pl.BlockSpec((s, LANES), lambda p, qi: (0, V_COL + p))],
        out_specs=[pl.BlockSpec((TQ, LANES), lambda p, qi: (qi, p)),
                   pl.BlockSpec((TQ, W2), lambda p, qi: (qi, p))],
        out_shape=[jax.ShapeDtypeStruct((s, ATTN_WIDTH), F32), jax.ShapeDtypeStruct((s, N_HEADS // 2 * W2), F32)],
        scratch_shapes=[pltpu.VMEM((TQ, LANES), BF16), pltpu.VMEM((W2, 2 * W2), BF16),
                        pltpu.VMEM((PIPE, TQ, W2), F32), pltpu.VMEM((2, TQ, W2), BF16),
                        pltpu.VMEM((TQ, 2 * W2), F32), pltpu.VMEM((2, TQ, W2), BF16),
                        pltpu.VMEM((TQ, W2), F32), pltpu.VMEM((TQ, W2), F32), pltpu.VMEM((TQ, LANES), F32)])
    return o, rsave, gathered


def _attn_fwd_rows(proj):
    s = proj.shape[0]
    nq = s // TQ
    diag = TQ // TK
    n_ch = 2 * TQ // CH
    assert s // TK <= LANES

    def body(q_ref, k_ref, v_ref, o_ref, rs_ref, q2_scr, z_scr, l0_scr, cs_scr, a_scr, r_scr, rall_scr, acc_scr):
        qi = pl.program_id(1)
        lm = [_head_lanes(0), _head_lanes(1)]
        tri = _tri(2 * TK, lambda j, ss: j > ss)
        lane = lax.broadcasted_iota(jnp.int32, (CH, LANES), 1)
        row = lax.broadcasted_iota(jnp.int32, (CH, 1), 0)
        col = lax.broadcasted_iota(jnp.int32, (1, TK), 1)
        for hh in range(2):
            q2_scr[hh * TQ:(hh + 1) * TQ, :] = (q_ref[...] * (SCALE * lm[hh])).astype(BF16)
        r_scr[...] = jnp.zeros_like(r_scr)
        rall_scr[...] = jnp.zeros_like(rall_scr)
        acc_scr[...] = jnp.zeros_like(acc_scr)

        def step(kb, masked):
            ks = pl.multiple_of(kb * TK, TK)
            kblk = k_ref[pl.ds(ks, TK), :].astype(BF16)
            vf = v_ref[pl.ds(ks, TK), :]
            v2 = jnp.concatenate([(vf * lm[0]).astype(BF16), (vf * lm[1]).astype(BF16)], axis=0)
            z_scr[...] = _dot(q2_scr[...], kblk, NT)

            def causal(c):
                return (ks + col) < (qi * TQ + (c * CH) % TQ + row)

            for c in range(n_ch):
                rows = slice(c * CH, (c + 1) * CH)
                l0 = _neg_softplus(z_scr[rows, :])
                if masked:
                    l0 = jnp.where(causal(c), l0, 0.0)
                l0_scr[rows, :] = l0.astype(BF16)
            cs_scr[...] = _dot(l0_scr[...], tri)
            for c in range(n_ch):
                rows = slice(c * CH, (c + 1) * CH)
                hh, r0 = (c * CH) // TQ, (c * CH) % TQ
                near = r_scr[rows, :]
                a = jnp.exp(l0_scr[rows, :].astype(F32) + z_scr[rows, :] + cs_scr[rows, :TK] + near)
                if masked:
                    a = jnp.where(causal(c), a, 0.0)
                a_scr[r0:r0 + CH, hh * TK:(hh + 1) * TK] = a.astype(BF16)
                rall_scr[rows, :] = jnp.where(lane == kb, near, rall_scr[rows, :])
                r_scr[rows, :] = near + cs_scr[rows, TK:]
            acc_scr[...] += _dot(a_scr[...], v2)

        def diag_step(it, carry):
            step((qi + 1) * diag - 1 - it, True)
            return carry

        def inner_step(it, carry):
            step(qi * diag - 1 - it, False)
            return carry

        lax.fori_loop(0, diag, diag_step, 0)
        lax.fori_loop(0, qi * diag, inner_step, 0)
        for hh in range(2):
            rs_ref[hh] = rall_scr[hh * TQ:(hh + 1) * TQ, :]
        o_ref[...] = acc_scr[...]

    return _call(
        body, name="attn_fwd", grid=(N_HEADS // 2, nq),
        in_specs=[pl.BlockSpec((TQ, LANES), lambda p, qi: (qi, Q_COL + p)),
                  pl.BlockSpec((s, LANES), lambda p, qi: (0, K_COL + p)),
                  pl.BlockSpec((s, LANES), lambda p, qi: (0, V_COL + p))],
        out_specs=[pl.BlockSpec((TQ, LANES), lambda p, qi: (qi, p)),
                   pl.BlockSpec((2, TQ, LANES), lambda p, qi: (p, qi, 0))],
        out_shape=[jax.ShapeDtypeStruct((s, ATTN_WIDTH), F32), jax.ShapeDtypeStruct((N_HEADS, s, LANES), F32)],
        scratch_shapes=[pltpu.VMEM((2 * TQ, LANES), BF16), pltpu.VMEM((2 * TQ, TK), F32),
                        pltpu.VMEM((2 * TQ, TK), BF16), pltpu.VMEM((2 * TQ, 2 * TK), F32),
                        pltpu.VMEM((TQ, 2 * TK), BF16), pltpu.VMEM((2 * TQ, LANES), F32),
                        pltpu.VMEM((2 * TQ, LANES), F32), pltpu.VMEM((TQ, LANES), F32)],
        compiler_params=_params("parallel", "arbitrary"),
    )(proj, proj, proj)


def _attn_bwd(proj, do, rsave, sent=()):
    s = proj.shape[0]
    nq = s // TQ
    diag = TQ // TK
    n_ch = TQ // CH
    assert diag % PIPE == 0

    def body(q_ref, k_ref, v_ref, do_ref, rs_ref, dq_ref, dk_ref, dv_ref,
             qb_scr, dob_scr, after_scr, before_scr, z_scr, da_scr, l0_scr, beta_scr, cs_scr, a_scr, g_scr, cg_scr,
             dz_scr, pg_scr, dq_scr, dk_scr, dv_scr):
        qi = pl.program_id(1)

        @pl.when(qi == 0)
        def _():
            dk_scr[...] = jnp.zeros_like(dk_scr)
            dv_scr[...] = jnp.zeros_like(dv_scr)

        lm = [_head_lanes(0), _head_lanes(1)]
        lane = lax.broadcasted_iota(jnp.int32, (CH, TK), 1)
        row = lax.broadcasted_iota(jnp.int32, (CH, 1), 0)
        col = lax.broadcasted_iota(jnp.int32, (1, W2), 1) & (TK - 1)
        qb_scr[...] = (q_ref[...] * SCALE).astype(BF16)
        dob_scr[...] = do_ref[...].astype(BF16)
        after_scr[...] = _pair_tri(lambda j, ss: j > ss)[:, :W2]
        before_scr[...] = _pair_tri(lambda j, ss: j < ss)
        pg_scr[...] = jnp.zeros_like(pg_scr)
        dq_scr[...] = jnp.zeros_like(dq_scr)
        dz_scr[...] = jnp.zeros_like(dz_scr)

        def causal(kb, c):
            return (kb * TK + col) < (qi * TQ + c * CH + row)

        def chunks(r0):
            return range(0 if r0 is None else r0 // CH, n_ch)

        def on_diagonal(r0, c):
            return r0 is not None and c * CH < r0 + TK

        def logits(kb, zb, r0=0):
            z_scr[zb, r0:, :] = _dot(qb_scr[r0:, :], _pair_rows(k_ref, kb, lm), NT)

        def do_dot_v(kb, db, r0=0):
            da_scr[db, r0:, :] = _dot(dob_scr[r0:, :], _pair_rows(v_ref, kb, lm), NT)

        def gates(kb, zb, lb, bb, r0=None):
            for c in chunks(r0):
                rows = slice(c * CH, (c + 1) * CH)
                z = z_scr[zb, rows, :]
                sp = _softplus(z)
                beta_scr[bb, rows, :] = jnp.exp(z - sp)
                if on_diagonal(r0, c):
                    sp = jnp.where(causal(kb, c), sp, 0.0)
                l0_scr[lb, rows, :] = sp.astype(BF16)

        def suffix_sums(lb, r0=0):
            cs_scr[r0:, :] = _dot(l0_scr[lb, r0:, :], after_scr[...])

        def weights(kb, zb, lb, db, ab, r0=None):
            for c in chunks(r0):
                rows = slice(c * CH, (c + 1) * CH)
                keep = (kb * TK + lane) < (qi * TQ + c * CH + row) if on_diagonal(r0, c) else None
                for hh in range(2):
                    cols = slice(hh * TK, (hh + 1) * TK)
                    near = jnp.sum(jnp.where(lane == kb, rs_ref[rows, cols], 0.0), axis=1, keepdims=True)
                    a = jnp.exp(z_scr[zb, rows, cols] - l0_scr[lb, rows, cols].astype(F32) - cs_scr[rows, cols] - near)
                    if keep is not None:
                        a = jnp.where(keep, a, 0.0)
                    a_scr[ab, rows, cols] = a.astype(BF16)
                    g_scr[ab, rows, cols] = (a * da_scr[db, rows, cols]).astype(BF16)

        def prefix_sums(ab, r0=0):
            cg_scr[r0:, :] = _dot(g_scr[ab, r0:, :], before_scr[...])

        def dlogits(kb, ab, bb, zb2, r0=None):
            for c in chunks(r0):
                rows = slice(c * CH, (c + 1) * CH)
                before = cg_scr[rows, :W2] + pg_scr[rows, :]
                beta = beta_scr[bb, rows, :]
                dz = g_scr[ab, rows, :].astype(F32) * (1.0 - beta) - beta * before
                if on_diagonal(r0, c):
                    dz = jnp.where(causal(kb, c), dz, 0.0)
                dz_scr[zb2, rows, :] = dz.astype(BF16)
                pg_scr[rows, :] += cg_scr[rows, W2:]

        def fold(t):
            return t[:TK, :] * lm[0] + t[TK:, :] * lm[1]

        def dq_dk(kb, zb2, r0=0):
            dq_scr[r0:, :] += _dot(dz_scr[zb2, r0:, :], _pair_rows(k_ref, kb, lm))
            dk_scr[pl.ds(pl.multiple_of(kb * TK, TK), TK), :] += fold(_dot(dz_scr[zb2, r0:, :], qb_scr[r0:, :], TN))

        def dv(kb, ab, r0=0):
            dv_scr[pl.ds(pl.multiple_of(kb * TK, TK), TK), :] += fold(_dot(a_scr[ab, r0:, :], dob_scr[r0:, :], TN))

        n = qi * diag

        def block(j):
            return jnp.clip(j, 0, jnp.maximum(n - 1, 0))

        logits(block(0), 0)
        logits(block(1), 1)
        logits(block(2), 2)
        do_dot_v(block(0), 0)
        do_dot_v(block(1), 1)
        gates(block(0), 0, 0, 0)
        gates(block(1), 1, 1, 1)
        suffix_sums(0)
        weights(block(0), 0, 0, 0, 0)

        def trip(m, carry):
            for u in range(PIPE):
                t = PIPE * m + u
                dq_dk(block(t - 1), (u - 1) % 2)
                dv(block(t), u % 2)
                prefix_sums(u % 2)
                suffix_sums((u + 1) % 2)
                logits(block(t + 3), (u + 3) % PIPE)
                do_dot_v(block(t + 2), u % 2)
                gates(block(t + 2), (u + 2) % PIPE, u % 2, (u + 2) % PIPE)
                weights(block(t + 1), (u + 1) % PIPE, (u + 1) % 2, (u + 1) % 2, (u + 1) % 2)
                dlogits(block(t), u % 2, u % PIPE, u % 2)
            return carry

        lax.fori_loop(0, n // PIPE, trip, 0)
        dq_dk(block(n - 1), (PIPE - 1) % 2)

        for dd in range(diag):
            kb, r0, two, four = n + dd, dd * TK, dd % 2, dd % PIPE
            logits(kb, four, r0)
            do_dot_v(kb, two, r0)
            gates(kb, four, two, four, r0)
            suffix_sums(two, r0)
            weights(kb, four, two, two, two, r0)
            dv(kb, two, r0)
            prefix_sums(two, r0)
            dlogits(kb, two, four, two, r0)
            dq_dk(kb, two, r0)
        dq_ref[...] = (dq_scr[...] * SCALE).astype(dq_ref.dtype)

        @pl.when(qi == nq - 1)
        def _():
            dk_ref[...] = dk_scr[...].astype(dk_ref.dtype)
            dv_ref[...] = dv_scr[...].astype(dv_ref.dtype)

    def rows(c0):
        return pl.BlockSpec((TQ, LANES), lambda p, qi: (qi, c0 + p))

    def whole(c0):
        return pl.BlockSpec((s, LANES), lambda p, qi: (0, c0 + p))

    def f32(*shape):
        return pltpu.VMEM(shape, F32)

    def bf16(*shape):
        return pltpu.VMEM(shape, BF16)

    out = jax.ShapeDtypeStruct((s, ATTN_WIDTH), BF16)
    (dq, dk, dv), parts = _call_hosting(
        body, (proj, proj, proj, do, rsave), _AllToAll([False] * len(sent)) if sent else None, sent,
        name="attn_bwd", grid=(N_HEADS // 2, nq),
        in_specs=[rows(Q_COL), whole(K_COL), whole(V_COL), rows(0), pl.BlockSpec((TQ, W2), lambda p, qi: (qi, p))],
        out_specs=[rows(0), whole(0), whole(0)], out_shape=[out] * 3,
        scratch_shapes=[bf16(TQ, LANES), bf16(TQ, LANES), bf16(W2, W2), bf16(W2, 2 * W2),
                        f32(PIPE, TQ, W2), f32(2, TQ, W2), bf16(2, TQ, W2), f32(PIPE, TQ, W2), f32(TQ, W2),
                        bf16(2, TQ, W2), bf16(2, TQ, W2), f32(TQ, 2 * W2), bf16(2, TQ, W2),
                        f32(TQ, W2), f32(TQ, LANES), f32(s, LANES), f32(s, LANES)])
    return dq, dk, dv, parts


def _attn_bwd_rows(proj, do, rsave):
    s = proj.shape[0]
    nq = s // TQ

    diag = TQ // TK
    n_ch = 2 * TQ // CH

    def body(q_ref, k_ref, v_ref, do_ref, rs_ref, dq_ref, dk_ref, dv_ref,
             q2_scr, do2_scr, z_scr, da_scr, l0_scr, beta_scr, cs_scr, a_scr, g_scr, cg_scr, dz_scr,
             pg_scr, dq_scr, dk_scr, dv_scr):
        qi = pl.program_id(1)

        @pl.when(qi == 0)
        def _():
            dk_scr[...] = jnp.zeros_like(dk_scr)
            dv_scr[...] = jnp.zeros_like(dv_scr)

        lm = [_head_lanes(0), _head_lanes(1)]
        tri_after = _tri(TK, lambda j, ss: j > ss)
        tri_before = _tri(2 * TK, lambda j, ss: j < ss)
        lane = lax.broadcasted_iota(jnp.int32, (CH, LANES), 1)
        row = lax.broadcasted_iota(jnp.int32, (CH, 1), 0)
        col = lax.broadcasted_iota(jnp.int32, (1, TK), 1)
        for hh in range(2):
            q2_scr[hh * TQ:(hh + 1) * TQ, :] = (q_ref[...] * (SCALE * lm[hh])).astype(BF16)
            do2_scr[hh * TQ:(hh + 1) * TQ, :] = (do_ref[...] * lm[hh]).astype(BF16)
        pg_scr[...] = jnp.zeros_like(pg_scr)
        dq_scr[...] = jnp.zeros_like(dq_scr)

        def step(kb, masked):
            ks = pl.multiple_of(kb * TK, TK)
            kblk = k_ref[pl.ds(ks, TK), :].astype(BF16)
            vblk = v_ref[pl.ds(ks, TK), :].astype(BF16)
            z_scr[...] = _dot(q2_scr[...], kblk, NT)
            da_scr[...] = _dot(do2_scr[...], vblk, NT)

            def causal(c):
                return (ks + col) < (qi * TQ + (c * CH) % TQ + row)

            for c in range(n_ch):
                rows = slice(c * CH, (c + 1) * CH)
                z = z_scr[rows, :]
                e = jnp.exp(-jnp.abs(z))
                w = 1.0 + e
                l0 = jnp.minimum(-z, 0.0) - jnp.log(w)
                if masked:
                    l0 = jnp.where(causal(c), l0, 0.0)
                l0_scr[rows, :] = l0.astype(BF16)
                rinv = 1.0 / w
                beta_scr[rows, :] = jnp.where(z >= 0.0, rinv, e * rinv)
            cs_scr[...] = _dot(l0_scr[...], tri_after)
            for c in range(n_ch):
                rows = slice(c * CH, (c + 1) * CH)
                hh, r0 = (c * CH) // TQ, (c * CH) % TQ
                near = jnp.sum(jnp.where(lane == kb, rs_ref[hh, r0:r0 + CH, :], 0.0), axis=1, keepdims=True)
                a = jnp.exp(l0_scr[rows, :].astype(F32) + z_scr[rows, :] + cs_scr[rows, :] + near)
                if masked:
                    a = jnp.where(causal(c), a, 0.0)
                a_scr[rows, :] = a.astype(BF16)
                g_scr[rows, :] = (a * da_scr[rows, :]).astype(BF16)
            cg_scr[...] = _dot(g_scr[...], tri_before)
            for c in range(n_ch):
                rows = slice(c * CH, (c + 1) * CH)
                before = cg_scr[rows, :TK] + pg_scr[rows, :]
                beta = beta_scr[rows, :]
                dz = g_scr[rows, :].astype(F32) * (1.0 - beta) - beta * before
                if masked:
                    dz = jnp.where(causal(c), dz, 0.0)
                dz_scr[rows, :] = dz.astype(BF16)
                pg_scr[rows, :] += cg_scr[rows, TK:]
            dq_scr[...] += _dot(dz_scr[...], kblk)
            dk_scr[pl.ds(ks, TK), :] += _dot(dz_scr[...], q2_scr[...], TN)
            dv_scr[pl.ds(ks, TK), :] += _dot(a_scr[...], do2_scr[...], TN)

        def inner_step(kb, carry):
            step(kb, False)
            return carry

        def diag_step(it, carry):
            step(qi * diag + it, True)
            return carry

        lax.fori_loop(0, qi * diag, inner_step, 0)
        lax.fori_loop(0, diag, diag_step, 0)
        dq = dq_scr[:TQ, :] * lm[0] + dq_scr[TQ:, :] * lm[1]
        dq_ref[...] = (dq * SCALE).astype(dq_ref.dtype)

        @pl.when(qi == nq - 1)
        def _():
            dk_ref[...] = dk_scr[...].astype(dk_ref.dtype)
            dv_ref[...] = dv_scr[...].astype(dv_ref.dtype)

    stacked_f32 = pltpu.VMEM((2 * TQ, TK), F32)
    stacked_bf16 = pltpu.VMEM((2 * TQ, TK), BF16)

    def rows(c0):
        return pl.BlockSpec((TQ, LANES), lambda p, qi: (qi, c0 + p))

    def whole(c0):
        return pl.BlockSpec((s, LANES), lambda p, qi: (0, c0 + p))

    out = jax.ShapeDtypeStruct((s, ATTN_WIDTH), BF16)
    return _call(
        body, name="attn_bwd", grid=(N_HEADS // 2, nq),
        in_specs=[rows(Q_COL), whole(K_COL), whole(V_COL), rows(0),
                  pl.BlockSpec((2, TQ, LANES), lambda p, qi: (p, qi, 0))],
        out_specs=[rows(0), whole(0), whole(0)], out_shape=[out] * 3,
        scratch_shapes=[stacked_bf16, stacked_bf16, stacked_f32, stacked_f32, stacked_bf16, stacked_f32, stacked_f32,
                        stacked_bf16, stacked_bf16, pltpu.VMEM((2 * TQ, 2 * TK), F32), stacked_bf16,
                        stacked_f32, stacked_f32, pltpu.VMEM((s, LANES), F32), pltpu.VMEM((s, LANES), F32)],
        compiler_params=_params("arbitrary", "arbitrary"),
    )(proj, proj, proj, do, rsave)


def _sum_adamw(name, parts, w, m, v, layer=None, into=None):
    n, r, c = parts.shape
    tr = r if r <= 256 else 256

    def body(p_ref, w_ref, m_ref, v_ref, g_ref, d_ref, nm_ref, nv_ref):
        g = p_ref[0].astype(F32)
        for j in range(1, n):
            g = g + p_ref[j].astype(F32)
        nm = ADAM_B1 * m_ref[...] + (1.0 - ADAM_B1) * g
        nv = ADAM_B2 * v_ref[...] + (1.0 - ADAM_B2) * (g * g)
        m_hat = nm / (1.0 - ADAM_B1 ** ADAM_STEP)
        v_hat = nv / (1.0 - ADAM_B2 ** ADAM_STEP)
        g_ref[...] = g
        d_ref[...] = -ADAM_LR * (m_hat / (jnp.sqrt(v_hat) + ADAM_EPS) + ADAM_WD * w_ref[...])
        nm_ref[...] = nm
        nv_ref[...] = nv

    if layer is None:
        mat = pl.BlockSpec((tr, c), lambda i: (i, 0))
        out = jax.ShapeDtypeStruct((r, c), F32)
    else:
        mat = pl.BlockSpec((None, tr, c), lambda i: (layer, i, 0))
        out = jax.ShapeDtypeStruct((DEPTH, r, c), F32)
    earlier = () if into is None else tuple(into)
    return _call(body if into is None else lambda *refs: body(*refs[:4], *refs[8:]),
                 name=name, grid=(r // tr,),
                 in_specs=[pl.BlockSpec((n, tr, c), lambda i: (0, i, 0)), mat, mat, mat]
                 + [pl.BlockSpec(memory_space=pl.ANY)] * len(earlier),
                 out_specs=[mat] * 4, out_shape=[out] * 4,
                 input_output_aliases={4 + k: k for k in range(len(earlier))},
                 compiler_params=_params("parallel"))(parts, w, m, v, *earlier)


def _natural(gathered):
    _, k, n = gathered.shape
    return gathered.transpose(1, 0, 2).reshape(k, N_DEV * n)


def _relu2_epi(acc):
    r = jnp.maximum(acc, 0.0)
    return acc, r * r


def _relu2_bwd_epi(acc, a_act):
    return (acc * (2.0 * jnp.maximum(a_act, 0.0)),)


def kernel(x, c, w_ada, b_ada, g_pre_mix, g_post_mix, g_pre_mlp, g_post_mlp, w_in, conv_w, w_proj_conv, w_proj_attn, w_out, w_mlp_in, w_mlp_out, loss_target, m_w_ada, m_b_ada, m_g_pre_mix, m_g_post_mix, m_g_pre_mlp, m_g_post_mlp, m_w_in, m_conv_w, m_w_proj_conv, m_w_proj_attn, m_w_out, m_w_mlp_in, m_w_mlp_out, v_w_ada, v_b_ada, v_g_pre_mix, v_g_post_mix, v_g_pre_mlp, v_g_post_mlp, v_w_in, v_conv_w, v_w_proj_conv, v_w_proj_attn, v_w_out, v_w_mlp_in, v_w_mlp_out):
    xi, yi, ci = _mesh_pos()
    me = 4 * xi + 2 * yi + ci
    d = D_MODEL
    x0 = x[0]
    seq = x0.shape[0]
    ada_cols = w_ada.shape[2]
    conv_cols = conv_w.shape[2]

    small = jnp.concatenate([c.reshape(-1), conv_w.reshape(-1)])
    small = jnp.pad(small, (0, 2 * d - small.shape[0])).reshape(8, 2 * d // 8)
    small_all = _all_gather("gather_c", [small])[0].reshape(N_DEV, 2 * d)
    c_all = small_all[:, :d]
    conv_all = small_all[:, d:d + DEPTH * 3 * conv_cols].reshape(N_DEV, DEPTH, 3, conv_cols)
    conv_all = conv_all.transpose(1, 2, 0, 3).reshape(DEPTH, 3, N_DEV * conv_cols)
    mod_cols = jnp.stack([_mm("mod_mm", c_all, w_ada[l], "nn", N_DEV, ada_cols, d, [F32], exact=True)
                          for l in range(DEPTH)], axis=1)
    mod_all = _all_gather("gather_mod", [mod_cols.reshape(N_DEV, DEPTH * ada_cols)])[0]
    mod_mine = lax.dynamic_index_in_dim(mod_all, me, axis=1, keepdims=False).reshape(N_DEV, DEPTH, ada_cols)
    mod = mod_mine.transpose(1, 0, 2).reshape(DEPTH, N_MOD * d) + b_ada

    sharded = {"w_in": w_in, "w_proj_conv": w_proj_conv, "w_proj_attn": w_proj_attn, "w_out": w_out,
               "w_mlp_in": w_mlp_in, "w_mlp_out": w_mlp_out}
    before_attention = ["w_in", "w_proj_conv"]

    def shard(key):
        nm, l = key
        return sharded[nm][l].astype(BF16)

    def natural(key, gathered):
        return gathered.reshape(-1, d) if key[0] in ("w_out", "w_mlp_out") else _natural(gathered)

    first = [(nm, 0) for nm in before_attention]
    full = {key: natural(key, g) for key, g in zip(first, _all_gather("gather_w", [shard(key) for key in first]))}
    saved = []
    xl = x0
    for l in range(DEPTH):
        riders = [(nm, l) for nm in sharded if (nm, l) not in full]
        if l + 1 < DEPTH:
            riders += [(nm, l + 1) for nm in sharded]
        sh1, sc1, gt1, sh2, sc2, gt2 = [mod[l:l + 1, i * d:(i + 1) * d] for i in range(N_MOD)]
        h = _prenorm_fwd(xl, g_pre_mix[l:l + 1], sc1, sh1)
        proj = _mm("proj", h, full[("w_in", l)], "nn", TMM, 1024, d, [F32])
        yc = _conv_fwd(proj, conv_all[l])
        y_conv = _mm("proj_conv", yc, full[("w_proj_conv", l)], "nn", TMM, d, CONV_WIDTH, [F32])
        o, rsave, gathered = _attn_fwd(proj, [shard(key) for key in riders])
        full.update({key: natural(key, g) for key, g in zip(riders, gathered)})
        wg_in, wg_pc, wg_pa, wg_out, wg_mi, wg_mo = [full[(nm, l)] for nm in sharded]
        y_attn = _mm("proj_attn", o, wg_pa, "nn", TMM,d, ATTN_WIDTH, [F32])
        merged = _gate_fwd(proj, y_conv, y_attn)
        mix_out = _mm("mix_out", merged, wg_out, "nn", TMM,d, d, [F32])
        x1 = _postnorm_fwd(xl, mix_out, g_post_mix[l:l + 1], gt1)
        h2 = _prenorm_fwd(x1, g_pre_mlp[l:l + 1], sc2, sh2)
        a_act, r = _mm("mlp_in", h2, wg_mi, "nn", TMM,1024, d, [F32, BF16], epi=_relu2_epi)
        ff = _mm("mlp_out", r, wg_mo, "nn", TMM,d, 1024, [F32])
        x2 = _postnorm_fwd(x1, ff, g_post_mlp[l:l + 1], gt2)
        saved.append((xl, h, proj, yc, o, rsave, y_conv, y_attn, merged, mix_out, x1, h2, a_act, r, ff))
        xl = x2

    dxo, sq = _loss(xl, loss_target[0])
    loss = lax.psum(sq[0, 0] * (0.5 / d), ("x", "y", "c"))

    olds = {"w_in": (w_in, m_w_in, v_w_in), "w_proj_conv": (w_proj_conv, m_w_proj_conv, v_w_proj_conv),
            "w_proj_attn": (w_proj_attn, m_w_proj_attn, v_w_proj_attn), "w_out": (w_out, m_w_out, v_w_out),
            "w_mlp_in": (w_mlp_in, m_w_mlp_in, v_w_mlp_in), "w_mlp_out": (w_mlp_out, m_w_mlp_out, v_w_mlp_out)}
    big = {}
    pending = []

    def col_blocks(gw):
        k, n = gw.shape
        return gw.reshape(k, N_DEV, n // N_DEV).transpose(1, 0, 2)

    def update(entries, parts):
        for (nm, ll, _), part in zip(entries, parts):
            w_, m_, v_ = olds[nm]
            big[nm] = _sum_adamw("adamw_" + nm, part, w_, m_, v_, layer=ll, into=big.get(nm))

    dmod, small_grads = [None] * DEPTH, [None] * DEPTH
    for l in reversed(range(DEPTH)):
        wg_in, wg_pc, wg_pa, wg_out, wg_mi, wg_mo = [full[(nm, l)] for nm in sharded]
        xin, h, proj, yc, o, rsave, y_conv, y_attn, merged, mix_out, x1, h2, a_act, r, ff = saved[l]
        sh1, sc1, gt1, sh2, sc2, gt2 = [mod[l:l + 1, i * d:(i + 1) * d] for i in range(N_MOD)]

        dff, dgt2, dg_post_mlp = _postnorm_bwd(dxo, ff, g_post_mlp[l:l + 1], gt2)
        da = _mm("d_relu2", dff, wg_mo, "nt", TMM,1024, d, [BF16], epi=_relu2_bwd_epi, extra=(a_act,))
        gw_mo = _mm("gw_mlp_out", r, dff, "tn", 1024, d, 1024, [BF16])
        dh2 = _mm("d_h2", da, wg_mi, "nt", TMM,d, 1024, [F32])
        gw_mi = _mm("gw_mlp_in", h2, da, "tn", d, 1024, 1024, [BF16])
        dx1, dsh2, dsc2, dg_pre_mlp = _prenorm_bwd(dh2, x1, g_pre_mlp[l:l + 1], sc2, dxo)

        dmix, dgt1, dg_post_mix = _postnorm_bwd(dx1, mix_out, g_post_mix[l:l + 1], gt1)
        dmerged = _mm("d_merged", dmix, wg_out, "nt", TMM,d, d, [F32])
        gw_out = _mm("gw_out", merged, dmix, "tn", d, d, 1024, [BF16])
        dy_conv, dy_attn, dga, dgb = _gate_bwd(dmerged, proj, y_conv, y_attn)
        do = _mm("d_o", dy_attn, wg_pa, "nt", TMM,ATTN_WIDTH, d, [F32])
        gw_pa = _mm("gw_proj_attn", o, dy_attn, "tn", ATTN_WIDTH, d, 1024, [BF16])
        dyc = _mm("d_yc", dy_conv, wg_pc, "nt", TMM,CONV_WIDTH, d, [F32])
        gw_pc = _mm("gw_proj_conv", yc, dy_conv, "tn", CONV_WIDTH, d, 1024, [BF16])
        pending += [("w_mlp_out", l, gw_mo.reshape(N_DEV, D_FF // N_DEV, d)), ("w_mlp_in", l, col_blocks(gw_mi)),
                    ("w_out", l, gw_out.reshape(N_DEV, d // N_DEV, d)), ("w_proj_attn", l, col_blocks(gw_pa)),
                    ("w_proj_conv", l, col_blocks(gw_pc))]
        if l == 0:
            dq, dk, dv, parts = _attn_bwd(proj, do, rsave, [blocks for _, _, blocks in pending])
            update(pending, parts)
            pending = []
        else:
            dq, dk, dv, _ = _attn_bwd(proj, do, rsave)
        dbg, dcg, du, dw0, dw1, dw2 = _conv_bwd(dyc, proj, conv_all[l])
        dproj = jnp.concatenate([dbg, dcg, du, dq, dk, dv, dga, dgb], axis=1)
        gw_in = _mm("gw_in", h, dproj, "tn", d, 1024, 1024, [BF16])
        pending.append(("w_in", l, col_blocks(gw_in)))
        if l == 0:
            dh, parts = _mm("d_h", dproj, wg_in, "nt", TMM, d, 1024, [F32], sent=[blocks for _, _, blocks in pending])
            update(pending, parts)
            pending = []
        else:
            dh = _mm("d_h", dproj, wg_in, "nt", TMM, d, 1024, [F32])
        dxo, dsh1, dsc1, dg_pre_mix = _prenorm_bwd(dh, xin, g_pre_mix[l:l + 1], sc1, dx1)

        dmod[l] = jnp.concatenate([dsh1, dsc1, dgt1, dsh2, dsc2, dgt2], axis=1)
        small_grads[l] = (dg_pre_mix, dg_post_mix, dg_pre_mlp, dg_post_mlp, jnp.concatenate([dw0, dw1, dw2], axis=0))
    assert not pending

    vec = jnp.concatenate(
        [dmod[l].reshape(-1) for l in range(DEPTH)]
        + [small_grads[l][i].reshape(-1) for i in range(4) for l in range(DEPTH)]
        + [small_grads[l][4].reshape(-1) for l in range(DEPTH)])
    n_vec = vec.shape[0]
    vec_all = _all_gather("gather_small", [vec.reshape(8, n_vec // 8)])[0].reshape(N_DEV, n_vec)
    n_mod = DEPTH * N_MOD * d
    dmod_all = vec_all[:, :n_mod].reshape(N_DEV, DEPTH, N_MOD * d)
    res = {}
    res["b_ada"] = _sum_adamw("adamw_b_ada", dmod_all, b_ada, m_b_ada, v_b_ada)
    off = n_mod
    for nm, (w_, m_, v_) in zip(
            ["g_pre_mix", "g_post_mix", "g_pre_mlp", "g_post_mlp"],
            [(g_pre_mix, m_g_pre_mix, v_g_pre_mix), (g_post_mix, m_g_post_mix, v_g_post_mix),
             (g_pre_mlp, m_g_pre_mlp, v_g_pre_mlp), (g_post_mlp, m_g_post_mlp, v_g_post_mlp)]):
        res[nm] = _sum_adamw("adamw_gain", vec_all[:, off:off + DEPTH * d].reshape(N_DEV, DEPTH, d), w_, m_, v_)
        off += DEPTH * d
    dconv_all = vec_all[:, off:].reshape(N_DEV, DEPTH * 3, CONV_WIDTH)
    dconv_mine = lax.dynamic_slice_in_dim(dconv_all, me * conv_cols, conv_cols, axis=2)
    res["conv_w"] = [t.reshape(DEPTH, 3, conv_cols) for t in _sum_adamw(
        "adamw_conv_w", dconv_mine, conv_w.reshape(DEPTH * 3, conv_cols), m_conv_w.reshape(DEPTH * 3, conv_cols),
        v_conv_w.reshape(DEPTH * 3, conv_cols))]

    c_t = jnp.pad(c_all.T, ((0, 0), (0, LANES - N_DEV)))
    dmod_mine = lax.dynamic_slice_in_dim(dmod_all, me * ada_cols, ada_cols, axis=2)
    for l in range(DEPTH):
        dm_l = jnp.pad(dmod_mine[:, l, :], ((0, LANES - N_DEV), (0, 0)))
        gw_ada = _mm("gw_ada", c_t, dm_l, "nn", 256, ada_cols, LANES, [F32], exact=True)
        res["w_ada"] = _sum_adamw("adamw_w_ada", gw_ada[None], w_ada, m_w_ada, v_w_ada, layer=l, into=res.get("w_ada"))
    res.update(big)

    order = ["w_ada", "b_ada", "g_pre_mix", "g_post_mix", "g_pre_mlp", "g_post_mlp", "w_in", "conv_w",
             "w_proj_conv", "w_proj_attn", "w_out", "w_mlp_in", "w_mlp_out"]
    outs = [loss, dxo[None]]
    for i in range(4):
        outs += [res[nm][i] for nm in order]
    return tuple(outs)
```

```python
import jax
import jax.numpy as jnp
from jax import lax
from jax.experimental import pallas as pl
from jax.experimental.pallas import tpu as pltpu

F32 = jnp.float32
BF16 = jnp.bfloat16
MESH = pl.DeviceIdType.MESH

N_DEV = 8
D_MODEL = 1024
CONV_WIDTH = 512
N_HEADS = 8
HEAD_DIM = 64
ATTN_WIDTH = N_HEADS * HEAD_DIM
D_FF = 4 * D_MODEL
N_MOD = 6
DEPTH = 2
EPS = 1e-6
IN_COLS = 3 * CONV_WIDTH + 3 * ATTN_WIDTH + 2 * D_MODEL
LANES = 128

ADAM_LR = 0.001
ADAM_B1 = 0.9
ADAM_B2 = 0.999
ADAM_EPS = 1e-08
ADAM_WD = 0.01
ADAM_STEP = 10

TM = 512
TMM = 2048
TQ = 512
TK = 128
CH = 64
VMEM_LIMIT = 56 * 1024 * 1024

NN = (((1,), (0,)), ((), ()))
NT = (((1,), (1,)), ((), ()))
TN = (((0,), (0,)), ((), ()))
_DIMS = {"nn": NN, "nt": NT, "tn": TN}


def _call(body, **kw):
    return pl.pallas_call(body, **kw)


def _params(*sem):
    return pltpu.CompilerParams(dimension_semantics=sem, vmem_limit_bytes=VMEM_LIMIT)


def _dot(a, b, dims=NN):
    return lax.dot_general(a, b, dims, preferred_element_type=F32)


def _mesh_pos():
    return lax.axis_index("x"), lax.axis_index("y"), lax.axis_index("c")


def _all_gather(name, arrs):
    n = len(arrs)

    def body(*refs):
        ins, outs = refs[:n], refs[n:2 * n]
        send_sems, recv_sems, local_sems = refs[2 * n:]
        x, y, c = _mesh_pos()
        me, sibling = (x, y, c), (x, y, 1 - c)
        chips = [(1 - x, y), (x, 1 - y), (1 - x, 1 - y)]

        def blk(t, p):
            return outs[t].at[4 * p[0] + 2 * p[1] + p[2]]

        def copy(t, k, block, to, src=None):
            return pltpu.make_async_remote_copy(
                src_ref=blk(t, block) if src is None else src, dst_ref=blk(t, block),
                send_sem=send_sems.at[7 * t + k], recv_sem=recv_sems.at[7 * t + k],
                device_id=to, device_id_type=MESH)

        mine, first, passed = [], [], []
        for t in range(n):
            cp = pltpu.make_async_copy(ins[t], blk(t, me), local_sems.at[t])
            cp.start()
            mine.append(cp)
            cps = [copy(t, 0, me, sibling, src=ins[t])]
            cps += [copy(t, 1 + j, me, (*chip, c), src=ins[t]) for j, chip in enumerate(chips)]
            for cp in cps:
                cp.start()
            first += cps
        for t in range(n):
            for j, chip in enumerate(chips):
                copy(t, 1 + j, (*chip, c), me).wait_recv()
                cp = copy(t, 4 + j, (*chip, c), sibling)
                cp.start()
                passed.append(cp)
        for t in range(n):
            copy(t, 0, sibling, me).wait_recv()
            for j, chip in enumerate(chips):
                copy(t, 4 + j, (*chip, 1 - c), me).wait_recv()
        for cp in first + passed:
            cp.wait_send()
        for cp in mine:
            cp.wait()

    any_spec = pl.BlockSpec(memory_space=pl.ANY)
    return _call(
        body, name=name,
        out_shape=[jax.ShapeDtypeStruct((N_DEV,) + a.shape, a.dtype) for a in arrs],
        in_specs=[any_spec] * n, out_specs=[any_spec] * n,
        scratch_shapes=[pltpu.SemaphoreType.DMA((7 * n,)), pltpu.SemaphoreType.DMA((7 * n,)),
                        pltpu.SemaphoreType.DMA((n,))],
    )(*arrs)


class _AllToAll:
    def __init__(self, whole):
        self.whole = list(whole)
        self.n = len(self.whole)

    def sem_shapes(self):
        return [pltpu.SemaphoreType.DMA((7 * self.n,)), pltpu.SemaphoreType.DMA((7 * self.n,)),
                pltpu.SemaphoreType.DMA((self.n,))]

    def out_shapes(self, arrs):
        return [jax.ShapeDtypeStruct(((N_DEV,) + a.shape) if w else a.shape, a.dtype) for a, w in zip(arrs, self.whole)]

    def _copies(self, ins, outs, sems):
        send_sems, recv_sems, local_sems = sems
        x, y, c = _mesh_pos()
        my_idx = 4 * x + 2 * y + c
        mine, sends, recvs = [], [], []
        for t in range(self.n):
            def src(idx):
                return ins[t] if self.whole[t] else ins[t].at[idx]
            mine.append(pltpu.make_async_copy(src(my_idx), outs[t].at[my_idx], local_sems.at[t]))
            for k in range(1, N_DEV):
                p = (1 - x if k & 4 else x, 1 - y if k & 2 else y, 1 - c if k & 1 else c)
                p_idx = 4 * p[0] + 2 * p[1] + p[2]
                for dst_idx, group in ((my_idx, sends), (p_idx, recvs)):
                    group.append(pltpu.make_async_remote_copy(
                        src_ref=src(p_idx), dst_ref=outs[t].at[dst_idx],
                        send_sem=send_sems.at[7 * t + k - 1], recv_sem=recv_sems.at[7 * t + k - 1],
                        device_id=p, device_id_type=MESH))
        return mine, sends, recvs

    def start(self, ins, outs, sems):
        mine, sends, _ = self._copies(ins, outs, sems)
        for cp in mine + sends:
            cp.start()

    def finish(self, ins, outs, sems):
        mine, sends, recvs = self._copies(ins, outs, sems)
        for cp in recvs:
            cp.wait_recv()
        for cp in sends:
            cp.wait_send()
        for cp in mine:
            cp.wait()


def _call_hosting(body, args, comm, comm_args, *, name, grid, in_specs, out_specs, out_shape, scratch_shapes):
    if comm is None:
        return _call(body, name=name, grid=grid, in_specs=in_specs, out_specs=out_specs, out_shape=out_shape,
                     scratch_shapes=scratch_shapes, compiler_params=_params(*["arbitrary"] * len(grid)))(*args), ()
    n, n_in, n_out, n_scr = comm.n, len(in_specs), len(out_specs), len(scratch_shapes)

    def hosted(*refs):
        ins, refs = refs[:n_in], refs[n_in:]
        c_ins, refs = refs[:n], refs[n:]
        outs, refs = refs[:n_out], refs[n_out:]
        c_outs, refs = refs[:n], refs[n:]
        scratch, sems = refs[:n_scr], refs[n_scr:]
        step = [pl.program_id(i) for i in range(len(grid))]

        def at(ends):
            hit = step[0] == ends[0]
            for sidx, e in zip(step[1:], ends[1:]):
                hit = jnp.logical_and(hit, sidx == e)
            return hit

        @pl.when(at([0] * len(grid)))
        def _():
            comm.start(c_ins, c_outs, sems)

        body(*ins, *outs, *scratch)

        @pl.when(at([g - 1 for g in grid]))
        def _():
            comm.finish(c_ins, c_outs, sems)

    any_spec = pl.BlockSpec(memory_space=pl.ANY)
    res = _call(hosted, name=name + "_hosting", grid=grid, in_specs=list(in_specs) + [any_spec] * n,
                out_specs=list(out_specs) + [any_spec] * n, out_shape=list(out_shape) + comm.out_shapes(comm_args),
                scratch_shapes=list(scratch_shapes) + comm.sem_shapes(),
                compiler_params=_params(*["arbitrary"] * len(grid)))(*args, *comm_args)
    return res[:n_out], res[n_out:]


def _exchange(name, arrs):
    n = len(arrs)
    comm = _AllToAll([False] * n)

    def body(*refs):
        ins, outs, sems = refs[:n], refs[n:2 * n], refs[2 * n:]
        comm.start(ins, outs, sems)
        comm.finish(ins, outs, sems)

    any_spec = pl.BlockSpec(memory_space=pl.ANY)
    return _call(body, name=name, out_shape=comm.out_shapes(arrs), in_specs=[any_spec] * n, out_specs=[any_spec] * n,
                 scratch_shapes=comm.sem_shapes())(*arrs)


def _mm(name, a, b, mode, tm, tn, tk, out_dtypes, epi=None, extra=(), blocked_out=False, exact=False, sent=()):
    if mode == "nn":
        (m, k), n = a.shape, b.shape[1]
    elif mode == "nt":
        (m, k), n = a.shape, b.shape[0]
    else:
        (k, m), n = a.shape, b.shape[1]
    tm, tn, tk = min(tm, m), min(tn, n), min(tk, k)
    nk = k // tk
    grid = (m // tm, n // tn, nk)
    n_extra, n_out = len(extra), len(out_dtypes)

    def body(*refs):
        a_ref, b_ref = refs[0], refs[1]
        extra_refs = refs[2:2 + n_extra]
        out_refs = refs[2 + n_extra:2 + n_extra + n_out]
        if exact:
            p = lax.dot_general(a_ref[...], b_ref[...], _DIMS[mode], preferred_element_type=F32,
                                precision=lax.Precision.HIGHEST)
        else:
            p = _dot(a_ref[...].astype(BF16), b_ref[...].astype(BF16), _DIMS[mode])

        def finish(acc):
            outs = (acc,) if epi is None else epi(acc, *[r[...] for r in extra_refs])
            for r, o in zip(out_refs, outs):
                r[...] = o.astype(r.dtype)

        if nk == 1:
            finish(p)
        else:
            acc_ref = refs[-1]
            kk = pl.program_id(2)

            @pl.when(kk == 0)
            def _():
                acc_ref[...] = p

            @pl.when(kk > 0)
            def _():
                acc_ref[...] += p

            @pl.when(kk == nk - 1)
            def _():
                finish(acc_ref[...])

    if mode == "tn":
        a_spec = pl.BlockSpec((tk, tm), lambda i, j, kk: (kk, i))
    else:
        a_spec = pl.BlockSpec((tm, tk), lambda i, j, kk: (i, kk))
    if mode == "nt":
        b_spec = pl.BlockSpec((tn, tk), lambda i, j, kk: (j, kk))
    else:
        b_spec = pl.BlockSpec((tk, tn), lambda i, j, kk: (kk, j))
    tile = pl.BlockSpec((tm, tn), lambda i, j, kk: (i, j))
    if blocked_out:
        o_shape, o_spec = (n // tn, m, tn), pl.BlockSpec((None, tm, tn), lambda i, j, kk: (j, i, 0))
    else:
        o_shape, o_spec = (m, n), tile
    out, parts = _call_hosting(
        body, (a, b, *extra), _AllToAll([False] * len(sent)) if sent else None, sent,
        name=name, grid=grid,
        in_specs=[a_spec, b_spec] + [tile] * n_extra,
        out_specs=[o_spec] * n_out,
        out_shape=[jax.ShapeDtypeStruct(o_shape, dt) for dt in out_dtypes],
        scratch_shapes=[pltpu.VMEM((tm, tn), F32)] if nk > 1 else [])
    out = out[0] if n_out == 1 else out
    return (out, parts) if sent else out


def _tile(width, col=0, rows=TM):
    return pl.BlockSpec((rows, width), lambda i: (i, col))


def _vec(width):
    return pl.BlockSpec((1, width), lambda i: (0, 0))


def _rstd(xf):
    return lax.rsqrt(jnp.mean(xf * xf, axis=-1, keepdims=True) + EPS)


def _colsum(v):
    return jnp.sum(v, axis=0, keepdims=True)


def _accumulate(refs, vals):
    first = pl.program_id(0) == 0

    @pl.when(first)
    def _():
        for r, v in zip(refs, vals):
            r[...] = v

    @pl.when(jnp.logical_not(first))
    def _():
        for r, v in zip(refs, vals):
            r[...] += v


def _prenorm_fwd(x, g, sc, sh):
    s, d = x.shape

    def body(x_ref, g_ref, sc_ref, sh_ref, h_ref):
        xf = x_ref[...]
        y = (xf * _rstd(xf)) * g_ref[...]
        h_ref[...] = (y * (1.0 + sc_ref[...]) + sh_ref[...]).astype(h_ref.dtype)

    return _call(body, name="prenorm_fwd", grid=(s // TM,),
                 in_specs=[_tile(d), _vec(d), _vec(d), _vec(d)], out_specs=_tile(d),
                 out_shape=jax.ShapeDtypeStruct((s, d), BF16), compiler_params=_params("parallel"))(x, g, sc, sh)


def _prenorm_bwd(dh, x, g, sc, dres):
    s, d = x.shape

    def body(dh_ref, x_ref, g_ref, sc_ref, dres_ref, dx_ref, dsh_ref, dsc_ref, dg_ref):
        xf, dhf = x_ref[...], dh_ref[...]
        rstd = _rstd(xf)
        xhat = xf * rstd
        one_sc = 1.0 + sc_ref[...]
        dxhat = dhf * (g_ref[...] * one_sc)
        dx_ref[...] = dres_ref[...] + rstd * (dxhat - xhat * jnp.mean(dxhat * xhat, axis=-1, keepdims=True))
        dhx = dhf * xhat
        _accumulate((dsh_ref, dsc_ref, dg_ref), (_colsum(dhf), _colsum(dhx) * g_ref[...], _colsum(dhx) * one_sc))

    vec_out = jax.ShapeDtypeStruct((1, d), F32)
    return _call(body, name="prenorm_bwd", grid=(s // TM,),
                 in_specs=[_tile(d), _tile(d), _vec(d), _vec(d), _tile(d)],
                 out_specs=[_tile(d), _vec(d), _vec(d), _vec(d)],
                 out_shape=[jax.ShapeDtypeStruct((s, d), F32), vec_out, vec_out, vec_out],
                 compiler_params=_params("arbitrary"))(dh, x, g, sc, dres)


def _postnorm_fwd(xres, m, g, gt):
    s, d = m.shape

    def body(x_ref, m_ref, g_ref, gt_ref, o_ref):
        mf = m_ref[...]
        o_ref[...] = x_ref[...] + gt_ref[...] * ((mf * _rstd(mf)) * g_ref[...])

    return _call(body, name="postnorm_fwd", grid=(s // TM,),
                 in_specs=[_tile(d), _tile(d), _vec(d), _vec(d)], out_specs=_tile(d),
                 out_shape=jax.ShapeDtypeStruct((s, d), F32), compiler_params=_params("parallel"))(xres, m, g, gt)


def _postnorm_bwd(dxn, m, g, gt):
    s, d = m.shape

    def body(dx_ref, m_ref, g_ref, gt_ref, dm_ref, dgt_ref, dg_ref):
        mf, dxf = m_ref[...], dx_ref[...]
        rstd = _rstd(mf)
        mhat = mf * rstd
        dmhat = dxf * (gt_ref[...] * g_ref[...])
        dm_ref[...] = (rstd * (dmhat - mhat * jnp.mean(dmhat * mhat, axis=-1, keepdims=True))).astype(dm_ref.dtype)
        dxm = _colsum(dxf * mhat)
        _accumulate((dgt_ref, dg_ref), (dxm * g_ref[...], dxm * gt_ref[...]))

    vec_out = jax.ShapeDtypeStruct((1, d), F32)
    return _call(body, name="postnorm_bwd", grid=(s // TM,),
                 in_specs=[_tile(d), _tile(d), _vec(d), _vec(d)], out_specs=[_tile(d), _vec(d), _vec(d)],
                 out_shape=[jax.ShapeDtypeStruct((s, d), BF16), vec_out, vec_out],
                 compiler_params=_params("arbitrary"))(dxn, m, g, gt)


def _loss(y, target):
    s, d = y.shape

    def body(y_ref, t_ref, dy_ref, sq_ref):
        err = y_ref[...] - t_ref[...]
        dy_ref[...] = err * (1.0 / d)
        tot = jnp.sum(_colsum(err * err), axis=1, keepdims=True)
        _accumulate((sq_ref,), (jnp.broadcast_to(tot, (1, LANES)),))

    return _call(body, name="loss", grid=(s // TM,), in_specs=[_tile(d), _tile(d)],
                 out_specs=[_tile(d), _vec(LANES)],
                 out_shape=[jax.ShapeDtypeStruct((s, d), F32), jax.ShapeDtypeStruct((1, LANES), F32)],
                 compiler_params=_params("arbitrary"))(y, target)


def _sigmoid(v):
    return 1.0 / (1.0 + jnp.exp(-v))


def _gate_fwd(proj, y_conv, y_attn):
    s, d = y_conv.shape
    ga_col, gb_col = (IN_COLS - 2 * d) // d, (IN_COLS - d) // d

    def body(ga_ref, gb_ref, yc_ref, ya_ref, o_ref):
        o_ref[...] = (_sigmoid(ga_ref[...]) * yc_ref[...] + _sigmoid(gb_ref[...]) * ya_ref[...]).astype(o_ref.dtype)

    return _call(body, name="gate_fwd", grid=(s // TM,),
                 in_specs=[_tile(d, ga_col), _tile(d, gb_col), _tile(d), _tile(d)], out_specs=_tile(d),
                 out_shape=jax.ShapeDtypeStruct((s, d), BF16),
                 compiler_params=_params("parallel"))(proj, proj, y_conv, y_attn)


def _gate_bwd(dmerged, proj, y_conv, y_attn):
    s, d = y_conv.shape
    ga_col, gb_col = (IN_COLS - 2 * d) // d, (IN_COLS - d) // d

    def body(dm_ref, ga_ref, gb_ref, yc_ref, ya_ref, dyc_ref, dya_ref, dga_ref, dgb_ref):
        dm = dm_ref[...]
        sa, sb = _sigmoid(ga_ref[...]), _sigmoid(gb_ref[...])
        dyc_ref[...] = (dm * sa).astype(BF16)
        dya_ref[...] = (dm * sb).astype(BF16)
        dga_ref[...] = (dm * yc_ref[...] * (sa * (1.0 - sa))).astype(BF16)
        dgb_ref[...] = (dm * ya_ref[...] * (sb * (1.0 - sb))).astype(BF16)

    out = jax.ShapeDtypeStruct((s, d), BF16)
    return _call(body, name="gate_bwd", grid=(s // TM,),
                 in_specs=[_tile(d), _tile(d, ga_col), _tile(d, gb_col), _tile(d), _tile(d)],
                 out_specs=[_tile(d)] * 4, out_shape=[out] * 4,
                 compiler_params=_params("parallel"))(dmerged, proj, proj, y_conv, y_attn)


def _shift_down(prev8, cur, by):
    ext = jnp.concatenate([prev8, cur], axis=0)
    return pltpu.roll(ext, by, 0)[8:]


def _shift_up(cur, next8, by):
    ext = jnp.concatenate([cur, next8], axis=0)
    return pltpu.roll(ext, ext.shape[0] - by, 0)[:cur.shape[0]]


def _conv_fwd(proj, conv_w):
    s, w = proj.shape[0], CONV_WIDTH
    per8 = TM // 8

    def prev(col):
        return pl.BlockSpec((8, w), lambda i: (jnp.maximum(i * per8 - 1, 0), col))

    def body(bg_ref, cg_ref, u_ref, cgp_ref, up_ref, w_ref, o_ref):
        vv = cg_ref[...] * u_ref[...]
        pv = cgp_ref[...] * up_ref[...] * jnp.where(pl.program_id(0) > 0, 1.0, 0.0)
        y = w_ref[0:1, :] * _shift_down(pv, vv, 2) + w_ref[1:2, :] * _shift_down(pv, vv, 1) + w_ref[2:3, :] * vv
        o_ref[...] = (bg_ref[...] * y).astype(o_ref.dtype)

    return _call(body, name="conv_fwd", grid=(s // TM,),
                 in_specs=[_tile(w, 0), _tile(w, 1), _tile(w, 2), prev(1), prev(2),
                           pl.BlockSpec((3, w), lambda i: (0, 0))],
                 out_specs=_tile(w), out_shape=jax.ShapeDtypeStruct((s, w), BF16),
                 compiler_params=_params("parallel"))(proj, proj, proj, proj, proj, conv_w)


def _conv_bwd(dyc, proj, conv_w):
    s, w = proj.shape[0], CONV_WIDTH
    per8 = TM // 8
    n_tiles = s // TM

    def prev(col):
        return pl.BlockSpec((8, w), lambda i: (jnp.maximum(i * per8 - 1, 0), col))

    def nxt(col):
        return pl.BlockSpec((8, w), lambda i: (jnp.minimum((i + 1) * per8, s // 8 - 1), col))

    def body(dyc_ref, bg_ref, cg_ref, u_ref, cgp_ref, up_ref, dycn_ref, bgn_ref, w_ref,
             dbg_ref, dcg_ref, du_ref, dw0_ref, dw1_ref, dw2_ref):
        i = pl.program_id(0)
        cg, u = cg_ref[...], u_ref[...]
        vv = cg * u
        pv = cgp_ref[...] * up_ref[...] * jnp.where(i > 0, 1.0, 0.0)
        v1, v2 = _shift_down(pv, vv, 1), _shift_down(pv, vv, 2)
        w0, w1, w2 = w_ref[0:1, :], w_ref[1:2, :], w_ref[2:3, :]
        dyc_t = dyc_ref[...]
        dbg_ref[...] = (dyc_t * (w0 * v2 + w1 * v1 + w2 * vv)).astype(BF16)
        dy = dyc_t * bg_ref[...]
        dyn = dycn_ref[...] * bgn_ref[...] * jnp.where(i < n_tiles - 1, 1.0, 0.0)
        dvv = w2 * dy + w1 * _shift_up(dy, dyn, 1) + w0 * _shift_up(dy, dyn, 2)
        dcg_ref[...] = (dvv * u).astype(BF16)
        du_ref[...] = (dvv * cg).astype(BF16)
        _accumulate((dw0_ref, dw1_ref, dw2_ref), (_colsum(dy * v2), _colsum(dy * v1), _colsum(dy * vv)))

    act = jax.ShapeDtypeStruct((s, w), BF16)
    tap = jax.ShapeDtypeStruct((1, w), F32)
    return _call(body, name="conv_bwd", grid=(n_tiles,),
                 in_specs=[_tile(w), _tile(w, 0), _tile(w, 1), _tile(w, 2), prev(1), prev(2), nxt(0), nxt(0),
                           pl.BlockSpec((3, w), lambda i: (0, 0))],
                 out_specs=[_tile(w)] * 3 + [_vec(w)] * 3, out_shape=[act] * 3 + [tap] * 3,
                 compiler_params=_params("arbitrary"))(dyc, proj, proj, proj, proj, proj, dyc, proj, conv_w)


Q_COL = 3 * CONV_WIDTH // LANES
K_COL = Q_COL + ATTN_WIDTH // LANES
V_COL = K_COL + ATTN_WIDTH // LANES
SCALE = HEAD_DIM ** -0.5


def _head_lanes(hh):
    lane = lax.broadcasted_iota(jnp.int32, (1, LANES), 1)
    return jnp.where((lane >= hh * HEAD_DIM) & (lane < (hh + 1) * HEAD_DIM), 1.0, 0.0)


def _tri(width, keep):
    j = lax.broadcasted_iota(jnp.int32, (TK, width), 0)
    s = lax.broadcasted_iota(jnp.int32, (TK, width), 1)
    return jnp.where((s >= TK) | keep(j, s), 1.0, 0.0).astype(BF16)


def _split_dot(v, tri):
    hi = v.astype(BF16)
    lo = (v - hi.astype(F32)).astype(BF16)
    return _dot(hi, tri) + _dot(lo, tri)


def _logits(qm, kblk, ks, tpos):
    z = _dot(qm, kblk, NT)
    mask = (ks + lax.broadcasted_iota(jnp.int32, (1, TK), 1)) < tpos
    e = jnp.exp(-jnp.abs(z))
    l0 = jnp.where(mask, -(jnp.maximum(z, 0.0) + jnp.log(1.0 + e)), 0.0)
    return z, mask, e, l0


def _neg_softplus(z):
    return jnp.minimum(-z, 0.0) - jnp.log(1.0 + jnp.exp(-jnp.abs(z)))


W2 = 2 * TK
PIPE = 4


def _softplus(z):
    neg_abs = lax.bitcast_convert_type(lax.bitcast_convert_type(z, jnp.uint32) | jnp.uint32(0x80000000), F32)
    return jnp.maximum(z, 0.0) + jnp.log(1.0 + jnp.exp(neg_abs))


def _pair_kv(proj):
    s = proj.shape[0]
    per_tile = TM // TK
    width_blocks = ATTN_WIDTH // LANES

    def body(k_ref, v_ref, kp_ref, vp_ref):
        lane = lax.broadcasted_iota(jnp.int32, (1, ATTN_WIDTH), 1) & (LANES - 1)
        keep = [jnp.where(lane < HEAD_DIM, 1.0, 0.0), jnp.where(lane >= HEAD_DIM, 1.0, 0.0)]
        for src, dst in ((k_ref, kp_ref), (v_ref, vp_ref)):
            for gi in range(per_tile):
                blk = src[gi * TK:(gi + 1) * TK, :]
                for hh in range(2):
                    dst[(2 * gi + hh) * TK:(2 * gi + hh + 1) * TK, :] = (blk * keep[hh]).astype(BF16)

    out = jax.ShapeDtypeStruct((2 * s, ATTN_WIDTH), BF16)
    return _call(body, name="pair_kv", grid=(s // TM,),
                 in_specs=[_tile(ATTN_WIDTH, K_COL // width_blocks), _tile(ATTN_WIDTH, V_COL // width_blocks)],
                 out_specs=[_tile(ATTN_WIDTH, rows=2 * TM)] * 2, out_shape=[out, out],
                 compiler_params=_params("parallel"))(proj, proj)


def _pair_rows(ref, kb):
    return ref[pl.ds(pl.multiple_of(kb * W2, W2), W2), :]


def _pair_tri(keep):
    j = lax.broadcasted_iota(jnp.int32, (W2, 2 * W2), 0)
    s = lax.broadcasted_iota(jnp.int32, (W2, 2 * W2), 1)
    same_head = (j >= TK) == ((s & (W2 - 1)) >= TK)
    return jnp.where(same_head & ((s >= W2) | keep(j & (TK - 1), s & (TK - 1))), 1.0, 0.0).astype(BF16)


def _attn_fwd(proj, kp, vp, shards=()):
    s = proj.shape[0]
    nq = s // TQ
    diag = TQ // TK
    n_ch = TQ // CH
    assert s // TK <= TK and diag % PIPE == 0

    def body(q_ref, k_ref, v_ref, o_ref, rs_ref, qb_scr, tri_scr, z_scr, l0_scr, cs_scr, a_scr, r_scr, rall_scr, acc_scr):
        qi = pl.program_id(1)
        lm = [_head_lanes(0), _head_lanes(1)]
        lane = lax.broadcasted_iota(jnp.int32, (CH, W2), 1) & (TK - 1)
        row = lax.broadcasted_iota(jnp.int32, (CH, 1), 0)
        col = lax.broadcasted_iota(jnp.int32, (1, W2), 1) & (TK - 1)
        qb_scr[...] = (q_ref[...] * SCALE).astype(BF16)
        tri_scr[...] = _pair_tri(lambda j, ss: j > ss)
        r_scr[...] = jnp.zeros_like(r_scr)
        rall_scr[...] = jnp.zeros_like(rall_scr)
        acc_scr[...] = jnp.zeros_like(acc_scr)

        def causal(kb, c):
            return (kb * TK + col) < (qi * TQ + c * CH + row)

        def chunks(r0):
            return range(0 if r0 is None else r0 // CH, n_ch)

        def on_diagonal(r0, c):
            return r0 is not None and c * CH < r0 + TK

        def logits(kb, zb, r0=0):
            z_scr[zb, r0:, :] = _dot(qb_scr[r0:, :], _pair_rows(k_ref, kb), NT)

        def log_one_minus_beta(kb, zb, lb, r0=None):
            for c in chunks(r0):
                rows = slice(c * CH, (c + 1) * CH)
                sp = _softplus(z_scr[zb, rows, :])
                if on_diagonal(r0, c):
                    sp = jnp.where(causal(kb, c), sp, 0.0)
                l0_scr[lb, rows, :] = sp.astype(BF16)

        def sums(lb, r0=0):
            cs_scr[r0:, :] = _dot(l0_scr[lb, r0:, :], tri_scr[...])

        def weights(kb, zb, lb, ab, r0=None):
            for c in chunks(r0):
                rows = slice(c * CH, (c + 1) * CH)
                near = r_scr[rows, :]
                a = jnp.exp(z_scr[zb, rows, :] - l0_scr[lb, rows, :].astype(F32) - cs_scr[rows, :W2] - near)
                if on_diagonal(r0, c):
                    a = jnp.where(causal(kb, c), a, 0.0)
                a_scr[ab, rows, :] = a.astype(BF16)
                rall_scr[rows, :] = jnp.where(lane == kb, near, rall_scr[rows, :])
                r_scr[rows, :] = near + cs_scr[rows, W2:]

        def weighted_values(kb, ab, r0=0):
            acc_scr[r0:, :] += _dot(a_scr[ab, r0:, :], _pair_rows(v_ref, kb))

        n = qi * diag
        for dd in reversed(range(diag)):
            kb, r0 = n + dd, dd * TK
            logits(kb, dd % PIPE, r0)
            log_one_minus_beta(kb, dd % PIPE, dd % 2, r0)
            sums(dd % 2, r0)
            weights(kb, dd % PIPE, dd % 2, dd % 2, r0)
            weighted_values(kb, dd % 2, r0)

        def block(j):
            return jnp.maximum(n - 1 - j, 0)

        a_scr[...] = jnp.zeros_like(a_scr)
        logits(block(0), 0)
        logits(block(1), 1)
        log_one_minus_beta(block(0), 0, 0)

        def trip(m, carry):
            for u in range(PIPE):
                j = PIPE * m + u
                weighted_values(block(j - 1), (u - 1) % 2)
                sums(u % 2)
                logits(block(j + 2), (u + 2) % PIPE)
                log_one_minus_beta(block(j + 1), (u + 1) % PIPE, (u + 1) % 2)
                weights(block(j), u % PIPE, u % 2, u % 2)
            return carry

        lax.fori_loop(0, n // PIPE, trip, 0)
        weighted_values(block(n - 1), (PIPE - 1) % 2)
        rs_ref[...] = rall_scr[...]
        o_ref[...] = acc_scr[...]

    (o, rsave), gathered = _call_hosting(
        body, (proj, kp, vp), _AllToAll([True] * len(shards)) if shards else None, shards,
        name="attn_fwd", grid=(N_HEADS // 2, nq),
        in_specs=[pl.BlockSpec((TQ, LANES), lambda p, qi: (qi, Q_COL + p)),
                  pl.BlockSpec((2 * s, LANES), lambda p, qi: (0, p)),
                  pl.BlockSpec((2 * s, LANES), lambda p, qi: (0, p))],
        out_specs=[pl.BlockSpec((TQ, LANES), lambda p, qi: (qi, p)),
                   pl.BlockSpec((TQ, W2), lambda p, qi: (qi, p))],
        out_shape=[jax.ShapeDtypeStruct((s, ATTN_WIDTH), F32), jax.ShapeDtypeStruct((s, N_HEADS // 2 * W2), F32)],
        scratch_shapes=[pltpu.VMEM((TQ, LANES), BF16), pltpu.VMEM((W2, 2 * W2), BF16),
                        pltpu.VMEM((PIPE, TQ, W2), F32), pltpu.VMEM((2, TQ, W2), BF16),
                        pltpu.VMEM((TQ, 2 * W2), F32), pltpu.VMEM((2, TQ, W2), BF16),
                        pltpu.VMEM((TQ, W2), F32), pltpu.VMEM((TQ, W2), F32), pltpu.VMEM((TQ, LANES), F32)])
    return o, rsave, gathered


def _attn_fwd_rows(proj):
    s = proj.shape[0]
    nq = s // TQ
    diag = TQ // TK
    n_ch = 2 * TQ // CH
    assert s // TK <= LANES

    def body(q_ref, k_ref, v_ref, o_ref, rs_ref, q2_scr, z_scr, l0_scr, cs_scr, a_scr, r_scr, rall_scr, acc_scr):
        qi = pl.program_id(1)
        lm = [_head_lanes(0), _head_lanes(1)]
        tri = _tri(2 * TK, lambda j, ss: j > ss)
        lane = lax.broadcasted_iota(jnp.int32, (CH, LANES), 1)
        row = lax.broadcasted_iota(jnp.int32, (CH, 1), 0)
        col = lax.broadcasted_iota(jnp.int32, (1, TK), 1)
        for hh in range(2):
            q2_scr[hh * TQ:(hh + 1) * TQ, :] = (q_ref[...] * (SCALE * lm[hh])).astype(BF16)
        r_scr[...] = jnp.zeros_like(r_scr)
        rall_scr[...] = jnp.zeros_like(rall_scr)
        acc_scr[...] = jnp.zeros_like(acc_scr)

        def step(kb, masked):
            ks = pl.multiple_of(kb * TK, TK)
            kblk = k_ref[pl.ds(ks, TK), :].astype(BF16)
            vf = v_ref[pl.ds(ks, TK), :]
            v2 = jnp.concatenate([(vf * lm[0]).astype(BF16), (vf * lm[1]).astype(BF16)], axis=0)
            z_scr[...] = _dot(q2_scr[...], kblk, NT)

            def causal(c):
                return (ks + col) < (qi * TQ + (c * CH) % TQ + row)

            for c in range(n_ch):
                rows = slice(c * CH, (c + 1) * CH)
                l0 = _neg_softplus(z_scr[rows, :])
                if masked:
                    l0 = jnp.where(causal(c), l0, 0.0)
                l0_scr[rows, :] = l0.astype(BF16)
            cs_scr[...] = _dot(l0_scr[...], tri)
            for c in range(n_ch):
                rows = slice(c * CH, (c + 1) * CH)
                hh, r0 = (c * CH) // TQ, (c * CH) % TQ
                near = r_scr[rows, :]
                a = jnp.exp(l0_scr[rows, :].astype(F32) + z_scr[rows, :] + cs_scr[rows, :TK] + near)
                if masked:
                    a = jnp.where(causal(c), a, 0.0)
                a_scr[r0:r0 + CH, hh * TK:(hh + 1) * TK] = a.astype(BF16)
                rall_scr[rows, :] = jnp.where(lane == kb, near, rall_scr[rows, :])
                r_scr[rows, :] = near + cs_scr[rows, TK:]
            acc_scr[...] += _dot(a_scr[...], v2)

        def diag_step(it, carry):
            step((qi + 1) * diag - 1 - it, True)
            return carry

        def inner_step(it, carry):
            step(qi * diag - 1 - it, False)
            return carry

        lax.fori_loop(0, diag, diag_step, 0)
        lax.fori_loop(0, qi * diag, inner_step, 0)
        for hh in range(2):
            rs_ref[hh] = rall_scr[hh * TQ:(hh + 1) * TQ, :]
        o_ref[...] = acc_scr[...]

    return _call(
        body, name="attn_fwd", grid=(N_HEADS // 2, nq),
        in_specs=[pl.BlockSpec((TQ, LANES), lambda p, qi: (qi, Q_COL + p)),
                  pl.BlockSpec((s, LANES), lambda p, qi: (0, K_COL + p)),
                  pl.BlockSpec((s, LANES), lambda p, qi: (0, V_COL + p))],
        out_specs=[pl.BlockSpec((TQ, LANES), lambda p, qi: (qi, p)),
                   pl.BlockSpec((2, TQ, LANES), lambda p, qi: (p, qi, 0))],
        out_shape=[jax.ShapeDtypeStruct((s, ATTN_WIDTH), F32), jax.ShapeDtypeStruct((N_HEADS, s, LANES), F32)],
        scratch_shapes=[pltpu.VMEM((2 * TQ, LANES), BF16), pltpu.VMEM((2 * TQ, TK), F32),
                        pltpu.VMEM((2 * TQ, TK), BF16), pltpu.VMEM((2 * TQ, 2 * TK), F32),
                        pltpu.VMEM((TQ, 2 * TK), BF16), pltpu.VMEM((2 * TQ, LANES), F32),
                        pltpu.VMEM((2 * TQ, LANES), F32), pltpu.VMEM((TQ, LANES), F32)],
        compiler_params=_params("parallel", "arbitrary"),
    )(proj, proj, proj)


def _attn_bwd(proj, kp, vp, do, rsave, sent=()):
    s = proj.shape[0]
    nq = s // TQ
    diag = TQ // TK
    n_ch = TQ // CH
    assert diag % PIPE == 0

    def body(q_ref, k_ref, v_ref, do_ref, rs_ref, dq_ref, dk_ref, dv_ref,
             qb_scr, dob_scr, after_scr, before_scr, z_scr, da_scr, l0_scr, beta_scr, cs_scr, a_scr, g_scr, cg_scr,
             dz_scr, pg_scr, dq_scr, dk_scr, dv_scr):
        qi = pl.program_id(1)

        @pl.when(qi == 0)
        def _():
            dk_scr[...] = jnp.zeros_like(dk_scr)
            dv_scr[...] = jnp.zeros_like(dv_scr)

        lm = [_head_lanes(0), _head_lanes(1)]
        lane = lax.broadcasted_iota(jnp.int32, (CH, TK), 1)
        row = lax.broadcasted_iota(jnp.int32, (CH, 1), 0)
        col = lax.broadcasted_iota(jnp.int32, (1, W2), 1) & (TK - 1)
        qb_scr[...] = (q_ref[...] * SCALE).astype(BF16)
        dob_scr[...] = do_ref[...].astype(BF16)
        after_scr[...] = _pair_tri(lambda j, ss: j > ss)[:, :W2]
        before_scr[...] = _pair_tri(lambda j, ss: j < ss)[:, :W2]
        pg_scr[...] = jnp.zeros_like(pg_scr)
        dq_scr[...] = jnp.zeros_like(dq_scr)
        dz_scr[...] = jnp.zeros_like(dz_scr)

        def causal(kb, c):
            return (kb * TK + col) < (qi * TQ + c * CH + row)

        def chunks(r0):
            return range(0 if r0 is None else r0 // CH, n_ch)

        def on_diagonal(r0, c):
            return r0 is not None and c * CH < r0 + TK

        def logits(kb, zb, r0=0):
            z_scr[zb, r0:, :] = _dot(qb_scr[r0:, :], _pair_rows(k_ref, kb), NT)

        def do_dot_v(kb, db, r0=0):
            da_scr[db, r0:, :] = _dot(dob_scr[r0:, :], _pair_rows(v_ref, kb), NT)

        def gates(kb, zb, lb, bb, r0=None):
            for c in chunks(r0):
                rows = slice(c * CH, (c + 1) * CH)
                z = z_scr[zb, rows, :]
                sp = _softplus(z)
                beta_scr[bb, rows, :] = jnp.exp(z - sp)
                if on_diagonal(r0, c):
                    sp = jnp.where(causal(kb, c), sp, 0.0)
                l0_scr[lb, rows, :] = sp.astype(BF16)

        def suffix_sums(lb, r0=0):
            cs_scr[r0:, :] = _dot(l0_scr[lb, r0:, :], after_scr[...])

        def weights(kb, zb, lb, db, ab, r0=None):
            for c in chunks(r0):
                rows = slice(c * CH, (c + 1) * CH)
                keep = (kb * TK + lane) < (qi * TQ + c * CH + row) if on_diagonal(r0, c) else None
                for hh in range(2):
                    cols = slice(hh * TK, (hh + 1) * TK)
                    near = jnp.sum(jnp.where(lane == kb, rs_ref[rows, cols], 0.0), axis=1, keepdims=True)
                    a = jnp.exp(z_scr[zb, rows, cols] - l0_scr[lb, rows, cols].astype(F32) - cs_scr[rows, cols] - near)
                    if keep is not None:
                        a = jnp.where(keep, a, 0.0)
                    a_scr[ab, rows, cols] = a.astype(BF16)
                    g_scr[ab, rows, cols] = (a * da_scr[db, rows, cols]).astype(BF16)

        def prefix_sums(ab, r0=0):
            cg_scr[r0:, :] = _dot(g_scr[ab, r0:, :], before_scr[...])

        def dlogits(kb, ab, bb, zb2, r0=None):
            for c in chunks(r0):
                rows = slice(c * CH, (c + 1) * CH)
                earlier = pg_scr[rows, :]
                beta = beta_scr[bb, rows, :]
                g = g_scr[ab, rows, :].astype(F32)
                dz = g - beta * (g + cg_scr[rows, :] + earlier)
                if on_diagonal(r0, c):
                    dz = jnp.where(causal(kb, c), dz, 0.0)
                dz_scr[zb2, rows, :] = dz.astype(BF16)
                for hh in range(2):
                    cols = slice(hh * TK, (hh + 1) * TK)
                    pg_scr[rows, cols] = earlier[:, cols] + jnp.sum(g[:, cols], axis=1, keepdims=True)

        def fold(t):
            return t[:TK, :] * lm[0] + t[TK:, :] * lm[1]

        def dq_dk(kb, zb2, r0=0):
            dq_scr[r0:, :] += _dot(dz_scr[zb2, r0:, :], _pair_rows(k_ref, kb))
            dk_scr[pl.ds(pl.multiple_of(kb * TK, TK), TK), :] += fold(_dot(dz_scr[zb2, r0:, :], qb_scr[r0:, :], TN))

        def dv(kb, ab, r0=0):
            dv_scr[pl.ds(pl.multiple_of(kb * TK, TK), TK), :] += fold(_dot(a_scr[ab, r0:, :], dob_scr[r0:, :], TN))

        n = qi * diag

        def block(j):
            return jnp.clip(j, 0, jnp.maximum(n - 1, 0))

        logits(block(0), 0)
        logits(block(1), 1)
        logits(block(2), 2)
        do_dot_v(block(0), 0)
        do_dot_v(block(1), 1)
        gates(block(0), 0, 0, 0)
        gates(block(1), 1, 1, 1)
        suffix_sums(0)
        weights(block(0), 0, 0, 0, 0)

        def trip(m, carry):
            for u in range(PIPE):
                t = PIPE * m + u
                dq_dk(block(t - 1), (u - 1) % 2)
                dv(block(t), u % 2)
                prefix_sums(u % 2)
                suffix_sums((u + 1) % 2)
                logits(block(t + 3), (u + 3) % PIPE)
                do_dot_v(block(t + 2), u % 2)
                gates(block(t + 2), (u + 2) % PIPE, u % 2, (u + 2) % PIPE)
                weights(block(t + 1), (u + 1) % PIPE, (u + 1) % 2, (u + 1) % 2, (u + 1) % 2)
                dlogits(block(t), u % 2, u % PIPE, u % 2)
            return carry

        lax.fori_loop(0, n // PIPE, trip, 0)
        dq_dk(block(n - 1), (PIPE - 1) % 2)

        for dd in range(diag):
            kb, r0, two, four = n + dd, dd * TK, dd % 2, dd % PIPE
            logits(kb, four, r0)
            do_dot_v(kb, two, r0)
            gates(kb, four, two, four, r0)
            suffix_sums(two, r0)
            weights(kb, four, two, two, two, r0)
            dv(kb, two, r0)
            prefix_sums(two, r0)
            dlogits(kb, two, four, two, r0)
            dq_dk(kb, two, r0)
        dq_ref[...] = (dq_scr[...] * SCALE).astype(dq_ref.dtype)

        @pl.when(qi == nq - 1)
        def _():
            dk_ref[...] = dk_scr[...].astype(dk_ref.dtype)
            dv_ref[...] = dv_scr[...].astype(dv_ref.dtype)

    def rows(c0):
        return pl.BlockSpec((TQ, LANES), lambda p, qi: (qi, c0 + p))

    def whole(c0):
        return pl.BlockSpec((s, LANES), lambda p, qi: (0, c0 + p))

    def f32(*shape):
        return pltpu.VMEM(shape, F32)

    def bf16(*shape):
        return pltpu.VMEM(shape, BF16)

    pairs = pl.BlockSpec((2 * s, LANES), lambda p, qi: (0, p))
    out = jax.ShapeDtypeStruct((s, ATTN_WIDTH), BF16)
    (dq, dk, dv), parts = _call_hosting(
        body, (proj, kp, vp, do, rsave), _AllToAll([False] * len(sent)) if sent else None, sent,
        name="attn_bwd", grid=(N_HEADS // 2, nq),
        in_specs=[rows(Q_COL), pairs, pairs, rows(0), pl.BlockSpec((TQ, W2), lambda p, qi: (qi, p))],
        out_specs=[rows(0), whole(0), whole(0)], out_shape=[out] * 3,
        scratch_shapes=[bf16(TQ, LANES), bf16(TQ, LANES), bf16(W2, W2), bf16(W2, W2),
                        f32(PIPE, TQ, W2), f32(2, TQ, W2), bf16(2, TQ, W2), f32(PIPE, TQ, W2), f32(TQ, W2),
                        bf16(2, TQ, W2), bf16(2, TQ, W2), f32(TQ, W2), bf16(2, TQ, W2),
                        f32(TQ, W2), f32(TQ, LANES), f32(s, LANES), f32(s, LANES)])
    return dq, dk, dv, parts


def _attn_bwd_rows(proj, do, rsave):
    s = proj.shape[0]
    nq = s // TQ

    diag = TQ // TK
    n_ch = 2 * TQ // CH

    def body(q_ref, k_ref, v_ref, do_ref, rs_ref, dq_ref, dk_ref, dv_ref,
             q2_scr, do2_scr, z_scr, da_scr, l0_scr, beta_scr, cs_scr, a_scr, g_scr, cg_scr, dz_scr,
             pg_scr, dq_scr, dk_scr, dv_scr):
        qi = pl.program_id(1)

        @pl.when(qi == 0)
        def _():
            dk_scr[...] = jnp.zeros_like(dk_scr)
            dv_scr[...] = jnp.zeros_like(dv_scr)

        lm = [_head_lanes(0), _head_lanes(1)]
        tri_after = _tri(TK, lambda j, ss: j > ss)
        tri_before = _tri(2 * TK, lambda j, ss: j < ss)
        lane = lax.broadcasted_iota(jnp.int32, (CH, LANES), 1)
        row = lax.broadcasted_iota(jnp.int32, (CH, 1), 0)
        col = lax.broadcasted_iota(jnp.int32, (1, TK), 1)
        for hh in range(2):
            q2_scr[hh * TQ:(hh + 1) * TQ, :] = (q_ref[...] * (SCALE * lm[hh])).astype(BF16)
            do2_scr[hh * TQ:(hh + 1) * TQ, :] = (do_ref[...] * lm[hh]).astype(BF16)
        pg_scr[...] = jnp.zeros_like(pg_scr)
        dq_scr[...] = jnp.zeros_like(dq_scr)

        def step(kb, masked):
            ks = pl.multiple_of(kb * TK, TK)
            kblk = k_ref[pl.ds(ks, TK), :].astype(BF16)
            vblk = v_ref[pl.ds(ks, TK), :].astype(BF16)
            z_scr[...] = _dot(q2_scr[...], kblk, NT)
            da_scr[...] = _dot(do2_scr[...], vblk, NT)

            def causal(c):
                return (ks + col) < (qi * TQ + (c * CH) % TQ + row)

            for c in range(n_ch):
                rows = slice(c * CH, (c + 1) * CH)
                z = z_scr[rows, :]
                e = jnp.exp(-jnp.abs(z))
                w = 1.0 + e
                l0 = jnp.minimum(-z, 0.0) - jnp.log(w)
                if masked:
                    l0 = jnp.where(causal(c), l0, 0.0)
                l0_scr[rows, :] = l0.astype(BF16)
                rinv = 1.0 / w
                beta_scr[rows, :] = jnp.where(z >= 0.0, rinv, e * rinv)
            cs_scr[...] = _dot(l0_scr[...], tri_after)
            for c in range(n_ch):
                rows = slice(c * CH, (c + 1) * CH)
                hh, r0 = (c * CH) // TQ, (c * CH) % TQ
                near = jnp.sum(jnp.where(lane == kb, rs_ref[hh, r0:r0 + CH, :], 0.0), axis=1, keepdims=True)
                a = jnp.exp(l0_scr[rows, :].astype(F32) + z_scr[rows, :] + cs_scr[rows, :] + near)
                if masked:
                    a = jnp.where(causal(c), a, 0.0)
                a_scr[rows, :] = a.astype(BF16)
                g_scr[rows, :] = (a * da_scr[rows, :]).astype(BF16)
            cg_scr[...] = _dot(g_scr[...], tri_before)
            for c in range(n_ch):
                rows = slice(c * CH, (c + 1) * CH)
                before = cg_scr[rows, :TK] + pg_scr[rows, :]
                beta = beta_scr[rows, :]
                dz = g_scr[rows, :].astype(F32) * (1.0 - beta) - beta * before
                if masked:
                    dz = jnp.where(causal(c), dz, 0.0)
                dz_scr[rows, :] = dz.astype(BF16)
                pg_scr[rows, :] += cg_scr[rows, TK:]
            dq_scr[...] += _dot(dz_scr[...], kblk)
            dk_scr[pl.ds(ks, TK), :] += _dot(dz_scr[...], q2_scr[...], TN)
            dv_scr[pl.ds(ks, TK), :] += _dot(a_scr[...], do2_scr[...], TN)

        def inner_step(kb, carry):
            step(kb, False)
            return carry

        def diag_step(it, carry):
            step(qi * diag + it, True)
            return carry

        lax.fori_loop(0, qi * diag, inner_step, 0)
        lax.fori_loop(0, diag, diag_step, 0)
        dq = dq_scr[:TQ, :] * lm[0] + dq_scr[TQ:, :] * lm[1]
        dq_ref[...] = (dq * SCALE).astype(dq_ref.dtype)

        @pl.when(qi == nq - 1)
        def _():
            dk_ref[...] = dk_scr[...].astype(dk_ref.dtype)
            dv_ref[...] = dv_scr[...].astype(dv_ref.dtype)

    stacked_f32 = pltpu.VMEM((2 * TQ, TK), F32)
    stacked_bf16 = pltpu.VMEM((2 * TQ, TK), BF16)

    def rows(c0):
        return pl.BlockSpec((TQ, LANES), lambda p, qi: (qi, c0 + p))

    def whole(c0):
        return pl.BlockSpec((s, LANES), lambda p, qi: (0, c0 + p))

    out = jax.ShapeDtypeStruct((s, ATTN_WIDTH), BF16)
    return _call(
        body, name="attn_bwd", grid=(N_HEADS // 2, nq),
        in_specs=[rows(Q_COL), whole(K_COL), whole(V_COL), rows(0),
                  pl.BlockSpec((2, TQ, LANES), lambda p, qi: (p, qi, 0))],
        out_specs=[rows(0), whole(0), whole(0)], out_shape=[out] * 3,
        scratch_shapes=[stacked_bf16, stacked_bf16, stacked_f32, stacked_f32, stacked_bf16, stacked_f32, stacked_f32,
                        stacked_bf16, stacked_bf16, pltpu.VMEM((2 * TQ, 2 * TK), F32), stacked_bf16,
                        stacked_f32, stacked_f32, pltpu.VMEM((s, LANES), F32), pltpu.VMEM((s, LANES), F32)],
        compiler_params=_params("arbitrary", "arbitrary"),
    )(proj, proj, proj, do, rsave)


def _sum_adamw(name, parts, w, m, v, layer=None, into=None):
    n, r, c = parts.shape
    tr = r if r <= 256 else 256

    def body(p_ref, w_ref, m_ref, v_ref, g_ref, d_ref, nm_ref, nv_ref):
        g = p_ref[0].astype(F32)
        for j in range(1, n):
            g = g + p_ref[j].astype(F32)
        nm = ADAM_B1 * m_ref[...] + (1.0 - ADAM_B1) * g
        nv = ADAM_B2 * v_ref[...] + (1.0 - ADAM_B2) * (g * g)
        m_hat = nm / (1.0 - ADAM_B1 ** ADAM_STEP)
        v_hat = nv / (1.0 - ADAM_B2 ** ADAM_STEP)
        g_ref[...] = g
        d_ref[...] = -ADAM_LR * (m_hat / (jnp.sqrt(v_hat) + ADAM_EPS) + ADAM_WD * w_ref[...])
        nm_ref[...] = nm
        nv_ref[...] = nv

    if layer is None:
        mat = pl.BlockSpec((tr, c), lambda i: (i, 0))
        out = jax.ShapeDtypeStruct((r, c), F32)
    else:
        mat = pl.BlockSpec((None, tr, c), lambda i: (layer, i, 0))
        out = jax.ShapeDtypeStruct((DEPTH, r, c), F32)
    earlier = () if into is None else tuple(into)
    return _call(body if into is None else lambda *refs: body(*refs[:4], *refs[8:]),
                 name=name, grid=(r // tr,),
                 in_specs=[pl.BlockSpec((n, tr, c), lambda i: (0, i, 0)), mat, mat, mat]
                 + [pl.BlockSpec(memory_space=pl.ANY)] * len(earlier),
                 out_specs=[mat] * 4, out_shape=[out] * 4,
                 input_output_aliases={4 + k: k for k in range(len(earlier))},
                 compiler_params=_params("parallel"))(parts, w, m, v, *earlier)


def _natural(gathered):
    _, k, n = gathered.shape
    return gathered.transpose(1, 0, 2).reshape(k, N_DEV * n)


def _relu2_epi(acc):
    r = jnp.maximum(acc, 0.0)
    return acc, r * r


def _relu2_bwd_epi(acc, a_act):
    return (acc * (2.0 * jnp.maximum(a_act, 0.0)),)


def kernel(x, c, w_ada, b_ada, g_pre_mix, g_post_mix, g_pre_mlp, g_post_mlp, w_in, conv_w, w_proj_conv, w_proj_attn, w_out, w_mlp_in, w_mlp_out, loss_target, m_w_ada, m_b_ada, m_g_pre_mix, m_g_post_mix, m_g_pre_mlp, m_g_post_mlp, m_w_in, m_conv_w, m_w_proj_conv, m_w_proj_attn, m_w_out, m_w_mlp_in, m_w_mlp_out, v_w_ada, v_b_ada, v_g_pre_mix, v_g_post_mix, v_g_pre_mlp, v_g_post_mlp, v_w_in, v_conv_w, v_w_proj_conv, v_w_proj_attn, v_w_out, v_w_mlp_in, v_w_mlp_out):
    xi, yi, ci = _mesh_pos()
    me = 4 * xi + 2 * yi + ci
    d = D_MODEL
    x0 = x[0]
    seq = x0.shape[0]
    ada_cols = w_ada.shape[2]
    conv_cols = conv_w.shape[2]

    small = jnp.concatenate([c.reshape(-1), conv_w.reshape(-1)])
    small = jnp.pad(small, (0, 2 * d - small.shape[0])).reshape(8, 2 * d // 8)
    small_all = _all_gather("gather_c", [small])[0].reshape(N_DEV, 2 * d)
    c_all = small_all[:, :d]
    conv_all = small_all[:, d:d + DEPTH * 3 * conv_cols].reshape(N_DEV, DEPTH, 3, conv_cols)
    conv_all = conv_all.transpose(1, 2, 0, 3).reshape(DEPTH, 3, N_DEV * conv_cols)
    mod_cols = jnp.stack([_mm("mod_mm", c_all, w_ada[l], "nn", N_DEV, ada_cols, d, [F32], exact=True)
                          for l in range(DEPTH)], axis=1)
    mod_all = _all_gather("gather_mod", [mod_cols.reshape(N_DEV, DEPTH * ada_cols)])[0]
    mod_mine = lax.dynamic_index_in_dim(mod_all, me, axis=1, keepdims=False).reshape(N_DEV, DEPTH, ada_cols)
    mod = mod_mine.transpose(1, 0, 2).reshape(DEPTH, N_MOD * d) + b_ada

    sharded = {"w_in": w_in, "w_proj_conv": w_proj_conv, "w_proj_attn": w_proj_attn, "w_out": w_out,
               "w_mlp_in": w_mlp_in, "w_mlp_out": w_mlp_out}
    before_attention = ["w_in", "w_proj_conv"]

    def shard(key):
        nm, l = key
        return sharded[nm][l].astype(BF16)

    def natural(key, gathered):
        return gathered.reshape(-1, d) if key[0] in ("w_out", "w_mlp_out") else _natural(gathered)

    first = [(nm, 0) for nm in before_attention]
    full = {key: natural(key, g) for key, g in zip(first, _all_gather("gather_w", [shard(key) for key in first]))}
    saved = []
    xl = x0
    for l in range(DEPTH):
        riders = [(nm, l) for nm in sharded if (nm, l) not in full]
        if l + 1 < DEPTH:
            riders += [(nm, l + 1) for nm in sharded]
        sh1, sc1, gt1, sh2, sc2, gt2 = [mod[l:l + 1, i * d:(i + 1) * d] for i in range(N_MOD)]
        h = _prenorm_fwd(xl, g_pre_mix[l:l + 1], sc1, sh1)
        proj = _mm("proj", h, full[("w_in", l)], "nn", TMM, 1024, d, [F32])
        yc = _conv_fwd(proj, conv_all[l])
        y_conv = _mm("proj_conv", yc, full[("w_proj_conv", l)], "nn", TMM, d, CONV_WIDTH, [F32])
        kp, vp = _pair_kv(proj)
        o, rsave, gathered = _attn_fwd(proj, kp, vp, [shard(key) for key in riders])
        full.update({key: natural(key, g) for key, g in zip(riders, gathered)})
        wg_in, wg_pc, wg_pa, wg_out, wg_mi, wg_mo = [full[(nm, l)] for nm in sharded]
        y_attn = _mm("proj_attn", o, wg_pa, "nn", TMM,d, ATTN_WIDTH, [F32])
        merged = _gate_fwd(proj, y_conv, y_attn)
        mix_out = _mm("mix_out", merged, wg_out, "nn", TMM,d, d, [F32])
        x1 = _postnorm_fwd(xl, mix_out, g_post_mix[l:l + 1], gt1)
        h2 = _prenorm_fwd(x1, g_pre_mlp[l:l + 1], sc2, sh2)
        a_act, r = _mm("mlp_in", h2, wg_mi, "nn", TMM,1024, d, [F32, BF16], epi=_relu2_epi)
        ff = _mm("mlp_out", r, wg_mo, "nn", TMM,d, 1024, [F32])
        x2 = _postnorm_fwd(x1, ff, g_post_mlp[l:l + 1], gt2)
        saved.append((xl, h, proj, yc, kp, vp, o, rsave, y_conv, y_attn, merged, mix_out, x1, h2, a_act, r, ff))
        xl = x2

    dxo, sq = _loss(xl, loss_target[0])
    loss = lax.psum(sq[0, 0] * (0.5 / d), ("x", "y", "c"))

    olds = {"w_in": (w_in, m_w_in, v_w_in), "w_proj_conv": (w_proj_conv, m_w_proj_conv, v_w_proj_conv),
            "w_proj_attn": (w_proj_attn, m_w_proj_attn, v_w_proj_attn), "w_out": (w_out, m_w_out, v_w_out),
            "w_mlp_in": (w_mlp_in, m_w_mlp_in, v_w_mlp_in), "w_mlp_out": (w_mlp_out, m_w_mlp_out, v_w_mlp_out)}
    big = {}
    pending = []

    def col_blocks(gw):
        k, n = gw.shape
        return gw.reshape(k, N_DEV, n // N_DEV).transpose(1, 0, 2)

    def update(entries, parts):
        for (nm, ll, _), part in zip(entries, parts):
            w_, m_, v_ = olds[nm]
            big[nm] = _sum_adamw("adamw_" + nm, part, w_, m_, v_, layer=ll, into=big.get(nm))

    dmod, small_grads = [None] * DEPTH, [None] * DEPTH
    for l in reversed(range(DEPTH)):
        wg_in, wg_pc, wg_pa, wg_out, wg_mi, wg_mo = [full[(nm, l)] for nm in sharded]
        xin, h, proj, yc, kp, vp, o, rsave, y_conv, y_attn, merged, mix_out, x1, h2, a_act, r, ff = saved[l]
        sh1, sc1, gt1, sh2, sc2, gt2 = [mod[l:l + 1, i * d:(i + 1) * d] for i in range(N_MOD)]

        dff, dgt2, dg_post_mlp = _postnorm_bwd(dxo, ff, g_post_mlp[l:l + 1], gt2)
        da = _mm("d_relu2", dff, wg_mo, "nt", TMM,1024, d, [BF16], epi=_relu2_bwd_epi, extra=(a_act,))
        gw_mo = _mm("gw_mlp_out", r, dff, "tn", 1024, d, 1024, [BF16])
        dh2 = _mm("d_h2", da, wg_mi, "nt", TMM,d, 1024, [F32])
        gw_mi = _mm("gw_mlp_in", h2, da, "tn", d, 1024, 1024, [BF16])
        dx1, dsh2, dsc2, dg_pre_mlp = _prenorm_bwd(dh2, x1, g_pre_mlp[l:l + 1], sc2, dxo)

        dmix, dgt1, dg_post_mix = _postnorm_bwd(dx1, mix_out, g_post_mix[l:l + 1], gt1)
        dmerged = _mm("d_merged", dmix, wg_out, "nt", TMM,d, d, [F32])
        gw_out = _mm("gw_out", merged, dmix, "tn", d, d, 1024, [BF16])
        dy_conv, dy_attn, dga, dgb = _gate_bwd(dmerged, proj, y_conv, y_attn)
        do = _mm("d_o", dy_attn, wg_pa, "nt", TMM,ATTN_WIDTH, d, [F32])
        gw_pa = _mm("gw_proj_attn", o, dy_attn, "tn", ATTN_WIDTH, d, 1024, [BF16])
        dyc = _mm("d_yc", dy_conv, wg_pc, "nt", TMM,CONV_WIDTH, d, [F32])
        gw_pc = _mm("gw_proj_conv", yc, dy_conv, "tn", CONV_WIDTH, d, 1024, [BF16])
        pending += [("w_mlp_out", l, gw_mo.reshape(N_DEV, D_FF // N_DEV, d)), ("w_mlp_in", l, col_blocks(gw_mi)),
                    ("w_out", l, gw_out.reshape(N_DEV, d // N_DEV, d)), ("w_proj_attn", l, col_blocks(gw_pa)),
                    ("w_proj_conv", l, col_blocks(gw_pc))]
        if l == 0:
            dq, dk, dv, parts = _attn_bwd(proj, kp, vp, do, rsave, [blocks for _, _, blocks in pending])
            update(pending, parts)
            pending = []
        else:
            dq, dk, dv, _ = _attn_bwd(proj, kp, vp, do, rsave)
        dbg, dcg, du, dw0, dw1, dw2 = _conv_bwd(dyc, proj, conv_all[l])
        dproj = jnp.concatenate([dbg, dcg, du, dq, dk, dv, dga, dgb], axis=1)
        gw_in = _mm("gw_in", h, dproj, "tn", d, 1024, 1024, [BF16])
        pending.append(("w_in", l, col_blocks(gw_in)))
        if l == 0:
            dh, parts = _mm("d_h", dproj, wg_in, "nt", TMM, d, 1024, [F32], sent=[blocks for _, _, blocks in pending])
            update(pending, parts)
            pending = []
        else:
            dh = _mm("d_h", dproj, wg_in, "nt", TMM, d, 1024, [F32])
        dxo, dsh1, dsc1, dg_pre_mix = _prenorm_bwd(dh, xin, g_pre_mix[l:l + 1], sc1, dx1)

        dmod[l] = jnp.concatenate([dsh1, dsc1, dgt1, dsh2, dsc2, dgt2], axis=1)
        small_grads[l] = (dg_pre_mix, dg_post_mix, dg_pre_mlp, dg_post_mlp, jnp.concatenate([dw0, dw1, dw2], axis=0))
    assert not pending

    vec = jnp.concatenate(
        [dmod[l].reshape(-1) for l in range(DEPTH)]
        + [small_grads[l][i].reshape(-1) for i in range(4) for l in range(DEPTH)]
        + [small_grads[l][4].reshape(-1) for l in range(DEPTH)])
    n_vec = vec.shape[0]
    vec_all = _all_gather("gather_small", [vec.reshape(8, n_vec // 8)])[0].reshape(N_DEV, n_vec)
    n_mod = DEPTH * N_MOD * d
    dmod_all = vec_all[:, :n_mod].reshape(N_DEV, DEPTH, N_MOD * d)
    res = {}
    res["b_ada"] = _sum_adamw("adamw_b_ada", dmod_all, b_ada, m_b_ada, v_b_ada)
    off = n_mod
    for nm, (w_, m_, v_) in zip(
            ["g_pre_mix", "g_post_mix", "g_pre_mlp", "g_post_mlp"],
            [(g_pre_mix, m_g_pre_mix, v_g_pre_mix), (g_post_mix, m_g_post_mix, v_g_post_mix),
             (g_pre_mlp, m_g_pre_mlp, v_g_pre_mlp), (g_post_mlp, m_g_post_mlp, v_g_post_mlp)]):
        res[nm] = _sum_adamw("adamw_gain", vec_all[:, off:off + DEPTH * d].reshape(N_DEV, DEPTH, d), w_, m_, v_)
        off += DEPTH * d
    dconv_all = vec_all[:, off:].reshape(N_DEV, DEPTH * 3, CONV_WIDTH)
    dconv_mine = lax.dynamic_slice_in_dim(dconv_all, me * conv_cols, conv_cols, axis=2)
    res["conv_w"] = [t.reshape(DEPTH, 3, conv_cols) for t in _sum_adamw(
        "adamw_conv_w", dconv_mine, conv_w.reshape(DEPTH * 3, conv_cols), m_conv_w.reshape(DEPTH * 3, conv_cols),
        v_conv_w.reshape(DEPTH * 3, conv_cols))]

    c_t = jnp.pad(c_all.T, ((0, 0), (0, LANES - N_DEV)))
    dmod_mine = lax.dynamic_slice_in_dim(dmod_all, me * ada_cols, ada_cols, axis=2)
    for l in range(DEPTH):
        dm_l = jnp.pad(dmod_mine[:, l, :], ((0, LANES - N_DEV), (0, 0)))
        gw_ada = _mm("gw_ada", c_t, dm_l, "nn", 256, ada_cols, LANES, [F32], exact=True)
        res["w_ada"] = _sum_adamw("adamw_w_ada", gw_ada[None], w_ada, m_w_ada, v_w_ada, layer=l, into=res.get("w_ada"))
    res.update(big)

    order = ["w_ada", "b_ada", "g_pre_mix", "g_post_mix", "g_pre_mlp", "g_post_mlp", "w_in", "conv_w",
             "w_proj_conv", "w_proj_attn", "w_out", "w_mlp_in", "w_mlp_out"]
    outs = [loss, dxo[None]]
    for i in range(4):
        outs += [res[nm][i] for nm in order]
    return tuple(outs)
```

```python
import jax
import jax.numpy as jnp
from jax import lax
from jax.experimental import pallas as pl
from jax.experimental.pallas import tpu as pltpu

F32 = jnp.float32
BF16 = jnp.bfloat16
MESH = pl.DeviceIdType.MESH

N_DEV = 8
D_MODEL = 1024
CONV_WIDTH = 512
N_HEADS = 8
HEAD_DIM = 64
ATTN_WIDTH = N_HEADS * HEAD_DIM
D_FF = 4 * D_MODEL
N_MOD = 6
DEPTH = 2
EPS = 1e-6
IN_COLS = 3 * CONV_WIDTH + 3 * ATTN_WIDTH + 2 * D_MODEL
LANES = 128

ADAM_LR = 0.001
ADAM_B1 = 0.9
ADAM_B2 = 0.999
ADAM_EPS = 1e-08
ADAM_WD = 0.01
ADAM_STEP = 10

TM = 512
TMM = 2048
TMK = 1024
TSK = 2048
TQ = 512
TK = 128
CH = 64
VMEM_LIMIT = 56 * 1024 * 1024

NN = (((1,), (0,)), ((), ()))
NT = (((1,), (1,)), ((), ()))
TN = (((0,), (0,)), ((), ()))
_DIMS = {"nn": NN, "nt": NT, "tn": TN}


def _call(body, **kw):
    return pl.pallas_call(body, **kw)


def _params(*sem):
    return pltpu.CompilerParams(dimension_semantics=sem, vmem_limit_bytes=VMEM_LIMIT)


def _dot(a, b, dims=NN):
    return lax.dot_general(a, b, dims, preferred_element_type=F32)


def _mesh_pos():
    return lax.axis_index("x"), lax.axis_index("y"), lax.axis_index("c")


def _all_gather(name, arrs):
    n = len(arrs)

    def body(*refs):
        ins, outs = refs[:n], refs[n:2 * n]
        send_sems, recv_sems, local_sems = refs[2 * n:]
        x, y, c = _mesh_pos()
        me, sibling = (x, y, c), (x, y, 1 - c)
        chips = [(1 - x, y), (x, 1 - y), (1 - x, 1 - y)]

        def blk(t, p):
            return outs[t].at[4 * p[0] + 2 * p[1] + p[2]]

        def copy(t, k, block, to, src=None):
            return pltpu.make_async_remote_copy(
                src_ref=blk(t, block) if src is None else src, dst_ref=blk(t, block),
                send_sem=send_sems.at[7 * t + k], recv_sem=recv_sems.at[7 * t + k],
                device_id=to, device_id_type=MESH)

        mine, first, passed = [], [], []
        for t in range(n):
            cp = pltpu.make_async_copy(ins[t], blk(t, me), local_sems.at[t])
            cp.start()
            mine.append(cp)
            cps = [copy(t, 0, me, sibling, src=ins[t])]
            cps += [copy(t, 1 + j, me, (*chip, c), src=ins[t]) for j, chip in enumerate(chips)]
            for cp in cps:
                cp.start()
            first += cps
        for t in range(n):
            for j, chip in enumerate(chips):
                copy(t, 1 + j, (*chip, c), me).wait_recv()
                cp = copy(t, 4 + j, (*chip, c), sibling)
                cp.start()
                passed.append(cp)
        for t in range(n):
            copy(t, 0, sibling, me).wait_recv()
            for j, chip in enumerate(chips):
                copy(t, 4 + j, (*chip, 1 - c), me).wait_recv()
        for cp in first + passed:
            cp.wait_send()
        for cp in mine:
            cp.wait()

    any_spec = pl.BlockSpec(memory_space=pl.ANY)
    return _call(
        body, name=name,
        out_shape=[jax.ShapeDtypeStruct((N_DEV,) + a.shape, a.dtype) for a in arrs],
        in_specs=[any_spec] * n, out_specs=[any_spec] * n,
        scratch_shapes=[pltpu.SemaphoreType.DMA((7 * n,)), pltpu.SemaphoreType.DMA((7 * n,)),
                        pltpu.SemaphoreType.DMA((n,))],
    )(*arrs)


class _AllToAll:
    def __init__(self, whole):
        self.whole = list(whole)
        self.n = len(self.whole)

    def sem_shapes(self):
        return [pltpu.SemaphoreType.DMA((7 * self.n,)), pltpu.SemaphoreType.DMA((7 * self.n,)),
                pltpu.SemaphoreType.DMA((self.n,))]

    def out_shapes(self, arrs):
        return [jax.ShapeDtypeStruct(((N_DEV,) + a.shape) if w else a.shape, a.dtype) for a, w in zip(arrs, self.whole)]

    def _copies(self, ins, outs, sems):
        send_sems, recv_sems, local_sems = sems
        x, y, c = _mesh_pos()
        my_idx = 4 * x + 2 * y + c
        mine, sends, recvs = [], [], []
        for t in range(self.n):
            def src(idx):
                return ins[t] if self.whole[t] else ins[t].at[idx]
            mine.append(pltpu.make_async_copy(src(my_idx), outs[t].at[my_idx], local_sems.at[t]))
            for k in range(1, N_DEV):
                p = (1 - x if k & 4 else x, 1 - y if k & 2 else y, 1 - c if k & 1 else c)
                p_idx = 4 * p[0] + 2 * p[1] + p[2]
                for dst_idx, group in ((my_idx, sends), (p_idx, recvs)):
                    group.append(pltpu.make_async_remote_copy(
                        src_ref=src(p_idx), dst_ref=outs[t].at[dst_idx],
                        send_sem=send_sems.at[7 * t + k - 1], recv_sem=recv_sems.at[7 * t + k - 1],
                        device_id=p, device_id_type=MESH))
        return mine, sends, recvs

    def start(self, ins, outs, sems):
        mine, sends, _ = self._copies(ins, outs, sems)
        for cp in mine + sends:
            cp.start()

    def finish(self, ins, outs, sems):
        mine, sends, recvs = self._copies(ins, outs, sems)
        for cp in recvs:
            cp.wait_recv()
        for cp in sends:
            cp.wait_send()
        for cp in mine:
            cp.wait()


def _call_hosting(body, args, comm, comm_args, *, name, grid, in_specs, out_specs, out_shape, scratch_shapes):
    if comm is None:
        return _call(body, name=name, grid=grid, in_specs=in_specs, out_specs=out_specs, out_shape=out_shape,
                     scratch_shapes=scratch_shapes, compiler_params=_params(*["arbitrary"] * len(grid)))(*args), ()
    n, n_in, n_out, n_scr = comm.n, len(in_specs), len(out_specs), len(scratch_shapes)

    def hosted(*refs):
        ins, refs = refs[:n_in], refs[n_in:]
        c_ins, refs = refs[:n], refs[n:]
        outs, refs = refs[:n_out], refs[n_out:]
        c_outs, refs = refs[:n], refs[n:]
        scratch, sems = refs[:n_scr], refs[n_scr:]
        step = [pl.program_id(i) for i in range(len(grid))]

        def at(ends):
            hit = step[0] == ends[0]
            for sidx, e in zip(step[1:], ends[1:]):
                hit = jnp.logical_and(hit, sidx == e)
            return hit

        @pl.when(at([0] * len(grid)))
        def _():
            comm.start(c_ins, c_outs, sems)

        body(*ins, *outs, *scratch)

        @pl.when(at([g - 1 for g in grid]))
        def _():
            comm.finish(c_ins, c_outs, sems)

    any_spec = pl.BlockSpec(memory_space=pl.ANY)
    res = _call(hosted, name=name + "_hosting", grid=grid, in_specs=list(in_specs) + [any_spec] * n,
                out_specs=list(out_specs) + [any_spec] * n, out_shape=list(out_shape) + comm.out_shapes(comm_args),
                scratch_shapes=list(scratch_shapes) + comm.sem_shapes(),
                compiler_params=_params(*["arbitrary"] * len(grid)))(*args, *comm_args)
    return res[:n_out], res[n_out:]


def _mm(name, a, b, mode, tm, tn, tk, out_dtypes, epi=None, extra=(), exact=False, sent=()):
    if mode == "nn":
        (m, k), n = a.shape, b.shape[1]
    elif mode == "nt":
        (m, k), n = a.shape, b.shape[0]
    else:
        (k, m), n = a.shape, b.shape[1]
    tm, tn, tk = min(tm, m), min(tn, n), min(tk, k)
    nk = k // tk
    grid = (m // tm, n // tn, nk)
    n_extra, n_out = len(extra), len(out_dtypes)

    def body(*refs):
        a_ref, b_ref = refs[0], refs[1]
        extra_refs = refs[2:2 + n_extra]
        out_refs = refs[2 + n_extra:2 + n_extra + n_out]
        if exact:
            p = lax.dot_general(a_ref[...], b_ref[...], _DIMS[mode], preferred_element_type=F32,
                                precision=lax.Precision.HIGHEST)
        else:
            p = _dot(a_ref[...].astype(BF16), b_ref[...].astype(BF16), _DIMS[mode])

        def finish(acc):
            outs = (acc,) if epi is None else epi(acc, *[r[...] for r in extra_refs])
            for r, o in zip(out_refs, outs):
                r[...] = o.astype(r.dtype)

        if nk == 1:
            finish(p)
        else:
            acc_ref = refs[-1]
            kk = pl.program_id(2)

            @pl.when(kk == 0)
            def _():
                acc_ref[...] = p

            @pl.when(kk > 0)
            def _():
                acc_ref[...] += p

            @pl.when(kk == nk - 1)
            def _():
                finish(acc_ref[...])

    if mode == "tn":
        a_spec = pl.BlockSpec((tk, tm), lambda i, j, kk: (kk, i))
    else:
        a_spec = pl.BlockSpec((tm, tk), lambda i, j, kk: (i, kk))
    if mode == "nt":
        b_spec = pl.BlockSpec((tn, tk), lambda i, j, kk: (j, kk))
    else:
        b_spec = pl.BlockSpec((tk, tn), lambda i, j, kk: (kk, j))
    tile = pl.BlockSpec((tm, tn), lambda i, j, kk: (i, j))
    o_shape, o_spec = (m, n), tile
    out, parts = _call_hosting(
        body, (a, b, *extra), _AllToAll([False] * len(sent)) if sent else None, sent,
        name=name, grid=grid,
        in_specs=[a_spec, b_spec] + [tile] * n_extra,
        out_specs=[o_spec] * n_out,
        out_shape=[jax.ShapeDtypeStruct(o_shape, dt) for dt in out_dtypes],
        scratch_shapes=[pltpu.VMEM((tm, tn), F32)] if nk > 1 else [])
    out = out[0] if n_out == 1 else out
    return (out, parts) if sent else out


def _tile(width, col=0, rows=TM):
    return pl.BlockSpec((rows, width), lambda i: (i, col))


def _vec(width):
    return pl.BlockSpec((1, width), lambda i: (0, 0))


def _rstd(xf):
    return lax.rsqrt(jnp.mean(xf * xf, axis=-1, keepdims=True) + EPS)


def _colsum(v):
    return jnp.sum(v, axis=0, keepdims=True)


def _accumulate(refs, vals):
    first = pl.program_id(0) == 0

    @pl.when(first)
    def _():
        for r, v in zip(refs, vals):
            r[...] = v

    @pl.when(jnp.logical_not(first))
    def _():
        for r, v in zip(refs, vals):
            r[...] += v


def _prenorm_fwd(x, g, sc, sh):
    s, d = x.shape

    def body(x_ref, g_ref, sc_ref, sh_ref, h_ref):
        xf = x_ref[...]
        y = (xf * _rstd(xf)) * g_ref[...]
        h_ref[...] = (y * (1.0 + sc_ref[...]) + sh_ref[...]).astype(h_ref.dtype)

    return _call(body, name="prenorm_fwd", grid=(s // TM,),
                 in_specs=[_tile(d), _vec(d), _vec(d), _vec(d)], out_specs=_tile(d),
                 out_shape=jax.ShapeDtypeStruct((s, d), BF16), compiler_params=_params("parallel"))(x, g, sc, sh)


def _prenorm_bwd(dh, x, g, sc, dres):
    s, d = x.shape

    def body(dh_ref, x_ref, g_ref, sc_ref, dres_ref, dx_ref, dsh_ref, dsc_ref, dg_ref):
        xf, dhf = x_ref[...], dh_ref[...]
        rstd = _rstd(xf)
        xhat = xf * rstd
        one_sc = 1.0 + sc_ref[...]
        dxhat = dhf * (g_ref[...] * one_sc)
        dx_ref[...] = dres_ref[...] + rstd * (dxhat - xhat * jnp.mean(dxhat * xhat, axis=-1, keepdims=True))
        dhx = dhf * xhat
        _accumulate((dsh_ref, dsc_ref, dg_ref), (_colsum(dhf), _colsum(dhx) * g_ref[...], _colsum(dhx) * one_sc))

    vec_out = jax.ShapeDtypeStruct((1, d), F32)
    return _call(body, name="prenorm_bwd", grid=(s // TM,),
                 in_specs=[_tile(d), _tile(d), _vec(d), _vec(d), _tile(d)],
                 out_specs=[_tile(d), _vec(d), _vec(d), _vec(d)],
                 out_shape=[jax.ShapeDtypeStruct((s, d), F32), vec_out, vec_out, vec_out],
                 compiler_params=_params("arbitrary"))(dh, x, g, sc, dres)


def _postnorm_fwd(xres, m, g, gt):
    s, d = m.shape

    def body(x_ref, m_ref, g_ref, gt_ref, o_ref):
        mf = m_ref[...]
        o_ref[...] = x_ref[...] + gt_ref[...] * ((mf * _rstd(mf)) * g_ref[...])

    return _call(body, name="postnorm_fwd", grid=(s // TM,),
                 in_specs=[_tile(d), _tile(d), _vec(d), _vec(d)], out_specs=_tile(d),
                 out_shape=jax.ShapeDtypeStruct((s, d), F32), compiler_params=_params("parallel"))(xres, m, g, gt)


def _postnorm_bwd(dxn, m, g, gt):
    s, d = m.shape

    def body(dx_ref, m_ref, g_ref, gt_ref, dm_ref, dgt_ref, dg_ref):
        mf, dxf = m_ref[...], dx_ref[...]
        rstd = _rstd(mf)
        mhat = mf * rstd
        dmhat = dxf * (gt_ref[...] * g_ref[...])
        dm_ref[...] = (rstd * (dmhat - mhat * jnp.mean(dmhat * mhat, axis=-1, keepdims=True))).astype(dm_ref.dtype)
        dxm = _colsum(dxf * mhat)
        _accumulate((dgt_ref, dg_ref), (dxm * g_ref[...], dxm * gt_ref[...]))

    vec_out = jax.ShapeDtypeStruct((1, d), F32)
    return _call(body, name="postnorm_bwd", grid=(s // TM,),
                 in_specs=[_tile(d), _tile(d), _vec(d), _vec(d)], out_specs=[_tile(d), _vec(d), _vec(d)],
                 out_shape=[jax.ShapeDtypeStruct((s, d), BF16), vec_out, vec_out],
                 compiler_params=_params("arbitrary"))(dxn, m, g, gt)


def _loss(y, target):
    s, d = y.shape

    def body(y_ref, t_ref, dy_ref, sq_ref):
        err = y_ref[...] - t_ref[...]
        dy_ref[...] = err * (1.0 / d)
        tot = jnp.sum(_colsum(err * err), axis=1, keepdims=True)
        _accumulate((sq_ref,), (jnp.broadcast_to(tot, (1, LANES)),))

    return _call(body, name="loss", grid=(s // TM,), in_specs=[_tile(d), _tile(d)],
                 out_specs=[_tile(d), _vec(LANES)],
                 out_shape=[jax.ShapeDtypeStruct((s, d), F32), jax.ShapeDtypeStruct((1, LANES), F32)],
                 compiler_params=_params("arbitrary"))(y, target)


def _sigmoid(v):
    return 1.0 / (1.0 + jnp.exp(-v))


def _gate_fwd(proj, y_conv, y_attn):
    s, d = y_conv.shape
    ga_col, gb_col = (IN_COLS - 2 * d) // d, (IN_COLS - d) // d

    def body(ga_ref, gb_ref, yc_ref, ya_ref, o_ref):
        o_ref[...] = (_sigmoid(ga_ref[...]) * yc_ref[...] + _sigmoid(gb_ref[...]) * ya_ref[...]).astype(o_ref.dtype)

    return _call(body, name="gate_fwd", grid=(s // TM,),
                 in_specs=[_tile(d, ga_col), _tile(d, gb_col), _tile(d), _tile(d)], out_specs=_tile(d),
                 out_shape=jax.ShapeDtypeStruct((s, d), BF16),
                 compiler_params=_params("parallel"))(proj, proj, y_conv, y_attn)


def _gate_bwd(dmerged, proj, y_conv, y_attn):
    s, d = y_conv.shape
    ga_col, gb_col = (IN_COLS - 2 * d) // d, (IN_COLS - d) // d

    def body(dm_ref, ga_ref, gb_ref, yc_ref, ya_ref, dyc_ref, dya_ref, dga_ref, dgb_ref):
        dm = dm_ref[...]
        sa, sb = _sigmoid(ga_ref[...]), _sigmoid(gb_ref[...])
        dyc_ref[...] = (dm * sa).astype(BF16)
        dya_ref[...] = (dm * sb).astype(BF16)
        dga_ref[...] = (dm * yc_ref[...] * (sa * (1.0 - sa))).astype(BF16)
        dgb_ref[...] = (dm * ya_ref[...] * (sb * (1.0 - sb))).astype(BF16)

    out = jax.ShapeDtypeStruct((s, d), BF16)
    return _call(body, name="gate_bwd", grid=(s // TM,),
                 in_specs=[_tile(d), _tile(d, ga_col), _tile(d, gb_col), _tile(d), _tile(d)],
                 out_specs=[_tile(d)] * 4, out_shape=[out] * 4,
                 compiler_params=_params("parallel"))(dmerged, proj, proj, y_conv, y_attn)


def _shift_down(prev8, cur, by):
    ext = jnp.concatenate([prev8, cur], axis=0)
    return pltpu.roll(ext, by, 0)[8:]


def _shift_up(cur, next8, by):
    ext = jnp.concatenate([cur, next8], axis=0)
    return pltpu.roll(ext, ext.shape[0] - by, 0)[:cur.shape[0]]


def _conv_fwd(proj, conv_w):
    s, w = proj.shape[0], CONV_WIDTH
    per8 = TM // 8

    def prev(col):
        return pl.BlockSpec((8, w), lambda i: (jnp.maximum(i * per8 - 1, 0), col))

    def body(bg_ref, cg_ref, u_ref, cgp_ref, up_ref, w_ref, o_ref):
        vv = cg_ref[...] * u_ref[...]
        pv = cgp_ref[...] * up_ref[...] * jnp.where(pl.program_id(0) > 0, 1.0, 0.0)
        y = w_ref[0:1, :] * _shift_down(pv, vv, 2) + w_ref[1:2, :] * _shift_down(pv, vv, 1) + w_ref[2:3, :] * vv
        o_ref[...] = (bg_ref[...] * y).astype(o_ref.dtype)

    return _call(body, name="conv_fwd", grid=(s // TM,),
                 in_specs=[_tile(w, 0), _tile(w, 1), _tile(w, 2), prev(1), prev(2),
                           pl.BlockSpec((3, w), lambda i: (0, 0))],
                 out_specs=_tile(w), out_shape=jax.ShapeDtypeStruct((s, w), BF16),
                 compiler_params=_params("parallel"))(proj, proj, proj, proj, proj, conv_w)


def _conv_bwd(dyc, proj, conv_w):
    s, w = proj.shape[0], CONV_WIDTH
    per8 = TM // 8
    n_tiles = s // TM

    def prev(col):
        return pl.BlockSpec((8, w), lambda i: (jnp.maximum(i * per8 - 1, 0), col))

    def nxt(col):
        return pl.BlockSpec((8, w), lambda i: (jnp.minimum((i + 1) * per8, s // 8 - 1), col))

    def body(dyc_ref, bg_ref, cg_ref, u_ref, cgp_ref, up_ref, dycn_ref, bgn_ref, w_ref,
             dbg_ref, dcg_ref, du_ref, dw0_ref, dw1_ref, dw2_ref):
        i = pl.program_id(0)
        cg, u = cg_ref[...], u_ref[...]
        vv = cg * u
        pv = cgp_ref[...] * up_ref[...] * jnp.where(i > 0, 1.0, 0.0)
        v1, v2 = _shift_down(pv, vv, 1), _shift_down(pv, vv, 2)
        w0, w1, w2 = w_ref[0:1, :], w_ref[1:2, :], w_ref[2:3, :]
        dyc_t = dyc_ref[...]
        dbg_ref[...] = (dyc_t * (w0 * v2 + w1 * v1 + w2 * vv)).astype(BF16)
        dy = dyc_t * bg_ref[...]
        dyn = dycn_ref[...] * bgn_ref[...] * jnp.where(i < n_tiles - 1, 1.0, 0.0)
        dvv = w2 * dy + w1 * _shift_up(dy, dyn, 1) + w0 * _shift_up(dy, dyn, 2)
        dcg_ref[...] = (dvv * u).astype(BF16)
        du_ref[...] = (dvv * cg).astype(BF16)
        _accumulate((dw0_ref, dw1_ref, dw2_ref), (_colsum(dy * v2), _colsum(dy * v1), _colsum(dy * vv)))

    act = jax.ShapeDtypeStruct((s, w), BF16)
    tap = jax.ShapeDtypeStruct((1, w), F32)
    return _call(body, name="conv_bwd", grid=(n_tiles,),
                 in_specs=[_tile(w), _tile(w, 0), _tile(w, 1), _tile(w, 2), prev(1), prev(2), nxt(0), nxt(0),
                           pl.BlockSpec((3, w), lambda i: (0, 0))],
                 out_specs=[_tile(w)] * 3 + [_vec(w)] * 3, out_shape=[act] * 3 + [tap] * 3,
                 compiler_params=_params("arbitrary"))(dyc, proj, proj, proj, proj, proj, dyc, proj, conv_w)


Q_COL = 3 * CONV_WIDTH // LANES
K_COL = Q_COL + ATTN_WIDTH // LANES
V_COL = K_COL + ATTN_WIDTH // LANES
SCALE = HEAD_DIM ** -0.5


def _head_lanes(hh):
    lane = lax.broadcasted_iota(jnp.int32, (1, LANES), 1)
    return jnp.where((lane >= hh * HEAD_DIM) & (lane < (hh + 1) * HEAD_DIM), 1.0, 0.0)


W2 = 2 * TK
PIPE = 4


def _softplus(z):
    neg_abs = lax.bitcast_convert_type(lax.bitcast_convert_type(z, jnp.uint32) | jnp.uint32(0x80000000), F32)
    return jnp.maximum(z, 0.0) + jnp.log(1.0 + jnp.exp(neg_abs))


def _pair_kv(proj):
    s = proj.shape[0]
    per_tile = TM // TK
    width_blocks = ATTN_WIDTH // LANES

    def body(k_ref, v_ref, kp_ref, vp_ref):
        lane = lax.broadcasted_iota(jnp.int32, (1, ATTN_WIDTH), 1) & (LANES - 1)
        keep = [jnp.where(lane < HEAD_DIM, 1.0, 0.0), jnp.where(lane >= HEAD_DIM, 1.0, 0.0)]
        for src, dst in ((k_ref, kp_ref), (v_ref, vp_ref)):
            for gi in range(per_tile):
                blk = src[gi * TK:(gi + 1) * TK, :]
                for hh in range(2):
                    dst[(2 * gi + hh) * TK:(2 * gi + hh + 1) * TK, :] = (blk * keep[hh]).astype(BF16)

    out = jax.ShapeDtypeStruct((2 * s, ATTN_WIDTH), BF16)
    return _call(body, name="pair_kv", grid=(s // TM,),
                 in_specs=[_tile(ATTN_WIDTH, K_COL // width_blocks), _tile(ATTN_WIDTH, V_COL // width_blocks)],
                 out_specs=[_tile(ATTN_WIDTH, rows=2 * TM)] * 2, out_shape=[out, out],
                 compiler_params=_params("parallel"))(proj, proj)


def _pair_rows(ref, kb):
    return ref[pl.ds(pl.multiple_of(kb * W2, W2), W2), :]


def _pair_tri(keep):
    j = lax.broadcasted_iota(jnp.int32, (W2, 2 * W2), 0)
    s = lax.broadcasted_iota(jnp.int32, (W2, 2 * W2), 1)
    same_head = (j >= TK) == ((s & (W2 - 1)) >= TK)
    return jnp.where(same_head & ((s >= W2) | keep(j & (TK - 1), s & (TK - 1))), 1.0, 0.0).astype(BF16)


def _attn_fwd(proj, kp, vp, shards=()):
    s = proj.shape[0]
    nq = s // TQ
    diag = TQ // TK
    n_ch = TQ // CH
    assert s // TK <= TK and diag % PIPE == 0

    def body(q_ref, k_ref, v_ref, o_ref, rs_ref, qb_scr, tri_scr, z_scr, l0_scr, cs_scr, a_scr, r_scr, rall_scr, acc_scr):
        qi = pl.program_id(1)
        lane = lax.broadcasted_iota(jnp.int32, (CH, W2), 1) & (TK - 1)
        row = lax.broadcasted_iota(jnp.int32, (CH, 1), 0)
        col = lax.broadcasted_iota(jnp.int32, (1, W2), 1) & (TK - 1)
        qb_scr[...] = (q_ref[...] * SCALE).astype(BF16)
        tri_scr[...] = _pair_tri(lambda j, ss: j > ss)
        r_scr[...] = jnp.zeros_like(r_scr)
        rall_scr[...] = jnp.zeros_like(rall_scr)
        acc_scr[...] = jnp.zeros_like(acc_scr)

        def causal(kb, c):
            return (kb * TK + col) < (qi * TQ + c * CH + row)

        def chunks(r0):
            return range(0 if r0 is None else r0 // CH, n_ch)

        def on_diagonal(r0, c):
            return r0 is not None and c * CH < r0 + TK

        def logits(kb, zb, r0=0):
            z_scr[zb, r0:, :] = _dot(qb_scr[r0:, :], _pair_rows(k_ref, kb), NT)

        def log_one_minus_beta(kb, zb, lb, r0=None):
            for c in chunks(r0):
                rows = slice(c * CH, (c + 1) * CH)
                sp = _softplus(z_scr[zb, rows, :])
                if on_diagonal(r0, c):
                    sp = jnp.where(causal(kb, c), sp, 0.0)
                l0_scr[lb, rows, :] = sp.astype(BF16)

        def sums(lb, r0=0):
            cs_scr[r0:, :] = _dot(l0_scr[lb, r0:, :], tri_scr[...])

        def weights(kb, zb, lb, ab, r0=None):
            for c in chunks(r0):
                rows = slice(c * CH, (c + 1) * CH)
                near = r_scr[rows, :]
                a = jnp.exp(z_scr[zb, rows, :] - l0_scr[lb, rows, :].astype(F32) - cs_scr[rows, :W2] - near)
                if on_diagonal(r0, c):
                    a = jnp.where(causal(kb, c), a, 0.0)
                a_scr[ab, rows, :] = a.astype(BF16)
                rall_scr[rows, :] = jnp.where(lane == kb, near, rall_scr[rows, :])
                r_scr[rows, :] = near + cs_scr[rows, W2:]

        def weighted_values(kb, ab, r0=0):
            acc_scr[r0:, :] += _dot(a_scr[ab, r0:, :], _pair_rows(v_ref, kb))

        n = qi * diag
        for dd in reversed(range(diag)):
            kb, r0 = n + dd, dd * TK
            logits(kb, dd % PIPE, r0)
            log_one_minus_beta(kb, dd % PIPE, dd % 2, r0)
            sums(dd % 2, r0)
            weights(kb, dd % PIPE, dd % 2, dd % 2, r0)
            weighted_values(kb, dd % 2, r0)

        def block(j):
            return jnp.maximum(n - 1 - j, 0)

        a_scr[...] = jnp.zeros_like(a_scr)
        logits(block(0), 0)
        logits(block(1), 1)
        log_one_minus_beta(block(0), 0, 0)

        def trip(m, carry):
            for u in range(PIPE):
                j = PIPE * m + u
                weighted_values(block(j - 1), (u - 1) % 2)
                sums(u % 2)
                logits(block(j + 2), (u + 2) % PIPE)
                log_one_minus_beta(block(j + 1), (u + 1) % PIPE, (u + 1) % 2)
                weights(block(j), u % PIPE, u % 2, u % 2)
            return carry

        lax.fori_loop(0, n // PIPE, trip, 0)
        weighted_values(block(n - 1), (PIPE - 1) % 2)
        rs_ref[...] = rall_scr[...]
        o_ref[...] = acc_scr[...]

    (o, rsave), gathered = _call_hosting(
        body, (proj, kp, vp), _AllToAll([True] * len(shards)) if shards else None, shards,
        name="attn_fwd", grid=(N_HEADS // 2, nq),
        in_specs=[pl.BlockSpec((TQ, LANES), lambda p, qi: (qi, Q_COL + p)),
                  pl.BlockSpec((2 * s, LANES), lambda p, qi: (0, p)),
                  pl.BlockSpec((2 * s, LANES), lambda p, qi: (0, p))],
        out_specs=[pl.BlockSpec((TQ, LANES), lambda p, qi: (qi, p)),
                   pl.BlockSpec((TQ, W2), lambda p, qi: (qi, p))],
        out_shape=[jax.ShapeDtypeStruct((s, ATTN_WIDTH), F32), jax.ShapeDtypeStruct((s, N_HEADS // 2 * W2), F32)],
        scratch_shapes=[pltpu.VMEM((TQ, LANES), BF16), pltpu.VMEM((W2, 2 * W2), BF16),
                        pltpu.VMEM((PIPE, TQ, W2), F32), pltpu.VMEM((2, TQ, W2), BF16),
                        pltpu.VMEM((TQ, 2 * W2), F32), pltpu.VMEM((2, TQ, W2), BF16),
                        pltpu.VMEM((TQ, W2), F32), pltpu.VMEM((TQ, W2), F32), pltpu.VMEM((TQ, LANES), F32)])
    return o, rsave, gathered


def _attn_bwd(proj, kp, vp, do, rsave, sent=()):
    s = proj.shape[0]
    nq = s // TQ
    diag = TQ // TK
    n_ch = TQ // CH
    assert diag % PIPE == 0

    def body(q_ref, k_ref, v_ref, do_ref, rs_ref, dq_ref, dk_ref, dv_ref,
             qb_scr, dob_scr, after_scr, before_scr, z_scr, da_scr, l0_scr, beta_scr, cs_scr, a_scr, g_scr, cg_scr,
             dz_scr, pg_scr, dq_scr, dk_scr, dv_scr):
        qi = pl.program_id(1)

        @pl.when(qi == 0)
        def _():
            dk_scr[...] = jnp.zeros_like(dk_scr)
            dv_scr[...] = jnp.zeros_like(dv_scr)

        lm = [_head_lanes(0), _head_lanes(1)]
        lane = lax.broadcasted_iota(jnp.int32, (CH, TK), 1)
        row = lax.broadcasted_iota(jnp.int32, (CH, 1), 0)
        col = lax.broadcasted_iota(jnp.int32, (1, W2), 1) & (TK - 1)
        qb_scr[...] = (q_ref[...] * SCALE).astype(BF16)
        dob_scr[...] = do_ref[...].astype(BF16)
        after_scr[...] = _pair_tri(lambda j, ss: j > ss)[:, :W2]
        before_scr[...] = _pair_tri(lambda j, ss: j < ss)
        pg_scr[...] = jnp.zeros_like(pg_scr)
        dq_scr[...] = jnp.zeros_like(dq_scr)
        dz_scr[...] = jnp.zeros_like(dz_scr)

        def causal(kb, c):
            return (kb * TK + col) < (qi * TQ + c * CH + row)

        def chunks(r0):
            return range(0 if r0 is None else r0 // CH, n_ch)

        def on_diagonal(r0, c):
            return r0 is not None and c * CH < r0 + TK

        def logits(kb, zb, r0=0):
            z_scr[zb, r0:, :] = _dot(qb_scr[r0:, :], _pair_rows(k_ref, kb), NT)

        def do_dot_v(kb, db, r0=0):
            da_scr[db, r0:, :] = _dot(dob_scr[r0:, :], _pair_rows(v_ref, kb), NT)

        def gates(kb, zb, lb, bb, r0=None):
            for c in chunks(r0):
                rows = slice(c * CH, (c + 1) * CH)
                z = z_scr[zb, rows, :]
                sp = _softplus(z)
                beta_scr[bb, rows, :] = jnp.exp(z - sp)
                if on_diagonal(r0, c):
                    sp = jnp.where(causal(kb, c), sp, 0.0)
                l0_scr[lb, rows, :] = sp.astype(BF16)

        def suffix_sums(lb, r0=0):
            cs_scr[r0:, :] = _dot(l0_scr[lb, r0:, :], after_scr[...])

        def weights(kb, zb, lb, db, ab, r0=None):
            for c in chunks(r0):
                rows = slice(c * CH, (c + 1) * CH)
                keep = (kb * TK + lane) < (qi * TQ + c * CH + row) if on_diagonal(r0, c) else None
                for hh in range(2):
                    cols = slice(hh * TK, (hh + 1) * TK)
                    near = jnp.sum(jnp.where(lane == kb, rs_ref[rows, cols], 0.0), axis=1, keepdims=True)
                    a = jnp.exp(z_scr[zb, rows, cols] - l0_scr[lb, rows, cols].astype(F32) - cs_scr[rows, cols] - near)
                    if keep is not None:
                        a = jnp.where(keep, a, 0.0)
                    a_scr[ab, rows, cols] = a.astype(BF16)
                    g_scr[ab, rows, cols] = (a * da_scr[db, rows, cols]).astype(BF16)

        def prefix_sums(ab, r0=0):
            cg_scr[r0:, :] = _dot(g_scr[ab, r0:, :], before_scr[...])

        def dlogits(kb, ab, bb, zb2, r0=None):
            for c in chunks(r0):
                rows = slice(c * CH, (c + 1) * CH)
                earlier = pg_scr[rows, :]
                beta = beta_scr[bb, rows, :]
                g = g_scr[ab, rows, :].astype(F32)
                dz = g - beta * (g + cg_scr[rows, :W2] + earlier)
                if on_diagonal(r0, c):
                    dz = jnp.where(causal(kb, c), dz, 0.0)
                dz_scr[zb2, rows, :] = dz.astype(BF16)
                pg_scr[rows, :] = earlier + cg_scr[rows, W2:]

        def fold(t):
            return t[:TK, :] * lm[0] + t[TK:, :] * lm[1]

        def dq_dk(kb, zb2, r0=0):
            dq_scr[r0:, :] += _dot(dz_scr[zb2, r0:, :], _pair_rows(k_ref, kb))
            dk_scr[pl.ds(pl.multiple_of(kb * TK, TK), TK), :] += fold(_dot(dz_scr[zb2, r0:, :], qb_scr[r0:, :], TN))

        def dv(kb, ab, r0=0):
            dv_scr[pl.ds(pl.multiple_of(kb * TK, TK), TK), :] += fold(_dot(a_scr[ab, r0:, :], dob_scr[r0:, :], TN))

        n = qi * diag

        def block(j):
            return jnp.clip(j, 0, jnp.maximum(n - 1, 0))

        logits(block(0), 0)
        logits(block(1), 1)
        logits(block(2), 2)
        do_dot_v(block(0), 0)
        do_dot_v(block(1), 1)
        gates(block(0), 0, 0, 0)
        gates(block(1), 1, 1, 1)
        suffix_sums(0)
        weights(block(0), 0, 0, 0, 0)

        def trip(m, carry):
            for u in range(PIPE):
                t = PIPE * m + u
                dq_dk(block(t - 1), (u - 1) % 2)
                dv(block(t), u % 2)
                prefix_sums(u % 2)
                suffix_sums((u + 1) % 2)
                logits(block(t + 3), (u + 3) % PIPE)
                do_dot_v(block(t + 2), u % 2)
                gates(block(t + 2), (u + 2) % PIPE, u % 2, (u + 2) % PIPE)
                weights(block(t + 1), (u + 1) % PIPE, (u + 1) % 2, (u + 1) % 2, (u + 1) % 2)
                dlogits(block(t), u % 2, u % PIPE, u % 2)
            return carry

        lax.fori_loop(0, n // PIPE, trip, 0)
        dq_dk(block(n - 1), (PIPE - 1) % 2)

        for dd in range(diag):
            kb, r0, two, four = n + dd, dd * TK, dd % 2, dd % PIPE
            logits(kb, four, r0)
            do_dot_v(kb, two, r0)
            gates(kb, four, two, four, r0)
            suffix_sums(two, r0)
            weights(kb, four, two, two, two, r0)
            dv(kb, two, r0)
            prefix_sums(two, r0)
            dlogits(kb, two, four, two, r0)
            dq_dk(kb, two, r0)
        dq_ref[...] = (dq_scr[...] * SCALE).astype(dq_ref.dtype)

        @pl.when(qi == nq - 1)
        def _():
            dk_ref[...] = dk_scr[...].astype(dk_ref.dtype)
            dv_ref[...] = dv_scr[...].astype(dv_ref.dtype)

    def rows(c0):
        return pl.BlockSpec((TQ, LANES), lambda p, qi: (qi, c0 + p))

    def whole(c0):
        return pl.BlockSpec((s, LANES), lambda p, qi: (0, c0 + p))

    def f32(*shape):
        return pltpu.VMEM(shape, F32)

    def bf16(*shape):
        return pltpu.VMEM(shape, BF16)

    pairs = pl.BlockSpec((2 * s, LANES), lambda p, qi: (0, p))
    out = jax.ShapeDtypeStruct((s, ATTN_WIDTH), BF16)
    (dq, dk, dv), parts = _call_hosting(
        body, (proj, kp, vp, do, rsave), _AllToAll([False] * len(sent)) if sent else None, sent,
        name="attn_bwd", grid=(N_HEADS // 2, nq),
        in_specs=[rows(Q_COL), pairs, pairs, rows(0), pl.BlockSpec((TQ, W2), lambda p, qi: (qi, p))],
        out_specs=[rows(0), whole(0), whole(0)], out_shape=[out] * 3,
        scratch_shapes=[bf16(TQ, LANES), bf16(TQ, LANES), bf16(W2, W2), bf16(W2, 2 * W2),
                        f32(PIPE, TQ, W2), f32(2, TQ, W2), bf16(2, TQ, W2), f32(PIPE, TQ, W2), f32(TQ, W2),
                        bf16(2, TQ, W2), bf16(2, TQ, W2), f32(TQ, 2 * W2), bf16(2, TQ, W2),
                        f32(TQ, W2), f32(TQ, LANES), f32(s, LANES), f32(s, LANES)])
    return dq, dk, dv, parts


def _sum_adamw(name, parts, w, m, v, layer=None, into=None):
    n, r, c = parts.shape
    tr = r if r <= 256 else 256

    def body(p_ref, w_ref, m_ref, v_ref, g_ref, d_ref, nm_ref, nv_ref):
        g = p_ref[0].astype(F32)
        for j in range(1, n):
            g = g + p_ref[j].astype(F32)
        nm = ADAM_B1 * m_ref[...] + (1.0 - ADAM_B1) * g
        nv = ADAM_B2 * v_ref[...] + (1.0 - ADAM_B2) * (g * g)
        m_hat = nm / (1.0 - ADAM_B1 ** ADAM_STEP)
        v_hat = nv / (1.0 - ADAM_B2 ** ADAM_STEP)
        g_ref[...] = g
        d_ref[...] = -ADAM_LR * (m_hat / (jnp.sqrt(v_hat) + ADAM_EPS) + ADAM_WD * w_ref[...])
        nm_ref[...] = nm
        nv_ref[...] = nv

    if layer is None:
        mat = pl.BlockSpec((tr, c), lambda i: (i, 0))
        out = jax.ShapeDtypeStruct((r, c), F32)
    else:
        mat = pl.BlockSpec((None, tr, c), lambda i: (layer, i, 0))
        out = jax.ShapeDtypeStruct((DEPTH, r, c), F32)
    earlier = () if into is None else tuple(into)
    return _call(body if into is None else lambda *refs: body(*refs[:4], *refs[8:]),
                 name=name, grid=(r // tr,),
                 in_specs=[pl.BlockSpec((n, tr, c), lambda i: (0, i, 0)), mat, mat, mat]
                 + [pl.BlockSpec(memory_space=pl.ANY)] * len(earlier),
                 out_specs=[mat] * 4, out_shape=[out] * 4,
                 input_output_aliases={4 + k: k for k in range(len(earlier))},
                 compiler_params=_params("parallel"))(parts, w, m, v, *earlier)


def _natural(gathered):
    _, k, n = gathered.shape
    return gathered.transpose(1, 0, 2).reshape(k, N_DEV * n)


def _relu2_epi(acc):
    r = jnp.maximum(acc, 0.0)
    return acc, r * r


def _relu2_bwd_epi(acc, a_act):
    return (acc * (2.0 * jnp.maximum(a_act, 0.0)),)


def kernel(x, c, w_ada, b_ada, g_pre_mix, g_post_mix, g_pre_mlp, g_post_mlp, w_in, conv_w, w_proj_conv, w_proj_attn, w_out, w_mlp_in, w_mlp_out, loss_target, m_w_ada, m_b_ada, m_g_pre_mix, m_g_post_mix, m_g_pre_mlp, m_g_post_mlp, m_w_in, m_conv_w, m_w_proj_conv, m_w_proj_attn, m_w_out, m_w_mlp_in, m_w_mlp_out, v_w_ada, v_b_ada, v_g_pre_mix, v_g_post_mix, v_g_pre_mlp, v_g_post_mlp, v_w_in, v_conv_w, v_w_proj_conv, v_w_proj_attn, v_w_out, v_w_mlp_in, v_w_mlp_out):
    xi, yi, ci = _mesh_pos()
    me = 4 * xi + 2 * yi + ci
    d = D_MODEL
    x0 = x[0]
    seq = x0.shape[0]
    ada_cols = w_ada.shape[2]
    conv_cols = conv_w.shape[2]

    small = jnp.concatenate([c.reshape(-1), conv_w.reshape(-1)])
    small = jnp.pad(small, (0, 2 * d - small.shape[0])).reshape(8, 2 * d // 8)
    small_all = _all_gather("gather_c", [small])[0].reshape(N_DEV, 2 * d)
    c_all = small_all[:, :d]
    conv_all = small_all[:, d:d + DEPTH * 3 * conv_cols].reshape(N_DEV, DEPTH, 3, conv_cols)
    conv_all = conv_all.transpose(1, 2, 0, 3).reshape(DEPTH, 3, N_DEV * conv_cols)
    mod_cols = jnp.stack([_mm("mod_mm", c_all, w_ada[l], "nn", N_DEV, ada_cols, d, [F32], exact=True)
                          for l in range(DEPTH)], axis=1)
    mod_all = _all_gather("gather_mod", [mod_cols.reshape(N_DEV, DEPTH * ada_cols)])[0]
    mod_mine = lax.dynamic_index_in_dim(mod_all, me, axis=1, keepdims=False).reshape(N_DEV, DEPTH, ada_cols)
    mod = mod_mine.transpose(1, 0, 2).reshape(DEPTH, N_MOD * d) + b_ada

    sharded = {"w_in": w_in, "w_proj_conv": w_proj_conv, "w_proj_attn": w_proj_attn, "w_out": w_out,
               "w_mlp_in": w_mlp_in, "w_mlp_out": w_mlp_out}
    before_attention = ["w_in", "w_proj_conv"]

    def shard(key):
        nm, l = key
        return sharded[nm][l].astype(BF16)

    def natural(key, gathered):
        return gathered.reshape(-1, d) if key[0] in ("w_out", "w_mlp_out") else _natural(gathered)

    first = [(nm, 0) for nm in before_attention]
    full = {key: natural(key, g) for key, g in zip(first, _all_gather("gather_w", [shard(key) for key in first]))}
    saved = []
    xl = x0
    for l in range(DEPTH):
        riders = [(nm, l) for nm in sharded if (nm, l) not in full]
        if l + 1 < DEPTH:
            riders += [(nm, l + 1) for nm in sharded]
        sh1, sc1, gt1, sh2, sc2, gt2 = [mod[l:l + 1, i * d:(i + 1) * d] for i in range(N_MOD)]
        h = _prenorm_fwd(xl, g_pre_mix[l:l + 1], sc1, sh1)
        proj = _mm("proj", h, full[("w_in", l)], "nn", TMM, 1024, d, [F32])
        yc = _conv_fwd(proj, conv_all[l])
        y_conv = _mm("proj_conv", yc, full[("w_proj_conv", l)], "nn", TMM, d, CONV_WIDTH, [F32])
        kp, vp = _pair_kv(proj)
        o, rsave, gathered = _attn_fwd(proj, kp, vp, [shard(key) for key in riders])
        full.update({key: natural(key, g) for key, g in zip(riders, gathered)})
        wg_in, wg_pc, wg_pa, wg_out, wg_mi, wg_mo = [full[(nm, l)] for nm in sharded]
        y_attn = _mm("proj_attn", o, wg_pa, "nn", TMM,d, ATTN_WIDTH, [F32])
        merged = _gate_fwd(proj, y_conv, y_attn)
        mix_out = _mm("mix_out", merged, wg_out, "nn", TMM,d, d, [F32])
        x1 = _postnorm_fwd(xl, mix_out, g_post_mix[l:l + 1], gt1)
        h2 = _prenorm_fwd(x1, g_pre_mlp[l:l + 1], sc2, sh2)
        a_act, r = _mm("mlp_in", h2, wg_mi, "nn", TMM,1024, d, [F32, BF16], epi=_relu2_epi)
        ff = _mm("mlp_out", r, wg_mo, "nn", TMK, d, D_FF, [F32])
        x2 = _postnorm_fwd(x1, ff, g_post_mlp[l:l + 1], gt2)
        saved.append((xl, h, proj, yc, kp, vp, o, rsave, y_conv, y_attn, merged, mix_out, x1, h2, a_act, r, ff))
        xl = x2

    dxo, sq = _loss(xl, loss_target[0])
    loss = lax.psum(sq[0, 0] * (0.5 / d), ("x", "y", "c"))

    olds = {"w_in": (w_in, m_w_in, v_w_in), "w_proj_conv": (w_proj_conv, m_w_proj_conv, v_w_proj_conv),
            "w_proj_attn": (w_proj_attn, m_w_proj_attn, v_w_proj_attn), "w_out": (w_out, m_w_out, v_w_out),
            "w_mlp_in": (w_mlp_in, m_w_mlp_in, v_w_mlp_in), "w_mlp_out": (w_mlp_out, m_w_mlp_out, v_w_mlp_out)}
    big = {}
    pending = []

    def col_blocks(gw):
        k, n = gw.shape
        return gw.reshape(k, N_DEV, n // N_DEV).transpose(1, 0, 2)

    def update(entries, parts):
        for (nm, ll, _), part in zip(entries, parts):
            w_, m_, v_ = olds[nm]
            big[nm] = _sum_adamw("adamw_" + nm, part, w_, m_, v_, layer=ll, into=big.get(nm))

    dmod, small_grads = [None] * DEPTH, [None] * DEPTH
    for l in reversed(range(DEPTH)):
        wg_in, wg_pc, wg_pa, wg_out, wg_mi, wg_mo = [full[(nm, l)] for nm in sharded]
        xin, h, proj, yc, kp, vp, o, rsave, y_conv, y_attn, merged, mix_out, x1, h2, a_act, r, ff = saved[l]
        sh1, sc1, gt1, sh2, sc2, gt2 = [mod[l:l + 1, i * d:(i + 1) * d] for i in range(N_MOD)]

        dff, dgt2, dg_post_mlp = _postnorm_bwd(dxo, ff, g_post_mlp[l:l + 1], gt2)
        da = _mm("d_relu2", dff, wg_mo, "nt", TMM,1024, d, [BF16], epi=_relu2_bwd_epi, extra=(a_act,))
        gw_mo = _mm("gw_mlp_out", r, dff, "tn", 1024, d, TSK, [BF16])
        dh2 = _mm("d_h2", da, wg_mi, "nt", TMK, d, D_FF, [F32])
        gw_mi = _mm("gw_mlp_in", h2, da, "tn", d, 1024, TSK, [BF16])
        dx1, dsh2, dsc2, dg_pre_mlp = _prenorm_bwd(dh2, x1, g_pre_mlp[l:l + 1], sc2, dxo)

        dmix, dgt1, dg_post_mix = _postnorm_bwd(dx1, mix_out, g_post_mix[l:l + 1], gt1)
        dmerged = _mm("d_merged", dmix, wg_out, "nt", TMM,d, d, [F32])
        gw_out = _mm("gw_out", merged, dmix, "tn", d, d, TSK, [BF16])
        dy_conv, dy_attn, dga, dgb = _gate_bwd(dmerged, proj, y_conv, y_attn)
        do = _mm("d_o", dy_attn, wg_pa, "nt", TMM,ATTN_WIDTH, d, [F32])
        gw_pa = _mm("gw_proj_attn", o, dy_attn, "tn", ATTN_WIDTH, d, TSK, [BF16])
        dyc = _mm("d_yc", dy_conv, wg_pc, "nt", TMM,CONV_WIDTH, d, [F32])
        gw_pc = _mm("gw_proj_conv", yc, dy_conv, "tn", CONV_WIDTH, d, TSK, [BF16])
        pending += [("w_mlp_out", l, gw_mo.reshape(N_DEV, D_FF // N_DEV, d)), ("w_mlp_in", l, col_blocks(gw_mi)),
                    ("w_out", l, gw_out.reshape(N_DEV, d // N_DEV, d)), ("w_proj_attn", l, col_blocks(gw_pa)),
                    ("w_proj_conv", l, col_blocks(gw_pc))]
        if l == 0:
            dq, dk, dv, parts = _attn_bwd(proj, kp, vp, do, rsave, [blocks for _, _, blocks in pending])
            update(pending, parts)
            pending = []
        else:
            dq, dk, dv, _ = _attn_bwd(proj, kp, vp, do, rsave)
        dbg, dcg, du, dw0, dw1, dw2 = _conv_bwd(dyc, proj, conv_all[l])
        dproj = jnp.concatenate([dbg, dcg, du, dq, dk, dv, dga, dgb], axis=1)
        gw_in = _mm("gw_in", h, dproj, "tn", d, 1024, TSK, [BF16])
        pending.append(("w_in", l, col_blocks(gw_in)))
        if l == 0:
            dh, parts = _mm("d_h", dproj, wg_in, "nt", TMK, d, IN_COLS, [F32], sent=[blocks for _, _, blocks in pending])
            update(pending, parts)
            pending = []
        else:
            dh = _mm("d_h", dproj, wg_in, "nt", TMK, d, IN_COLS, [F32])
        dxo, dsh1, dsc1, dg_pre_mix = _prenorm_bwd(dh, xin, g_pre_mix[l:l + 1], sc1, dx1)

        dmod[l] = jnp.concatenate([dsh1, dsc1, dgt1, dsh2, dsc2, dgt2], axis=1)
        small_grads[l] = (dg_pre_mix, dg_post_mix, dg_pre_mlp, dg_post_mlp, jnp.concatenate([dw0, dw1, dw2], axis=0))
    assert not pending

    vec = jnp.concatenate(
        [dmod[l].reshape(-1) for l in range(DEPTH)]
        + [small_grads[l][i].reshape(-1) for i in range(4) for l in range(DEPTH)]
        + [small_grads[l][4].reshape(-1) for l in range(DEPTH)])
    n_vec = vec.shape[0]
    vec_all = _all_gather("gather_small", [vec.reshape(8, n_vec // 8)])[0].reshape(N_DEV, n_vec)
    n_mod = DEPTH * N_MOD * d
    dmod_all = vec_all[:, :n_mod].reshape(N_DEV, DEPTH, N_MOD * d)
    res = {}
    res["b_ada"] = _sum_adamw("adamw_b_ada", dmod_all, b_ada, m_b_ada, v_b_ada)
    off = n_mod
    for nm, (w_, m_, v_) in zip(
            ["g_pre_mix", "g_post_mix", "g_pre_mlp", "g_post_mlp"],
            [(g_pre_mix, m_g_pre_mix, v_g_pre_mix), (g_post_mix, m_g_post_mix, v_g_post_mix),
             (g_pre_mlp, m_g_pre_mlp, v_g_pre_mlp), (g_post_mlp, m_g_post_mlp, v_g_post_mlp)]):
        res[nm] = _sum_adamw("adamw_gain", vec_all[:, off:off + DEPTH * d].reshape(N_DEV, DEPTH, d), w_, m_, v_)
        off += DEPTH * d
    dconv_all = vec_all[:, off:].reshape(N_DEV, DEPTH * 3, CONV_WIDTH)
    dconv_mine = lax.dynamic_slice_in_dim(dconv_all, me * conv_cols, conv_cols, axis=2)
    res["conv_w"] = [t.reshape(DEPTH, 3, conv_cols) for t in _sum_adamw(
        "adamw_conv_w", dconv_mine, conv_w.reshape(DEPTH * 3, conv_cols), m_conv_w.reshape(DEPTH * 3, conv_cols),
        v_conv_w.reshape(DEPTH * 3, conv_cols))]

    c_t = jnp.pad(c_all.T, ((0, 0), (0, LANES - N_DEV)))
    dmod_mine = lax.dynamic_slice_in_dim(dmod_all, me * ada_cols, ada_cols, axis=2)
    for l in range(DEPTH):
        dm_l = jnp.pad(dmod_mine[:, l, :], ((0, LANES - N_DEV), (0, 0)))
        gw_ada = _mm("gw_ada", c_t, dm_l, "nn", 256, ada_cols, LANES, [F32], exact=True)
        res["w_ada"] = _sum_adamw("adamw_w_ada", gw_ada[None], w_ada, m_w_ada, v_w_ada, layer=l, into=res.get("w_ada"))
    res.update(big)

    order = ["w_ada", "b_ada", "g_pre_mix", "g_post_mix", "g_pre_mlp", "g_post_mlp", "w_in", "conv_w",
             "w_proj_conv", "w_proj_attn", "w_out", "w_mlp_in", "w_mlp_out"]
    outs = [loss, dxo[None]]
    for i in range(4):
        outs += [res[nm][i] for nm in order]
    return tuple(outs)
```

```python
import jax
import jax.numpy as jnp
from jax import lax
from jax.experimental import pallas as pl
from jax.experimental.pallas import tpu as pltpu

F32 = jnp.float32
BF16 = jnp.bfloat16
MESH = pl.DeviceIdType.MESH

N_DEV = 8
D_MODEL = 1024
CONV_WIDTH = 512
N_HEADS = 8
HEAD_DIM = 64
ATTN_WIDTH = N_HEADS * HEAD_DIM
D_FF = 4 * D_MODEL
N_MOD = 6
DEPTH = 2
EPS = 1e-6
IN_COLS = 3 * CONV_WIDTH + 3 * ATTN_WIDTH + 2 * D_MODEL
LANES = 128

ADAM_LR = 0.001
ADAM_B1 = 0.9
ADAM_B2 = 0.999
ADAM_EPS = 1e-08
ADAM_WD = 0.01
ADAM_STEP = 10

TM = 512
TMM = 2048
TMK = 1024
TSK = 2048
TQ = 512
TK = 128
CH = 64
VMEM_LIMIT = 56 * 1024 * 1024

NN = (((1,), (0,)), ((), ()))
NT = (((1,), (1,)), ((), ()))
TN = (((0,), (0,)), ((), ()))
_DIMS = {"nn": NN, "nt": NT, "tn": TN}


def _call(body, **kw):
    return pl.pallas_call(body, **kw)


def _params(*sem):
    return pltpu.CompilerParams(dimension_semantics=sem, vmem_limit_bytes=VMEM_LIMIT)


def _dot(a, b, dims=NN):
    return lax.dot_general(a, b, dims, preferred_element_type=F32)


def _mesh_pos():
    return lax.axis_index("x"), lax.axis_index("y"), lax.axis_index("c")


def _all_gather(name, arrs):
    n = len(arrs)

    def body(*refs):
        ins, outs = refs[:n], refs[n:2 * n]
        send_sems, recv_sems, local_sems = refs[2 * n:]
        x, y, c = _mesh_pos()
        me, sibling = (x, y, c), (x, y, 1 - c)
        chips = [(1 - x, y), (x, 1 - y), (1 - x, 1 - y)]

        def blk(t, p):
            return outs[t].at[4 * p[0] + 2 * p[1] + p[2]]

        def copy(t, k, block, to, src=None):
            return pltpu.make_async_remote_copy(
                src_ref=blk(t, block) if src is None else src, dst_ref=blk(t, block),
                send_sem=send_sems.at[7 * t + k], recv_sem=recv_sems.at[7 * t + k],
                device_id=to, device_id_type=MESH)

        mine, first, passed = [], [], []
        for t in range(n):
            cp = pltpu.make_async_copy(ins[t], blk(t, me), local_sems.at[t])
            cp.start()
            mine.append(cp)
            cps = [copy(t, 0, me, sibling, src=ins[t])]
            cps += [copy(t, 1 + j, me, (*chip, c), src=ins[t]) for j, chip in enumerate(chips)]
            for cp in cps:
                cp.start()
            first += cps
        for t in range(n):
            for j, chip in enumerate(chips):
                copy(t, 1 + j, (*chip, c), me).wait_recv()
                cp = copy(t, 4 + j, (*chip, c), sibling)
                cp.start()
                passed.append(cp)
        for t in range(n):
            copy(t, 0, sibling, me).wait_recv()
            for j, chip in enumerate(chips):
                copy(t, 4 + j, (*chip, 1 - c), me).wait_recv()
        for cp in first + passed:
            cp.wait_send()
        for cp in mine:
            cp.wait()

    any_spec = pl.BlockSpec(memory_space=pl.ANY)
    return _call(
        body, name=name,
        out_shape=[jax.ShapeDtypeStruct((N_DEV,) + a.shape, a.dtype) for a in arrs],
        in_specs=[any_spec] * n, out_specs=[any_spec] * n,
        scratch_shapes=[pltpu.SemaphoreType.DMA((7 * n,)), pltpu.SemaphoreType.DMA((7 * n,)),
                        pltpu.SemaphoreType.DMA((n,))],
    )(*arrs)


class _AllToAll:
    def __init__(self, whole):
        self.whole = list(whole)
        self.n = len(self.whole)

    def sem_shapes(self):
        return [pltpu.SemaphoreType.DMA((7 * self.n,)), pltpu.SemaphoreType.DMA((7 * self.n,)),
                pltpu.SemaphoreType.DMA((self.n,))]

    def out_shapes(self, arrs):
        return [jax.ShapeDtypeStruct(((N_DEV,) + a.shape) if w else a.shape, a.dtype) for a, w in zip(arrs, self.whole)]

    def _copies(self, ins, outs, sems):
        send_sems, recv_sems, local_sems = sems
        x, y, c = _mesh_pos()
        my_idx = 4 * x + 2 * y + c
        mine, sends, recvs = [], [], []
        for t in range(self.n):
            def src(idx):
                return ins[t] if self.whole[t] else ins[t].at[idx]
            mine.append(pltpu.make_async_copy(src(my_idx), outs[t].at[my_idx], local_sems.at[t]))
            for k in range(1, N_DEV):
                p = (1 - x if k & 4 else x, 1 - y if k & 2 else y, 1 - c if k & 1 else c)
                p_idx = 4 * p[0] + 2 * p[1] + p[2]
                for dst_idx, group in ((my_idx, sends), (p_idx, recvs)):
                    group.append(pltpu.make_async_remote_copy(
                        src_ref=src(p_idx), dst_ref=outs[t].at[dst_idx],
                        send_sem=send_sems.at[7 * t + k - 1], recv_sem=recv_sems.at[7 * t + k - 1],
                        device_id=p, device_id_type=MESH))
        return mine, sends, recvs

    def start(self, ins, outs, sems):
        mine, sends, _ = self._copies(ins, outs, sems)
        for cp in mine + sends:
            cp.start()

    def finish(self, ins, outs, sems):
        mine, sends, recvs = self._copies(ins, outs, sems)
        for cp in recvs:
            cp.wait_recv()
        for cp in sends:
            cp.wait_send()
        for cp in mine:
            cp.wait()


def _call_hosting(body, args, comm, comm_args, *, name, grid, in_specs, out_specs, out_shape, scratch_shapes):
    if comm is None:
        return _call(body, name=name, grid=grid, in_specs=in_specs, out_specs=out_specs, out_shape=out_shape,
                     scratch_shapes=scratch_shapes, compiler_params=_params(*["arbitrary"] * len(grid)))(*args), ()
    n, n_in, n_out, n_scr = comm.n, len(in_specs), len(out_specs), len(scratch_shapes)

    def hosted(*refs):
        ins, refs = refs[:n_in], refs[n_in:]
        c_ins, refs = refs[:n], refs[n:]
        outs, refs = refs[:n_out], refs[n_out:]
        c_outs, refs = refs[:n], refs[n:]
        scratch, sems = refs[:n_scr], refs[n_scr:]
        step = [pl.program_id(i) for i in range(len(grid))]

        def at(ends):
            hit = step[0] == ends[0]
            for sidx, e in zip(step[1:], ends[1:]):
                hit = jnp.logical_and(hit, sidx == e)
            return hit

        @pl.when(at([0] * len(grid)))
        def _():
            comm.start(c_ins, c_outs, sems)

        body(*ins, *outs, *scratch)

        @pl.when(at([g - 1 for g in grid]))
        def _():
            comm.finish(c_ins, c_outs, sems)

    any_spec = pl.BlockSpec(memory_space=pl.ANY)
    res = _call(hosted, name=name + "_hosting", grid=grid, in_specs=list(in_specs) + [any_spec] * n,
                out_specs=list(out_specs) + [any_spec] * n, out_shape=list(out_shape) + comm.out_shapes(comm_args),
                scratch_shapes=list(scratch_shapes) + comm.sem_shapes(),
                compiler_params=_params(*["arbitrary"] * len(grid)))(*args, *comm_args)
    return res[:n_out], res[n_out:]


def _mm(name, a, b, mode, tm, tn, tk, out_dtypes, epi=None, extra=(), exact=False, sent=()):
    if mode == "nn":
        (m, k), n = a.shape, b.shape[1]
    elif mode == "nt":
        (m, k), n = a.shape, b.shape[0]
    else:
        (k, m), n = a.shape, b.shape[1]
    tm, tn, tk = min(tm, m), min(tn, n), min(tk, k)
    nk = k // tk
    grid = (m // tm, n // tn, nk)
    n_extra, n_out = len(extra), len(out_dtypes)

    def body(*refs):
        a_ref, b_ref = refs[0], refs[1]
        extra_refs = refs[2:2 + n_extra]
        out_refs = refs[2 + n_extra:2 + n_extra + n_out]
        if exact:
            p = lax.dot_general(a_ref[...], b_ref[...], _DIMS[mode], preferred_element_type=F32,
                                precision=lax.Precision.HIGHEST)
        else:
            p = _dot(a_ref[...].astype(BF16), b_ref[...].astype(BF16), _DIMS[mode])

        def finish(acc):
            outs = (acc,) if epi is None else epi(acc, *[r[...] for r in extra_refs])
            for r, o in zip(out_refs, outs):
                r[...] = o.astype(r.dtype)

        if nk == 1:
            finish(p)
        else:
            acc_ref = refs[-1]
            kk = pl.program_id(2)

            @pl.when(kk == 0)
            def _():
                acc_ref[...] = p

            @pl.when(kk > 0)
            def _():
                acc_ref[...] += p

            @pl.when(kk == nk - 1)
            def _():
                finish(acc_ref[...])

    if mode == "tn":
        a_spec = pl.BlockSpec((tk, tm), lambda i, j, kk: (kk, i))
    else:
        a_spec = pl.BlockSpec((tm, tk), lambda i, j, kk: (i, kk))
    if mode == "nt":
        b_spec = pl.BlockSpec((tn, tk), lambda i, j, kk: (j, kk))
    else:
        b_spec = pl.BlockSpec((tk, tn), lambda i, j, kk: (kk, j))
    tile = pl.BlockSpec((tm, tn), lambda i, j, kk: (i, j))
    o_shape, o_spec = (m, n), tile
    out, parts = _call_hosting(
        body, (a, b, *extra), _AllToAll([False] * len(sent)) if sent else None, sent,
        name=name, grid=grid,
        in_specs=[a_spec, b_spec] + [tile] * n_extra,
        out_specs=[o_spec] * n_out,
        out_shape=[jax.ShapeDtypeStruct(o_shape, dt) for dt in out_dtypes],
        scratch_shapes=[pltpu.VMEM((tm, tn), F32)] if nk > 1 else [])
    out = out[0] if n_out == 1 else out
    return (out, parts) if sent else out


def _tile(width, col=0, rows=TM):
    return pl.BlockSpec((rows, width), lambda i: (i, col))


def _vec(width):
    return pl.BlockSpec((1, width), lambda i: (0, 0))


def _rstd(xf):
    return lax.rsqrt(jnp.mean(xf * xf, axis=-1, keepdims=True) + EPS)


def _colsum(v):
    return jnp.sum(v, axis=0, keepdims=True)


def _accumulate(refs, vals):
    first = pl.program_id(0) == 0

    @pl.when(first)
    def _():
        for r, v in zip(refs, vals):
            r[...] = v

    @pl.when(jnp.logical_not(first))
    def _():
        for r, v in zip(refs, vals):
            r[...] += v


def _prenorm_fwd(x, g, sc, sh):
    s, d = x.shape

    def body(x_ref, g_ref, sc_ref, sh_ref, h_ref):
        xf = x_ref[...]
        y = (xf * _rstd(xf)) * g_ref[...]
        h_ref[...] = (y * (1.0 + sc_ref[...]) + sh_ref[...]).astype(h_ref.dtype)

    return _call(body, name="prenorm_fwd", grid=(s // TM,),
                 in_specs=[_tile(d), _vec(d), _vec(d), _vec(d)], out_specs=_tile(d),
                 out_shape=jax.ShapeDtypeStruct((s, d), BF16), compiler_params=_params("parallel"))(x, g, sc, sh)


def _prenorm_bwd(dh, x, g, sc, dres):
    s, d = x.shape

    def body(dh_ref, x_ref, g_ref, sc_ref, dres_ref, dx_ref, dsh_ref, dsc_ref, dg_ref):
        xf, dhf = x_ref[...], dh_ref[...]
        rstd = _rstd(xf)
        xhat = xf * rstd
        one_sc = 1.0 + sc_ref[...]
        dxhat = dhf * (g_ref[...] * one_sc)
        dx_ref[...] = dres_ref[...] + rstd * (dxhat - xhat * jnp.mean(dxhat * xhat, axis=-1, keepdims=True))
        dhx = dhf * xhat
        _accumulate((dsh_ref, dsc_ref, dg_ref), (_colsum(dhf), _colsum(dhx) * g_ref[...], _colsum(dhx) * one_sc))

    vec_out = jax.ShapeDtypeStruct((1, d), F32)
    return _call(body, name="prenorm_bwd", grid=(s // TM,),
                 in_specs=[_tile(d), _tile(d), _vec(d), _vec(d), _tile(d)],
                 out_specs=[_tile(d), _vec(d), _vec(d), _vec(d)],
                 out_shape=[jax.ShapeDtypeStruct((s, d), F32), vec_out, vec_out, vec_out],
                 compiler_params=_params("arbitrary"))(dh, x, g, sc, dres)


def _postnorm_fwd(xres, m, g, gt):
    s, d = m.shape

    def body(x_ref, m_ref, g_ref, gt_ref, o_ref):
        mf = m_ref[...]
        o_ref[...] = x_ref[...] + gt_ref[...] * ((mf * _rstd(mf)) * g_ref[...])

    return _call(body, name="postnorm_fwd", grid=(s // TM,),
                 in_specs=[_tile(d), _tile(d), _vec(d), _vec(d)], out_specs=_tile(d),
                 out_shape=jax.ShapeDtypeStruct((s, d), F32), compiler_params=_params("parallel"))(xres, m, g, gt)


def _postnorm_bwd(dxn, m, g, gt):
    s, d = m.shape

    def body(dx_ref, m_ref, g_ref, gt_ref, dm_ref, dgt_ref, dg_ref):
        mf, dxf = m_ref[...], dx_ref[...]
        rstd = _rstd(mf)
        mhat = mf * rstd
        dmhat = dxf * (gt_ref[...] * g_ref[...])
        dm_ref[...] = (rstd * (dmhat - mhat * jnp.mean(dmhat * mhat, axis=-1, keepdims=True))).astype(dm_ref.dtype)
        dxm = _colsum(dxf * mhat)
        _accumulate((dgt_ref, dg_ref), (dxm * g_ref[...], dxm * gt_ref[...]))

    vec_out = jax.ShapeDtypeStruct((1, d), F32)
    return _call(body, name="postnorm_bwd", grid=(s // TM,),
                 in_specs=[_tile(d), _tile(d), _vec(d), _vec(d)], out_specs=[_tile(d), _vec(d), _vec(d)],
                 out_shape=[jax.ShapeDtypeStruct((s, d), BF16), vec_out, vec_out],
                 compiler_params=_params("arbitrary"))(dxn, m, g, gt)


def _loss(y, target):
    s, d = y.shape

    def body(y_ref, t_ref, dy_ref, sq_ref):
        err = y_ref[...] - t_ref[...]
        dy_ref[...] = err * (1.0 / d)
        tot = jnp.sum(_colsum(err * err), axis=1, keepdims=True)
        _accumulate((sq_ref,), (jnp.broadcast_to(tot, (1, LANES)),))

    return _call(body, name="loss", grid=(s // TM,), in_specs=[_tile(d), _tile(d)],
                 out_specs=[_tile(d), _vec(LANES)],
                 out_shape=[jax.ShapeDtypeStruct((s, d), F32), jax.ShapeDtypeStruct((1, LANES), F32)],
                 compiler_params=_params("arbitrary"))(y, target)


def _sigmoid(v):
    return 1.0 / (1.0 + jnp.exp(-v))


def _gate_fwd(proj, y_conv, y_attn):
    s, d = y_conv.shape
    ga_col, gb_col = (IN_COLS - 2 * d) // d, (IN_COLS - d) // d

    def body(ga_ref, gb_ref, yc_ref, ya_ref, o_ref):
        o_ref[...] = (_sigmoid(ga_ref[...]) * yc_ref[...] + _sigmoid(gb_ref[...]) * ya_ref[...]).astype(o_ref.dtype)

    return _call(body, name="gate_fwd", grid=(s // TM,),
                 in_specs=[_tile(d, ga_col), _tile(d, gb_col), _tile(d), _tile(d)], out_specs=_tile(d),
                 out_shape=jax.ShapeDtypeStruct((s, d), BF16),
                 compiler_params=_params("parallel"))(proj, proj, y_conv, y_attn)


def _gate_bwd(dmerged, proj, y_conv, y_attn):
    s, d = y_conv.shape
    ga_col, gb_col = (IN_COLS - 2 * d) // d, (IN_COLS - d) // d

    def body(dm_ref, ga_ref, gb_ref, yc_ref, ya_ref, dyc_ref, dya_ref, dga_ref, dgb_ref):
        dm = dm_ref[...].astype(F32)
        sa, sb = _sigmoid(ga_ref[...]), _sigmoid(gb_ref[...])
        dyc_ref[...] = (dm * sa).astype(BF16)
        dya_ref[...] = (dm * sb).astype(BF16)
        dga_ref[...] = (dm * yc_ref[...] * (sa * (1.0 - sa))).astype(BF16)
        dgb_ref[...] = (dm * ya_ref[...] * (sb * (1.0 - sb))).astype(BF16)

    out = jax.ShapeDtypeStruct((s, d), BF16)
    return _call(body, name="gate_bwd", grid=(s // TM,),
                 in_specs=[_tile(d), _tile(d, ga_col), _tile(d, gb_col), _tile(d), _tile(d)],
                 out_specs=[_tile(d)] * 4, out_shape=[out] * 4,
                 compiler_params=_params("parallel"))(dmerged, proj, proj, y_conv, y_attn)


def _shift_down(prev8, cur, by):
    ext = jnp.concatenate([prev8, cur], axis=0)
    return pltpu.roll(ext, by, 0)[8:]


def _shift_up(cur, next8, by):
    ext = jnp.concatenate([cur, next8], axis=0)
    return pltpu.roll(ext, ext.shape[0] - by, 0)[:cur.shape[0]]


def _conv_fwd(proj, conv_w):
    s, w = proj.shape[0], CONV_WIDTH
    per8 = TM // 8

    def prev(col):
        return pl.BlockSpec((8, w), lambda i: (jnp.maximum(i * per8 - 1, 0), col))

    def body(bg_ref, cg_ref, u_ref, cgp_ref, up_ref, w_ref, o_ref):
        vv = cg_ref[...] * u_ref[...]
        pv = cgp_ref[...] * up_ref[...] * jnp.where(pl.program_id(0) > 0, 1.0, 0.0)
        y = w_ref[0:1, :] * _shift_down(pv, vv, 2) + w_ref[1:2, :] * _shift_down(pv, vv, 1) + w_ref[2:3, :] * vv
        o_ref[...] = (bg_ref[...] * y).astype(o_ref.dtype)

    return _call(body, name="conv_fwd", grid=(s // TM,),
                 in_specs=[_tile(w, 0), _tile(w, 1), _tile(w, 2), prev(1), prev(2),
                           pl.BlockSpec((3, w), lambda i: (0, 0))],
                 out_specs=_tile(w), out_shape=jax.ShapeDtypeStruct((s, w), BF16),
                 compiler_params=_params("parallel"))(proj, proj, proj, proj, proj, conv_w)


def _conv_bwd(dyc, proj, conv_w):
    s, w = proj.shape[0], CONV_WIDTH
    per8 = TM // 8
    n_tiles = s // TM

    def prev(col):
        return pl.BlockSpec((8, w), lambda i: (jnp.maximum(i * per8 - 1, 0), col))

    def nxt(col):
        return pl.BlockSpec((8, w), lambda i: (jnp.minimum((i + 1) * per8, s // 8 - 1), col))

    def body(dyc_ref, bg_ref, cg_ref, u_ref, cgp_ref, up_ref, dycn_ref, bgn_ref, w_ref,
             dbg_ref, dcg_ref, du_ref, dw0_ref, dw1_ref, dw2_ref):
        i = pl.program_id(0)
        cg, u = cg_ref[...], u_ref[...]
        vv = cg * u
        pv = cgp_ref[...] * up_ref[...] * jnp.where(i > 0, 1.0, 0.0)
        v1, v2 = _shift_down(pv, vv, 1), _shift_down(pv, vv, 2)
        w0, w1, w2 = w_ref[0:1, :], w_ref[1:2, :], w_ref[2:3, :]
        dyc_t = dyc_ref[...]
        dbg_ref[...] = (dyc_t * (w0 * v2 + w1 * v1 + w2 * vv)).astype(BF16)
        dy = dyc_t * bg_ref[...]
        dyn = dycn_ref[...] * bgn_ref[...] * jnp.where(i < n_tiles - 1, 1.0, 0.0)
        dvv = w2 * dy + w1 * _shift_up(dy, dyn, 1) + w0 * _shift_up(dy, dyn, 2)
        dcg_ref[...] = (dvv * u).astype(BF16)
        du_ref[...] = (dvv * cg).astype(BF16)
        _accumulate((dw0_ref, dw1_ref, dw2_ref), (_colsum(dy * v2), _colsum(dy * v1), _colsum(dy * vv)))

    act = jax.ShapeDtypeStruct((s, w), BF16)
    tap = jax.ShapeDtypeStruct((1, w), F32)
    return _call(body, name="conv_bwd", grid=(n_tiles,),
                 in_specs=[_tile(w), _tile(w, 0), _tile(w, 1), _tile(w, 2), prev(1), prev(2), nxt(0), nxt(0),
                           pl.BlockSpec((3, w), lambda i: (0, 0))],
                 out_specs=[_tile(w)] * 3 + [_vec(w)] * 3, out_shape=[act] * 3 + [tap] * 3,
                 compiler_params=_params("arbitrary"))(dyc, proj, proj, proj, proj, proj, dyc, proj, conv_w)


Q_COL = 3 * CONV_WIDTH // LANES
K_COL = Q_COL + ATTN_WIDTH // LANES
V_COL = K_COL + ATTN_WIDTH // LANES
SCALE = HEAD_DIM ** -0.5


def _head_lanes(hh):
    lane = lax.broadcasted_iota(jnp.int32, (1, LANES), 1)
    return jnp.where((lane >= hh * HEAD_DIM) & (lane < (hh + 1) * HEAD_DIM), 1.0, 0.0)


W2 = 2 * TK
PIPE = 4


def _softplus(z):
    neg_abs = lax.bitcast_convert_type(lax.bitcast_convert_type(z, jnp.uint32) | jnp.uint32(0x80000000), F32)
    return jnp.maximum(z, 0.0) + jnp.log(1.0 + jnp.exp(neg_abs))


def _pair_kv(proj):
    s = proj.shape[0]
    per_tile = TM // TK
    width_blocks = ATTN_WIDTH // LANES

    def body(k_ref, v_ref, kp_ref, vp_ref):
        lane = lax.broadcasted_iota(jnp.int32, (1, ATTN_WIDTH), 1) & (LANES - 1)
        keep = [jnp.where(lane < HEAD_DIM, 1.0, 0.0), jnp.where(lane >= HEAD_DIM, 1.0, 0.0)]
        for src, dst in ((k_ref, kp_ref), (v_ref, vp_ref)):
            for gi in range(per_tile):
                blk = src[gi * TK:(gi + 1) * TK, :]
                for hh in range(2):
                    dst[(2 * gi + hh) * TK:(2 * gi + hh + 1) * TK, :] = (blk * keep[hh]).astype(BF16)

    out = jax.ShapeDtypeStruct((2 * s, ATTN_WIDTH), BF16)
    return _call(body, name="pair_kv", grid=(s // TM,),
                 in_specs=[_tile(ATTN_WIDTH, K_COL // width_blocks), _tile(ATTN_WIDTH, V_COL // width_blocks)],
                 out_specs=[_tile(ATTN_WIDTH, rows=2 * TM)] * 2, out_shape=[out, out],
                 compiler_params=_params("parallel"))(proj, proj)


def _pair_rows(ref, kb):
    return ref[pl.ds(pl.multiple_of(kb * W2, W2), W2), :]


def _pair_tri(keep):
    j = lax.broadcasted_iota(jnp.int32, (W2, 2 * W2), 0)
    s = lax.broadcasted_iota(jnp.int32, (W2, 2 * W2), 1)
    same_head = (j >= TK) == ((s & (W2 - 1)) >= TK)
    return jnp.where(same_head & ((s >= W2) | keep(j & (TK - 1), s & (TK - 1))), 1.0, 0.0).astype(BF16)


def _attn_fwd(proj, kp, vp, shards=()):
    s = proj.shape[0]
    nq = s // TQ
    diag = TQ // TK
    n_ch = TQ // CH
    assert s // TK <= TK and diag % PIPE == 0

    def body(q_ref, k_ref, v_ref, o_ref, rs_ref, qb_scr, tri_scr, z_scr, l0_scr, cs_scr, a_scr, r_scr, rall_scr, acc_scr):
        qi = pl.program_id(1)
        lane = lax.broadcasted_iota(jnp.int32, (CH, W2), 1) & (TK - 1)
        row = lax.broadcasted_iota(jnp.int32, (CH, 1), 0)
        col = lax.broadcasted_iota(jnp.int32, (1, W2), 1) & (TK - 1)
        qb_scr[...] = (q_ref[...] * SCALE).astype(BF16)

        @pl.when(qi == 0)
        def _():
            tri_scr[...] = _pair_tri(lambda j, ss: j > ss)

        r_scr[...] = jnp.zeros_like(r_scr)
        rall_scr[...] = jnp.zeros_like(rall_scr)
        acc_scr[...] = jnp.zeros_like(acc_scr)

        def causal(kb, c):
            return (kb * TK + col) < (qi * TQ + c * CH + row)

        def chunks(r0):
            return range(0 if r0 is None else r0 // CH, n_ch)

        def on_diagonal(r0, c):
            return r0 is not None and c * CH < r0 + TK

        def logits(kb, zb, r0=0):
            z_scr[zb, r0:, :] = _dot(qb_scr[r0:, :], _pair_rows(k_ref, kb), NT)

        def log_one_minus_beta(kb, zb, lb, r0=None):
            for c in chunks(r0):
                rows = slice(c * CH, (c + 1) * CH)
                sp = _softplus(z_scr[zb, rows, :])
                if on_diagonal(r0, c):
                    sp = jnp.where(causal(kb, c), sp, 0.0)
                l0_scr[lb, rows, :] = sp.astype(BF16)

        def sums(lb, r0=0):
            cs_scr[r0:, :] = _dot(l0_scr[lb, r0:, :], tri_scr[...])

        def weights(kb, zb, lb, ab, r0=None):
            for c in chunks(r0):
                rows = slice(c * CH, (c + 1) * CH)
                near = r_scr[rows, :]
                a = jnp.exp(z_scr[zb, rows, :] - l0_scr[lb, rows, :].astype(F32) - cs_scr[rows, :W2] - near)
                if on_diagonal(r0, c):
                    a = jnp.where(causal(kb, c), a, 0.0)
                a_scr[ab, rows, :] = a.astype(BF16)
                rall_scr[rows, :] = jnp.where(lane == kb, near, rall_scr[rows, :])
                r_scr[rows, :] = near + cs_scr[rows, W2:]

        def weighted_values(kb, ab, r0=0):
            acc_scr[r0:, :] += _dot(a_scr[ab, r0:, :], _pair_rows(v_ref, kb))

        n = qi * diag
        for dd in reversed(range(diag)):
            kb, r0 = n + dd, dd * TK
            logits(kb, dd % PIPE, r0)
            log_one_minus_beta(kb, dd % PIPE, dd % 2, r0)
            sums(dd % 2, r0)
            weights(kb, dd % PIPE, dd % 2, dd % 2, r0)
            weighted_values(kb, dd % 2, r0)

        def block(j):
            return jnp.maximum(n - 1 - j, 0)

        a_scr[...] = jnp.zeros_like(a_scr)
        logits(block(0), 0)
        logits(block(1), 1)
        log_one_minus_beta(block(0), 0, 0)

        def trip(m, carry):
            for u in range(PIPE):
                j = PIPE * m + u
                weighted_values(block(j - 1), (u - 1) % 2)
                sums(u % 2)
                logits(block(j + 2), (u + 2) % PIPE)
                log_one_minus_beta(block(j + 1), (u + 1) % PIPE, (u + 1) % 2)
                weights(block(j), u % PIPE, u % 2, u % 2)
            return carry

        lax.fori_loop(0, n // PIPE, trip, 0)
        weighted_values(block(n - 1), (PIPE - 1) % 2)
        rs_ref[...] = rall_scr[...]
        o_ref[...] = acc_scr[...]

    (o, rsave), gathered = _call_hosting(
        body, (proj, kp, vp), _AllToAll([True] * len(shards)) if shards else None, shards,
        name="attn_fwd", grid=(N_HEADS // 2, nq),
        in_specs=[pl.BlockSpec((TQ, LANES), lambda p, qi: (qi, Q_COL + p)),
                  pl.BlockSpec((2 * s, LANES), lambda p, qi: (0, p)),
                  pl.BlockSpec((2 * s, LANES), lambda p, qi: (0, p))],
        out_specs=[pl.BlockSpec((TQ, LANES), lambda p, qi: (qi, p)),
                   pl.BlockSpec((TQ, W2), lambda p, qi: (qi, p))],
        out_shape=[jax.ShapeDtypeStruct((s, ATTN_WIDTH), F32), jax.ShapeDtypeStruct((s, N_HEADS // 2 * W2), F32)],
        scratch_shapes=[pltpu.VMEM((TQ, LANES), BF16), pltpu.VMEM((W2, 2 * W2), BF16),
                        pltpu.VMEM((PIPE, TQ, W2), F32), pltpu.VMEM((2, TQ, W2), BF16),
                        pltpu.VMEM((TQ, 2 * W2), F32), pltpu.VMEM((2, TQ, W2), BF16),
                        pltpu.VMEM((TQ, W2), F32), pltpu.VMEM((TQ, W2), F32), pltpu.VMEM((TQ, LANES), F32)])
    return o, rsave, gathered


def _attn_bwd(proj, kp, vp, do, rsave, sent=()):
    s = proj.shape[0]
    nq = s // TQ
    diag = TQ // TK
    n_ch = TQ // CH
    assert diag % PIPE == 0

    def body(q_ref, k_ref, v_ref, do_ref, rs_ref, dq_ref, dk_ref, dv_ref,
             qb_scr, dob_scr, after_scr, before_scr, z_scr, da_scr, l0_scr, beta_scr, cs_scr, a_scr, g_scr, cg_scr,
             dz_scr, pg_scr, dq_scr, dk_scr, dv_scr):
        qi = pl.program_id(1)

        @pl.when(qi == 0)
        def _():
            dk_scr[...] = jnp.zeros_like(dk_scr)
            dv_scr[...] = jnp.zeros_like(dv_scr)
            after_scr[...] = _pair_tri(lambda j, ss: j > ss)[:, :W2]
            before_scr[...] = _pair_tri(lambda j, ss: j < ss)

        lm = [_head_lanes(0), _head_lanes(1)]
        lane = lax.broadcasted_iota(jnp.int32, (CH, TK), 1)
        row = lax.broadcasted_iota(jnp.int32, (CH, 1), 0)
        col = lax.broadcasted_iota(jnp.int32, (1, W2), 1) & (TK - 1)
        qb_scr[...] = (q_ref[...] * SCALE).astype(BF16)
        dob_scr[...] = do_ref[...].astype(BF16)
        pg_scr[...] = jnp.zeros_like(pg_scr)
        dq_scr[...] = jnp.zeros_like(dq_scr)
        dz_scr[...] = jnp.zeros_like(dz_scr)

        def causal(kb, c):
            return (kb * TK + col) < (qi * TQ + c * CH + row)

        def chunks(r0):
            return range(0 if r0 is None else r0 // CH, n_ch)

        def on_diagonal(r0, c):
            return r0 is not None and c * CH < r0 + TK

        def logits(kb, zb, r0=0):
            z_scr[zb, r0:, :] = _dot(qb_scr[r0:, :], _pair_rows(k_ref, kb), NT)

        def do_dot_v(kb, db, r0=0):
            da_scr[db, r0:, :] = _dot(dob_scr[r0:, :], _pair_rows(v_ref, kb), NT)

        def gates(kb, zb, lb, bb, r0=None):
            for c in chunks(r0):
                rows = slice(c * CH, (c + 1) * CH)
                z = z_scr[zb, rows, :]
                sp = _softplus(z)
                beta_scr[bb, rows, :] = jnp.exp(z - sp)
                if on_diagonal(r0, c):
                    sp = jnp.where(causal(kb, c), sp, 0.0)
                l0_scr[lb, rows, :] = sp.astype(BF16)

        def suffix_sums(lb, r0=0):
            cs_scr[r0:, :] = _dot(l0_scr[lb, r0:, :], after_scr[...])

        def weights(kb, zb, lb, db, ab, r0=None):
            for c in chunks(r0):
                rows = slice(c * CH, (c + 1) * CH)
                keep = (kb * TK + lane) < (qi * TQ + c * CH + row) if on_diagonal(r0, c) else None
                for hh in range(2):
                    cols = slice(hh * TK, (hh + 1) * TK)
                    near = jnp.sum(jnp.where(lane == kb, rs_ref[rows, cols], 0.0), axis=1, keepdims=True)
                    a = jnp.exp(z_scr[zb, rows, cols] - l0_scr[lb, rows, cols].astype(F32) - cs_scr[rows, cols] - near)
                    if keep is not None:
                        a = jnp.where(keep, a, 0.0)
                    a_scr[ab, rows, cols] = a.astype(BF16)
                    g_scr[ab, rows, cols] = (a * da_scr[db, rows, cols]).astype(BF16)

        def prefix_sums(ab, r0=0):
            cg_scr[r0:, :] = _dot(g_scr[ab, r0:, :], before_scr[...])

        def dlogits(kb, ab, bb, zb2, r0=None):
            for c in chunks(r0):
                rows = slice(c * CH, (c + 1) * CH)
                earlier = pg_scr[rows, :]
                beta = beta_scr[bb, rows, :]
                g = g_scr[ab, rows, :].astype(F32)
                dz = g - beta * (g + cg_scr[rows, :W2] + earlier)
                if on_diagonal(r0, c):
                    dz = jnp.where(causal(kb, c), dz, 0.0)
                dz_scr[zb2, rows, :] = dz.astype(BF16)
                pg_scr[rows, :] = earlier + cg_scr[rows, W2:]

        def fold(t):
            return t[:TK, :] * lm[0] + t[TK:, :] * lm[1]

        def dq_dk(kb, zb2, r0=0):
            dq_scr[r0:, :] += _dot(dz_scr[zb2, r0:, :], _pair_rows(k_ref, kb))
            dk_scr[pl.ds(pl.multiple_of(kb * TK, TK), TK), :] += fold(_dot(dz_scr[zb2, r0:, :], qb_scr[r0:, :], TN))

        def dv(kb, ab, r0=0):
            dv_scr[pl.ds(pl.multiple_of(kb * TK, TK), TK), :] += fold(_dot(a_scr[ab, r0:, :], dob_scr[r0:, :], TN))

        n = qi * diag

        def block(j):
            return jnp.clip(j, 0, jnp.maximum(n - 1, 0))

        logits(block(0), 0)
        logits(block(1), 1)
        logits(block(2), 2)
        do_dot_v(block(0), 0)
        do_dot_v(block(1), 1)
        gates(block(0), 0, 0, 0)
        gates(block(1), 1, 1, 1)
        suffix_sums(0)
        weights(block(0), 0, 0, 0, 0)

        def trip(m, carry):
            for u in range(PIPE):
                t = PIPE * m + u
                dq_dk(block(t - 1), (u - 1) % 2)
                dv(block(t), u % 2)
                prefix_sums(u % 2)
                suffix_sums((u + 1) % 2)
                logits(block(t + 3), (u + 3) % PIPE)
                do_dot_v(block(t + 2), u % 2)
                gates(block(t + 2), (u + 2) % PIPE, u % 2, (u + 2) % PIPE)
                weights(block(t + 1), (u + 1) % PIPE, (u + 1) % 2, (u + 1) % 2, (u + 1) % 2)
                dlogits(block(t), u % 2, u % PIPE, u % 2)
            return carry

        lax.fori_loop(0, n // PIPE, trip, 0)
        dq_dk(block(n - 1), (PIPE - 1) % 2)

        for dd in range(diag):
            kb, r0, two, four = n + dd, dd * TK, dd % 2, dd % PIPE
            logits(kb, four, r0)
            do_dot_v(kb, two, r0)
            gates(kb, four, two, four, r0)
            suffix_sums(two, r0)
            weights(kb, four, two, two, two, r0)
            dv(kb, two, r0)
            prefix_sums(two, r0)
            dlogits(kb, two, four, two, r0)
            dq_dk(kb, two, r0)
        dq_ref[...] = (dq_scr[...] * SCALE).astype(dq_ref.dtype)

        @pl.when(qi == nq - 1)
        def _():
            dk_ref[...] = dk_scr[...].astype(dk_ref.dtype)
            dv_ref[...] = dv_scr[...].astype(dv_ref.dtype)

    def rows(c0):
        return pl.BlockSpec((TQ, LANES), lambda p, qi: (qi, c0 + p))

    def whole(c0):
        return pl.BlockSpec((s, LANES), lambda p, qi: (0, c0 + p))

    def f32(*shape):
        return pltpu.VMEM(shape, F32)

    def bf16(*shape):
        return pltpu.VMEM(shape, BF16)

    pairs = pl.BlockSpec((2 * s, LANES), lambda p, qi: (0, p))
    out = jax.ShapeDtypeStruct((s, ATTN_WIDTH), BF16)
    (dq, dk, dv), parts = _call_hosting(
        body, (proj, kp, vp, do, rsave), _AllToAll([False] * len(sent)) if sent else None, sent,
        name="attn_bwd", grid=(N_HEADS // 2, nq),
        in_specs=[rows(Q_COL), pairs, pairs, rows(0), pl.BlockSpec((TQ, W2), lambda p, qi: (qi, p))],
        out_specs=[rows(0), whole(0), whole(0)], out_shape=[out] * 3,
        scratch_shapes=[bf16(TQ, LANES), bf16(TQ, LANES), bf16(W2, W2), bf16(W2, 2 * W2),
                        f32(PIPE, TQ, W2), f32(2, TQ, W2), bf16(2, TQ, W2), f32(PIPE, TQ, W2), f32(TQ, W2),
                        bf16(2, TQ, W2), bf16(2, TQ, W2), f32(TQ, 2 * W2), bf16(2, TQ, W2),
                        f32(TQ, W2), f32(TQ, LANES), f32(s, LANES), f32(s, LANES)])
    return dq, dk, dv, parts


def _sum_adamw(name, parts, w, m, v, layer=None, into=None):
    n, r, c = parts.shape
    tr = r if r <= 256 else 256

    def body(p_ref, w_ref, m_ref, v_ref, g_ref, d_ref, nm_ref, nv_ref):
        g = p_ref[0].astype(F32)
        for j in range(1, n):
            g = g + p_ref[j].astype(F32)
        nm = ADAM_B1 * m_ref[...] + (1.0 - ADAM_B1) * g
        nv = ADAM_B2 * v_ref[...] + (1.0 - ADAM_B2) * (g * g)
        m_hat = nm / (1.0 - ADAM_B1 ** ADAM_STEP)
        v_hat = nv / (1.0 - ADAM_B2 ** ADAM_STEP)
        g_ref[...] = g
        d_ref[...] = -ADAM_LR * (m_hat / (jnp.sqrt(v_hat) + ADAM_EPS) + ADAM_WD * w_ref[...])
        nm_ref[...] = nm
        nv_ref[...] = nv

    if layer is None:
        mat = pl.BlockSpec((tr, c), lambda i: (i, 0))
        out = jax.ShapeDtypeStruct((r, c), F32)
    else:
        mat = pl.BlockSpec((None, tr, c), lambda i: (layer, i, 0))
        out = jax.ShapeDtypeStruct((DEPTH, r, c), F32)
    earlier = () if into is None else tuple(into)
    return _call(body if into is None else lambda *refs: body(*refs[:4], *refs[8:]),
                 name=name, grid=(r // tr,),
                 in_specs=[pl.BlockSpec((n, tr, c), lambda i: (0, i, 0)), mat, mat, mat]
                 + [pl.BlockSpec(memory_space=pl.ANY)] * len(earlier),
                 out_specs=[mat] * 4, out_shape=[out] * 4,
                 input_output_aliases={4 + k: k for k in range(len(earlier))},
                 compiler_params=_params("parallel"))(parts, w, m, v, *earlier)


def _natural(gathered):
    _, k, n = gathered.shape
    return gathered.transpose(1, 0, 2).reshape(k, N_DEV * n)


def _relu2_epi(acc):
    r = jnp.maximum(acc, 0.0)
    return acc, r * r


def _relu2_bwd_epi(acc, a_act):
    return (acc * (2.0 * jnp.maximum(a_act, 0.0)),)


def kernel(x, c, w_ada, b_ada, g_pre_mix, g_post_mix, g_pre_mlp, g_post_mlp, w_in, conv_w, w_proj_conv, w_proj_attn, w_out, w_mlp_in, w_mlp_out, loss_target, m_w_ada, m_b_ada, m_g_pre_mix, m_g_post_mix, m_g_pre_mlp, m_g_post_mlp, m_w_in, m_conv_w, m_w_proj_conv, m_w_proj_attn, m_w_out, m_w_mlp_in, m_w_mlp_out, v_w_ada, v_b_ada, v_g_pre_mix, v_g_post_mix, v_g_pre_mlp, v_g_post_mlp, v_w_in, v_conv_w, v_w_proj_conv, v_w_proj_attn, v_w_out, v_w_mlp_in, v_w_mlp_out):
    xi, yi, ci = _mesh_pos()
    me = 4 * xi + 2 * yi + ci
    d = D_MODEL
    x0 = x[0]
    seq = x0.shape[0]
    ada_cols = w_ada.shape[2]
    conv_cols = conv_w.shape[2]

    small = jnp.concatenate([c.reshape(-1), conv_w.reshape(-1)])
    small = jnp.pad(small, (0, 2 * d - small.shape[0])).reshape(8, 2 * d // 8)
    small_all = _all_gather("gather_c", [small])[0].reshape(N_DEV, 2 * d)
    c_all = small_all[:, :d]
    conv_all = small_all[:, d:d + DEPTH * 3 * conv_cols].reshape(N_DEV, DEPTH, 3, conv_cols)
    conv_all = conv_all.transpose(1, 2, 0, 3).reshape(DEPTH, 3, N_DEV * conv_cols)
    mod_cols = jnp.stack([_mm("mod_mm", c_all, w_ada[l], "nn", N_DEV, ada_cols, d, [F32], exact=True)
                          for l in range(DEPTH)], axis=1)
    mod_all = _all_gather("gather_mod", [mod_cols.reshape(N_DEV, DEPTH * ada_cols)])[0]
    mod_mine = lax.dynamic_index_in_dim(mod_all, me, axis=1, keepdims=False).reshape(N_DEV, DEPTH, ada_cols)
    mod = mod_mine.transpose(1, 0, 2).reshape(DEPTH, N_MOD * d) + b_ada

    sharded = {"w_in": w_in, "w_proj_conv": w_proj_conv, "w_proj_attn": w_proj_attn, "w_out": w_out,
               "w_mlp_in": w_mlp_in, "w_mlp_out": w_mlp_out}
    before_attention = ["w_in", "w_proj_conv"]

    def shard(key):
        nm, l = key
        return sharded[nm][l].astype(BF16)

    def natural(key, gathered):
        return gathered.reshape(-1, d) if key[0] in ("w_out", "w_mlp_out") else _natural(gathered)

    first = [(nm, 0) for nm in before_attention]
    full = {key: natural(key, g) for key, g in zip(first, _all_gather("gather_w", [shard(key) for key in first]))}
    saved = []
    xl = x0
    for l in range(DEPTH):
        riders = [(nm, l) for nm in sharded if (nm, l) not in full]
        if l + 1 < DEPTH:
            riders += [(nm, l + 1) for nm in sharded]
        sh1, sc1, gt1, sh2, sc2, gt2 = [mod[l:l + 1, i * d:(i + 1) * d] for i in range(N_MOD)]
        h = _prenorm_fwd(xl, g_pre_mix[l:l + 1], sc1, sh1)
        proj = _mm("proj", h, full[("w_in", l)], "nn", TMM, 1024, d, [F32])
        yc = _conv_fwd(proj, conv_all[l])
        y_conv = _mm("proj_conv", yc, full[("w_proj_conv", l)], "nn", TMM, d, CONV_WIDTH, [BF16])
        kp, vp = _pair_kv(proj)
        o, rsave, gathered = _attn_fwd(proj, kp, vp, [shard(key) for key in riders])
        full.update({key: natural(key, g) for key, g in zip(riders, gathered)})
        wg_in, wg_pc, wg_pa, wg_out, wg_mi, wg_mo = [full[(nm, l)] for nm in sharded]
        y_attn = _mm("proj_attn", o, wg_pa, "nn", TMM, d, ATTN_WIDTH, [BF16])
        merged = _gate_fwd(proj, y_conv, y_attn)
        mix_out = _mm("mix_out", merged, wg_out, "nn", TMM,d, d, [F32])
        x1 = _postnorm_fwd(xl, mix_out, g_post_mix[l:l + 1], gt1)
        h2 = _prenorm_fwd(x1, g_pre_mlp[l:l + 1], sc2, sh2)
        a_act, r = _mm("mlp_in", h2, wg_mi, "nn", TMM,1024, d, [F32, BF16], epi=_relu2_epi)
        ff = _mm("mlp_out", r, wg_mo, "nn", TMK, d, D_FF, [F32])
        x2 = _postnorm_fwd(x1, ff, g_post_mlp[l:l + 1], gt2)
        saved.append((xl, h, proj, yc, kp, vp, o, rsave, y_conv, y_attn, merged, mix_out, x1, h2, a_act, r, ff))
        xl = x2

    dxo, sq = _loss(xl, loss_target[0])
    loss = lax.psum(sq[0, 0] * (0.5 / d), ("x", "y", "c"))

    olds = {"w_in": (w_in, m_w_in, v_w_in), "w_proj_conv": (w_proj_conv, m_w_proj_conv, v_w_proj_conv),
            "w_proj_attn": (w_proj_attn, m_w_proj_attn, v_w_proj_attn), "w_out": (w_out, m_w_out, v_w_out),
            "w_mlp_in": (w_mlp_in, m_w_mlp_in, v_w_mlp_in), "w_mlp_out": (w_mlp_out, m_w_mlp_out, v_w_mlp_out)}
    big = {}
    pending = []

    def col_blocks(gw):
        k, n = gw.shape
        return gw.reshape(k, N_DEV, n // N_DEV).transpose(1, 0, 2)

    def update(entries, parts):
        for (nm, ll, _), part in zip(entries, parts):
            w_, m_, v_ = olds[nm]
            big[nm] = _sum_adamw("adamw_" + nm, part, w_, m_, v_, layer=ll, into=big.get(nm))

    dmod, small_grads = [None] * DEPTH, [None] * DEPTH
    for l in reversed(range(DEPTH)):
        wg_in, wg_pc, wg_pa, wg_out, wg_mi, wg_mo = [full[(nm, l)] for nm in sharded]
        xin, h, proj, yc, kp, vp, o, rsave, y_conv, y_attn, merged, mix_out, x1, h2, a_act, r, ff = saved[l]
        sh1, sc1, gt1, sh2, sc2, gt2 = [mod[l:l + 1, i * d:(i + 1) * d] for i in range(N_MOD)]

        dff, dgt2, dg_post_mlp = _postnorm_bwd(dxo, ff, g_post_mlp[l:l + 1], gt2)
        da = _mm("d_relu2", dff, wg_mo, "nt", TMM,1024, d, [BF16], epi=_relu2_bwd_epi, extra=(a_act,))
        gw_mo = _mm("gw_mlp_out", r, dff, "tn", 1024, d, TSK, [BF16])
        dh2 = _mm("d_h2", da, wg_mi, "nt", TMK, d, D_FF, [F32])
        gw_mi = _mm("gw_mlp_in", h2, da, "tn", d, 1024, TSK, [BF16])
        dx1, dsh2, dsc2, dg_pre_mlp = _prenorm_bwd(dh2, x1, g_pre_mlp[l:l + 1], sc2, dxo)

        dmix, dgt1, dg_post_mix = _postnorm_bwd(dx1, mix_out, g_post_mix[l:l + 1], gt1)
        dmerged = _mm("d_merged", dmix, wg_out, "nt", TMM, d, d, [BF16])
        gw_out = _mm("gw_out", merged, dmix, "tn", d, d, TSK, [BF16])
        dy_conv, dy_attn, dga, dgb = _gate_bwd(dmerged, proj, y_conv, y_attn)
        do = _mm("d_o", dy_attn, wg_pa, "nt", TMM,ATTN_WIDTH, d, [F32])
        gw_pa = _mm("gw_proj_attn", o, dy_attn, "tn", ATTN_WIDTH, d, TSK, [BF16])
        dyc = _mm("d_yc", dy_conv, wg_pc, "nt", TMM,CONV_WIDTH, d, [F32])
        gw_pc = _mm("gw_proj_conv", yc, dy_conv, "tn", CONV_WIDTH, d, TSK, [BF16])
        pending += [("w_mlp_out", l, gw_mo.reshape(N_DEV, D_FF // N_DEV, d)), ("w_mlp_in", l, col_blocks(gw_mi)),
                    ("w_out", l, gw_out.reshape(N_DEV, d // N_DEV, d)), ("w_proj_attn", l, col_blocks(gw_pa)),
                    ("w_proj_conv", l, col_blocks(gw_pc))]
        if l == 0:
            dq, dk, dv, parts = _attn_bwd(proj, kp, vp, do, rsave, [blocks for _, _, blocks in pending])
            update(pending, parts)
            pending = []
        else:
            dq, dk, dv, _ = _attn_bwd(proj, kp, vp, do, rsave)
        dbg, dcg, du, dw0, dw1, dw2 = _conv_bwd(dyc, proj, conv_all[l])
        dproj = jnp.concatenate([dbg, dcg, du, dq, dk, dv, dga, dgb], axis=1)
        gw_in = _mm("gw_in", h, dproj, "tn", d, 1024, TSK, [BF16])
        pending.append(("w_in", l, col_blocks(gw_in)))
        if l == 0:
            dh, parts = _mm("d_h", dproj, wg_in, "nt", TMK, d, IN_COLS, [F32], sent=[blocks for _, _, blocks in pending])
            update(pending, parts)
            pending = []
        else:
            dh = _mm("d_h", dproj, wg_in, "nt", TMK, d, IN_COLS, [F32])
        dxo, dsh1, dsc1, dg_pre_mix = _prenorm_bwd(dh, xin, g_pre_mix[l:l + 1], sc1, dx1)

        dmod[l] = jnp.concatenate([dsh1, dsc1, dgt1, dsh2, dsc2, dgt2], axis=1)
        small_grads[l] = (dg_pre_mix, dg_post_mix, dg_pre_mlp, dg_post_mlp, jnp.concatenate([dw0, dw1, dw2], axis=0))
    assert not pending

    vec = jnp.concatenate(
        [dmod[l].reshape(-1) for l in range(DEPTH)]
        + [small_grads[l][i].reshape(-1) for i in range(4) for l in range(DEPTH)]
        + [small_grads[l][4].reshape(-1) for l in range(DEPTH)])
    n_vec = vec.shape[0]
    vec_all = _all_gather("gather_small", [vec.reshape(8, n_vec // 8)])[0].reshape(N_DEV, n_vec)
    n_mod = DEPTH * N_MOD * d
    dmod_all = vec_all[:, :n_mod].reshape(N_DEV, DEPTH, N_MOD * d)
    res = {}
    res["b_ada"] = _sum_adamw("adamw_b_ada", dmod_all, b_ada, m_b_ada, v_b_ada)
    off = n_mod
    for nm, (w_, m_, v_) in zip(
            ["g_pre_mix", "g_post_mix", "g_pre_mlp", "g_post_mlp"],
            [(g_pre_mix, m_g_pre_mix, v_g_pre_mix), (g_post_mix, m_g_post_mix, v_g_post_mix),
             (g_pre_mlp, m_g_pre_mlp, v_g_pre_mlp), (g_post_mlp, m_g_post_mlp, v_g_post_mlp)]):
        res[nm] = _sum_adamw("adamw_gain", vec_all[:, off:off + DEPTH * d].reshape(N_DEV, DEPTH, d), w_, m_, v_)
        off += DEPTH * d
    dconv_all = vec_all[:, off:].reshape(N_DEV, DEPTH * 3, CONV_WIDTH)
    dconv_mine = lax.dynamic_slice_in_dim(dconv_all, me * conv_cols, conv_cols, axis=2)
    res["conv_w"] = [t.reshape(DEPTH, 3, conv_cols) for t in _sum_adamw(
        "adamw_conv_w", dconv_mine, conv_w.reshape(DEPTH * 3, conv_cols), m_conv_w.reshape(DEPTH * 3, conv_cols),
        v_conv_w.reshape(DEPTH * 3, conv_cols))]

    c_t = jnp.pad(c_all.T, ((0, 0), (0, LANES - N_DEV)))
    dmod_mine = lax.dynamic_slice_in_dim(dmod_all, me * ada_cols, ada_cols, axis=2)
    for l in range(DEPTH):
        dm_l = jnp.pad(dmod_mine[:, l, :], ((0, LANES - N_DEV), (0, 0)))
        gw_ada = _mm("gw_ada", c_t, dm_l, "nn", 256, ada_cols, LANES, [F32], exact=True)
        res["w_ada"] = _sum_adamw("adamw_w_ada", gw_ada[None], w_ada, m_w_ada, v_w_ada, layer=l, into=res.get("w_ada"))
    res.update(big)

    order = ["w_ada", "b_ada", "g_pre_mix", "g_post_mix", "g_pre_mlp", "g_post_mlp", "w_in", "conv_w",
             "w_proj_conv", "w_proj_attn", "w_out", "w_mlp_in", "w_mlp_out"]
    outs = [loss, dxo[None]]
    for i in range(4):
        outs += [res[nm][i] for nm in order]
    return tuple(outs)
```

```python
import jax
import jax.numpy as jnp
from jax import lax
from jax.experimental import pallas as pl
from jax.experimental.pallas import tpu as pltpu

F32 = jnp.float32
BF16 = jnp.bfloat16
MESH = pl.DeviceIdType.MESH

N_DEV = 8
D_MODEL = 1024
CONV_WIDTH = 512
N_HEADS = 8
HEAD_DIM = 64
ATTN_WIDTH = N_HEADS * HEAD_DIM
D_FF = 4 * D_MODEL
N_MOD = 6
DEPTH = 2
EPS = 1e-6
IN_COLS = 3 * CONV_WIDTH + 3 * ATTN_WIDTH + 2 * D_MODEL
LANES = 128

ADAM_LR = 0.001
ADAM_B1 = 0.9
ADAM_B2 = 0.999
ADAM_EPS = 1e-08
ADAM_WD = 0.01
ADAM_STEP = 10

TM = 512
TMM = 2048
TMK = 1024
TSK = 2048
TQ = 512
TK = 128
CH = 64
VMEM_LIMIT = 56 * 1024 * 1024

NN = (((1,), (0,)), ((), ()))
NT = (((1,), (1,)), ((), ()))
TN = (((0,), (0,)), ((), ()))
_DIMS = {"nn": NN, "nt": NT, "tn": TN}


def _call(body, **kw):
    return pl.pallas_call(body, **kw)


def _params(*sem):
    return pltpu.CompilerParams(dimension_semantics=sem, vmem_limit_bytes=VMEM_LIMIT)


def _dot(a, b, dims=NN):
    return lax.dot_general(a, b, dims, preferred_element_type=F32)


def _mesh_pos():
    return lax.axis_index("x"), lax.axis_index("y"), lax.axis_index("c")


def _all_gather(name, arrs):
    n = len(arrs)

    def body(*refs):
        ins, outs = refs[:n], refs[n:2 * n]
        send_sems, recv_sems, local_sems = refs[2 * n:]
        x, y, c = _mesh_pos()
        me, sibling = (x, y, c), (x, y, 1 - c)
        chips = [(1 - x, y), (x, 1 - y), (1 - x, 1 - y)]

        def blk(t, p):
            return outs[t].at[4 * p[0] + 2 * p[1] + p[2]]

        def copy(t, k, block, to, src=None):
            return pltpu.make_async_remote_copy(
                src_ref=blk(t, block) if src is None else src, dst_ref=blk(t, block),
                send_sem=send_sems.at[7 * t + k], recv_sem=recv_sems.at[7 * t + k],
                device_id=to, device_id_type=MESH)

        mine, first, passed = [], [], []
        for t in range(n):
            cp = pltpu.make_async_copy(ins[t], blk(t, me), local_sems.at[t])
            cp.start()
            mine.append(cp)
            cps = [copy(t, 0, me, sibling, src=ins[t])]
            cps += [copy(t, 1 + j, me, (*chip, c), src=ins[t]) for j, chip in enumerate(chips)]
            for cp in cps:
                cp.start()
            first += cps
        for t in range(n):
            for j, chip in enumerate(chips):
                copy(t, 1 + j, (*chip, c), me).wait_recv()
                cp = copy(t, 4 + j, (*chip, c), sibling)
                cp.start()
                passed.append(cp)
        for t in range(n):
            copy(t, 0, sibling, me).wait_recv()
            for j, chip in enumerate(chips):
                copy(t, 4 + j, (*chip, 1 - c), me).wait_recv()
        for cp in first + passed:
            cp.wait_send()
        for cp in mine:
            cp.wait()

    any_spec = pl.BlockSpec(memory_space=pl.ANY)
    return _call(
        body, name=name,
        out_shape=[jax.ShapeDtypeStruct((N_DEV,) + a.shape, a.dtype) for a in arrs],
        in_specs=[any_spec] * n, out_specs=[any_spec] * n,
        scratch_shapes=[pltpu.SemaphoreType.DMA((7 * n,)), pltpu.SemaphoreType.DMA((7 * n,)),
                        pltpu.SemaphoreType.DMA((n,))],
    )(*arrs)


class _AllToAll:
    def __init__(self, whole):
        self.whole = list(whole)
        self.n = len(self.whole)

    def sem_shapes(self):
        return [pltpu.SemaphoreType.DMA((7 * self.n,)), pltpu.SemaphoreType.DMA((7 * self.n,)),
                pltpu.SemaphoreType.DMA((self.n,))]

    def out_shapes(self, arrs):
        return [jax.ShapeDtypeStruct(((N_DEV,) + a.shape) if w else a.shape, a.dtype) for a, w in zip(arrs, self.whole)]

    def _copies(self, ins, outs, sems):
        send_sems, recv_sems, local_sems = sems
        x, y, c = _mesh_pos()
        my_idx = 4 * x + 2 * y + c
        mine, sends, recvs = [], [], []
        for t in range(self.n):
            def src(idx):
                return ins[t] if self.whole[t] else ins[t].at[idx]
            mine.append(pltpu.make_async_copy(src(my_idx), outs[t].at[my_idx], local_sems.at[t]))
            for k in range(1, N_DEV):
                p = (1 - x if k & 4 else x, 1 - y if k & 2 else y, 1 - c if k & 1 else c)
                p_idx = 4 * p[0] + 2 * p[1] + p[2]
                for dst_idx, group in ((my_idx, sends), (p_idx, recvs)):
                    group.append(pltpu.make_async_remote_copy(
                        src_ref=src(p_idx), dst_ref=outs[t].at[dst_idx],
                        send_sem=send_sems.at[7 * t + k - 1], recv_sem=recv_sems.at[7 * t + k - 1],
                        device_id=p, device_id_type=MESH))
        return mine, sends, recvs

    def start(self, ins, outs, sems):
        mine, sends, _ = self._copies(ins, outs, sems)
        for cp in mine + sends:
            cp.start()

    def finish(self, ins, outs, sems):
        mine, sends, recvs = self._copies(ins, outs, sems)
        for cp in recvs:
            cp.wait_recv()
        for cp in sends:
            cp.wait_send()
        for cp in mine:
            cp.wait()


def _call_hosting(body, args, comm, comm_args, *, name, grid, in_specs, out_specs, out_shape, scratch_shapes):
    if comm is None:
        return _call(body, name=name, grid=grid, in_specs=in_specs, out_specs=out_specs, out_shape=out_shape,
                     scratch_shapes=scratch_shapes, compiler_params=_params(*["arbitrary"] * len(grid)))(*args), ()
    n, n_in, n_out, n_scr = comm.n, len(in_specs), len(out_specs), len(scratch_shapes)

    def hosted(*refs):
        ins, refs = refs[:n_in], refs[n_in:]
        c_ins, refs = refs[:n], refs[n:]
        outs, refs = refs[:n_out], refs[n_out:]
        c_outs, refs = refs[:n], refs[n:]
        scratch, sems = refs[:n_scr], refs[n_scr:]
        step = [pl.program_id(i) for i in range(len(grid))]

        def at(ends):
            hit = step[0] == ends[0]
            for sidx, e in zip(step[1:], ends[1:]):
                hit = jnp.logical_and(hit, sidx == e)
            return hit

        @pl.when(at([0] * len(grid)))
        def _():
            comm.start(c_ins, c_outs, sems)

        body(*ins, *outs, *scratch)

        @pl.when(at([g - 1 for g in grid]))
        def _():
            comm.finish(c_ins, c_outs, sems)

    any_spec = pl.BlockSpec(memory_space=pl.ANY)
    res = _call(hosted, name=name + "_hosting", grid=grid, in_specs=list(in_specs) + [any_spec] * n,
                out_specs=list(out_specs) + [any_spec] * n, out_shape=list(out_shape) + comm.out_shapes(comm_args),
                scratch_shapes=list(scratch_shapes) + comm.sem_shapes(),
                compiler_params=_params(*["arbitrary"] * len(grid)))(*args, *comm_args)
    return res[:n_out], res[n_out:]


def _mm(name, a, b, mode, tm, tn, tk, out_dtypes, epi=None, extra=(), exact=False, sent=()):
    if mode == "nn":
        (m, k), n = a.shape, b.shape[1]
    elif mode == "nt":
        (m, k), n = a.shape, b.shape[0]
    else:
        (k, m), n = a.shape, b.shape[1]
    tm, tn, tk = min(tm, m), min(tn, n), min(tk, k)
    nk = k // tk
    grid = (m // tm, n // tn, nk)
    n_extra, n_out = len(extra), len(out_dtypes)

    def body(*refs):
        a_ref, b_ref = refs[0], refs[1]
        extra_refs = refs[2:2 + n_extra]
        out_refs = refs[2 + n_extra:2 + n_extra + n_out]
        if exact:
            p = lax.dot_general(a_ref[...], b_ref[...], _DIMS[mode], preferred_element_type=F32,
                                precision=lax.Precision.HIGHEST)
        else:
            p = _dot(a_ref[...].astype(BF16), b_ref[...].astype(BF16), _DIMS[mode])

        def finish(acc):
            outs = (acc,) if epi is None else epi(acc, *[r[...] for r in extra_refs])
            for r, o in zip(out_refs, outs):
                r[...] = o.astype(r.dtype)

        if nk == 1:
            finish(p)
        else:
            acc_ref = refs[-1]
            kk = pl.program_id(2)

            @pl.when(kk == 0)
            def _():
                acc_ref[...] = p

            @pl.when(kk > 0)
            def _():
                acc_ref[...] += p

            @pl.when(kk == nk - 1)
            def _():
                finish(acc_ref[...])

    if mode == "tn":
        a_spec = pl.BlockSpec((tk, tm), lambda i, j, kk: (kk, i))
    else:
        a_spec = pl.BlockSpec((tm, tk), lambda i, j, kk: (i, kk))
    if mode == "nt":
        b_spec = pl.BlockSpec((tn, tk), lambda i, j, kk: (j, kk))
    else:
        b_spec = pl.BlockSpec((tk, tn), lambda i, j, kk: (kk, j))
    tile = pl.BlockSpec((tm, tn), lambda i, j, kk: (i, j))
    o_shape, o_spec = (m, n), tile
    out, parts = _call_hosting(
        body, (a, b, *extra), _AllToAll([False] * len(sent)) if sent else None, sent,
        name=name, grid=grid,
        in_specs=[a_spec, b_spec] + [tile] * n_extra,
        out_specs=[o_spec] * n_out,
        out_shape=[jax.ShapeDtypeStruct(o_shape, dt) for dt in out_dtypes],
        scratch_shapes=[pltpu.VMEM((tm, tn), F32)] if nk > 1 else [])
    out = out[0] if n_out == 1 else out
    return (out, parts) if sent else out


def _tile(width, col=0, rows=TM):
    return pl.BlockSpec((rows, width), lambda i: (i, col))


def _vec(width):
    return pl.BlockSpec((1, width), lambda i: (0, 0))


def _rstd(xf):
    return lax.rsqrt(jnp.mean(xf * xf, axis=-1, keepdims=True) + EPS)


def _colsum(v):
    return jnp.sum(v, axis=0, keepdims=True)


def _accumulate(refs, vals):
    first = pl.program_id(0) == 0

    @pl.when(first)
    def _():
        for r, v in zip(refs, vals):
            r[...] = v

    @pl.when(jnp.logical_not(first))
    def _():
        for r, v in zip(refs, vals):
            r[...] += v


def _prenorm_fwd(x, g, sc, sh):
    s, d = x.shape

    def body(x_ref, g_ref, sc_ref, sh_ref, h_ref):
        xf = x_ref[...]
        y = (xf * _rstd(xf)) * g_ref[...]
        h_ref[...] = (y * (1.0 + sc_ref[...]) + sh_ref[...]).astype(h_ref.dtype)

    return _call(body, name="prenorm_fwd", grid=(s // TM,),
                 in_specs=[_tile(d), _vec(d), _vec(d), _vec(d)], out_specs=_tile(d),
                 out_shape=jax.ShapeDtypeStruct((s, d), BF16), compiler_params=_params("parallel"))(x, g, sc, sh)


def _prenorm_bwd(dh, x, g, sc, dres):
    s, d = x.shape

    def body(dh_ref, x_ref, g_ref, sc_ref, dres_ref, dx_ref, dsh_ref, dsc_ref, dg_ref):
        xf, dhf = x_ref[...], dh_ref[...]
        rstd = _rstd(xf)
        xhat = xf * rstd
        one_sc = 1.0 + sc_ref[...]
        dxhat = dhf * (g_ref[...] * one_sc)
        dx_ref[...] = dres_ref[...] + rstd * (dxhat - xhat * jnp.mean(dxhat * xhat, axis=-1, keepdims=True))
        dhx = dhf * xhat
        _accumulate((dsh_ref, dsc_ref, dg_ref), (_colsum(dhf), _colsum(dhx) * g_ref[...], _colsum(dhx) * one_sc))

    vec_out = jax.ShapeDtypeStruct((1, d), F32)
    return _call(body, name="prenorm_bwd", grid=(s // TM,),
                 in_specs=[_tile(d), _tile(d), _vec(d), _vec(d), _tile(d)],
                 out_specs=[_tile(d), _vec(d), _vec(d), _vec(d)],
                 out_shape=[jax.ShapeDtypeStruct((s, d), F32), vec_out, vec_out, vec_out],
                 compiler_params=_params("arbitrary"))(dh, x, g, sc, dres)


def _postnorm_fwd(xres, m, g, gt):
    s, d = m.shape

    def body(x_ref, m_ref, g_ref, gt_ref, o_ref):
        mf = m_ref[...]
        o_ref[...] = x_ref[...] + gt_ref[...] * ((mf * _rstd(mf)) * g_ref[...])

    return _call(body, name="postnorm_fwd", grid=(s // TM,),
                 in_specs=[_tile(d), _tile(d), _vec(d), _vec(d)], out_specs=_tile(d),
                 out_shape=jax.ShapeDtypeStruct((s, d), F32), compiler_params=_params("parallel"))(xres, m, g, gt)


def _postnorm_bwd(dxn, m, g, gt):
    s, d = m.shape

    def body(dx_ref, m_ref, g_ref, gt_ref, dm_ref, dgt_ref, dg_ref):
        mf, dxf = m_ref[...], dx_ref[...]
        rstd = _rstd(mf)
        mhat = mf * rstd
        dmhat = dxf * (gt_ref[...] * g_ref[...])
        dm_ref[...] = (rstd * (dmhat - mhat * jnp.mean(dmhat * mhat, axis=-1, keepdims=True))).astype(dm_ref.dtype)
        dxm = _colsum(dxf * mhat)
        _accumulate((dgt_ref, dg_ref), (dxm * g_ref[...], dxm * gt_ref[...]))

    vec_out = jax.ShapeDtypeStruct((1, d), F32)
    return _call(body, name="postnorm_bwd", grid=(s // TM,),
                 in_specs=[_tile(d), _tile(d), _vec(d), _vec(d)], out_specs=[_tile(d), _vec(d), _vec(d)],
                 out_shape=[jax.ShapeDtypeStruct((s, d), BF16), vec_out, vec_out],
                 compiler_params=_params("arbitrary"))(dxn, m, g, gt)


def _loss(y, target):
    s, d = y.shape

    def body(y_ref, t_ref, dy_ref, sq_ref):
        err = y_ref[...] - t_ref[...]
        dy_ref[...] = err * (1.0 / d)
        tot = jnp.sum(_colsum(err * err), axis=1, keepdims=True)
        _accumulate((sq_ref,), (jnp.broadcast_to(tot, (1, LANES)),))

    return _call(body, name="loss", grid=(s // TM,), in_specs=[_tile(d), _tile(d)],
                 out_specs=[_tile(d), _vec(LANES)],
                 out_shape=[jax.ShapeDtypeStruct((s, d), F32), jax.ShapeDtypeStruct((1, LANES), F32)],
                 compiler_params=_params("arbitrary"))(y, target)


def _sigmoid(v):
    return 1.0 / (1.0 + jnp.exp(-v))


def _gate_fwd(proj, y_conv, y_attn):
    s, d = y_conv.shape
    ga_col, gb_col = (IN_COLS - 2 * d) // d, (IN_COLS - d) // d

    def body(ga_ref, gb_ref, yc_ref, ya_ref, o_ref):
        o_ref[...] = (_sigmoid(ga_ref[...]) * yc_ref[...] + _sigmoid(gb_ref[...]) * ya_ref[...]).astype(o_ref.dtype)

    return _call(body, name="gate_fwd", grid=(s // TM,),
                 in_specs=[_tile(d, ga_col), _tile(d, gb_col), _tile(d), _tile(d)], out_specs=_tile(d),
                 out_shape=jax.ShapeDtypeStruct((s, d), BF16),
                 compiler_params=_params("parallel"))(proj, proj, y_conv, y_attn)


def _gate_bwd(dmerged, proj, y_conv, y_attn):
    s, d = y_conv.shape
    ga_col, gb_col = (IN_COLS - 2 * d) // d, (IN_COLS - d) // d

    def body(dm_ref, ga_ref, gb_ref, yc_ref, ya_ref, dyc_ref, dya_ref, dga_ref, dgb_ref):
        dm = dm_ref[...].astype(F32)
        sa, sb = _sigmoid(ga_ref[...]), _sigmoid(gb_ref[...])
        dyc_ref[...] = (dm * sa).astype(BF16)
        dya_ref[...] = (dm * sb).astype(BF16)
        dga_ref[...] = (dm * yc_ref[...] * (sa * (1.0 - sa))).astype(BF16)
        dgb_ref[...] = (dm * ya_ref[...] * (sb * (1.0 - sb))).astype(BF16)

    out = jax.ShapeDtypeStruct((s, d), BF16)
    return _call(body, name="gate_bwd", grid=(s // TM,),
                 in_specs=[_tile(d), _tile(d, ga_col), _tile(d, gb_col), _tile(d), _tile(d)],
                 out_specs=[_tile(d)] * 4, out_shape=[out] * 4,
                 compiler_params=_params("parallel"))(dmerged, proj, proj, y_conv, y_attn)


def _shift_down(prev8, cur, by):
    ext = jnp.concatenate([prev8, cur], axis=0)
    return pltpu.roll(ext, by, 0)[8:]


def _shift_up(cur, next8, by):
    ext = jnp.concatenate([cur, next8], axis=0)
    return pltpu.roll(ext, ext.shape[0] - by, 0)[:cur.shape[0]]


def _conv_fwd(proj, conv_w):
    s, w = proj.shape[0], CONV_WIDTH
    per8 = TM // 8

    def prev(col):
        return pl.BlockSpec((8, w), lambda i: (jnp.maximum(i * per8 - 1, 0), col))

    def body(bg_ref, cg_ref, u_ref, cgp_ref, up_ref, w_ref, o_ref):
        vv = cg_ref[...] * u_ref[...]
        pv = cgp_ref[...] * up_ref[...] * jnp.where(pl.program_id(0) > 0, 1.0, 0.0)
        y = w_ref[0:1, :] * _shift_down(pv, vv, 2) + w_ref[1:2, :] * _shift_down(pv, vv, 1) + w_ref[2:3, :] * vv
        o_ref[...] = (bg_ref[...] * y).astype(o_ref.dtype)

    return _call(body, name="conv_fwd", grid=(s // TM,),
                 in_specs=[_tile(w, 0), _tile(w, 1), _tile(w, 2), prev(1), prev(2),
                           pl.BlockSpec((3, w), lambda i: (0, 0))],
                 out_specs=_tile(w), out_shape=jax.ShapeDtypeStruct((s, w), BF16),
                 compiler_params=_params("parallel"))(proj, proj, proj, proj, proj, conv_w)


def _conv_bwd(dyc, proj, conv_w):
    s, w = proj.shape[0], CONV_WIDTH
    per8 = TM // 8
    n_tiles = s // TM

    def prev(col):
        return pl.BlockSpec((8, w), lambda i: (jnp.maximum(i * per8 - 1, 0), col))

    def nxt(col):
        return pl.BlockSpec((8, w), lambda i: (jnp.minimum((i + 1) * per8, s // 8 - 1), col))

    def body(dyc_ref, bg_ref, cg_ref, u_ref, cgp_ref, up_ref, dycn_ref, bgn_ref, w_ref,
             dbg_ref, dcg_ref, du_ref, dw0_ref, dw1_ref, dw2_ref):
        i = pl.program_id(0)
        cg, u = cg_ref[...], u_ref[...]
        vv = cg * u
        pv = cgp_ref[...] * up_ref[...] * jnp.where(i > 0, 1.0, 0.0)
        v1, v2 = _shift_down(pv, vv, 1), _shift_down(pv, vv, 2)
        w0, w1, w2 = w_ref[0:1, :], w_ref[1:2, :], w_ref[2:3, :]
        dyc_t = dyc_ref[...]
        dbg_ref[...] = (dyc_t * (w0 * v2 + w1 * v1 + w2 * vv)).astype(BF16)
        dy = dyc_t * bg_ref[...]
        dyn = dycn_ref[...] * bgn_ref[...] * jnp.where(i < n_tiles - 1, 1.0, 0.0)
        dvv = w2 * dy + w1 * _shift_up(dy, dyn, 1) + w0 * _shift_up(dy, dyn, 2)
        dcg_ref[...] = (dvv * u).astype(BF16)
        du_ref[...] = (dvv * cg).astype(BF16)
        _accumulate((dw0_ref, dw1_ref, dw2_ref), (_colsum(dy * v2), _colsum(dy * v1), _colsum(dy * vv)))

    act = jax.ShapeDtypeStruct((s, w), BF16)
    tap = jax.ShapeDtypeStruct((1, w), F32)
    return _call(body, name="conv_bwd", grid=(n_tiles,),
                 in_specs=[_tile(w), _tile(w, 0), _tile(w, 1), _tile(w, 2), prev(1), prev(2), nxt(0), nxt(0),
                           pl.BlockSpec((3, w), lambda i: (0, 0))],
                 out_specs=[_tile(w)] * 3 + [_vec(w)] * 3, out_shape=[act] * 3 + [tap] * 3,
                 compiler_params=_params("arbitrary"))(dyc, proj, proj, proj, proj, proj, dyc, proj, conv_w)


Q_COL = 3 * CONV_WIDTH // LANES
K_COL = Q_COL + ATTN_WIDTH // LANES
V_COL = K_COL + ATTN_WIDTH // LANES
SCALE = HEAD_DIM ** -0.5


def _head_lanes(hh):
    lane = lax.broadcasted_iota(jnp.int32, (1, LANES), 1)
    return jnp.where((lane >= hh * HEAD_DIM) & (lane < (hh + 1) * HEAD_DIM), 1.0, 0.0)


W2 = 2 * TK
PIPE = 4


def _softplus(z):
    neg_abs = lax.bitcast_convert_type(lax.bitcast_convert_type(z, jnp.uint32) | jnp.uint32(0x80000000), F32)
    return jnp.maximum(z, 0.0) + jnp.log(1.0 + jnp.exp(neg_abs))


def _pair_kv(proj):
    s = proj.shape[0]
    per_tile = TM // TK
    width_blocks = ATTN_WIDTH // LANES

    def body(k_ref, v_ref, kp_ref, vp_ref):
        lane = lax.broadcasted_iota(jnp.int32, (1, ATTN_WIDTH), 1) & (LANES - 1)
        keep = [jnp.where(lane < HEAD_DIM, 1.0, 0.0), jnp.where(lane >= HEAD_DIM, 1.0, 0.0)]
        for src, dst in ((k_ref, kp_ref), (v_ref, vp_ref)):
            for gi in range(per_tile):
                blk = src[gi * TK:(gi + 1) * TK, :]
                for hh in range(2):
                    dst[(2 * gi + hh) * TK:(2 * gi + hh + 1) * TK, :] = (blk * keep[hh]).astype(BF16)

    out = jax.ShapeDtypeStruct((2 * s, ATTN_WIDTH), BF16)
    return _call(body, name="pair_kv", grid=(s // TM,),
                 in_specs=[_tile(ATTN_WIDTH, K_COL // width_blocks), _tile(ATTN_WIDTH, V_COL // width_blocks)],
                 out_specs=[_tile(ATTN_WIDTH, rows=2 * TM)] * 2, out_shape=[out, out],
                 compiler_params=_params("parallel"))(proj, proj)


def _pair_rows(ref, kb):
    return ref[pl.ds(pl.multiple_of(kb * W2, W2), W2), :]


def _pair_tri(keep):
    j = lax.broadcasted_iota(jnp.int32, (W2, 2 * W2), 0)
    s = lax.broadcasted_iota(jnp.int32, (W2, 2 * W2), 1)
    same_head = (j >= TK) == ((s & (W2 - 1)) >= TK)
    return jnp.where(same_head & ((s >= W2) | keep(j & (TK - 1), s & (TK - 1))), 1.0, 0.0).astype(BF16)


def _attn_fwd(proj, kp, vp, shards=()):
    s = proj.shape[0]
    nq = s // TQ
    diag = TQ // TK
    n_ch = TQ // CH
    assert s // TK <= TK and diag % PIPE == 0

    def body(q_ref, k_ref, v_ref, o_ref, rs_ref, qb_scr, tri_scr, z_scr, l0_scr, cs_scr, a_scr, r_scr, rall_scr, acc_scr):
        qi = pl.program_id(1)
        lane = lax.broadcasted_iota(jnp.int32, (CH, W2), 1) & (TK - 1)
        row = lax.broadcasted_iota(jnp.int32, (CH, 1), 0)
        col = lax.broadcasted_iota(jnp.int32, (1, W2), 1) & (TK - 1)
        qb_scr[...] = (q_ref[...] * SCALE).astype(BF16)
        tri_scr[...] = _pair_tri(lambda j, ss: j > ss)
        r_scr[...] = jnp.zeros_like(r_scr)
        rall_scr[...] = jnp.zeros_like(rall_scr)
        acc_scr[...] = jnp.zeros_like(acc_scr)

        def causal(kb, c):
            return (kb * TK + col) < (qi * TQ + c * CH + row)

        def chunks(r0):
            return range(0 if r0 is None else r0 // CH, n_ch)

        def on_diagonal(r0, c):
            return r0 is not None and c * CH < r0 + TK

        def logits(kb, zb, r0=0):
            z_scr[zb, r0:, :] = _dot(qb_scr[r0:, :], _pair_rows(k_ref, kb), NT)

        def log_one_minus_beta(kb, zb, lb, r0=None):
            for c in chunks(r0):
                rows = slice(c * CH, (c + 1) * CH)
                sp = _softplus(z_scr[zb, rows, :])
                if on_diagonal(r0, c):
                    sp = jnp.where(causal(kb, c), sp, 0.0)
                l0_scr[lb, rows, :] = sp.astype(BF16)

        def sums(lb, r0=0):
            cs_scr[r0:, :] = _dot(l0_scr[lb, r0:, :], tri_scr[...])

        def weights(kb, zb, lb, ab, r0=None):
            for c in chunks(r0):
                rows = slice(c * CH, (c + 1) * CH)
                near = r_scr[rows, :]
                a = jnp.exp(z_scr[zb, rows, :] - l0_scr[lb, rows, :].astype(F32) - cs_scr[rows, :W2] - near)
                if on_diagonal(r0, c):
                    a = jnp.where(causal(kb, c), a, 0.0)
                a_scr[ab, rows, :] = a.astype(BF16)
                rall_scr[rows, :] = jnp.where(lane == kb, near, rall_scr[rows, :])
                r_scr[rows, :] = near + cs_scr[rows, W2:]

        def weighted_values(kb, ab, r0=0):
            acc_scr[r0:, :] += _dot(a_scr[ab, r0:, :], _pair_rows(v_ref, kb))

        n = qi * diag
        for dd in reversed(range(diag)):
            kb, r0 = n + dd, dd * TK
            logits(kb, dd % PIPE, r0)
            log_one_minus_beta(kb, dd % PIPE, dd % 2, r0)
            sums(dd % 2, r0)
            weights(kb, dd % PIPE, dd % 2, dd % 2, r0)
            weighted_values(kb, dd % 2, r0)

        def block(j):
            return jnp.maximum(n - 1 - j, 0)

        a_scr[...] = jnp.zeros_like(a_scr)
        logits(block(0), 0)
        logits(block(1), 1)
        log_one_minus_beta(block(0), 0, 0)

        def trip(m, carry):
            for u in range(PIPE):
                j = PIPE * m + u
                weighted_values(block(j - 1), (u - 1) % 2)
                sums(u % 2)
                logits(block(j + 2), (u + 2) % PIPE)
                log_one_minus_beta(block(j + 1), (u + 1) % PIPE, (u + 1) % 2)
                weights(block(j), u % PIPE, u % 2, u % 2)
            return carry

        lax.fori_loop(0, n // PIPE, trip, 0)
        weighted_values(block(n - 1), (PIPE - 1) % 2)
        rs_ref[...] = rall_scr[...]
        o_ref[...] = acc_scr[...]

    (o, rsave), gathered = _call_hosting(
        body, (proj, kp, vp), _AllToAll([True] * len(shards)) if shards else None, shards,
        name="attn_fwd", grid=(N_HEADS // 2, nq),
        in_specs=[pl.BlockSpec((TQ, LANES), lambda p, qi: (qi, Q_COL + p)),
                  pl.BlockSpec((2 * s, LANES), lambda p, qi: (0, p)),
                  pl.BlockSpec((2 * s, LANES), lambda p, qi: (0, p))],
        out_specs=[pl.BlockSpec((TQ, LANES), lambda p, qi: (qi, p)),
                   pl.BlockSpec((TQ, W2), lambda p, qi: (qi, p))],
        out_shape=[jax.ShapeDtypeStruct((s, ATTN_WIDTH), F32), jax.ShapeDtypeStruct((s, N_HEADS // 2 * W2), F32)],
        scratch_shapes=[pltpu.VMEM((TQ, LANES), BF16), pltpu.VMEM((W2, 2 * W2), BF16),
                        pltpu.VMEM((PIPE, TQ, W2), F32), pltpu.VMEM((2, TQ, W2), BF16),
                        pltpu.VMEM((TQ, 2 * W2), F32), pltpu.VMEM((2, TQ, W2), BF16),
                        pltpu.VMEM((TQ, W2), F32), pltpu.VMEM((TQ, W2), F32), pltpu.VMEM((TQ, LANES), F32)])
    return o, rsave, gathered


def _attn_bwd(proj, kp, vp, do, rsave, sent=()):
    s = proj.shape[0]
    nq = s // TQ
    diag = TQ // TK
    n_ch = TQ // CH
    assert diag % PIPE == 0

    def body(q_ref, k_ref, v_ref, do_ref, rs_ref, dq_ref, dk_ref, dv_ref,
             qb_scr, dob_scr, after_scr, before_scr, z_scr, da_scr, l0_scr, beta_scr, cs_scr, a_scr, g_scr, cg_scr,
             dz_scr, pg_scr, dq_scr, dk_scr, dv_scr):
        qi = pl.program_id(1)

        @pl.when(qi == 0)
        def _():
            dk_scr[...] = jnp.zeros_like(dk_scr)
            dv_scr[...] = jnp.zeros_like(dv_scr)

        lm = [_head_lanes(0), _head_lanes(1)]
        lane = lax.broadcasted_iota(jnp.int32, (CH, TK), 1)
        row = lax.broadcasted_iota(jnp.int32, (CH, 1), 0)
        col = lax.broadcasted_iota(jnp.int32, (1, W2), 1) & (TK - 1)
        qb_scr[...] = (q_ref[...] * SCALE).astype(BF16)
        dob_scr[...] = do_ref[...].astype(BF16)
        after_scr[...] = _pair_tri(lambda j, ss: j > ss)[:, :W2]
        before_scr[...] = _pair_tri(lambda j, ss: j < ss)
        pg_scr[...] = jnp.zeros_like(pg_scr)
        dq_scr[...] = jnp.zeros_like(dq_scr)
        dz_scr[...] = jnp.zeros_like(dz_scr)

        def causal(kb, c):
            return (kb * TK + col) < (qi * TQ + c * CH + row)

        def chunks(r0):
            return range(0 if r0 is None else r0 // CH, n_ch)

        def on_diagonal(r0, c):
            return r0 is not None and c * CH < r0 + TK

        def logits(kb, zb, r0=0):
            z_scr[zb, r0:, :] = _dot(qb_scr[r0:, :], _pair_rows(k_ref, kb), NT)

        def do_dot_v(kb, db, r0=0):
            da_scr[db, r0:, :] = _dot(dob_scr[r0:, :], _pair_rows(v_ref, kb), NT)

        def gates(kb, zb, lb, bb, r0=None):
            for c in chunks(r0):
                rows = slice(c * CH, (c + 1) * CH)
                z = z_scr[zb, rows, :]
                sp = _softplus(z)
                beta_scr[bb, rows, :] = jnp.exp(z - sp)
                if on_diagonal(r0, c):
                    sp = jnp.where(causal(kb, c), sp, 0.0)
                l0_scr[lb, rows, :] = sp.astype(BF16)

        def suffix_sums(lb, r0=0):
            cs_scr[r0:, :] = _dot(l0_scr[lb, r0:, :], after_scr[...])

        def weights(kb, zb, lb, db, ab, r0=None):
            for c in chunks(r0):
                rows = slice(c * CH, (c + 1) * CH)
                keep = (kb * TK + lane) < (qi * TQ + c * CH + row) if on_diagonal(r0, c) else None
                for hh in range(2):
                    cols = slice(hh * TK, (hh + 1) * TK)
                    near = jnp.sum(jnp.where(lane == kb, rs_ref[rows, cols], 0.0), axis=1, keepdims=True)
                    a = jnp.exp(z_scr[zb, rows, cols] - l0_scr[lb, rows, cols].astype(F32) - cs_scr[rows, cols] - near)
                    if keep is not None:
                        a = jnp.where(keep, a, 0.0)
                    a_scr[ab, rows, cols] = a.astype(BF16)
                    g_scr[ab, rows, cols] = (a * da_scr[db, rows, cols]).astype(BF16)

        def prefix_sums(ab, r0=0):
            cg_scr[r0:, :] = _dot(g_scr[ab, r0:, :], before_scr[...])

        def dlogits(kb, ab, bb, zb2, r0=None):
            for c in chunks(r0):
                rows = slice(c * CH, (c + 1) * CH)
                earlier = pg_scr[rows, :]
                beta = beta_scr[bb, rows, :]
                g = g_scr[ab, rows, :].astype(F32)
                dz = g - beta * (g + cg_scr[rows, :W2] + earlier)
                if on_diagonal(r0, c):
                    dz = jnp.where(causal(kb, c), dz, 0.0)
                dz_scr[zb2, rows, :] = dz.astype(BF16)
                pg_scr[rows, :] = earlier + cg_scr[rows, W2:]

        def fold(t):
            return t[:TK, :] * lm[0] + t[TK:, :] * lm[1]

        def dq_dk(kb, zb2, r0=0):
            dq_scr[r0:, :] += _dot(dz_scr[zb2, r0:, :], _pair_rows(k_ref, kb))
            dk_scr[pl.ds(pl.multiple_of(kb * TK, TK), TK), :] += fold(_dot(dz_scr[zb2, r0:, :], qb_scr[r0:, :], TN))

        def dv(kb, ab, r0=0):
            dv_scr[pl.ds(pl.multiple_of(kb * TK, TK), TK), :] += fold(_dot(a_scr[ab, r0:, :], dob_scr[r0:, :], TN))

        n = qi * diag

        def block(j):
            return jnp.clip(j, 0, jnp.maximum(n - 1, 0))

        logits(block(0), 0)
        logits(block(1), 1)
        logits(block(2), 2)
        do_dot_v(block(0), 0)
        do_dot_v(block(1), 1)
        gates(block(0), 0, 0, 0)
        gates(block(1), 1, 1, 1)
        suffix_sums(0)
        weights(block(0), 0, 0, 0, 0)

        def trip(m, carry):
            for u in range(PIPE):
                t = PIPE * m + u
                dq_dk(block(t - 1), (u - 1) % 2)
                dv(block(t), u % 2)
                prefix_sums(u % 2)
                suffix_sums((u + 1) % 2)
                logits(block(t + 3), (u + 3) % PIPE)
                do_dot_v(block(t + 2), u % 2)
                gates(block(t + 2), (u + 2) % PIPE, u % 2, (u + 2) % PIPE)
                weights(block(t + 1), (u + 1) % PIPE, (u + 1) % 2, (u + 1) % 2, (u + 1) % 2)
                dlogits(block(t), u % 2, u % PIPE, u % 2)
            return carry

        lax.fori_loop(0, n // PIPE, trip, 0)
        dq_dk(block(n - 1), (PIPE - 1) % 2)

        for dd in range(diag):
            kb, r0, two, four = n + dd, dd * TK, dd % 2, dd % PIPE
            logits(kb, four, r0)
            do_dot_v(kb, two, r0)
            gates(kb, four, two, four, r0)
            suffix_sums(two, r0)
            weights(kb, four, two, two, two, r0)
            dv(kb, two, r0)
            prefix_sums(two, r0)
            dlogits(kb, two, four, two, r0)
            dq_dk(kb, two, r0)
        dq_ref[...] = (dq_scr[...] * SCALE).astype(dq_ref.dtype)

        @pl.when(qi == nq - 1)
        def _():
            dk_ref[...] = dk_scr[...].astype(dk_ref.dtype)
            dv_ref[...] = dv_scr[...].astype(dv_ref.dtype)

    def rows(c0):
        return pl.BlockSpec((TQ, LANES), lambda p, qi: (qi, c0 + p))

    def whole(c0):
        return pl.BlockSpec((s, LANES), lambda p, qi: (0, c0 + p))

    def f32(*shape):
        return pltpu.VMEM(shape, F32)

    def bf16(*shape):
        return pltpu.VMEM(shape, BF16)

    pairs = pl.BlockSpec((2 * s, LANES), lambda p, qi: (0, p))
    out = jax.ShapeDtypeStruct((s, ATTN_WIDTH), BF16)
    (dq, dk, dv), parts = _call_hosting(
        body, (proj, kp, vp, do, rsave), _AllToAll([False] * len(sent)) if sent else None, sent,
        name="attn_bwd", grid=(N_HEADS // 2, nq),
        in_specs=[rows(Q_COL), pairs, pairs, rows(0), pl.BlockSpec((TQ, W2), lambda p, qi: (qi, p))],
        out_specs=[rows(0), whole(0), whole(0)], out_shape=[out] * 3,
        scratch_shapes=[bf16(TQ, LANES), bf16(TQ, LANES), bf16(W2, W2), bf16(W2, 2 * W2),
                        f32(PIPE, TQ, W2), f32(2, TQ, W2), bf16(2, TQ, W2), f32(PIPE, TQ, W2), f32(TQ, W2),
                        bf16(2, TQ, W2), bf16(2, TQ, W2), f32(TQ, 2 * W2), bf16(2, TQ, W2),
                        f32(TQ, W2), f32(TQ, LANES), f32(s, LANES), f32(s, LANES)])
    return dq, dk, dv, parts


def _sum_adamw(name, parts, w, m, v, layer=None, into=None):
    n, r, c = parts.shape
    tr = r if r <= 256 else 256

    def body(p_ref, w_ref, m_ref, v_ref, g_ref, d_ref, nm_ref, nv_ref):
        g = p_ref[0].astype(F32)
        for j in range(1, n):
            g = g + p_ref[j].astype(F32)
        nm = ADAM_B1 * m_ref[...] + (1.0 - ADAM_B1) * g
        nv = ADAM_B2 * v_ref[...] + (1.0 - ADAM_B2) * (g * g)
        m_hat = nm / (1.0 - ADAM_B1 ** ADAM_STEP)
        v_hat = nv / (1.0 - ADAM_B2 ** ADAM_STEP)
        g_ref[...] = g
        d_ref[...] = -ADAM_LR * (m_hat / (jnp.sqrt(v_hat) + ADAM_EPS) + ADAM_WD * w_ref[...])
        nm_ref[...] = nm
        nv_ref[...] = nv

    if layer is None:
        mat = pl.BlockSpec((tr, c), lambda i: (i, 0))
        out = jax.ShapeDtypeStruct((r, c), F32)
    else:
        mat = pl.BlockSpec((None, tr, c), lambda i: (layer, i, 0))
        out = jax.ShapeDtypeStruct((DEPTH, r, c), F32)
    earlier = () if into is None else tuple(into)
    return _call(body if into is None else lambda *refs: body(*refs[:4], *refs[8:]),
                 name=name, grid=(r // tr,),
                 in_specs=[pl.BlockSpec((n, tr, c), lambda i: (0, i, 0)), mat, mat, mat]
                 + [pl.BlockSpec(memory_space=pl.ANY)] * len(earlier),
                 out_specs=[mat] * 4, out_shape=[out] * 4,
                 input_output_aliases={4 + k: k for k in range(len(earlier))},
                 compiler_params=_params("parallel"))(parts, w, m, v, *earlier)


def _natural(gathered):
    _, k, n = gathered.shape
    return gathered.transpose(1, 0, 2).reshape(k, N_DEV * n)


def _relu2_epi(acc):
    r = jnp.maximum(acc, 0.0)
    return acc, r * r


def _relu2_bwd_epi(acc, a_act):
    return (acc * (2.0 * jnp.maximum(a_act.astype(F32), 0.0)),)


def kernel(x, c, w_ada, b_ada, g_pre_mix, g_post_mix, g_pre_mlp, g_post_mlp, w_in, conv_w, w_proj_conv, w_proj_attn, w_out, w_mlp_in, w_mlp_out, loss_target, m_w_ada, m_b_ada, m_g_pre_mix, m_g_post_mix, m_g_pre_mlp, m_g_post_mlp, m_w_in, m_conv_w, m_w_proj_conv, m_w_proj_attn, m_w_out, m_w_mlp_in, m_w_mlp_out, v_w_ada, v_b_ada, v_g_pre_mix, v_g_post_mix, v_g_pre_mlp, v_g_post_mlp, v_w_in, v_conv_w, v_w_proj_conv, v_w_proj_attn, v_w_out, v_w_mlp_in, v_w_mlp_out):
    xi, yi, ci = _mesh_pos()
    me = 4 * xi + 2 * yi + ci
    d = D_MODEL
    x0 = x[0]
    seq = x0.shape[0]
    ada_cols = w_ada.shape[2]
    conv_cols = conv_w.shape[2]

    small = jnp.concatenate([c.reshape(-1), conv_w.reshape(-1)])
    small = jnp.pad(small, (0, 2 * d - small.shape[0])).reshape(8, 2 * d // 8)
    small_all = _all_gather("gather_c", [small])[0].reshape(N_DEV, 2 * d)
    c_all = small_all[:, :d]
    conv_all = small_all[:, d:d + DEPTH * 3 * conv_cols].reshape(N_DEV, DEPTH, 3, conv_cols)
    conv_all = conv_all.transpose(1, 2, 0, 3).reshape(DEPTH, 3, N_DEV * conv_cols)
    mod_cols = jnp.stack([_mm("mod_mm", c_all, w_ada[l], "nn", N_DEV, ada_cols, d, [F32], exact=True)
                          for l in range(DEPTH)], axis=1)
    mod_all = _all_gather("gather_mod", [mod_cols.reshape(N_DEV, DEPTH * ada_cols)])[0]
    mod_mine = lax.dynamic_index_in_dim(mod_all, me, axis=1, keepdims=False).reshape(N_DEV, DEPTH, ada_cols)
    mod = mod_mine.transpose(1, 0, 2).reshape(DEPTH, N_MOD * d) + b_ada

    sharded = {"w_in": w_in, "w_proj_conv": w_proj_conv, "w_proj_attn": w_proj_attn, "w_out": w_out,
               "w_mlp_in": w_mlp_in, "w_mlp_out": w_mlp_out}
    before_attention = ["w_in", "w_proj_conv"]

    def shard(key):
        nm, l = key
        return sharded[nm][l].astype(BF16)

    def natural(key, gathered):
        return gathered.reshape(-1, d) if key[0] in ("w_out", "w_mlp_out") else _natural(gathered)

    first = [(nm, 0) for nm in before_attention]
    full = {key: natural(key, g) for key, g in zip(first, _all_gather("gather_w", [shard(key) for key in first]))}
    saved = []
    xl = x0
    for l in range(DEPTH):
        riders = [(nm, l) for nm in sharded if (nm, l) not in full]
        if l + 1 < DEPTH:
            riders += [(nm, l + 1) for nm in sharded]
        sh1, sc1, gt1, sh2, sc2, gt2 = [mod[l:l + 1, i * d:(i + 1) * d] for i in range(N_MOD)]
        h = _prenorm_fwd(xl, g_pre_mix[l:l + 1], sc1, sh1)
        proj = _mm("proj", h, full[("w_in", l)], "nn", TMM, 1024, d, [F32])
        yc = _conv_fwd(proj, conv_all[l])
        y_conv = _mm("proj_conv", yc, full[("w_proj_conv", l)], "nn", TMM, d, CONV_WIDTH, [BF16])
        kp, vp = _pair_kv(proj)
        o, rsave, gathered = _attn_fwd(proj, kp, vp, [shard(key) for key in riders])
        full.update({key: natural(key, g) for key, g in zip(riders, gathered)})
        wg_in, wg_pc, wg_pa, wg_out, wg_mi, wg_mo = [full[(nm, l)] for nm in sharded]
        y_attn = _mm("proj_attn", o, wg_pa, "nn", TMM, d, ATTN_WIDTH, [BF16])
        merged = _gate_fwd(proj, y_conv, y_attn)
        mix_out = _mm("mix_out", merged, wg_out, "nn", TMM,d, d, [F32])
        x1 = _postnorm_fwd(xl, mix_out, g_post_mix[l:l + 1], gt1)
        h2 = _prenorm_fwd(x1, g_pre_mlp[l:l + 1], sc2, sh2)
        a_act, r = _mm("mlp_in", h2, wg_mi, "nn", TMM, 1024, d, [BF16, BF16], epi=_relu2_epi)
        ff = _mm("mlp_out", r, wg_mo, "nn", TMK, d, D_FF, [F32])
        x2 = _postnorm_fwd(x1, ff, g_post_mlp[l:l + 1], gt2)
        saved.append((xl, h, proj, yc, kp, vp, o, rsave, y_conv, y_attn, merged, mix_out, x1, h2, a_act, r, ff))
        xl = x2

    dxo, sq = _loss(xl, loss_target[0])
    loss = lax.psum(sq[0, 0] * (0.5 / d), ("x", "y", "c"))

    olds = {"w_in": (w_in, m_w_in, v_w_in), "w_proj_conv": (w_proj_conv, m_w_proj_conv, v_w_proj_conv),
            "w_proj_attn": (w_proj_attn, m_w_proj_attn, v_w_proj_attn), "w_out": (w_out, m_w_out, v_w_out),
            "w_mlp_in": (w_mlp_in, m_w_mlp_in, v_w_mlp_in), "w_mlp_out": (w_mlp_out, m_w_mlp_out, v_w_mlp_out)}
    big = {}
    pending = []

    def col_blocks(gw):
        k, n = gw.shape
        return gw.reshape(k, N_DEV, n // N_DEV).transpose(1, 0, 2)

    def update(entries, parts):
        for (nm, ll, _), part in zip(entries, parts):
            w_, m_, v_ = olds[nm]
            big[nm] = _sum_adamw("adamw_" + nm, part, w_, m_, v_, layer=ll, into=big.get(nm))

    dmod, small_grads = [None] * DEPTH, [None] * DEPTH
    for l in reversed(range(DEPTH)):
        wg_in, wg_pc, wg_pa, wg_out, wg_mi, wg_mo = [full[(nm, l)] for nm in sharded]
        xin, h, proj, yc, kp, vp, o, rsave, y_conv, y_attn, merged, mix_out, x1, h2, a_act, r, ff = saved[l]
        sh1, sc1, gt1, sh2, sc2, gt2 = [mod[l:l + 1, i * d:(i + 1) * d] for i in range(N_MOD)]

        dff, dgt2, dg_post_mlp = _postnorm_bwd(dxo, ff, g_post_mlp[l:l + 1], gt2)
        da = _mm("d_relu2", dff, wg_mo, "nt", TMM,1024, d, [BF16], epi=_relu2_bwd_epi, extra=(a_act,))
        gw_mo = _mm("gw_mlp_out", r, dff, "tn", 1024, d, TSK, [BF16])
        dh2 = _mm("d_h2", da, wg_mi, "nt", TMK, d, D_FF, [F32])
        gw_mi = _mm("gw_mlp_in", h2, da, "tn", d, 1024, TSK, [BF16])
        dx1, dsh2, dsc2, dg_pre_mlp = _prenorm_bwd(dh2, x1, g_pre_mlp[l:l + 1], sc2, dxo)

        dmix, dgt1, dg_post_mix = _postnorm_bwd(dx1, mix_out, g_post_mix[l:l + 1], gt1)
        dmerged = _mm("d_merged", dmix, wg_out, "nt", TMM, d, d, [BF16])
        gw_out = _mm("gw_out", merged, dmix, "tn", d, d, TSK, [BF16])
        dy_conv, dy_attn, dga, dgb = _gate_bwd(dmerged, proj, y_conv, y_attn)
        do = _mm("d_o", dy_attn, wg_pa, "nt", TMM, ATTN_WIDTH, d, [BF16])
        gw_pa = _mm("gw_proj_attn", o, dy_attn, "tn", ATTN_WIDTH, d, TSK, [BF16])
        dyc = _mm("d_yc", dy_conv, wg_pc, "nt", TMM,CONV_WIDTH, d, [F32])
        gw_pc = _mm("gw_proj_conv", yc, dy_conv, "tn", CONV_WIDTH, d, TSK, [BF16])
        pending += [("w_mlp_out", l, gw_mo.reshape(N_DEV, D_FF // N_DEV, d)), ("w_mlp_in", l, col_blocks(gw_mi)),
                    ("w_out", l, gw_out.reshape(N_DEV, d // N_DEV, d)), ("w_proj_attn", l, col_blocks(gw_pa)),
                    ("w_proj_conv", l, col_blocks(gw_pc))]
        if l == 0:
            dq, dk, dv, parts = _attn_bwd(proj, kp, vp, do, rsave, [blocks for _, _, blocks in pending])
            update(pending, parts)
            pending = []
        else:
            dq, dk, dv, _ = _attn_bwd(proj, kp, vp, do, rsave)
        dbg, dcg, du, dw0, dw1, dw2 = _conv_bwd(dyc, proj, conv_all[l])
        dproj = jnp.concatenate([dbg, dcg, du, dq, dk, dv, dga, dgb], axis=1)
        gw_in = _mm("gw_in", h, dproj, "tn", d, 1024, TSK, [BF16])
        pending.append(("w_in", l, col_blocks(gw_in)))
        if l == 0:
            dh, parts = _mm("d_h", dproj, wg_in, "nt", TMK, d, IN_COLS, [F32], sent=[blocks for _, _, blocks in pending])
            update(pending, parts)
            pending = []
        else:
            dh = _mm("d_h", dproj, wg_in, "nt", TMK, d, IN_COLS, [F32])
        dxo, dsh1, dsc1, dg_pre_mix = _prenorm_bwd(dh, xin, g_pre_mix[l:l + 1], sc1, dx1)

        dmod[l] = jnp.concatenate([dsh1, dsc1, dgt1, dsh2, dsc2, dgt2], axis=1)
        small_grads[l] = (dg_pre_mix, dg_post_mix, dg_pre_mlp, dg_post_mlp, jnp.concatenate([dw0, dw1, dw2], axis=0))
    assert not pending

    vec = jnp.concatenate(
        [dmod[l].reshape(-1) for l in range(DEPTH)]
        + [small_grads[l][i].reshape(-1) for i in range(4) for l in range(DEPTH)]
        + [small_grads[l][4].reshape(-1) for l in range(DEPTH)])
    n_vec = vec.shape[0]
    vec_all = _all_gather("gather_small", [vec.reshape(8, n_vec // 8)])[0].reshape(N_DEV, n_vec)
    n_mod = DEPTH * N_MOD * d
    dmod_all = vec_all[:, :n_mod].reshape(N_DEV, DEPTH, N_MOD * d)
    res = {}
    res["b_ada"] = _sum_adamw("adamw_b_ada", dmod_all, b_ada, m_b_ada, v_b_ada)
    off = n_mod
    for nm, (w_, m_, v_) in zip(
            ["g_pre_mix", "g_post_mix", "g_pre_mlp", "g_post_mlp"],
            [(g_pre_mix, m_g_pre_mix, v_g_pre_mix), (g_post_mix, m_g_post_mix, v_g_post_mix),
             (g_pre_mlp, m_g_pre_mlp, v_g_pre_mlp), (g_post_mlp, m_g_post_mlp, v_g_post_mlp)]):
        res[nm] = _sum_adamw("adamw_gain", vec_all[:, off:off + DEPTH * d].reshape(N_DEV, DEPTH, d), w_, m_, v_)
        off += DEPTH * d
    dconv_all = vec_all[:, off:].reshape(N_DEV, DEPTH * 3, CONV_WIDTH)
    dconv_mine = lax.dynamic_slice_in_dim(dconv_all, me * conv_cols, conv_cols, axis=2)
    res["conv_w"] = [t.reshape(DEPTH, 3, conv_cols) for t in _sum_adamw(
        "adamw_conv_w", dconv_mine, conv_w.reshape(DEPTH * 3, conv_cols), m_conv_w.reshape(DEPTH * 3, conv_cols),
        v_conv_w.reshape(DEPTH * 3, conv_cols))]

    c_t = jnp.pad(c_all.T, ((0, 0), (0, LANES - N_DEV)))
    dmod_mine = lax.dynamic_slice_in_dim(dmod_all, me * ada_cols, ada_cols, axis=2)
    for l in range(DEPTH):
        dm_l = jnp.pad(dmod_mine[:, l, :], ((0, LANES - N_DEV), (0, 0)))
        gw_ada = _mm("gw_ada", c_t, dm_l, "nn", 256, ada_cols, LANES, [F32], exact=True)
        res["w_ada"] = _sum_adamw("adamw_w_ada", gw_ada[None], w_ada, m_w_ada, v_w_ada, layer=l, into=res.get("w_ada"))
    res.update(big)

    order = ["w_ada", "b_ada", "g_pre_mix", "g_post_mix", "g_pre_mlp", "g_post_mlp", "w_in", "conv_w",
             "w_proj_conv", "w_proj_attn", "w_out", "w_mlp_in", "w_mlp_out"]
    outs = [loss, dxo[None]]
    for i in range(4):
        outs += [res[nm][i] for nm in order]
    return tuple(outs)
```

```python
import jax
import jax.numpy as jnp
from jax import lax
from jax.experimental import pallas as pl
from jax.experimental.pallas import tpu as pltpu

F32 = jnp.float32
BF16 = jnp.bfloat16
MESH = pl.DeviceIdType.MESH

N_DEV = 8
D_MODEL = 1024
CONV_WIDTH = 512
N_HEADS = 8
HEAD_DIM = 64
ATTN_WIDTH = N_HEADS * HEAD_DIM
D_FF = 4 * D_MODEL
N_MOD = 6
DEPTH = 2
EPS = 1e-6
IN_COLS = 3 * CONV_WIDTH + 3 * ATTN_WIDTH + 2 * D_MODEL
LANES = 128

ADAM_LR = 0.001
ADAM_B1 = 0.9
ADAM_B2 = 0.999
ADAM_EPS = 1e-08
ADAM_WD = 0.01
ADAM_STEP = 10

TM = 1024
TMM = 2048
TMK = 1024
TSK = 2048
TQ = 512
TK = 128
CH = 64
VMEM_LIMIT = 56 * 1024 * 1024

NN = (((1,), (0,)), ((), ()))
NT = (((1,), (1,)), ((), ()))
TN = (((0,), (0,)), ((), ()))
_DIMS = {"nn": NN, "nt": NT, "tn": TN}


def _call(body, **kw):
    return pl.pallas_call(body, **kw)


def _params(*sem):
    return pltpu.CompilerParams(dimension_semantics=sem, vmem_limit_bytes=VMEM_LIMIT)


def _dot(a, b, dims=NN):
    return lax.dot_general(a, b, dims, preferred_element_type=F32)


def _mesh_pos():
    return lax.axis_index("x"), lax.axis_index("y"), lax.axis_index("c")


def _all_gather(name, arrs):
    n = len(arrs)

    def body(*refs):
        ins, outs = refs[:n], refs[n:2 * n]
        send_sems, recv_sems, local_sems = refs[2 * n:]
        x, y, c = _mesh_pos()
        me, sibling = (x, y, c), (x, y, 1 - c)
        chips = [(1 - x, y), (x, 1 - y), (1 - x, 1 - y)]

        def blk(t, p):
            return outs[t].at[4 * p[0] + 2 * p[1] + p[2]]

        def copy(t, k, block, to, src=None):
            return pltpu.make_async_remote_copy(
                src_ref=blk(t, block) if src is None else src, dst_ref=blk(t, block),
                send_sem=send_sems.at[7 * t + k], recv_sem=recv_sems.at[7 * t + k],
                device_id=to, device_id_type=MESH)

        mine, first, passed = [], [], []
        for t in range(n):
            cp = pltpu.make_async_copy(ins[t], blk(t, me), local_sems.at[t])
            cp.start()
            mine.append(cp)
            cps = [copy(t, 0, me, sibling, src=ins[t])]
            cps += [copy(t, 1 + j, me, (*chip, c), src=ins[t]) for j, chip in enumerate(chips)]
            for cp in cps:
                cp.start()
            first += cps
        for t in range(n):
            for j, chip in enumerate(chips):
                copy(t, 1 + j, (*chip, c), me).wait_recv()
                cp = copy(t, 4 + j, (*chip, c), sibling)
                cp.start()
                passed.append(cp)
        for t in range(n):
            copy(t, 0, sibling, me).wait_recv()
            for j, chip in enumerate(chips):
                copy(t, 4 + j, (*chip, 1 - c), me).wait_recv()
        for cp in first + passed:
            cp.wait_send()
        for cp in mine:
            cp.wait()

    any_spec = pl.BlockSpec(memory_space=pl.ANY)
    return _call(
        body, name=name,
        out_shape=[jax.ShapeDtypeStruct((N_DEV,) + a.shape, a.dtype) for a in arrs],
        in_specs=[any_spec] * n, out_specs=[any_spec] * n,
        scratch_shapes=[pltpu.SemaphoreType.DMA((7 * n,)), pltpu.SemaphoreType.DMA((7 * n,)),
                        pltpu.SemaphoreType.DMA((n,))],
    )(*arrs)


class _AllToAll:
    def __init__(self, whole):
        self.whole = list(whole)
        self.n = len(self.whole)

    def sem_shapes(self):
        return [pltpu.SemaphoreType.DMA((7 * self.n,)), pltpu.SemaphoreType.DMA((7 * self.n,)),
                pltpu.SemaphoreType.DMA((self.n,))]

    def out_shapes(self, arrs):
        return [jax.ShapeDtypeStruct(((N_DEV,) + a.shape) if w else a.shape, a.dtype) for a, w in zip(arrs, self.whole)]

    def _copies(self, ins, outs, sems):
        send_sems, recv_sems, local_sems = sems
        x, y, c = _mesh_pos()
        my_idx = 4 * x + 2 * y + c
        mine, sends, recvs = [], [], []
        for t in range(self.n):
            def src(idx):
                return ins[t] if self.whole[t] else ins[t].at[idx]
            mine.append(pltpu.make_async_copy(src(my_idx), outs[t].at[my_idx], local_sems.at[t]))
            for k in range(1, N_DEV):
                p = (1 - x if k & 4 else x, 1 - y if k & 2 else y, 1 - c if k & 1 else c)
                p_idx = 4 * p[0] + 2 * p[1] + p[2]
                for dst_idx, group in ((my_idx, sends), (p_idx, recvs)):
                    group.append(pltpu.make_async_remote_copy(
                        src_ref=src(p_idx), dst_ref=outs[t].at[dst_idx],
                        send_sem=send_sems.at[7 * t + k - 1], recv_sem=recv_sems.at[7 * t + k - 1],
                        device_id=p, device_id_type=MESH))
        return mine, sends, recvs

    def start(self, ins, outs, sems):
        mine, sends, _ = self._copies(ins, outs, sems)
        for cp in mine + sends:
            cp.start()

    def finish(self, ins, outs, sems):
        mine, sends, recvs = self._copies(ins, outs, sems)
        for cp in recvs:
            cp.wait_recv()
        for cp in sends:
            cp.wait_send()
        for cp in mine:
            cp.wait()


def _call_hosting(body, args, comm, comm_args, *, name, grid, in_specs, out_specs, out_shape, scratch_shapes):
    if comm is None:
        return _call(body, name=name, grid=grid, in_specs=in_specs, out_specs=out_specs, out_shape=out_shape,
                     scratch_shapes=scratch_shapes, compiler_params=_params(*["arbitrary"] * len(grid)))(*args), ()
    n, n_in, n_out, n_scr = comm.n, len(in_specs), len(out_specs), len(scratch_shapes)

    def hosted(*refs):
        ins, refs = refs[:n_in], refs[n_in:]
        c_ins, refs = refs[:n], refs[n:]
        outs, refs = refs[:n_out], refs[n_out:]
        c_outs, refs = refs[:n], refs[n:]
        scratch, sems = refs[:n_scr], refs[n_scr:]
        step = [pl.program_id(i) for i in range(len(grid))]

        def at(ends):
            hit = step[0] == ends[0]
            for sidx, e in zip(step[1:], ends[1:]):
                hit = jnp.logical_and(hit, sidx == e)
            return hit

        @pl.when(at([0] * len(grid)))
        def _():
            comm.start(c_ins, c_outs, sems)

        body(*ins, *outs, *scratch)

        @pl.when(at([g - 1 for g in grid]))
        def _():
            comm.finish(c_ins, c_outs, sems)

    any_spec = pl.BlockSpec(memory_space=pl.ANY)
    res = _call(hosted, name=name + "_hosting", grid=grid, in_specs=list(in_specs) + [any_spec] * n,
                out_specs=list(out_specs) + [any_spec] * n, out_shape=list(out_shape) + comm.out_shapes(comm_args),
                scratch_shapes=list(scratch_shapes) + comm.sem_shapes(),
                compiler_params=_params(*["arbitrary"] * len(grid)))(*args, *comm_args)
    return res[:n_out], res[n_out:]


def _mm(name, a, b, mode, tm, tn, tk, out_dtypes, epi=None, extra=(), exact=False, sent=()):
    if mode == "nn":
        (m, k), n = a.shape, b.shape[1]
    elif mode == "nt":
        (m, k), n = a.shape, b.shape[0]
    else:
        (k, m), n = a.shape, b.shape[1]
    tm, tn, tk = min(tm, m), min(tn, n), min(tk, k)
    nk = k // tk
    grid = (m // tm, n // tn, nk)
    n_extra, n_out = len(extra), len(out_dtypes)

    def body(*refs):
        a_ref, b_ref = refs[0], refs[1]
        extra_refs = refs[2:2 + n_extra]
        out_refs = refs[2 + n_extra:2 + n_extra + n_out]
        if exact:
            p = lax.dot_general(a_ref[...], b_ref[...], _DIMS[mode], preferred_element_type=F32,
                                precision=lax.Precision.HIGHEST)
        else:
            p = _dot(a_ref[...].astype(BF16), b_ref[...].astype(BF16), _DIMS[mode])

        def finish(acc):
            outs = (acc,) if epi is None else epi(acc, *[r[...] for r in extra_refs])
            for r, o in zip(out_refs, outs):
                r[...] = o.astype(r.dtype)

        if nk == 1:
            finish(p)
        else:
            acc_ref = refs[-1]
            kk = pl.program_id(2)

            @pl.when(kk == 0)
            def _():
                acc_ref[...] = p

            @pl.when(kk > 0)
            def _():
                acc_ref[...] += p

            @pl.when(kk == nk - 1)
            def _():
                finish(acc_ref[...])

    if mode == "tn":
        a_spec = pl.BlockSpec((tk, tm), lambda i, j, kk: (kk, i))
    else:
        a_spec = pl.BlockSpec((tm, tk), lambda i, j, kk: (i, kk))
    if mode == "nt":
        b_spec = pl.BlockSpec((tn, tk), lambda i, j, kk: (j, kk))
    else:
        b_spec = pl.BlockSpec((tk, tn), lambda i, j, kk: (kk, j))
    tile = pl.BlockSpec((tm, tn), lambda i, j, kk: (i, j))
    o_shape, o_spec = (m, n), tile
    out, parts = _call_hosting(
        body, (a, b, *extra), _AllToAll([False] * len(sent)) if sent else None, sent,
        name=name, grid=grid,
        in_specs=[a_spec, b_spec] + [tile] * n_extra,
        out_specs=[o_spec] * n_out,
        out_shape=[jax.ShapeDtypeStruct(o_shape, dt) for dt in out_dtypes],
        scratch_shapes=[pltpu.VMEM((tm, tn), F32)] if nk > 1 else [])
    out = out[0] if n_out == 1 else out
    return (out, parts) if sent else out


def _tile(width, col=0, rows=TM):
    return pl.BlockSpec((rows, width), lambda i: (i, col))


def _vec(width):
    return pl.BlockSpec((1, width), lambda i: (0, 0))


def _rstd(xf):
    return lax.rsqrt(jnp.mean(xf * xf, axis=-1, keepdims=True) + EPS)


def _colsum(v):
    return jnp.sum(v, axis=0, keepdims=True)


def _accumulate(refs, vals):
    first = pl.program_id(0) == 0

    @pl.when(first)
    def _():
        for r, v in zip(refs, vals):
            r[...] = v

    @pl.when(jnp.logical_not(first))
    def _():
        for r, v in zip(refs, vals):
            r[...] += v


def _prenorm_fwd(x, g, sc, sh):
    s, d = x.shape

    def body(x_ref, g_ref, sc_ref, sh_ref, h_ref):
        xf = x_ref[...]
        y = (xf * _rstd(xf)) * g_ref[...]
        h_ref[...] = (y * (1.0 + sc_ref[...]) + sh_ref[...]).astype(h_ref.dtype)

    return _call(body, name="prenorm_fwd", grid=(s // TM,),
                 in_specs=[_tile(d), _vec(d), _vec(d), _vec(d)], out_specs=_tile(d),
                 out_shape=jax.ShapeDtypeStruct((s, d), BF16), compiler_params=_params("parallel"))(x, g, sc, sh)


def _prenorm_bwd(dh, x, g, sc, dres):
    s, d = x.shape

    def body(dh_ref, x_ref, g_ref, sc_ref, dres_ref, dx_ref, dsh_ref, dsc_ref, dg_ref):
        xf, dhf = x_ref[...], dh_ref[...]
        rstd = _rstd(xf)
        xhat = xf * rstd
        one_sc = 1.0 + sc_ref[...]
        dxhat = dhf * (g_ref[...] * one_sc)
        dx_ref[...] = dres_ref[...] + rstd * (dxhat - xhat * jnp.mean(dxhat * xhat, axis=-1, keepdims=True))
        dhx = dhf * xhat
        _accumulate((dsh_ref, dsc_ref, dg_ref), (_colsum(dhf), _colsum(dhx) * g_ref[...], _colsum(dhx) * one_sc))

    vec_out = jax.ShapeDtypeStruct((1, d), F32)
    return _call(body, name="prenorm_bwd", grid=(s // TM,),
                 in_specs=[_tile(d), _tile(d), _vec(d), _vec(d), _tile(d)],
                 out_specs=[_tile(d), _vec(d), _vec(d), _vec(d)],
                 out_shape=[jax.ShapeDtypeStruct((s, d), F32), vec_out, vec_out, vec_out],
                 compiler_params=_params("arbitrary"))(dh, x, g, sc, dres)


def _postnorm_fwd(xres, m, g, gt):
    s, d = m.shape

    def body(x_ref, m_ref, g_ref, gt_ref, o_ref):
        mf = m_ref[...]
        o_ref[...] = x_ref[...] + gt_ref[...] * ((mf * _rstd(mf)) * g_ref[...])

    return _call(body, name="postnorm_fwd", grid=(s // TM,),
                 in_specs=[_tile(d), _tile(d), _vec(d), _vec(d)], out_specs=_tile(d),
                 out_shape=jax.ShapeDtypeStruct((s, d), F32), compiler_params=_params("parallel"))(xres, m, g, gt)


def _postnorm_bwd(dxn, m, g, gt):
    s, d = m.shape

    def body(dx_ref, m_ref, g_ref, gt_ref, dm_ref, dgt_ref, dg_ref):
        mf, dxf = m_ref[...], dx_ref[...]
        rstd = _rstd(mf)
        mhat = mf * rstd
        dmhat = dxf * (gt_ref[...] * g_ref[...])
        dm_ref[...] = (rstd * (dmhat - mhat * jnp.mean(dmhat * mhat, axis=-1, keepdims=True))).astype(dm_ref.dtype)
        dxm = _colsum(dxf * mhat)
        _accumulate((dgt_ref, dg_ref), (dxm * g_ref[...], dxm * gt_ref[...]))

    vec_out = jax.ShapeDtypeStruct((1, d), F32)
    return _call(body, name="postnorm_bwd", grid=(s // TM,),
                 in_specs=[_tile(d), _tile(d), _vec(d), _vec(d)], out_specs=[_tile(d), _vec(d), _vec(d)],
                 out_shape=[jax.ShapeDtypeStruct((s, d), BF16), vec_out, vec_out],
                 compiler_params=_params("arbitrary"))(dxn, m, g, gt)


def _loss(y, target):
    s, d = y.shape

    def body(y_ref, t_ref, dy_ref, sq_ref):
        err = y_ref[...] - t_ref[...]
        dy_ref[...] = err * (1.0 / d)
        tot = jnp.sum(_colsum(err * err), axis=1, keepdims=True)
        _accumulate((sq_ref,), (jnp.broadcast_to(tot, (1, LANES)),))

    return _call(body, name="loss", grid=(s // TM,), in_specs=[_tile(d), _tile(d)],
                 out_specs=[_tile(d), _vec(LANES)],
                 out_shape=[jax.ShapeDtypeStruct((s, d), F32), jax.ShapeDtypeStruct((1, LANES), F32)],
                 compiler_params=_params("arbitrary"))(y, target)


def _sigmoid(v):
    return 1.0 / (1.0 + jnp.exp(-v))


def _gate_fwd(proj, y_conv, y_attn):
    s, d = y_conv.shape
    ga_col, gb_col = (IN_COLS - 2 * d) // d, (IN_COLS - d) // d

    def body(ga_ref, gb_ref, yc_ref, ya_ref, o_ref):
        o_ref[...] = (_sigmoid(ga_ref[...]) * yc_ref[...] + _sigmoid(gb_ref[...]) * ya_ref[...]).astype(o_ref.dtype)

    return _call(body, name="gate_fwd", grid=(s // TM,),
                 in_specs=[_tile(d, ga_col), _tile(d, gb_col), _tile(d), _tile(d)], out_specs=_tile(d),
                 out_shape=jax.ShapeDtypeStruct((s, d), BF16),
                 compiler_params=_params("parallel"))(proj, proj, y_conv, y_attn)


def _gate_bwd(dmerged, proj, y_conv, y_attn):
    s, d = y_conv.shape
    ga_col, gb_col = (IN_COLS - 2 * d) // d, (IN_COLS - d) // d

    def body(dm_ref, ga_ref, gb_ref, yc_ref, ya_ref, dyc_ref, dya_ref, dga_ref, dgb_ref):
        dm = dm_ref[...].astype(F32)
        sa, sb = _sigmoid(ga_ref[...]), _sigmoid(gb_ref[...])
        dyc_ref[...] = (dm * sa).astype(BF16)
        dya_ref[...] = (dm * sb).astype(BF16)
        dga_ref[...] = (dm * yc_ref[...] * (sa * (1.0 - sa))).astype(BF16)
        dgb_ref[...] = (dm * ya_ref[...] * (sb * (1.0 - sb))).astype(BF16)

    out = jax.ShapeDtypeStruct((s, d), BF16)
    return _call(body, name="gate_bwd", grid=(s // TM,),
                 in_specs=[_tile(d), _tile(d, ga_col), _tile(d, gb_col), _tile(d), _tile(d)],
                 out_specs=[_tile(d)] * 4, out_shape=[out] * 4,
                 compiler_params=_params("parallel"))(dmerged, proj, proj, y_conv, y_attn)


def _shift_down(prev8, cur, by):
    ext = jnp.concatenate([prev8, cur], axis=0)
    return pltpu.roll(ext, by, 0)[8:]


def _shift_up(cur, next8, by):
    ext = jnp.concatenate([cur, next8], axis=0)
    return pltpu.roll(ext, ext.shape[0] - by, 0)[:cur.shape[0]]


def _conv_fwd(proj, conv_w):
    s, w = proj.shape[0], CONV_WIDTH
    per8 = TM // 8

    def prev(col):
        return pl.BlockSpec((8, w), lambda i: (jnp.maximum(i * per8 - 1, 0), col))

    def body(bg_ref, cg_ref, u_ref, cgp_ref, up_ref, w_ref, o_ref):
        vv = cg_ref[...] * u_ref[...]
        pv = cgp_ref[...] * up_ref[...] * jnp.where(pl.program_id(0) > 0, 1.0, 0.0)
        y = w_ref[0:1, :] * _shift_down(pv, vv, 2) + w_ref[1:2, :] * _shift_down(pv, vv, 1) + w_ref[2:3, :] * vv
        o_ref[...] = (bg_ref[...] * y).astype(o_ref.dtype)

    return _call(body, name="conv_fwd", grid=(s // TM,),
                 in_specs=[_tile(w, 0), _tile(w, 1), _tile(w, 2), prev(1), prev(2),
                           pl.BlockSpec((3, w), lambda i: (0, 0))],
                 out_specs=_tile(w), out_shape=jax.ShapeDtypeStruct((s, w), BF16),
                 compiler_params=_params("parallel"))(proj, proj, proj, proj, proj, conv_w)


def _conv_bwd(dyc, proj, conv_w):
    s, w = proj.shape[0], CONV_WIDTH
    per8 = TM // 8
    n_tiles = s // TM

    def prev(col):
        return pl.BlockSpec((8, w), lambda i: (jnp.maximum(i * per8 - 1, 0), col))

    def nxt(col):
        return pl.BlockSpec((8, w), lambda i: (jnp.minimum((i + 1) * per8, s // 8 - 1), col))

    def body(dyc_ref, bg_ref, cg_ref, u_ref, cgp_ref, up_ref, dycn_ref, bgn_ref, w_ref,
             dbg_ref, dcg_ref, du_ref, dw0_ref, dw1_ref, dw2_ref):
        i = pl.program_id(0)
        cg, u = cg_ref[...], u_ref[...]
        vv = cg * u
        pv = cgp_ref[...] * up_ref[...] * jnp.where(i > 0, 1.0, 0.0)
        v1, v2 = _shift_down(pv, vv, 1), _shift_down(pv, vv, 2)
        w0, w1, w2 = w_ref[0:1, :], w_ref[1:2, :], w_ref[2:3, :]
        dyc_t = dyc_ref[...]
        dbg_ref[...] = (dyc_t * (w0 * v2 + w1 * v1 + w2 * vv)).astype(BF16)
        dy = dyc_t * bg_ref[...]
        dyn = dycn_ref[...] * bgn_ref[...] * jnp.where(i < n_tiles - 1, 1.0, 0.0)
        dvv = w2 * dy + w1 * _shift_up(dy, dyn, 1) + w0 * _shift_up(dy, dyn, 2)
        dcg_ref[...] = (dvv * u).astype(BF16)
        du_ref[...] = (dvv * cg).astype(BF16)
        _accumulate((dw0_ref, dw1_ref, dw2_ref), (_colsum(dy * v2), _colsum(dy * v1), _colsum(dy * vv)))

    act = jax.ShapeDtypeStruct((s, w), BF16)
    tap = jax.ShapeDtypeStruct((1, w), F32)
    return _call(body, name="conv_bwd", grid=(n_tiles,),
                 in_specs=[_tile(w), _tile(w, 0), _tile(w, 1), _tile(w, 2), prev(1), prev(2), nxt(0), nxt(0),
                           pl.BlockSpec((3, w), lambda i: (0, 0))],
                 out_specs=[_tile(w)] * 3 + [_vec(w)] * 3, out_shape=[act] * 3 + [tap] * 3,
                 compiler_params=_params("arbitrary"))(dyc, proj, proj, proj, proj, proj, dyc, proj, conv_w)


Q_COL = 3 * CONV_WIDTH // LANES
K_COL = Q_COL + ATTN_WIDTH // LANES
V_COL = K_COL + ATTN_WIDTH // LANES
SCALE = HEAD_DIM ** -0.5


def _head_lanes(hh):
    lane = lax.broadcasted_iota(jnp.int32, (1, LANES), 1)
    return jnp.where((lane >= hh * HEAD_DIM) & (lane < (hh + 1) * HEAD_DIM), 1.0, 0.0)


W2 = 2 * TK
PIPE = 4


def _softplus(z):
    neg_abs = lax.bitcast_convert_type(lax.bitcast_convert_type(z, jnp.uint32) | jnp.uint32(0x80000000), F32)
    return jnp.maximum(z, 0.0) + jnp.log(1.0 + jnp.exp(neg_abs))


def _pair_kv(proj):
    s = proj.shape[0]
    per_tile = TM // TK
    width_blocks = ATTN_WIDTH // LANES

    def body(k_ref, v_ref, kp_ref, vp_ref):
        lane = lax.broadcasted_iota(jnp.int32, (1, ATTN_WIDTH), 1) & (LANES - 1)
        keep = [jnp.where(lane < HEAD_DIM, 1.0, 0.0), jnp.where(lane >= HEAD_DIM, 1.0, 0.0)]
        for src, dst in ((k_ref, kp_ref), (v_ref, vp_ref)):
            for gi in range(per_tile):
                blk = src[gi * TK:(gi + 1) * TK, :]
                for hh in range(2):
                    dst[(2 * gi + hh) * TK:(2 * gi + hh + 1) * TK, :] = (blk * keep[hh]).astype(BF16)

    out = jax.ShapeDtypeStruct((2 * s, ATTN_WIDTH), BF16)
    return _call(body, name="pair_kv", grid=(s // TM,),
                 in_specs=[_tile(ATTN_WIDTH, K_COL // width_blocks), _tile(ATTN_WIDTH, V_COL // width_blocks)],
                 out_specs=[_tile(ATTN_WIDTH, rows=2 * TM)] * 2, out_shape=[out, out],
                 compiler_params=_params("parallel"))(proj, proj)


def _pair_rows(ref, kb):
    return ref[pl.ds(pl.multiple_of(kb * W2, W2), W2), :]


def _pair_tri(keep):
    j = lax.broadcasted_iota(jnp.int32, (W2, 2 * W2), 0)
    s = lax.broadcasted_iota(jnp.int32, (W2, 2 * W2), 1)
    same_head = (j >= TK) == ((s & (W2 - 1)) >= TK)
    return jnp.where(same_head & ((s >= W2) | keep(j & (TK - 1), s & (TK - 1))), 1.0, 0.0).astype(BF16)


def _attn_fwd(proj, kp, vp, shards=()):
    s = proj.shape[0]
    nq = s // TQ
    diag = TQ // TK
    n_ch = TQ // CH
    assert s // TK <= TK and diag % PIPE == 0

    def body(q_ref, k_ref, v_ref, o_ref, rs_ref, qb_scr, tri_scr, z_scr, l0_scr, cs_scr, a_scr, r_scr, rall_scr, acc_scr):
        qi = pl.program_id(1)
        lane = lax.broadcasted_iota(jnp.int32, (CH, W2), 1) & (TK - 1)
        row = lax.broadcasted_iota(jnp.int32, (CH, 1), 0)
        col = lax.broadcasted_iota(jnp.int32, (1, W2), 1) & (TK - 1)
        qb_scr[...] = (q_ref[...] * SCALE).astype(BF16)
        tri_scr[...] = _pair_tri(lambda j, ss: j > ss)
        r_scr[...] = jnp.zeros_like(r_scr)
        rall_scr[...] = jnp.zeros_like(rall_scr)
        acc_scr[...] = jnp.zeros_like(acc_scr)

        def causal(kb, c):
            return (kb * TK + col) < (qi * TQ + c * CH + row)

        def chunks(r0):
            return range(0 if r0 is None else r0 // CH, n_ch)

        def on_diagonal(r0, c):
            return r0 is not None and c * CH < r0 + TK

        def logits(kb, zb, r0=0):
            z_scr[zb, r0:, :] = _dot(qb_scr[r0:, :], _pair_rows(k_ref, kb), NT)

        def log_one_minus_beta(kb, zb, lb, r0=None):
            for c in chunks(r0):
                rows = slice(c * CH, (c + 1) * CH)
                sp = _softplus(z_scr[zb, rows, :])
                if on_diagonal(r0, c):
                    sp = jnp.where(causal(kb, c), sp, 0.0)
                l0_scr[lb, rows, :] = sp.astype(BF16)

        def sums(lb, r0=0):
            cs_scr[r0:, :] = _dot(l0_scr[lb, r0:, :], tri_scr[...])

        def weights(kb, zb, lb, ab, r0=None):
            for c in chunks(r0):
                rows = slice(c * CH, (c + 1) * CH)
                near = r_scr[rows, :]
                a = jnp.exp(z_scr[zb, rows, :] - l0_scr[lb, rows, :].astype(F32) - cs_scr[rows, :W2] - near)
                if on_diagonal(r0, c):
                    a = jnp.where(causal(kb, c), a, 0.0)
                a_scr[ab, rows, :] = a.astype(BF16)
                rall_scr[rows, :] = jnp.where(lane == kb, near, rall_scr[rows, :])
                r_scr[rows, :] = near + cs_scr[rows, W2:]

        def weighted_values(kb, ab, r0=0):
            acc_scr[r0:, :] += _dot(a_scr[ab, r0:, :], _pair_rows(v_ref, kb))

        n = qi * diag
        for dd in reversed(range(diag)):
            kb, r0 = n + dd, dd * TK
            logits(kb, dd % PIPE, r0)
            log_one_minus_beta(kb, dd % PIPE, dd % 2, r0)
            sums(dd % 2, r0)
            weights(kb, dd % PIPE, dd % 2, dd % 2, r0)
            weighted_values(kb, dd % 2, r0)

        def block(j):
            return jnp.maximum(n - 1 - j, 0)

        a_scr[...] = jnp.zeros_like(a_scr)
        logits(block(0), 0)
        logits(block(1), 1)
        log_one_minus_beta(block(0), 0, 0)

        def trip(m, carry):
            for u in range(PIPE):
                j = PIPE * m + u
                weighted_values(block(j - 1), (u - 1) % 2)
                sums(u % 2)
                logits(block(j + 2), (u + 2) % PIPE)
                log_one_minus_beta(block(j + 1), (u + 1) % PIPE, (u + 1) % 2)
                weights(block(j), u % PIPE, u % 2, u % 2)
            return carry

        lax.fori_loop(0, n // PIPE, trip, 0)
        weighted_values(block(n - 1), (PIPE - 1) % 2)
        rs_ref[...] = rall_scr[...]
        o_ref[...] = acc_scr[...].astype(o_ref.dtype)

    (o, rsave), gathered = _call_hosting(
        body, (proj, kp, vp), _AllToAll([True] * len(shards)) if shards else None, shards,
        name="attn_fwd", grid=(N_HEADS // 2, nq),
        in_specs=[pl.BlockSpec((TQ, LANES), lambda p, qi: (qi, Q_COL + p)),
                  pl.BlockSpec((2 * s, LANES), lambda p, qi: (0, p)),
                  pl.BlockSpec((2 * s, LANES), lambda p, qi: (0, p))],
        out_specs=[pl.BlockSpec((TQ, LANES), lambda p, qi: (qi, p)),
                   pl.BlockSpec((TQ, W2), lambda p, qi: (qi, p))],
        out_shape=[jax.ShapeDtypeStruct((s, ATTN_WIDTH), BF16), jax.ShapeDtypeStruct((s, N_HEADS // 2 * W2), F32)],
        scratch_shapes=[pltpu.VMEM((TQ, LANES), BF16), pltpu.VMEM((W2, 2 * W2), BF16),
                        pltpu.VMEM((PIPE, TQ, W2), F32), pltpu.VMEM((2, TQ, W2), BF16),
                        pltpu.VMEM((TQ, 2 * W2), F32), pltpu.VMEM((2, TQ, W2), BF16),
                        pltpu.VMEM((TQ, W2), F32), pltpu.VMEM((TQ, W2), F32), pltpu.VMEM((TQ, LANES), F32)])
    return o, rsave, gathered


def _attn_bwd(proj, kp, vp, do, rsave, sent=()):
    s = proj.shape[0]
    nq = s // TQ
    diag = TQ // TK
    n_ch = TQ // CH
    assert diag % PIPE == 0

    def body(q_ref, k_ref, v_ref, do_ref, rs_ref, dq_ref, dk_ref, dv_ref,
             qb_scr, dob_scr, after_scr, before_scr, z_scr, da_scr, l0_scr, beta_scr, cs_scr, a_scr, g_scr, cg_scr,
             dz_scr, pg_scr, dq_scr, dk_scr, dv_scr):
        qi = pl.program_id(1)

        @pl.when(qi == 0)
        def _():
            dk_scr[...] = jnp.zeros_like(dk_scr)
            dv_scr[...] = jnp.zeros_like(dv_scr)

        lm = [_head_lanes(0), _head_lanes(1)]
        lane = lax.broadcasted_iota(jnp.int32, (CH, TK), 1)
        row = lax.broadcasted_iota(jnp.int32, (CH, 1), 0)
        col = lax.broadcasted_iota(jnp.int32, (1, W2), 1) & (TK - 1)
        qb_scr[...] = (q_ref[...] * SCALE).astype(BF16)
        dob_scr[...] = do_ref[...].astype(BF16)
        after_scr[...] = _pair_tri(lambda j, ss: j > ss)[:, :W2]
        before_scr[...] = _pair_tri(lambda j, ss: j < ss)
        pg_scr[...] = jnp.zeros_like(pg_scr)
        dq_scr[...] = jnp.zeros_like(dq_scr)
        dz_scr[...] = jnp.zeros_like(dz_scr)

        def causal(kb, c):
            return (kb * TK + col) < (qi * TQ + c * CH + row)

        def chunks(r0):
            return range(0 if r0 is None else r0 // CH, n_ch)

        def on_diagonal(r0, c):
            return r0 is not None and c * CH < r0 + TK

        def logits(kb, zb, r0=0):
            z_scr[zb, r0:, :] = _dot(qb_scr[r0:, :], _pair_rows(k_ref, kb), NT)

        def do_dot_v(kb, db, r0=0):
            da_scr[db, r0:, :] = _dot(dob_scr[r0:, :], _pair_rows(v_ref, kb), NT)

        def gates(kb, zb, lb, bb, r0=None):
            for c in chunks(r0):
                rows = slice(c * CH, (c + 1) * CH)
                z = z_scr[zb, rows, :]
                sp = _softplus(z)
                beta_scr[bb, rows, :] = jnp.exp(z - sp)
                if on_diagonal(r0, c):
                    sp = jnp.where(causal(kb, c), sp, 0.0)
                l0_scr[lb, rows, :] = sp.astype(BF16)

        def suffix_sums(lb, r0=0):
            cs_scr[r0:, :] = _dot(l0_scr[lb, r0:, :], after_scr[...])

        def weights(kb, zb, lb, db, ab, r0=None):
            for c in chunks(r0):
                rows = slice(c * CH, (c + 1) * CH)
                keep = (kb * TK + lane) < (qi * TQ + c * CH + row) if on_diagonal(r0, c) else None
                for hh in range(2):
                    cols = slice(hh * TK, (hh + 1) * TK)
                    near = jnp.sum(jnp.where(lane == kb, rs_ref[rows, cols], 0.0), axis=1, keepdims=True)
                    a = jnp.exp(z_scr[zb, rows, cols] - l0_scr[lb, rows, cols].astype(F32) - cs_scr[rows, cols] - near)
                    if keep is not None:
                        a = jnp.where(keep, a, 0.0)
                    a_scr[ab, rows, cols] = a.astype(BF16)
                    g_scr[ab, rows, cols] = (a * da_scr[db, rows, cols]).astype(BF16)

        def prefix_sums(ab, r0=0):
            cg_scr[r0:, :] = _dot(g_scr[ab, r0:, :], before_scr[...])

        def dlogits(kb, ab, bb, zb2, r0=None):
            for c in chunks(r0):
                rows = slice(c * CH, (c + 1) * CH)
                earlier = pg_scr[rows, :]
                beta = beta_scr[bb, rows, :]
                g = g_scr[ab, rows, :].astype(F32)
                dz = g - beta * (g + cg_scr[rows, :W2] + earlier)
                if on_diagonal(r0, c):
                    dz = jnp.where(causal(kb, c), dz, 0.0)
                dz_scr[zb2, rows, :] = dz.astype(BF16)
                pg_scr[rows, :] = earlier + cg_scr[rows, W2:]

        def fold(t):
            return t[:TK, :] * lm[0] + t[TK:, :] * lm[1]

        def dq_dk(kb, zb2, r0=0):
            dq_scr[r0:, :] += _dot(dz_scr[zb2, r0:, :], _pair_rows(k_ref, kb))
            dk_scr[pl.ds(pl.multiple_of(kb * TK, TK), TK), :] += fold(_dot(dz_scr[zb2, r0:, :], qb_scr[r0:, :], TN))

        def dv(kb, ab, r0=0):
            dv_scr[pl.ds(pl.multiple_of(kb * TK, TK), TK), :] += fold(_dot(a_scr[ab, r0:, :], dob_scr[r0:, :], TN))

        n = qi * diag

        def block(j):
            return jnp.clip(j, 0, jnp.maximum(n - 1, 0))

        logits(block(0), 0)
        logits(block(1), 1)
        logits(block(2), 2)
        do_dot_v(block(0), 0)
        do_dot_v(block(1), 1)
        gates(block(0), 0, 0, 0)
        gates(block(1), 1, 1, 1)
        suffix_sums(0)
        weights(block(0), 0, 0, 0, 0)

        def trip(m, carry):
            for u in range(PIPE):
                t = PIPE * m + u
                dq_dk(block(t - 1), (u - 1) % 2)
                dv(block(t), u % 2)
                prefix_sums(u % 2)
                suffix_sums((u + 1) % 2)
                logits(block(t + 3), (u + 3) % PIPE)
                do_dot_v(block(t + 2), u % 2)
                gates(block(t + 2), (u + 2) % PIPE, u % 2, (u + 2) % PIPE)
                weights(block(t + 1), (u + 1) % PIPE, (u + 1) % 2, (u + 1) % 2, (u + 1) % 2)
                dlogits(block(t), u % 2, u % PIPE, u % 2)
            return carry

        lax.fori_loop(0, n // PIPE, trip, 0)
        dq_dk(block(n - 1), (PIPE - 1) % 2)

        for dd in range(diag):
            kb, r0, two, four = n + dd, dd * TK, dd % 2, dd % PIPE
            logits(kb, four, r0)
            do_dot_v(kb, two, r0)
            gates(kb, four, two, four, r0)
            suffix_sums(two, r0)
            weights(kb, four, two, two, two, r0)
            dv(kb, two, r0)
            prefix_sums(two, r0)
            dlogits(kb, two, four, two, r0)
            dq_dk(kb, two, r0)
        dq_ref[...] = (dq_scr[...] * SCALE).astype(dq_ref.dtype)

        @pl.when(qi == nq - 1)
        def _():
            dk_ref[...] = dk_scr[...].astype(dk_ref.dtype)
            dv_ref[...] = dv_scr[...].astype(dv_ref.dtype)

    def rows(c0):
        return pl.BlockSpec((TQ, LANES), lambda p, qi: (qi, c0 + p))

    def whole(c0):
        return pl.BlockSpec((s, LANES), lambda p, qi: (0, c0 + p))

    def f32(*shape):
        return pltpu.VMEM(shape, F32)

    def bf16(*shape):
        return pltpu.VMEM(shape, BF16)

    pairs = pl.BlockSpec((2 * s, LANES), lambda p, qi: (0, p))
    out = jax.ShapeDtypeStruct((s, ATTN_WIDTH), BF16)
    (dq, dk, dv), parts = _call_hosting(
        body, (proj, kp, vp, do, rsave), _AllToAll([False] * len(sent)) if sent else None, sent,
        name="attn_bwd", grid=(N_HEADS // 2, nq),
        in_specs=[rows(Q_COL), pairs, pairs, rows(0), pl.BlockSpec((TQ, W2), lambda p, qi: (qi, p))],
        out_specs=[rows(0), whole(0), whole(0)], out_shape=[out] * 3,
        scratch_shapes=[bf16(TQ, LANES), bf16(TQ, LANES), bf16(W2, W2), bf16(W2, 2 * W2),
                        f32(PIPE, TQ, W2), f32(2, TQ, W2), bf16(2, TQ, W2), f32(PIPE, TQ, W2), f32(TQ, W2),
                        bf16(2, TQ, W2), bf16(2, TQ, W2), f32(TQ, 2 * W2), bf16(2, TQ, W2),
                        f32(TQ, W2), f32(TQ, LANES), f32(s, LANES), f32(s, LANES)])
    return dq, dk, dv, parts


def _sum_adamw(name, parts, w, m, v, layer=None, into=None):
    n, r, c = parts.shape
    tr = r if r <= 256 else 256

    def body(p_ref, w_ref, m_ref, v_ref, g_ref, d_ref, nm_ref, nv_ref):
        g = p_ref[0].astype(F32)
        for j in range(1, n):
            g = g + p_ref[j].astype(F32)
        nm = ADAM_B1 * m_ref[...] + (1.0 - ADAM_B1) * g
        nv = ADAM_B2 * v_ref[...] + (1.0 - ADAM_B2) * (g * g)
        m_hat = nm / (1.0 - ADAM_B1 ** ADAM_STEP)
        v_hat = nv / (1.0 - ADAM_B2 ** ADAM_STEP)
        g_ref[...] = g
        d_ref[...] = -ADAM_LR * (m_hat / (jnp.sqrt(v_hat) + ADAM_EPS) + ADAM_WD * w_ref[...])
        nm_ref[...] = nm
        nv_ref[...] = nv

    if layer is None:
        mat = pl.BlockSpec((tr, c), lambda i: (i, 0))
        out = jax.ShapeDtypeStruct((r, c), F32)
    else:
        mat = pl.BlockSpec((None, tr, c), lambda i: (layer, i, 0))
        out = jax.ShapeDtypeStruct((DEPTH, r, c), F32)
    earlier = () if into is None else tuple(into)
    return _call(body if into is None else lambda *refs: body(*refs[:4], *refs[8:]),
                 name=name, grid=(r // tr,),
                 in_specs=[pl.BlockSpec((n, tr, c), lambda i: (0, i, 0)), mat, mat, mat]
                 + [pl.BlockSpec(memory_space=pl.ANY)] * len(earlier),
                 out_specs=[mat] * 4, out_shape=[out] * 4,
                 input_output_aliases={4 + k: k for k in range(len(earlier))},
                 compiler_params=_params("parallel"))(parts, w, m, v, *earlier)


def _natural(gathered):
    _, k, n = gathered.shape
    return gathered.transpose(1, 0, 2).reshape(k, N_DEV * n)


def _relu2_epi(acc):
    r = jnp.maximum(acc, 0.0)
    return acc, r * r


def _relu2_bwd_epi(acc, a_act):
    return (acc * (2.0 * jnp.maximum(a_act.astype(F32), 0.0)),)


def kernel(x, c, w_ada, b_ada, g_pre_mix, g_post_mix, g_pre_mlp, g_post_mlp, w_in, conv_w, w_proj_conv, w_proj_attn, w_out, w_mlp_in, w_mlp_out, loss_target, m_w_ada, m_b_ada, m_g_pre_mix, m_g_post_mix, m_g_pre_mlp, m_g_post_mlp, m_w_in, m_conv_w, m_w_proj_conv, m_w_proj_attn, m_w_out, m_w_mlp_in, m_w_mlp_out, v_w_ada, v_b_ada, v_g_pre_mix, v_g_post_mix, v_g_pre_mlp, v_g_post_mlp, v_w_in, v_conv_w, v_w_proj_conv, v_w_proj_attn, v_w_out, v_w_mlp_in, v_w_mlp_out):
    xi, yi, ci = _mesh_pos()
    me = 4 * xi + 2 * yi + ci
    d = D_MODEL
    x0 = x[0]
    seq = x0.shape[0]
    ada_cols = w_ada.shape[2]
    conv_cols = conv_w.shape[2]

    small = jnp.concatenate([c.reshape(-1), conv_w.reshape(-1)])
    small = jnp.pad(small, (0, 2 * d - small.shape[0])).reshape(8, 2 * d // 8)
    small_all = _all_gather("gather_c", [small])[0].reshape(N_DEV, 2 * d)
    c_all = small_all[:, :d]
    conv_all = small_all[:, d:d + DEPTH * 3 * conv_cols].reshape(N_DEV, DEPTH, 3, conv_cols)
    conv_all = conv_all.transpose(1, 2, 0, 3).reshape(DEPTH, 3, N_DEV * conv_cols)
    mod_cols = jnp.stack([_mm("mod_mm", c_all, w_ada[l], "nn", N_DEV, ada_cols, d, [F32], exact=True)
                          for l in range(DEPTH)], axis=1)
    mod_all = _all_gather("gather_mod", [mod_cols.reshape(N_DEV, DEPTH * ada_cols)])[0]
    mod_mine = lax.dynamic_index_in_dim(mod_all, me, axis=1, keepdims=False).reshape(N_DEV, DEPTH, ada_cols)
    mod = mod_mine.transpose(1, 0, 2).reshape(DEPTH, N_MOD * d) + b_ada

    sharded = {"w_in": w_in, "w_proj_conv": w_proj_conv, "w_proj_attn": w_proj_attn, "w_out": w_out,
               "w_mlp_in": w_mlp_in, "w_mlp_out": w_mlp_out}
    before_attention = ["w_in", "w_proj_conv"]

    def shard(key):
        nm, l = key
        return sharded[nm][l].astype(BF16)

    def natural(key, gathered):
        return gathered.reshape(-1, d) if key[0] in ("w_out", "w_mlp_out") else _natural(gathered)

    first = [(nm, 0) for nm in before_attention]
    full = {key: natural(key, g) for key, g in zip(first, _all_gather("gather_w", [shard(key) for key in first]))}
    saved = []
    xl = x0
    for l in range(DEPTH):
        riders = [(nm, l) for nm in sharded if (nm, l) not in full]
        if l + 1 < DEPTH:
            riders += [(nm, l + 1) for nm in sharded]
        sh1, sc1, gt1, sh2, sc2, gt2 = [mod[l:l + 1, i * d:(i + 1) * d] for i in range(N_MOD)]
        h = _prenorm_fwd(xl, g_pre_mix[l:l + 1], sc1, sh1)
        proj = _mm("proj", h, full[("w_in", l)], "nn", TMM, 1024, d, [F32])
        yc = _conv_fwd(proj, conv_all[l])
        y_conv = _mm("proj_conv", yc, full[("w_proj_conv", l)], "nn", TMM, d, CONV_WIDTH, [BF16])
        kp, vp = _pair_kv(proj)
        o, rsave, gathered = _attn_fwd(proj, kp, vp, [shard(key) for key in riders])
        full.update({key: natural(key, g) for key, g in zip(riders, gathered)})
        wg_in, wg_pc, wg_pa, wg_out, wg_mi, wg_mo = [full[(nm, l)] for nm in sharded]
        y_attn = _mm("proj_attn", o, wg_pa, "nn", TMM, d, ATTN_WIDTH, [BF16])
        merged = _gate_fwd(proj, y_conv, y_attn)
        mix_out = _mm("mix_out", merged, wg_out, "nn", TMM,d, d, [F32])
        x1 = _postnorm_fwd(xl, mix_out, g_post_mix[l:l + 1], gt1)
        h2 = _prenorm_fwd(x1, g_pre_mlp[l:l + 1], sc2, sh2)
        a_act, r = _mm("mlp_in", h2, wg_mi, "nn", TMM, 1024, d, [BF16, BF16], epi=_relu2_epi)
        ff = _mm("mlp_out", r, wg_mo, "nn", TMK, d, D_FF, [F32])
        x2 = _postnorm_fwd(x1, ff, g_post_mlp[l:l + 1], gt2)
        saved.append((xl, h, proj, yc, kp, vp, o, rsave, y_conv, y_attn, merged, mix_out, x1, h2, a_act, r, ff))
        xl = x2

    dxo, sq = _loss(xl, loss_target[0])
    loss = lax.psum(sq[0, 0] * (0.5 / d), ("x", "y", "c"))

    olds = {"w_in": (w_in, m_w_in, v_w_in), "w_proj_conv": (w_proj_conv, m_w_proj_conv, v_w_proj_conv),
            "w_proj_attn": (w_proj_attn, m_w_proj_attn, v_w_proj_attn), "w_out": (w_out, m_w_out, v_w_out),
            "w_mlp_in": (w_mlp_in, m_w_mlp_in, v_w_mlp_in), "w_mlp_out": (w_mlp_out, m_w_mlp_out, v_w_mlp_out)}
    big = {}
    pending = []

    def col_blocks(gw):
        k, n = gw.shape
        return gw.reshape(k, N_DEV, n // N_DEV).transpose(1, 0, 2)

    def update(entries, parts):
        for (nm, ll, _), part in zip(entries, parts):
            w_, m_, v_ = olds[nm]
            big[nm] = _sum_adamw("adamw_" + nm, part, w_, m_, v_, layer=ll, into=big.get(nm))

    dmod, small_grads = [None] * DEPTH, [None] * DEPTH
    for l in reversed(range(DEPTH)):
        wg_in, wg_pc, wg_pa, wg_out, wg_mi, wg_mo = [full[(nm, l)] for nm in sharded]
        xin, h, proj, yc, kp, vp, o, rsave, y_conv, y_attn, merged, mix_out, x1, h2, a_act, r, ff = saved[l]
        sh1, sc1, gt1, sh2, sc2, gt2 = [mod[l:l + 1, i * d:(i + 1) * d] for i in range(N_MOD)]

        dff, dgt2, dg_post_mlp = _postnorm_bwd(dxo, ff, g_post_mlp[l:l + 1], gt2)
        da = _mm("d_relu2", dff, wg_mo, "nt", TMM,1024, d, [BF16], epi=_relu2_bwd_epi, extra=(a_act,))
        gw_mo = _mm("gw_mlp_out", r, dff, "tn", 1024, d, TSK, [BF16])
        dh2 = _mm("d_h2", da, wg_mi, "nt", TMK, d, D_FF, [F32])
        gw_mi = _mm("gw_mlp_in", h2, da, "tn", d, 1024, TSK, [BF16])
        dx1, dsh2, dsc2, dg_pre_mlp = _prenorm_bwd(dh2, x1, g_pre_mlp[l:l + 1], sc2, dxo)

        dmix, dgt1, dg_post_mix = _postnorm_bwd(dx1, mix_out, g_post_mix[l:l + 1], gt1)
        dmerged = _mm("d_merged", dmix, wg_out, "nt", TMM, d, d, [BF16])
        gw_out = _mm("gw_out", merged, dmix, "tn", d, d, TSK, [BF16])
        dy_conv, dy_attn, dga, dgb = _gate_bwd(dmerged, proj, y_conv, y_attn)
        do = _mm("d_o", dy_attn, wg_pa, "nt", TMM, ATTN_WIDTH, d, [BF16])
        gw_pa = _mm("gw_proj_attn", o, dy_attn, "tn", ATTN_WIDTH, d, TSK, [BF16])
        dyc = _mm("d_yc", dy_conv, wg_pc, "nt", TMM,CONV_WIDTH, d, [F32])
        gw_pc = _mm("gw_proj_conv", yc, dy_conv, "tn", CONV_WIDTH, d, TSK, [BF16])
        pending += [("w_mlp_out", l, gw_mo.reshape(N_DEV, D_FF // N_DEV, d)), ("w_mlp_in", l, col_blocks(gw_mi)),
                    ("w_out", l, gw_out.reshape(N_DEV, d // N_DEV, d)), ("w_proj_attn", l, col_blocks(gw_pa)),
                    ("w_proj_conv", l, col_blocks(gw_pc))]
        if l == 0:
            dq, dk, dv, parts = _attn_bwd(proj, kp, vp, do, rsave, [blocks for _, _, blocks in pending])
            update(pending, parts)
            pending = []
        else:
            dq, dk, dv, _ = _attn_bwd(proj, kp, vp, do, rsave)
        dbg, dcg, du, dw0, dw1, dw2 = _conv_bwd(dyc, proj, conv_all[l])
        dproj = jnp.concatenate([dbg, dcg, du, dq, dk, dv, dga, dgb], axis=1)
        gw_in = _mm("gw_in", h, dproj, "tn", d, 1024, TSK, [BF16])
        pending.append(("w_in", l, col_blocks(gw_in)))
        if l == 0:
            dh, parts = _mm("d_h", dproj, wg_in, "nt", TMK, d, IN_COLS, [F32], sent=[blocks for _, _, blocks in pending])
            update(pending, parts)
            pending = []
        else:
            dh = _mm("d_h", dproj, wg_in, "nt", TMK, d, IN_COLS, [F32])
        dxo, dsh1, dsc1, dg_pre_mix = _prenorm_bwd(dh, xin, g_pre_mix[l:l + 1], sc1, dx1)

        dmod[l] = jnp.concatenate([dsh1, dsc1, dgt1, dsh2, dsc2, dgt2], axis=1)
        small_grads[l] = (dg_pre_mix, dg_post_mix, dg_pre_mlp, dg_post_mlp, jnp.concatenate([dw0, dw1, dw2], axis=0))
    assert not pending

    vec = jnp.concatenate(
        [dmod[l].reshape(-1) for l in range(DEPTH)]
        + [small_grads[l][i].reshape(-1) for i in range(4) for l in range(DEPTH)]
        + [small_grads[l][4].reshape(-1) for l in range(DEPTH)])
    n_vec = vec.shape[0]
    vec_all = _all_gather("gather_small", [vec.reshape(8, n_vec // 8)])[0].reshape(N_DEV, n_vec)
    n_mod = DEPTH * N_MOD * d
    dmod_all = vec_all[:, :n_mod].reshape(N_DEV, DEPTH, N_MOD * d)
    res = {}
    res["b_ada"] = _sum_adamw("adamw_b_ada", dmod_all, b_ada, m_b_ada, v_b_ada)
    off = n_mod
    for nm, (w_, m_, v_) in zip(
            ["g_pre_mix", "g_post_mix", "g_pre_mlp", "g_post_mlp"],
            [(g_pre_mix, m_g_pre_mix, v_g_pre_mix), (g_post_mix, m_g_post_mix, v_g_post_mix),
             (g_pre_mlp, m_g_pre_mlp, v_g_pre_mlp), (g_post_mlp, m_g_post_mlp, v_g_post_mlp)]):
        res[nm] = _sum_adamw("adamw_gain", vec_all[:, off:off + DEPTH * d].reshape(N_DEV, DEPTH, d), w_, m_, v_)
        off += DEPTH * d
    dconv_all = vec_all[:, off:].reshape(N_DEV, DEPTH * 3, CONV_WIDTH)
    dconv_mine = lax.dynamic_slice_in_dim(dconv_all, me * conv_cols, conv_cols, axis=2)
    res["conv_w"] = [t.reshape(DEPTH, 3, conv_cols) for t in _sum_adamw(
        "adamw_conv_w", dconv_mine, conv_w.reshape(DEPTH * 3, conv_cols), m_conv_w.reshape(DEPTH * 3, conv_cols),
        v_conv_w.reshape(DEPTH * 3, conv_cols))]

    c_t = jnp.pad(c_all.T, ((0, 0), (0, LANES - N_DEV)))
    dmod_mine = lax.dynamic_slice_in_dim(dmod_all, me * ada_cols, ada_cols, axis=2)
    for l in range(DEPTH):
        dm_l = jnp.pad(dmod_mine[:, l, :], ((0, LANES - N_DEV), (0, 0)))
        gw_ada = _mm("gw_ada", c_t, dm_l, "nn", 256, ada_cols, LANES, [F32], exact=True)
        res["w_ada"] = _sum_adamw("adamw_w_ada", gw_ada[None], w_ada, m_w_ada, v_w_ada, layer=l, into=res.get("w_ada"))
    res.update(big)

    order = ["w_ada", "b_ada", "g_pre_mix", "g_post_mix", "g_pre_mlp", "g_post_mlp", "w_in", "conv_w",
             "w_proj_conv", "w_proj_attn", "w_out", "w_mlp_in", "w_mlp_out"]
    outs = [loss, dxo[None]]
    for i in range(4):
        outs += [res[nm][i] for nm in order]
    return tuple(outs)
```

```python
import jax
import jax.numpy as jnp
from jax import lax
from jax.experimental import pallas as pl
from jax.experimental.pallas import tpu as pltpu

F32 = jnp.float32
BF16 = jnp.bfloat16
MESH = pl.DeviceIdType.MESH

N_DEV = 8
D_MODEL = 1024
CONV_WIDTH = 512
N_HEADS = 8
HEAD_DIM = 64
ATTN_WIDTH = N_HEADS * HEAD_DIM
D_FF = 4 * D_MODEL
N_MOD = 6
DEPTH = 2
EPS = 1e-6
IN_COLS = 3 * CONV_WIDTH + 3 * ATTN_WIDTH + 2 * D_MODEL
LANES = 128

ADAM_LR = 0.001
ADAM_B1 = 0.9
ADAM_B2 = 0.999
ADAM_EPS = 1e-08
ADAM_WD = 0.01
ADAM_STEP = 10

TM = 1024
TMM = 2048
TMK = 1024
TSK = 2048
TQ = 512
TK = 128
CH = 64
VMEM_LIMIT = 56 * 1024 * 1024

NN = (((1,), (0,)), ((), ()))
NT = (((1,), (1,)), ((), ()))
TN = (((0,), (0,)), ((), ()))
_DIMS = {"nn": NN, "nt": NT, "tn": TN}


def _call(body, **kw):
    return pl.pallas_call(body, **kw)


def _params(*sem):
    return pltpu.CompilerParams(dimension_semantics=sem, vmem_limit_bytes=VMEM_LIMIT)


def _dot(a, b, dims=NN):
    return lax.dot_general(a, b, dims, preferred_element_type=F32)


def _mesh_pos():
    return lax.axis_index("x"), lax.axis_index("y"), lax.axis_index("c")


def _all_gather(name, arrs):
    n = len(arrs)

    def body(*refs):
        ins, outs = refs[:n], refs[n:2 * n]
        send_sems, recv_sems, local_sems = refs[2 * n:]
        x, y, c = _mesh_pos()
        me, sibling = (x, y, c), (x, y, 1 - c)
        chips = [(1 - x, y), (x, 1 - y), (1 - x, 1 - y)]

        def blk(t, p):
            return outs[t].at[4 * p[0] + 2 * p[1] + p[2]]

        def copy(t, k, block, to, src=None):
            return pltpu.make_async_remote_copy(
                src_ref=blk(t, block) if src is None else src, dst_ref=blk(t, block),
                send_sem=send_sems.at[7 * t + k], recv_sem=recv_sems.at[7 * t + k],
                device_id=to, device_id_type=MESH)

        mine, first, passed = [], [], []
        for t in range(n):
            cp = pltpu.make_async_copy(ins[t], blk(t, me), local_sems.at[t])
            cp.start()
            mine.append(cp)
            cps = [copy(t, 0, me, sibling, src=ins[t])]
            cps += [copy(t, 1 + j, me, (*chip, c), src=ins[t]) for j, chip in enumerate(chips)]
            for cp in cps:
                cp.start()
            first += cps
        for t in range(n):
            for j, chip in enumerate(chips):
                copy(t, 1 + j, (*chip, c), me).wait_recv()
                cp = copy(t, 4 + j, (*chip, c), sibling)
                cp.start()
                passed.append(cp)
        for t in range(n):
            copy(t, 0, sibling, me).wait_recv()
            for j, chip in enumerate(chips):
                copy(t, 4 + j, (*chip, 1 - c), me).wait_recv()
        for cp in first + passed:
            cp.wait_send()
        for cp in mine:
            cp.wait()

    any_spec = pl.BlockSpec(memory_space=pl.ANY)
    return _call(
        body, name=name,
        out_shape=[jax.ShapeDtypeStruct((N_DEV,) + a.shape, a.dtype) for a in arrs],
        in_specs=[any_spec] * n, out_specs=[any_spec] * n,
        scratch_shapes=[pltpu.SemaphoreType.DMA((7 * n,)), pltpu.SemaphoreType.DMA((7 * n,)),
                        pltpu.SemaphoreType.DMA((n,))],
    )(*arrs)


class _AllToAll:
    def __init__(self, whole):
        self.whole = list(whole)
        self.n = len(self.whole)

    def sem_shapes(self):
        return [pltpu.SemaphoreType.DMA((7 * self.n,)), pltpu.SemaphoreType.DMA((7 * self.n,)),
                pltpu.SemaphoreType.DMA((self.n,))]

    def out_shapes(self, arrs):
        return [jax.ShapeDtypeStruct(((N_DEV,) + a.shape) if w else a.shape, a.dtype) for a, w in zip(arrs, self.whole)]

    def _copies(self, ins, outs, sems):
        send_sems, recv_sems, local_sems = sems
        x, y, c = _mesh_pos()
        my_idx = 4 * x + 2 * y + c
        mine, sends, recvs = [], [], []
        for t in range(self.n):
            def src(idx):
                return ins[t] if self.whole[t] else ins[t].at[idx]
            mine.append(pltpu.make_async_copy(src(my_idx), outs[t].at[my_idx], local_sems.at[t]))
            for k in range(1, N_DEV):
                p = (1 - x if k & 4 else x, 1 - y if k & 2 else y, 1 - c if k & 1 else c)
                p_idx = 4 * p[0] + 2 * p[1] + p[2]
                for dst_idx, group in ((my_idx, sends), (p_idx, recvs)):
                    group.append(pltpu.make_async_remote_copy(
                        src_ref=src(p_idx), dst_ref=outs[t].at[dst_idx],
                        send_sem=send_sems.at[7 * t + k - 1], recv_sem=recv_sems.at[7 * t + k - 1],
                        device_id=p, device_id_type=MESH))
        return mine, sends, recvs

    def start(self, ins, outs, sems):
        mine, sends, _ = self._copies(ins, outs, sems)
        for cp in mine + sends:
            cp.start()

    def finish(self, ins, outs, sems):
        mine, sends, recvs = self._copies(ins, outs, sems)
        for cp in recvs:
            cp.wait_recv()
        for cp in sends:
            cp.wait_send()
        for cp in mine:
            cp.wait()


def _call_hosting(body, args, comm, comm_args, *, name, grid, in_specs, out_specs, out_shape, scratch_shapes):
    if comm is None:
        return _call(body, name=name, grid=grid, in_specs=in_specs, out_specs=out_specs, out_shape=out_shape,
                     scratch_shapes=scratch_shapes, compiler_params=_params(*["arbitrary"] * len(grid)))(*args), ()
    n, n_in, n_out, n_scr = comm.n, len(in_specs), len(out_specs), len(scratch_shapes)

    def hosted(*refs):
        ins, refs = refs[:n_in], refs[n_in:]
        c_ins, refs = refs[:n], refs[n:]
        outs, refs = refs[:n_out], refs[n_out:]
        c_outs, refs = refs[:n], refs[n:]
        scratch, sems = refs[:n_scr], refs[n_scr:]
        step = [pl.program_id(i) for i in range(len(grid))]

        def at(ends):
            hit = step[0] == ends[0]
            for sidx, e in zip(step[1:], ends[1:]):
                hit = jnp.logical_and(hit, sidx == e)
            return hit

        @pl.when(at([0] * len(grid)))
        def _():
            comm.start(c_ins, c_outs, sems)

        body(*ins, *outs, *scratch)

        @pl.when(at([g - 1 for g in grid]))
        def _():
            comm.finish(c_ins, c_outs, sems)

    any_spec = pl.BlockSpec(memory_space=pl.ANY)
    res = _call(hosted, name=name + "_hosting", grid=grid, in_specs=list(in_specs) + [any_spec] * n,
                out_specs=list(out_specs) + [any_spec] * n, out_shape=list(out_shape) + comm.out_shapes(comm_args),
                scratch_shapes=list(scratch_shapes) + comm.sem_shapes(),
                compiler_params=_params(*["arbitrary"] * len(grid)))(*args, *comm_args)
    return res[:n_out], res[n_out:]


def _mm(name, a, b, mode, tm, tn, tk, out_dtypes, epi=None, extra=(), exact=False, sent=()):
    if mode == "nn":
        (m, k), n = a.shape, b.shape[1]
    elif mode == "nt":
        (m, k), n = a.shape, b.shape[0]
    else:
        (k, m), n = a.shape, b.shape[1]
    tm, tn, tk = min(tm, m), min(tn, n), min(tk, k)
    nk = k // tk
    grid = (m // tm, n // tn, nk)
    n_extra, n_out = len(extra), len(out_dtypes)

    def body(*refs):
        a_ref, b_ref = refs[0], refs[1]
        extra_refs = refs[2:2 + n_extra]
        out_refs = refs[2 + n_extra:2 + n_extra + n_out]
        if exact:
            p = lax.dot_general(a_ref[...], b_ref[...], _DIMS[mode], preferred_element_type=F32,
                                precision=lax.Precision.HIGHEST)
        else:
            p = _dot(a_ref[...].astype(BF16), b_ref[...].astype(BF16), _DIMS[mode])

        def finish(acc):
            outs = (acc,) if epi is None else epi(acc, *[r[...] for r in extra_refs])
            for r, o in zip(out_refs, outs):
                r[...] = o.astype(r.dtype)

        if nk == 1:
            finish(p)
        else:
            acc_ref = refs[-1]
            kk = pl.program_id(2)

            @pl.when(kk == 0)
            def _():
                acc_ref[...] = p

            @pl.when(kk > 0)
            def _():
                acc_ref[...] += p

            @pl.when(kk == nk - 1)
            def _():
                finish(acc_ref[...])

    if mode == "tn":
        a_spec = pl.BlockSpec((tk, tm), lambda i, j, kk: (kk, i))
    else:
        a_spec = pl.BlockSpec((tm, tk), lambda i, j, kk: (i, kk))
    if mode == "nt":
        b_spec = pl.BlockSpec((tn, tk), lambda i, j, kk: (j, kk))
    else:
        b_spec = pl.BlockSpec((tk, tn), lambda i, j, kk: (kk, j))
    tile = pl.BlockSpec((tm, tn), lambda i, j, kk: (i, j))
    o_shape, o_spec = (m, n), tile
    out, parts = _call_hosting(
        body, (a, b, *extra), _AllToAll([False] * len(sent)) if sent else None, sent,
        name=name, grid=grid,
        in_specs=[a_spec, b_spec] + [tile] * n_extra,
        out_specs=[o_spec] * n_out,
        out_shape=[jax.ShapeDtypeStruct(o_shape, dt) for dt in out_dtypes],
        scratch_shapes=[pltpu.VMEM((tm, tn), F32)] if nk > 1 else [])
    out = out[0] if n_out == 1 else out
    return (out, parts) if sent else out


def _tile(width, col=0, rows=TM):
    return pl.BlockSpec((rows, width), lambda i: (i, col))


def _vec(width):
    return pl.BlockSpec((1, width), lambda i: (0, 0))


def _rstd(xf):
    return lax.rsqrt(jnp.mean(xf * xf, axis=-1, keepdims=True) + EPS)


def _colsum(v):
    return jnp.sum(v, axis=0, keepdims=True)


def _accumulate(refs, vals):
    first = pl.program_id(0) == 0

    @pl.when(first)
    def _():
        for r, v in zip(refs, vals):
            r[...] = v

    @pl.when(jnp.logical_not(first))
    def _():
        for r, v in zip(refs, vals):
            r[...] += v


def _prenorm_fwd(x, g, sc, sh):
    s, d = x.shape

    def body(x_ref, g_ref, sc_ref, sh_ref, h_ref):
        xf = x_ref[...]
        y = (xf * _rstd(xf)) * g_ref[...]
        h_ref[...] = (y * (1.0 + sc_ref[...]) + sh_ref[...]).astype(h_ref.dtype)

    return _call(body, name="prenorm_fwd", grid=(s // TM,),
                 in_specs=[_tile(d), _vec(d), _vec(d), _vec(d)], out_specs=_tile(d),
                 out_shape=jax.ShapeDtypeStruct((s, d), BF16), compiler_params=_params("parallel"))(x, g, sc, sh)


def _prenorm_bwd(dh, x, g, sc, dres):
    s, d = x.shape

    def body(dh_ref, x_ref, g_ref, sc_ref, dres_ref, dx_ref, dsh_ref, dsc_ref, dg_ref):
        xf, dhf = x_ref[...], dh_ref[...]
        rstd = _rstd(xf)
        xhat = xf * rstd
        one_sc = 1.0 + sc_ref[...]
        dxhat = dhf * (g_ref[...] * one_sc)
        dx_ref[...] = dres_ref[...] + rstd * (dxhat - xhat * jnp.mean(dxhat * xhat, axis=-1, keepdims=True))
        dhx = dhf * xhat
        _accumulate((dsh_ref, dsc_ref, dg_ref), (_colsum(dhf), _colsum(dhx) * g_ref[...], _colsum(dhx) * one_sc))

    vec_out = jax.ShapeDtypeStruct((1, d), F32)
    return _call(body, name="prenorm_bwd", grid=(s // TM,),
                 in_specs=[_tile(d), _tile(d), _vec(d), _vec(d), _tile(d)],
                 out_specs=[_tile(d), _vec(d), _vec(d), _vec(d)],
                 out_shape=[jax.ShapeDtypeStruct((s, d), F32), vec_out, vec_out, vec_out],
                 compiler_params=_params("arbitrary"))(dh, x, g, sc, dres)


def _postnorm_fwd(xres, m, g, gt):
    s, d = m.shape

    def body(x_ref, m_ref, g_ref, gt_ref, o_ref):
        mf = m_ref[...]
        o_ref[...] = x_ref[...] + gt_ref[...] * ((mf * _rstd(mf)) * g_ref[...])

    return _call(body, name="postnorm_fwd", grid=(s // TM,),
                 in_specs=[_tile(d), _tile(d), _vec(d), _vec(d)], out_specs=_tile(d),
                 out_shape=jax.ShapeDtypeStruct((s, d), F32), compiler_params=_params("parallel"))(xres, m, g, gt)


def _postnorm_bwd(dxn, m, g, gt):
    s, d = m.shape

    def body(dx_ref, m_ref, g_ref, gt_ref, dm_ref, dgt_ref, dg_ref):
        mf, dxf = m_ref[...], dx_ref[...]
        rstd = _rstd(mf)
        mhat = mf * rstd
        dmhat = dxf * (gt_ref[...] * g_ref[...])
        dm_ref[...] = (rstd * (dmhat - mhat * jnp.mean(dmhat * mhat, axis=-1, keepdims=True))).astype(dm_ref.dtype)
        dxm = _colsum(dxf * mhat)
        _accumulate((dgt_ref, dg_ref), (dxm * g_ref[...], dxm * gt_ref[...]))

    vec_out = jax.ShapeDtypeStruct((1, d), F32)
    return _call(body, name="postnorm_bwd", grid=(s // TM,),
                 in_specs=[_tile(d), _tile(d), _vec(d), _vec(d)], out_specs=[_tile(d), _vec(d), _vec(d)],
                 out_shape=[jax.ShapeDtypeStruct((s, d), BF16), vec_out, vec_out],
                 compiler_params=_params("arbitrary"))(dxn, m, g, gt)


def _postnorm_loss(xres, m, g, gt, target):
    s, d = m.shape

    def body(x_ref, m_ref, g_ref, gt_ref, t_ref, dy_ref, sq_ref):
        mf = m_ref[...]
        y = x_ref[...] + gt_ref[...] * ((mf * _rstd(mf)) * g_ref[...])
        err = y - t_ref[...]
        dy_ref[...] = err * (1.0 / d)
        tot = jnp.sum(_colsum(err * err), axis=1, keepdims=True)
        _accumulate((sq_ref,), (jnp.broadcast_to(tot, (1, LANES)),))

    return _call(body, name="postnorm_loss", grid=(s // TM,),
                 in_specs=[_tile(d), _tile(d), _vec(d), _vec(d), _tile(d)],
                 out_specs=[_tile(d), _vec(LANES)],
                 out_shape=[jax.ShapeDtypeStruct((s, d), F32), jax.ShapeDtypeStruct((1, LANES), F32)],
                 compiler_params=_params("arbitrary"))(xres, m, g, gt, target)


def _sigmoid(v):
    return 1.0 / (1.0 + jnp.exp(-v))


def _gate_fwd(proj, y_conv, y_attn):
    s, d = y_conv.shape
    ga_col, gb_col = (IN_COLS - 2 * d) // d, (IN_COLS - d) // d

    def body(ga_ref, gb_ref, yc_ref, ya_ref, o_ref):
        o_ref[...] = (_sigmoid(ga_ref[...]) * yc_ref[...] + _sigmoid(gb_ref[...]) * ya_ref[...]).astype(o_ref.dtype)

    return _call(body, name="gate_fwd", grid=(s // TM,),
                 in_specs=[_tile(d, ga_col), _tile(d, gb_col), _tile(d), _tile(d)], out_specs=_tile(d),
                 out_shape=jax.ShapeDtypeStruct((s, d), BF16),
                 compiler_params=_params("parallel"))(proj, proj, y_conv, y_attn)


def _gate_bwd(dmerged, proj, y_conv, y_attn):
    s, d = y_conv.shape
    ga_col, gb_col = (IN_COLS - 2 * d) // d, (IN_COLS - d) // d

    def body(dm_ref, ga_ref, gb_ref, yc_ref, ya_ref, dyc_ref, dya_ref, dga_ref, dgb_ref):
        dm = dm_ref[...].astype(F32)
        sa, sb = _sigmoid(ga_ref[...]), _sigmoid(gb_ref[...])
        dyc_ref[...] = (dm * sa).astype(BF16)
        dya_ref[...] = (dm * sb).astype(BF16)
        dga_ref[...] = (dm * yc_ref[...] * (sa * (1.0 - sa))).astype(BF16)
        dgb_ref[...] = (dm * ya_ref[...] * (sb * (1.0 - sb))).astype(BF16)

    out = jax.ShapeDtypeStruct((s, d), BF16)
    return _call(body, name="gate_bwd", grid=(s // TM,),
                 in_specs=[_tile(d), _tile(d, ga_col), _tile(d, gb_col), _tile(d), _tile(d)],
                 out_specs=[_tile(d)] * 4, out_shape=[out] * 4,
                 compiler_params=_params("parallel"))(dmerged, proj, proj, y_conv, y_attn)


def _shift_down(prev8, cur, by):
    ext = jnp.concatenate([prev8, cur], axis=0)
    return pltpu.roll(ext, by, 0)[8:]


def _shift_up(cur, next8, by):
    ext = jnp.concatenate([cur, next8], axis=0)
    return pltpu.roll(ext, ext.shape[0] - by, 0)[:cur.shape[0]]


def _conv_fwd(proj, conv_w):
    s, w = proj.shape[0], CONV_WIDTH
    per8 = TM // 8

    def prev(col):
        return pl.BlockSpec((8, w), lambda i: (jnp.maximum(i * per8 - 1, 0), col))

    def body(bg_ref, cg_ref, u_ref, cgp_ref, up_ref, w_ref, o_ref):
        vv = cg_ref[...] * u_ref[...]
        pv = cgp_ref[...] * up_ref[...] * jnp.where(pl.program_id(0) > 0, 1.0, 0.0)
        y = w_ref[0:1, :] * _shift_down(pv, vv, 2) + w_ref[1:2, :] * _shift_down(pv, vv, 1) + w_ref[2:3, :] * vv
        o_ref[...] = (bg_ref[...] * y).astype(o_ref.dtype)

    return _call(body, name="conv_fwd", grid=(s // TM,),
                 in_specs=[_tile(w, 0), _tile(w, 1), _tile(w, 2), prev(1), prev(2),
                           pl.BlockSpec((3, w), lambda i: (0, 0))],
                 out_specs=_tile(w), out_shape=jax.ShapeDtypeStruct((s, w), BF16),
                 compiler_params=_params("parallel"))(proj, proj, proj, proj, proj, conv_w)


def _conv_bwd(dyc, proj, conv_w):
    s, w = proj.shape[0], CONV_WIDTH
    per8 = TM // 8
    n_tiles = s // TM

    def prev(col):
        return pl.BlockSpec((8, w), lambda i: (jnp.maximum(i * per8 - 1, 0), col))

    def nxt(col):
        return pl.BlockSpec((8, w), lambda i: (jnp.minimum((i + 1) * per8, s // 8 - 1), col))

    def body(dyc_ref, bg_ref, cg_ref, u_ref, cgp_ref, up_ref, dycn_ref, bgn_ref, w_ref,
             dbg_ref, dcg_ref, du_ref, dw0_ref, dw1_ref, dw2_ref):
        i = pl.program_id(0)
        cg, u = cg_ref[...], u_ref[...]
        vv = cg * u
        pv = cgp_ref[...] * up_ref[...] * jnp.where(i > 0, 1.0, 0.0)
        v1, v2 = _shift_down(pv, vv, 1), _shift_down(pv, vv, 2)
        w0, w1, w2 = w_ref[0:1, :], w_ref[1:2, :], w_ref[2:3, :]
        dyc_t = dyc_ref[...]
        dbg_ref[...] = (dyc_t * (w0 * v2 + w1 * v1 + w2 * vv)).astype(BF16)
        dy = dyc_t * bg_ref[...]
        dyn = dycn_ref[...] * bgn_ref[...] * jnp.where(i < n_tiles - 1, 1.0, 0.0)
        dvv = w2 * dy + w1 * _shift_up(dy, dyn, 1) + w0 * _shift_up(dy, dyn, 2)
        dcg_ref[...] = (dvv * u).astype(BF16)
        du_ref[...] = (dvv * cg).astype(BF16)
        _accumulate((dw0_ref, dw1_ref, dw2_ref), (_colsum(dy * v2), _colsum(dy * v1), _colsum(dy * vv)))

    act = jax.ShapeDtypeStruct((s, w), BF16)
    tap = jax.ShapeDtypeStruct((1, w), F32)
    return _call(body, name="conv_bwd", grid=(n_tiles,),
                 in_specs=[_tile(w), _tile(w, 0), _tile(w, 1), _tile(w, 2), prev(1), prev(2), nxt(0), nxt(0),
                           pl.BlockSpec((3, w), lambda i: (0, 0))],
                 out_specs=[_tile(w)] * 3 + [_vec(w)] * 3, out_shape=[act] * 3 + [tap] * 3,
                 compiler_params=_params("arbitrary"))(dyc, proj, proj, proj, proj, proj, dyc, proj, conv_w)


Q_COL = 3 * CONV_WIDTH // LANES
K_COL = Q_COL + ATTN_WIDTH // LANES
V_COL = K_COL + ATTN_WIDTH // LANES
SCALE = HEAD_DIM ** -0.5


def _head_lanes(hh):
    lane = lax.broadcasted_iota(jnp.int32, (1, LANES), 1)
    return jnp.where((lane >= hh * HEAD_DIM) & (lane < (hh + 1) * HEAD_DIM), 1.0, 0.0)


W2 = 2 * TK
PIPE = 4


def _softplus(z):
    neg_abs = lax.bitcast_convert_type(lax.bitcast_convert_type(z, jnp.uint32) | jnp.uint32(0x80000000), F32)
    return jnp.maximum(z, 0.0) + jnp.log(1.0 + jnp.exp(neg_abs))


def _pair_kv(proj):
    s = proj.shape[0]
    per_tile = TM // TK
    width_blocks = ATTN_WIDTH // LANES

    def body(k_ref, v_ref, kp_ref, vp_ref):
        lane = lax.broadcasted_iota(jnp.int32, (1, ATTN_WIDTH), 1) & (LANES - 1)
        keep = [jnp.where(lane < HEAD_DIM, 1.0, 0.0), jnp.where(lane >= HEAD_DIM, 1.0, 0.0)]
        for src, dst in ((k_ref, kp_ref), (v_ref, vp_ref)):
            for gi in range(per_tile):
                blk = src[gi * TK:(gi + 1) * TK, :]
                for hh in range(2):
                    dst[(2 * gi + hh) * TK:(2 * gi + hh + 1) * TK, :] = (blk * keep[hh]).astype(BF16)

    out = jax.ShapeDtypeStruct((2 * s, ATTN_WIDTH), BF16)
    return _call(body, name="pair_kv", grid=(s // TM,),
                 in_specs=[_tile(ATTN_WIDTH, K_COL // width_blocks), _tile(ATTN_WIDTH, V_COL // width_blocks)],
                 out_specs=[_tile(ATTN_WIDTH, rows=2 * TM)] * 2, out_shape=[out, out],
                 compiler_params=_params("parallel"))(proj, proj)


def _pair_rows(ref, kb):
    return ref[pl.ds(pl.multiple_of(kb * W2, W2), W2), :]


def _pair_tri(keep):
    j = lax.broadcasted_iota(jnp.int32, (W2, 2 * W2), 0)
    s = lax.broadcasted_iota(jnp.int32, (W2, 2 * W2), 1)
    same_head = (j >= TK) == ((s & (W2 - 1)) >= TK)
    return jnp.where(same_head & ((s >= W2) | keep(j & (TK - 1), s & (TK - 1))), 1.0, 0.0).astype(BF16)


def _attn_fwd(proj, kp, vp, shards=()):
    s = proj.shape[0]
    nq = s // TQ
    diag = TQ // TK
    n_ch = TQ // CH
    assert s // TK <= TK and diag % PIPE == 0

    def body(q_ref, k_ref, v_ref, o_ref, rs_ref, qb_scr, tri_scr, z_scr, l0_scr, cs_scr, a_scr, r_scr, rall_scr, acc_scr):
        qi = pl.program_id(1)
        lane = lax.broadcasted_iota(jnp.int32, (CH, W2), 1) & (TK - 1)
        row = lax.broadcasted_iota(jnp.int32, (CH, 1), 0)
        col = lax.broadcasted_iota(jnp.int32, (1, W2), 1) & (TK - 1)
        qb_scr[...] = (q_ref[...] * SCALE).astype(BF16)
        tri_scr[...] = _pair_tri(lambda j, ss: j > ss)
        r_scr[...] = jnp.zeros_like(r_scr)
        rall_scr[...] = jnp.zeros_like(rall_scr)
        acc_scr[...] = jnp.zeros_like(acc_scr)

        def causal(kb, c):
            return (kb * TK + col) < (qi * TQ + c * CH + row)

        def chunks(r0):
            return range(0 if r0 is None else r0 // CH, n_ch)

        def on_diagonal(r0, c):
            return r0 is not None and c * CH < r0 + TK

        def logits(kb, zb, r0=0):
            z_scr[zb, r0:, :] = _dot(qb_scr[r0:, :], _pair_rows(k_ref, kb), NT)

        def log_one_minus_beta(kb, zb, lb, r0=None):
            for c in chunks(r0):
                rows = slice(c * CH, (c + 1) * CH)
                sp = _softplus(z_scr[zb, rows, :])
                if on_diagonal(r0, c):
                    sp = jnp.where(causal(kb, c), sp, 0.0)
                l0_scr[lb, rows, :] = sp.astype(BF16)

        def sums(lb, r0=0):
            cs_scr[r0:, :] = _dot(l0_scr[lb, r0:, :], tri_scr[...])

        def weights(kb, zb, lb, ab, r0=None):
            for c in chunks(r0):
                rows = slice(c * CH, (c + 1) * CH)
                near = r_scr[rows, :]
                a = jnp.exp(z_scr[zb, rows, :] - l0_scr[lb, rows, :].astype(F32) - cs_scr[rows, :W2] - near)
                if on_diagonal(r0, c):
                    a = jnp.where(causal(kb, c), a, 0.0)
                a_scr[ab, rows, :] = a.astype(BF16)
                rall_scr[rows, :] = jnp.where(lane == kb, near, rall_scr[rows, :])
                r_scr[rows, :] = near + cs_scr[rows, W2:]

        def weighted_values(kb, ab, r0=0):
            acc_scr[r0:, :] += _dot(a_scr[ab, r0:, :], _pair_rows(v_ref, kb))

        n = qi * diag
        for dd in reversed(range(diag)):
            kb, r0 = n + dd, dd * TK
            logits(kb, dd % PIPE, r0)
            log_one_minus_beta(kb, dd % PIPE, dd % 2, r0)
            sums(dd % 2, r0)
            weights(kb, dd % PIPE, dd % 2, dd % 2, r0)
            weighted_values(kb, dd % 2, r0)

        def block(j):
            return jnp.maximum(n - 1 - j, 0)

        a_scr[...] = jnp.zeros_like(a_scr)
        logits(block(0), 0)
        logits(block(1), 1)
        log_one_minus_beta(block(0), 0, 0)

        def trip(m, carry):
            for u in range(PIPE):
                j = PIPE * m + u
                weighted_values(block(j - 1), (u - 1) % 2)
                sums(u % 2)
                logits(block(j + 2), (u + 2) % PIPE)
                log_one_minus_beta(block(j + 1), (u + 1) % PIPE, (u + 1) % 2)
                weights(block(j), u % PIPE, u % 2, u % 2)
            return carry

        lax.fori_loop(0, n // PIPE, trip, 0)
        weighted_values(block(n - 1), (PIPE - 1) % 2)
        rs_ref[...] = rall_scr[...]
        o_ref[...] = acc_scr[...].astype(o_ref.dtype)

    (o, rsave), gathered = _call_hosting(
        body, (proj, kp, vp), _AllToAll([True] * len(shards)) if shards else None, shards,
        name="attn_fwd", grid=(N_HEADS // 2, nq),
        in_specs=[pl.BlockSpec((TQ, LANES), lambda p, qi: (qi, Q_COL + p)),
                  pl.BlockSpec((2 * s, LANES), lambda p, qi: (0, p)),
                  pl.BlockSpec((2 * s, LANES), lambda p, qi: (0, p))],
        out_specs=[pl.BlockSpec((TQ, LANES), lambda p, qi: (qi, p)),
                   pl.BlockSpec((TQ, W2), lambda p, qi: (qi, p))],
        out_shape=[jax.ShapeDtypeStruct((s, ATTN_WIDTH), BF16), jax.ShapeDtypeStruct((s, N_HEADS // 2 * W2), F32)],
        scratch_shapes=[pltpu.VMEM((TQ, LANES), BF16), pltpu.VMEM((W2, 2 * W2), BF16),
                        pltpu.VMEM((PIPE, TQ, W2), F32), pltpu.VMEM((2, TQ, W2), BF16),
                        pltpu.VMEM((TQ, 2 * W2), F32), pltpu.VMEM((2, TQ, W2), BF16),
                        pltpu.VMEM((TQ, W2), F32), pltpu.VMEM((TQ, W2), F32), pltpu.VMEM((TQ, LANES), F32)])
    return o, rsave, gathered


def _attn_bwd(proj, kp, vp, do, rsave, sent=()):
    s = proj.shape[0]
    nq = s // TQ
    diag = TQ // TK
    n_ch = TQ // CH
    assert diag % PIPE == 0

    def body(q_ref, k_ref, v_ref, do_ref, rs_ref, dq_ref, dk_ref, dv_ref,
             qb_scr, dob_scr, after_scr, before_scr, z_scr, da_scr, l0_scr, beta_scr, cs_scr, a_scr, g_scr, cg_scr,
             dz_scr, pg_scr, dq_scr, dk_scr, dv_scr):
        qi = pl.program_id(1)

        @pl.when(qi == 0)
        def _():
            dk_scr[...] = jnp.zeros_like(dk_scr)
            dv_scr[...] = jnp.zeros_like(dv_scr)

        lm = [_head_lanes(0), _head_lanes(1)]
        lane = lax.broadcasted_iota(jnp.int32, (CH, TK), 1)
        row = lax.broadcasted_iota(jnp.int32, (CH, 1), 0)
        col = lax.broadcasted_iota(jnp.int32, (1, W2), 1) & (TK - 1)
        qb_scr[...] = (q_ref[...] * SCALE).astype(BF16)
        dob_scr[...] = do_ref[...].astype(BF16)
        after_scr[...] = _pair_tri(lambda j, ss: j > ss)[:, :W2]
        before_scr[...] = _pair_tri(lambda j, ss: j < ss)
        pg_scr[...] = jnp.zeros_like(pg_scr)
        dq_scr[...] = jnp.zeros_like(dq_scr)
        dz_scr[...] = jnp.zeros_like(dz_scr)

        def causal(kb, c):
            return (kb * TK + col) < (qi * TQ + c * CH + row)

        def chunks(r0):
            return range(0 if r0 is None else r0 // CH, n_ch)

        def on_diagonal(r0, c):
            return r0 is not None and c * CH < r0 + TK

        def logits(kb, zb, r0=0):
            z_scr[zb, r0:, :] = _dot(qb_scr[r0:, :], _pair_rows(k_ref, kb), NT)

        def do_dot_v(kb, db, r0=0):
            da_scr[db, r0:, :] = _dot(dob_scr[r0:, :], _pair_rows(v_ref, kb), NT)

        def gates(kb, zb, lb, bb, r0=None):
            for c in chunks(r0):
                rows = slice(c * CH, (c + 1) * CH)
                z = z_scr[zb, rows, :]
                sp = _softplus(z)
                beta_scr[bb, rows, :] = jnp.exp(z - sp)
                if on_diagonal(r0, c):
                    sp = jnp.where(causal(kb, c), sp, 0.0)
                l0_scr[lb, rows, :] = sp.astype(BF16)

        def suffix_sums(lb, r0=0):
            cs_scr[r0:, :] = _dot(l0_scr[lb, r0:, :], after_scr[...])

        def weights(kb, zb, lb, db, ab, r0=None):
            for c in chunks(r0):
                rows = slice(c * CH, (c + 1) * CH)
                keep = (kb * TK + lane) < (qi * TQ + c * CH + row) if on_diagonal(r0, c) else None
                for hh in range(2):
                    cols = slice(hh * TK, (hh + 1) * TK)
                    near = jnp.sum(jnp.where(lane == kb, rs_ref[rows, cols], 0.0), axis=1, keepdims=True)
                    a = jnp.exp(z_scr[zb, rows, cols] - l0_scr[lb, rows, cols].astype(F32) - cs_scr[rows, cols] - near)
                    if keep is not None:
                        a = jnp.where(keep, a, 0.0)
                    a_scr[ab, rows, cols] = a.astype(BF16)
                    g_scr[ab, rows, cols] = (a * da_scr[db, rows, cols]).astype(BF16)

        def prefix_sums(ab, r0=0):
            cg_scr[r0:, :] = _dot(g_scr[ab, r0:, :], before_scr[...])

        def dlogits(kb, ab, bb, zb2, r0=None):
            for c in chunks(r0):
                rows = slice(c * CH, (c + 1) * CH)
                earlier = pg_scr[rows, :]
                beta = beta_scr[bb, rows, :]
                g = g_scr[ab, rows, :].astype(F32)
                dz = g - beta * (g + cg_scr[rows, :W2] + earlier)
                if on_diagonal(r0, c):
                    dz = jnp.where(causal(kb, c), dz, 0.0)
                dz_scr[zb2, rows, :] = dz.astype(BF16)
                pg_scr[rows, :] = earlier + cg_scr[rows, W2:]

        def fold(t):
            return t[:TK, :] * lm[0] + t[TK:, :] * lm[1]

        def dq_dk(kb, zb2, r0=0):
            dq_scr[r0:, :] += _dot(dz_scr[zb2, r0:, :], _pair_rows(k_ref, kb))
            dk_scr[pl.ds(pl.multiple_of(kb * TK, TK), TK), :] += fold(_dot(dz_scr[zb2, r0:, :], qb_scr[r0:, :], TN))

        def dv(kb, ab, r0=0):
            dv_scr[pl.ds(pl.multiple_of(kb * TK, TK), TK), :] += fold(_dot(a_scr[ab, r0:, :], dob_scr[r0:, :], TN))

        n = qi * diag

        def block(j):
            return jnp.clip(j, 0, jnp.maximum(n - 1, 0))

        logits(block(0), 0)
        logits(block(1), 1)
        logits(block(2), 2)
        do_dot_v(block(0), 0)
        do_dot_v(block(1), 1)
        gates(block(0), 0, 0, 0)
        gates(block(1), 1, 1, 1)
        suffix_sums(0)
        weights(block(0), 0, 0, 0, 0)

        def trip(m, carry):
            for u in range(PIPE):
                t = PIPE * m + u
                dq_dk(block(t - 1), (u - 1) % 2)
                dv(block(t), u % 2)
                prefix_sums(u % 2)
                suffix_sums((u + 1) % 2)
                logits(block(t + 3), (u + 3) % PIPE)
                do_dot_v(block(t + 2), u % 2)
                gates(block(t + 2), (u + 2) % PIPE, u % 2, (u + 2) % PIPE)
                weights(block(t + 1), (u + 1) % PIPE, (u + 1) % 2, (u + 1) % 2, (u + 1) % 2)
                dlogits(block(t), u % 2, u % PIPE, u % 2)
            return carry

        lax.fori_loop(0, n // PIPE, trip, 0)
        dq_dk(block(n - 1), (PIPE - 1) % 2)

        for dd in range(diag):
            kb, r0, two, four = n + dd, dd * TK, dd % 2, dd % PIPE
            logits(kb, four, r0)
            do_dot_v(kb, two, r0)
            gates(kb, four, two, four, r0)
            suffix_sums(two, r0)
            weights(kb, four, two, two, two, r0)
            dv(kb, two, r0)
            prefix_sums(two, r0)
            dlogits(kb, two, four, two, r0)
            dq_dk(kb, two, r0)
        dq_ref[...] = (dq_scr[...] * SCALE).astype(dq_ref.dtype)

        @pl.when(qi == nq - 1)
        def _():
            dk_ref[...] = dk_scr[...].astype(dk_ref.dtype)
            dv_ref[...] = dv_scr[...].astype(dv_ref.dtype)

    def rows(c0):
        return pl.BlockSpec((TQ, LANES), lambda p, qi: (qi, c0 + p))

    def whole(c0):
        return pl.BlockSpec((s, LANES), lambda p, qi: (0, c0 + p))

    def f32(*shape):
        return pltpu.VMEM(shape, F32)

    def bf16(*shape):
        return pltpu.VMEM(shape, BF16)

    pairs = pl.BlockSpec((2 * s, LANES), lambda p, qi: (0, p))
    out = jax.ShapeDtypeStruct((s, ATTN_WIDTH), BF16)
    (dq, dk, dv), parts = _call_hosting(
        body, (proj, kp, vp, do, rsave), _AllToAll([False] * len(sent)) if sent else None, sent,
        name="attn_bwd", grid=(N_HEADS // 2, nq),
        in_specs=[rows(Q_COL), pairs, pairs, rows(0), pl.BlockSpec((TQ, W2), lambda p, qi: (qi, p))],
        out_specs=[rows(0), whole(0), whole(0)], out_shape=[out] * 3,
        scratch_shapes=[bf16(TQ, LANES), bf16(TQ, LANES), bf16(W2, W2), bf16(W2, 2 * W2),
                        f32(PIPE, TQ, W2), f32(2, TQ, W2), bf16(2, TQ, W2), f32(PIPE, TQ, W2), f32(TQ, W2),
                        bf16(2, TQ, W2), bf16(2, TQ, W2), f32(TQ, 2 * W2), bf16(2, TQ, W2),
                        f32(TQ, W2), f32(TQ, LANES), f32(s, LANES), f32(s, LANES)])
    return dq, dk, dv, parts


def _sum_adamw(name, parts, w, m, v, layer=None, into=None):
    n, r, c = parts.shape
    tr = r if r <= 256 else 256

    def body(p_ref, w_ref, m_ref, v_ref, g_ref, d_ref, nm_ref, nv_ref):
        g = p_ref[0].astype(F32)
        for j in range(1, n):
            g = g + p_ref[j].astype(F32)
        nm = ADAM_B1 * m_ref[...] + (1.0 - ADAM_B1) * g
        nv = ADAM_B2 * v_ref[...] + (1.0 - ADAM_B2) * (g * g)
        m_hat = nm / (1.0 - ADAM_B1 ** ADAM_STEP)
        v_hat = nv / (1.0 - ADAM_B2 ** ADAM_STEP)
        g_ref[...] = g
        d_ref[...] = -ADAM_LR * (m_hat / (jnp.sqrt(v_hat) + ADAM_EPS) + ADAM_WD * w_ref[...])
        nm_ref[...] = nm
        nv_ref[...] = nv

    if layer is None:
        mat = pl.BlockSpec((tr, c), lambda i: (i, 0))
        out = jax.ShapeDtypeStruct((r, c), F32)
    else:
        mat = pl.BlockSpec((None, tr, c), lambda i: (layer, i, 0))
        out = jax.ShapeDtypeStruct((DEPTH, r, c), F32)
    earlier = () if into is None else tuple(into)
    return _call(body if into is None else lambda *refs: body(*refs[:4], *refs[8:]),
                 name=name, grid=(r // tr,),
                 in_specs=[pl.BlockSpec((n, tr, c), lambda i: (0, i, 0)), mat, mat, mat]
                 + [pl.BlockSpec(memory_space=pl.ANY)] * len(earlier),
                 out_specs=[mat] * 4, out_shape=[out] * 4,
                 input_output_aliases={4 + k: k for k in range(len(earlier))},
                 compiler_params=_params("parallel"))(parts, w, m, v, *earlier)


def _natural(gathered):
    _, k, n = gathered.shape
    return gathered.transpose(1, 0, 2).reshape(k, N_DEV * n)


def _relu2_epi(acc):
    r = jnp.maximum(acc, 0.0)
    return acc, r * r


def _relu2_bwd_epi(acc, a_act):
    return (acc * (2.0 * jnp.maximum(a_act.astype(F32), 0.0)),)


def kernel(x, c, w_ada, b_ada, g_pre_mix, g_post_mix, g_pre_mlp, g_post_mlp, w_in, conv_w, w_proj_conv, w_proj_attn, w_out, w_mlp_in, w_mlp_out, loss_target, m_w_ada, m_b_ada, m_g_pre_mix, m_g_post_mix, m_g_pre_mlp, m_g_post_mlp, m_w_in, m_conv_w, m_w_proj_conv, m_w_proj_attn, m_w_out, m_w_mlp_in, m_w_mlp_out, v_w_ada, v_b_ada, v_g_pre_mix, v_g_post_mix, v_g_pre_mlp, v_g_post_mlp, v_w_in, v_conv_w, v_w_proj_conv, v_w_proj_attn, v_w_out, v_w_mlp_in, v_w_mlp_out):
    xi, yi, ci = _mesh_pos()
    me = 4 * xi + 2 * yi + ci
    d = D_MODEL
    x0 = x[0]
    seq = x0.shape[0]
    ada_cols = w_ada.shape[2]
    conv_cols = conv_w.shape[2]

    small = jnp.concatenate([c.reshape(-1), conv_w.reshape(-1)])
    small = jnp.pad(small, (0, 2 * d - small.shape[0])).reshape(8, 2 * d // 8)
    small_all = _all_gather("gather_c", [small])[0].reshape(N_DEV, 2 * d)
    c_all = small_all[:, :d]
    conv_all = small_all[:, d:d + DEPTH * 3 * conv_cols].reshape(N_DEV, DEPTH, 3, conv_cols)
    conv_all = conv_all.transpose(1, 2, 0, 3).reshape(DEPTH, 3, N_DEV * conv_cols)
    mod_cols = jnp.stack([_mm("mod_mm", c_all, w_ada[l], "nn", N_DEV, ada_cols, d, [F32], exact=True)
                          for l in range(DEPTH)], axis=1)
    mod_all = _all_gather("gather_mod", [mod_cols.reshape(N_DEV, DEPTH * ada_cols)])[0]
    mod_mine = lax.dynamic_index_in_dim(mod_all, me, axis=1, keepdims=False).reshape(N_DEV, DEPTH, ada_cols)
    mod = mod_mine.transpose(1, 0, 2).reshape(DEPTH, N_MOD * d) + b_ada

    sharded = {"w_in": w_in, "w_proj_conv": w_proj_conv, "w_proj_attn": w_proj_attn, "w_out": w_out,
               "w_mlp_in": w_mlp_in, "w_mlp_out": w_mlp_out}
    before_attention = ["w_in", "w_proj_conv"]

    def shard(key):
        nm, l = key
        return sharded[nm][l].astype(BF16)

    def natural(key, gathered):
        return gathered.reshape(-1, d) if key[0] in ("w_out", "w_mlp_out") else _natural(gathered)

    first = [(nm, 0) for nm in before_attention]
    full = {key: natural(key, g) for key, g in zip(first, _all_gather("gather_w", [shard(key) for key in first]))}
    saved = []
    xl = x0
    for l in range(DEPTH):
        riders = [(nm, l) for nm in sharded if (nm, l) not in full]
        if l + 1 < DEPTH:
            riders += [(nm, l + 1) for nm in sharded]
        sh1, sc1, gt1, sh2, sc2, gt2 = [mod[l:l + 1, i * d:(i + 1) * d] for i in range(N_MOD)]
        h = _prenorm_fwd(xl, g_pre_mix[l:l + 1], sc1, sh1)
        proj = _mm("proj", h, full[("w_in", l)], "nn", TMM, 1024, d, [F32])
        yc = _conv_fwd(proj, conv_all[l])
        y_conv = _mm("proj_conv", yc, full[("w_proj_conv", l)], "nn", TMM, d, CONV_WIDTH, [BF16])
        kp, vp = _pair_kv(proj)
        o, rsave, gathered = _attn_fwd(proj, kp, vp, [shard(key) for key in riders])
        full.update({key: natural(key, g) for key, g in zip(riders, gathered)})
        wg_in, wg_pc, wg_pa, wg_out, wg_mi, wg_mo = [full[(nm, l)] for nm in sharded]
        y_attn = _mm("proj_attn", o, wg_pa, "nn", TMM, d, ATTN_WIDTH, [BF16])
        merged = _gate_fwd(proj, y_conv, y_attn)
        mix_out = _mm("mix_out", merged, wg_out, "nn", TMM,d, d, [F32])
        x1 = _postnorm_fwd(xl, mix_out, g_post_mix[l:l + 1], gt1)
        h2 = _prenorm_fwd(x1, g_pre_mlp[l:l + 1], sc2, sh2)
        a_act, r = _mm("mlp_in", h2, wg_mi, "nn", TMM, 1024, d, [BF16, BF16], epi=_relu2_epi)
        ff = _mm("mlp_out", r, wg_mo, "nn", TMK, d, D_FF, [F32])
        saved.append((xl, h, proj, yc, kp, vp, o, rsave, y_conv, y_attn, merged, mix_out, x1, h2, a_act, r, ff))
        if l + 1 < DEPTH:
            xl = _postnorm_fwd(x1, ff, g_post_mlp[l:l + 1], gt2)
        else:
            dxo, sq = _postnorm_loss(x1, ff, g_post_mlp[l:l + 1], gt2, loss_target[0])
    loss = lax.psum(sq[0, 0] * (0.5 / d), ("x", "y", "c"))

    olds = {"w_in": (w_in, m_w_in, v_w_in), "w_proj_conv": (w_proj_conv, m_w_proj_conv, v_w_proj_conv),
            "w_proj_attn": (w_proj_attn, m_w_proj_attn, v_w_proj_attn), "w_out": (w_out, m_w_out, v_w_out),
            "w_mlp_in": (w_mlp_in, m_w_mlp_in, v_w_mlp_in), "w_mlp_out": (w_mlp_out, m_w_mlp_out, v_w_mlp_out)}
    big = {}
    pending = []

    def col_blocks(gw):
        k, n = gw.shape
        return gw.reshape(k, N_DEV, n // N_DEV).transpose(1, 0, 2)

    def update(entries, parts):
        for (nm, ll, _), part in zip(entries, parts):
            w_, m_, v_ = olds[nm]
            big[nm] = _sum_adamw("adamw_" + nm, part, w_, m_, v_, layer=ll, into=big.get(nm))

    dmod, small_grads = [None] * DEPTH, [None] * DEPTH
    for l in reversed(range(DEPTH)):
        wg_in, wg_pc, wg_pa, wg_out, wg_mi, wg_mo = [full[(nm, l)] for nm in sharded]
        xin, h, proj, yc, kp, vp, o, rsave, y_conv, y_attn, merged, mix_out, x1, h2, a_act, r, ff = saved[l]
        sh1, sc1, gt1, sh2, sc2, gt2 = [mod[l:l + 1, i * d:(i + 1) * d] for i in range(N_MOD)]

        dff, dgt2, dg_post_mlp = _postnorm_bwd(dxo, ff, g_post_mlp[l:l + 1], gt2)
        da = _mm("d_relu2", dff, wg_mo, "nt", TMM,1024, d, [BF16], epi=_relu2_bwd_epi, extra=(a_act,))
        gw_mo = _mm("gw_mlp_out", r, dff, "tn", 1024, d, TSK, [BF16])
        dh2 = _mm("d_h2", da, wg_mi, "nt", TMK, d, D_FF, [F32])
        gw_mi = _mm("gw_mlp_in", h2, da, "tn", d, 1024, TSK, [BF16])
        dx1, dsh2, dsc2, dg_pre_mlp = _prenorm_bwd(dh2, x1, g_pre_mlp[l:l + 1], sc2, dxo)

        dmix, dgt1, dg_post_mix = _postnorm_bwd(dx1, mix_out, g_post_mix[l:l + 1], gt1)
        dmerged = _mm("d_merged", dmix, wg_out, "nt", TMM, d, d, [BF16])
        gw_out = _mm("gw_out", merged, dmix, "tn", d, d, TSK, [BF16])
        dy_conv, dy_attn, dga, dgb = _gate_bwd(dmerged, proj, y_conv, y_attn)
        do = _mm("d_o", dy_attn, wg_pa, "nt", TMM, ATTN_WIDTH, d, [BF16])
        gw_pa = _mm("gw_proj_attn", o, dy_attn, "tn", ATTN_WIDTH, d, TSK, [BF16])
        dyc = _mm("d_yc", dy_conv, wg_pc, "nt", TMM,CONV_WIDTH, d, [F32])
        gw_pc = _mm("gw_proj_conv", yc, dy_conv, "tn", CONV_WIDTH, d, TSK, [BF16])
        pending += [("w_mlp_out", l, gw_mo.reshape(N_DEV, D_FF // N_DEV, d)), ("w_mlp_in", l, col_blocks(gw_mi)),
                    ("w_out", l, gw_out.reshape(N_DEV, d // N_DEV, d)), ("w_proj_attn", l, col_blocks(gw_pa)),
                    ("w_proj_conv", l, col_blocks(gw_pc))]
        if l == 0:
            dq, dk, dv, parts = _attn_bwd(proj, kp, vp, do, rsave, [blocks for _, _, blocks in pending])
            update(pending, parts)
            pending = []
        else:
            dq, dk, dv, _ = _attn_bwd(proj, kp, vp, do, rsave)
        dbg, dcg, du, dw0, dw1, dw2 = _conv_bwd(dyc, proj, conv_all[l])
        dproj = jnp.concatenate([dbg, dcg, du, dq, dk, dv, dga, dgb], axis=1)
        gw_in = _mm("gw_in", h, dproj, "tn", d, 1024, TSK, [BF16])
        pending.append(("w_in", l, col_blocks(gw_in)))
        if l == 0:
            dh, parts = _mm("d_h", dproj, wg_in, "nt", TMK, d, IN_COLS, [F32], sent=[blocks for _, _, blocks in pending])
            update(pending, parts)
            pending = []
        else:
            dh = _mm("d_h", dproj, wg_in, "nt", TMK, d, IN_COLS, [F32])
        dxo, dsh1, dsc1, dg_pre_mix = _prenorm_bwd(dh, xin, g_pre_mix[l:l + 1], sc1, dx1)

        dmod[l] = jnp.concatenate([dsh1, dsc1, dgt1, dsh2, dsc2, dgt2], axis=1)
        small_grads[l] = (dg_pre_mix, dg_post_mix, dg_pre_mlp, dg_post_mlp, jnp.concatenate([dw0, dw1, dw2], axis=0))
    assert not pending

    vec = jnp.concatenate(
        [dmod[l].reshape(-1) for l in range(DEPTH)]
        + [small_grads[l][i].reshape(-1) for i in range(4) for l in range(DEPTH)]
        + [small_grads[l][4].reshape(-1) for l in range(DEPTH)])
    n_vec = vec.shape[0]
    vec_all = _all_gather("gather_small", [vec.reshape(8, n_vec // 8)])[0].reshape(N_DEV, n_vec)
    n_mod = DEPTH * N_MOD * d
    dmod_all = vec_all[:, :n_mod].reshape(N_DEV, DEPTH, N_MOD * d)
    res = {}
    res["b_ada"] = _sum_adamw("adamw_b_ada", dmod_all, b_ada, m_b_ada, v_b_ada)
    off = n_mod
    for nm, (w_, m_, v_) in zip(
            ["g_pre_mix", "g_post_mix", "g_pre_mlp", "g_post_mlp"],
            [(g_pre_mix, m_g_pre_mix, v_g_pre_mix), (g_post_mix, m_g_post_mix, v_g_post_mix),
             (g_pre_mlp, m_g_pre_mlp, v_g_pre_mlp), (g_post_mlp, m_g_post_mlp, v_g_post_mlp)]):
        res[nm] = _sum_adamw("adamw_gain", vec_all[:, off:off + DEPTH * d].reshape(N_DEV, DEPTH, d), w_, m_, v_)
        off += DEPTH * d
    dconv_all = vec_all[:, off:].reshape(N_DEV, DEPTH * 3, CONV_WIDTH)
    dconv_mine = lax.dynamic_slice_in_dim(dconv_all, me * conv_cols, conv_cols, axis=2)
    res["conv_w"] = [t.reshape(DEPTH, 3, conv_cols) for t in _sum_adamw(
        "adamw_conv_w", dconv_mine, conv_w.reshape(DEPTH * 3, conv_cols), m_conv_w.reshape(DEPTH * 3, conv_cols),
        v_conv_w.reshape(DEPTH * 3, conv_cols))]

    c_t = jnp.pad(c_all.T, ((0, 0), (0, LANES - N_DEV)))
    dmod_mine = lax.dynamic_slice_in_dim(dmod_all, me * ada_cols, ada_cols, axis=2)
    for l in range(DEPTH):
        dm_l = jnp.pad(dmod_mine[:, l, :], ((0, LANES - N_DEV), (0, 0)))
        gw_ada = _mm("gw_ada", c_t, dm_l, "nn", 256, ada_cols, LANES, [F32], exact=True)
        res["w_ada"] = _sum_adamw("adamw_w_ada", gw_ada[None], w_ada, m_w_ada, v_w_ada, layer=l, into=res.get("w_ada"))
    res.update(big)

    order = ["w_ada", "b_ada", "g_pre_mix", "g_post_mix", "g_pre_mlp", "g_post_mlp", "w_in", "conv_w",
             "w_proj_conv", "w_proj_attn", "w_out", "w_mlp_in", "w_mlp_out"]
    outs = [loss, dxo[None]]
    for i in range(4):
        outs += [res[nm][i] for nm in order]
    return tuple(outs)
```
